```python
import math
import jax, jax.numpy as jnp
from jax import lax
import numpy as np

D_MODEL = 1024
BATCH = 4
SEQ = 4096
DEPTH = 1

MEM_LEN = 256
D_MIX = D_MODEL
D_DIFF = D_MIX // 2
D_HGRN = D_MIX // 4
D_XMEM = D_MIX // 4
H_DIFF = 4
DH_DIFF = D_DIFF // (2 * H_DIFF)
H_HGRN = 4
DK_HGRN = D_HGRN // H_HGRN
H_XMEM = 4
DH_XMEM = D_XMEM // H_XMEM
D_IN_PROJ = 3 * D_DIFF + 5 * D_HGRN + D_XMEM
Q_BLOCK = 128
HGRN_CHUNK = 16
N_EXPERTS = 32
TOP_K = 4
D_EXPERT = D_MODEL
SWIGLU_ALPHA = 1.702
SWIGLU_LIMIT = 7.0
MOE_BLOCK = 256
NORM_EPS = 1e-5

kernel_name = "hybrid_diffattn_hgrn2_memxattn_moe_encoder"


def layer_norm(t, g, b):
    t32 = t.astype(jnp.float32)
    mu = jnp.mean(t32, axis=-1, keepdims=True)
    var = jnp.mean(jnp.square(t32 - mu), axis=-1, keepdims=True)
    return ((t32 - mu) * lax.rsqrt(var + NORM_EPS) * g + b).astype(t.dtype)


def rms_norm(t):
    t32 = t.astype(jnp.float32)
    return t32 * lax.rsqrt(jnp.mean(jnp.square(t32), axis=-1, keepdims=True) + NORM_EPS)


def alibi_slopes(n_heads):
    return jnp.asarray(2.0 ** (-8.0 * np.arange(1, n_heads + 1) / n_heads), dtype=jnp.float32)


def diff_attention(q, k, v, lam):
    B, H, _, S, Dh = q.shape
    nb = S // Q_BLOCK
    scale = 1.0 / math.sqrt(Dh)
    slopes = alibi_slopes(H)
    kpos = jnp.arange(S, dtype=jnp.int32)
    qb = jnp.moveaxis(q.reshape(B, H, 2, nb, Q_BLOCK, Dh), 3, 0)

    def block(args):
        q_blk, i = args
        qpos = i * Q_BLOCK + jnp.arange(Q_BLOCK, dtype=jnp.int32)
        dist = jnp.abs(qpos[:, None] - kpos[None, :]).astype(jnp.float32)
        bias = -slopes[:, None, None] * dist
        s = jnp.einsum('bhmqd,bhmkd->bhmqk', q_blk, k).astype(jnp.float32) * scale
        p = jax.nn.softmax(s + bias[None, :, None], axis=-1)
        p = p[:, :, 0] - lam * p[:, :, 1]
        return jnp.einsum('bhqk,bhkd->bhqd', p.astype(v.dtype), v)

    out = lax.map(block, (qb, jnp.arange(nb, dtype=jnp.int32)))
    return jnp.moveaxis(out, 0, 2).reshape(B, H, S, 2 * Dh)


def hgrn2_chunkwise(q, k, logf, v):
    B, H, S, Dk = q.shape
    Dv = v.shape[-1]
    C = HGRN_CHUNK
    N = S // C
    q = q.reshape(B, H, N, C, Dk)
    k = k.reshape(B, H, N, C, Dk).astype(jnp.float32)
    v = v.reshape(B, H, N, C, Dv)
    b = jnp.cumsum(logf.reshape(B, H, N, C, Dk).astype(jnp.float32), axis=3)
    mask = jnp.tril(jnp.ones((C, C), dtype=bool))
    rel = b[:, :, :, :, None, :] - b[:, :, :, None, :, :]
    decay = jnp.exp(jnp.where(mask[:, :, None], rel, -jnp.inf))
    A = jnp.einsum('bhntd,bhnsd,bhntsd->bhnts', q.astype(jnp.float32), k, decay)
    o_intra = jnp.einsum('bhnts,bhnsv->bhntv', A, v.astype(jnp.float32))
    b_last = b[:, :, :, -1, :]
    k_to_end = k * jnp.exp(b_last[:, :, :, None, :] - b)
    U = jnp.einsum('bhncd,bhncv->bhndv', k_to_end, v.astype(jnp.float32))
    a = jnp.exp(b_last)

    def step(S_prev, inp):
        a_n, U_n = inp
        return a_n[..., None] * S_prev + U_n, S_prev

    S0 = jnp.zeros((B, H, Dk, Dv), jnp.float32)
    _, S_before = lax.scan(step, S0, (jnp.moveaxis(a, 2, 0), jnp.moveaxis(U, 2, 0)))
    S_before = jnp.moveaxis(S_before, 0, 2)
    o_inter = jnp.einsum('bhncd,bhndv->bhncv', q.astype(jnp.float32) * jnp.exp(b), S_before)
    return (o_intra + o_inter).reshape(B, H, S, Dv)


def hybrid_mixer(h, mem, w_in, lq1, lk1, lq2, lk2, diff_norm_w, lb_f, lb_b,
                 hgrn_norm_w, w_mem_kv, w_o, layer_idx):
    B, S, _ = h.shape
    sizes = [D_DIFF] * 3 + [D_HGRN] * 5 + [D_XMEM]
    points = [int(p) for p in np.cumsum(sizes)[:-1]]
    u = h @ w_in
    dq, dk, dv, hq, hf_f, hf_b, hi, hg, mq = jnp.split(u, points, axis=-1)

    q = dq.reshape(B, S, H_DIFF, 2, DH_DIFF).transpose(0, 2, 3, 1, 4)
    k = dk.reshape(B, S, H_DIFF, 2, DH_DIFF).transpose(0, 2, 3, 1, 4)
    v = dv.reshape(B, S, H_DIFF, 2 * DH_DIFF).transpose(0, 2, 1, 3)
    lam_init = 0.8 - 0.6 * math.exp(-0.3 * layer_idx)
    lam = (jnp.exp(jnp.sum(lq1.astype(jnp.float32) * lk1.astype(jnp.float32)))
           - jnp.exp(jnp.sum(lq2.astype(jnp.float32) * lk2.astype(jnp.float32))) + lam_init)
    o = diff_attention(q, k, v, lam)
    o = rms_norm(o) * diff_norm_w * (1.0 - lam_init)
    o_diff = o.transpose(0, 2, 1, 3).reshape(B, S, D_DIFF)

    heads = lambda t: t.reshape(B, S, H_HGRN, DK_HGRN).transpose(0, 2, 1, 3)
    q_h = heads(jax.nn.silu(hq))
    v_h = heads(hi)

    def gate(f_logit, lb):
        f = lb + (1.0 - lb) * jax.nn.sigmoid(f_logit.astype(jnp.float32))
        return heads(1.0 - f), heads(jnp.log(f))

    k_f, logf_f = gate(hf_f, lb_f)
    k_b, logf_b = gate(hf_b, lb_b)
    o_fwd = hgrn2_chunkwise(q_h, k_f, logf_f, v_h)
    flip = lambda t: jnp.flip(t, axis=2)
    o_bwd = flip(hgrn2_chunkwise(flip(q_h), flip(k_b), flip(logf_b), flip(v_h)))
    o = (o_fwd + o_bwd).transpose(0, 2, 1, 3)
    o = rms_norm(o) * hgrn_norm_w.reshape(H_HGRN, DK_HGRN)
    o_hgrn = o.reshape(B, S, D_HGRN) * jax.nn.sigmoid(hg.astype(jnp.float32))

    kv = mem @ w_mem_kv
    mk, mv = jnp.split(kv, 2, axis=-1)
    M = mem.shape[1]
    mk = mk.reshape(B, M, H_XMEM, DH_XMEM).transpose(0, 2, 1, 3)
    mv = mv.reshape(B, M, H_XMEM, DH_XMEM).transpose(0, 2, 1, 3)
    mqh = mq.reshape(B, S, H_XMEM, DH_XMEM).transpose(0, 2, 1, 3)
    s = jnp.einsum('bhsd,bhmd->bhsm', mqh, mk).astype(jnp.float32) / math.sqrt(DH_XMEM)
    p = jax.nn.softmax(s, axis=-1)
    o_mem = jnp.einsum('bhsm,bhmd->bshd', p.astype(mv.dtype), mv).reshape(B, S, D_XMEM)

    cat = jnp.concatenate([o_diff.astype(h.dtype), o_hgrn.astype(h.dtype), o_mem.astype(h.dtype)], axis=-1)
    return cat @ w_o


def moe_ffn(x2d, router_w, router_b, w_gu, b_gu, w_dn, b_dn):
    T, D = x2d.shape
    TK = T * TOP_K
    logits = (x2d @ router_w + router_b).astype(jnp.float32)
    top_val, top_idx = lax.top_k(logits, TOP_K)
    gates = jax.nn.softmax(top_val, axis=-1)
    flat_e = top_idx.reshape(-1).astype(jnp.int32)
    flat_tok = jnp.repeat(jnp.arange(T, dtype=jnp.int32), TOP_K)
    order = jnp.argsort(flat_e)
    sorted_e = flat_e[order]
    sorted_tok = flat_tok[order]
    sorted_gate = gates.reshape(-1)[order]
    counts = jnp.bincount(flat_e, length=N_EXPERTS).astype(jnp.int32)
    padded = (counts + MOE_BLOCK - 1) // MOE_BLOCK * MOE_BLOCK
    start = jnp.cumsum(counts) - counts
    pend = jnp.cumsum(padded)
    pstart = pend - padded
    dest = pstart[sorted_e] + jnp.arange(TK, dtype=jnp.int32) - start[sorted_e]
    n_blocks = -(-TK // MOE_BLOCK) + N_EXPERTS
    n_rows = n_blocks * MOE_BLOCK
    row_tok = jnp.zeros((n_rows,), jnp.int32).at[dest].set(sorted_tok)
    row_gate = jnp.zeros((n_rows,), jnp.float32).at[dest].set(sorted_gate)
    block_e = jnp.minimum(
        jnp.searchsorted(pend, jnp.arange(n_blocks, dtype=jnp.int32) * MOE_BLOCK, side='right'),
        N_EXPERTS - 1)
    xs = x2d[row_tok].reshape(n_blocks, MOE_BLOCK, D)

    def expert_block(args):
        xb, e = args
        hh = xb @ w_gu[e] + b_gu[e]
        glu = jnp.minimum(hh[:, 0::2], SWIGLU_LIMIT)
        lin = jnp.clip(hh[:, 1::2], -SWIGLU_LIMIT, SWIGLU_LIMIT)
        act = glu * jax.nn.sigmoid(SWIGLU_ALPHA * glu) * (lin + 1.0)
        return act @ w_dn[e] + b_dn[e]

    ys = lax.map(expert_block, (xs, block_e)).reshape(n_rows, D).astype(jnp.float32)
    out = jnp.zeros((T, D), jnp.float32).at[row_tok].add(ys * row_gate[:, None])
    return out.astype(x2d.dtype)


def setup_inputs(seed: int = 0) -> dict:
    key = jax.random.key(seed)
    ks = jax.random.split(key, 24)
    nrm = lambda k, shape, sc: jax.random.normal(k, shape, jnp.float32) * sc
    beta = (8.0 * DEPTH) ** -0.25
    L = DEPTH
    return {
        "x": nrm(ks[0], (BATCH, SEQ, D_MODEL), 1.0),
        "mem": nrm(ks[1], (BATCH, MEM_LEN, D_MODEL), 1.0),
        "w_in": nrm(ks[2], (L, D_MODEL, D_IN_PROJ), D_MODEL ** -0.5),
        "lam_q1": nrm(ks[3], (L, DH_DIFF), 0.1),
        "lam_k1": nrm(ks[4], (L, DH_DIFF), 0.1),
        "lam_q2": nrm(ks[5], (L, DH_DIFF), 0.1),
        "lam_k2": nrm(ks[6], (L, DH_DIFF), 0.1),
        "diff_norm_w": 1.0 + nrm(ks[7], (L, 2 * DH_DIFF), 0.01),
        "hgrn_lb_fwd": nrm(ks[8], (L + 1, D_HGRN), 1.0),
        "hgrn_lb_bwd": nrm(ks[9], (L + 1, D_HGRN), 1.0),
        "hgrn_norm_w": 1.0 + nrm(ks[10], (L, D_HGRN), 0.01),
        "w_mem_kv": nrm(ks[11], (L, D_MODEL, 2 * D_XMEM), D_MODEL ** -0.5),
        "w_o": nrm(ks[12], (L, D_MIX, D_MODEL), D_MIX ** -0.5 * beta),
        "ln1_g": 1.0 + nrm(ks[13], (L, D_MODEL), 0.01),
        "ln1_b": nrm(ks[14], (L, D_MODEL), 0.01),
        "router_w": nrm(ks[15], (L, D_MODEL, N_EXPERTS), D_MODEL ** -0.5),
        "router_b": nrm(ks[16], (L, N_EXPERTS), 0.01),
        "w_gate_up": nrm(ks[17], (L, N_EXPERTS, D_MODEL, 2 * D_EXPERT), D_MODEL ** -0.5),
        "b_gate_up": nrm(ks[18], (L, N_EXPERTS, 2 * D_EXPERT), 0.01),
        "w_down": nrm(ks[19], (L, N_EXPERTS, D_EXPERT, D_MODEL), D_EXPERT ** -0.5 * beta),
        "b_down": nrm(ks[20], (L, N_EXPERTS, D_MODEL), 0.01),
        "ln2_g": 1.0 + nrm(ks[21], (L, D_MODEL), 0.01),
        "ln2_b": nrm(ks[22], (L, D_MODEL), 0.01),
    }


def reference(x, mem, w_in, lam_q1, lam_k1, lam_q2, lam_k2, diff_norm_w, hgrn_lb_fwd,
              hgrn_lb_bwd, hgrn_norm_w, w_mem_kv, w_o, ln1_g, ln1_b, router_w, router_b,
              w_gate_up, b_gate_up, w_down, b_down, ln2_g, ln2_b):
    B, S, D = x.shape
    alpha = (2.0 * DEPTH) ** 0.25
    lb_fwd_all = jnp.cumsum(jax.nn.softmax(hgrn_lb_fwd.astype(jnp.float32), axis=0), axis=0)
    lb_bwd_all = jnp.cumsum(jax.nn.softmax(hgrn_lb_bwd.astype(jnp.float32), axis=0), axis=0)
    h = x
    for l in range(DEPTH):
        mix = hybrid_mixer(h, mem, w_in[l], lam_q1[l], lam_k1[l], lam_q2[l], lam_k2[l],
                           diff_norm_w[l], lb_fwd_all[l], lb_bwd_all[l], hgrn_norm_w[l],
                           w_mem_kv[l], w_o[l], l)
        h = layer_norm(alpha * h + mix, ln1_g[l], ln1_b[l])
        ffn = moe_ffn(h.reshape(B * S, D), router_w[l], router_b[l], w_gate_up[l],
                      b_gate_up[l], w_down[l], b_down[l]).reshape(B, S, D)
        h = layer_norm(alpha * h + ffn, ln2_g[l], ln2_b[l])
    return h
```

```python
import functools
import math

import jax
import jax.numpy as jnp
import numpy as np
from jax import lax
from jax.experimental import pallas as pl
from jax.experimental.pallas import tpu as pltpu

F32 = jnp.float32
BF16 = jnp.bfloat16
I32 = jnp.int32

D_MODEL = 1024
N_LAYERS = 1
D_DIFF = 512
D_HGRN = 256
D_XMEM = 256
N_HEADS = 4
DH = 64
DH_SHIFT = 6
D_IN_PROJ = 3 * D_DIFF + 5 * D_HGRN + D_XMEM
CHUNK = 16
N_EXPERTS = 32
TOP_K = 4
D_EXPERT = 1024
MOE_BLOCK = 256
SWIGLU_ALPHA = 1.702
SWIGLU_LIMIT = 7.0
NORM_EPS = 1e-5
LANES = 128
VMEM_LIMIT = 56 * 1024 * 1024


def _nt_dot(a, b):
    return lax.dot_general(a, b, (((1,), (1,)), ((), ())), preferred_element_type=F32)


def _tn_dot(a, b):
    return lax.dot_general(a, b, (((0,), (0,)), ((), ())), preferred_element_type=F32)


def _dot(a, b):
    return jnp.dot(a, b, preferred_element_type=F32)


def _sigmoid(x):
    return 1.0 / (1.0 + jnp.exp(-x))


def _layer_norm(y, g, b):
    mu = jnp.mean(y, axis=-1, keepdims=True)
    yc = y - mu
    var = jnp.mean(yc * yc, axis=-1, keepdims=True)
    return yc * lax.rsqrt(var + NORM_EPS) * g + b


def _in_proj_kernel(x_ref, w_ref, lbf_ref, lbb_ref, ua_ref, uh_ref, um_ref):
    xb = x_ref[...].astype(BF16)

    def proj(c0, width):
        return _dot(xb, w_ref[:, c0:c0 + width])

    ua_ref[:, 0:D_DIFF] = (proj(0, D_DIFF) * (1.0 / math.sqrt(DH))).astype(BF16)
    ua_ref[:, D_DIFF:2 * D_DIFF] = proj(D_DIFF, D_DIFF).astype(BF16)
    ua_ref[:, 2 * D_DIFF:3 * D_DIFF] = proj(2 * D_DIFF, D_DIFF).astype(BF16)

    def lower_bound(lb_ref):
        a = lb_ref[0:1, :]
        b = lb_ref[1:2, :]
        m = jnp.maximum(a, b)
        ea = jnp.exp(a - m)
        eb = jnp.exp(b - m)
        return ea / (ea + eb)

    base = 3 * D_DIFF
    hq = proj(base, D_HGRN)
    uh_ref[:, 0:D_HGRN] = hq * _sigmoid(hq)
    for d, lb_ref in enumerate((lbf_ref, lbb_ref)):
        lb = lower_bound(lb_ref)
        f = lb + (1.0 - lb) * _sigmoid(proj(base + (1 + d) * D_HGRN, D_HGRN))
        uh_ref[:, (1 + 2 * d) * D_HGRN:(2 + 2 * d) * D_HGRN] = 1.0 - f
        uh_ref[:, (2 + 2 * d) * D_HGRN:(3 + 2 * d) * D_HGRN] = jnp.log(f)
    uh_ref[:, 5 * D_HGRN:6 * D_HGRN] = proj(base + 3 * D_HGRN, D_HGRN)
    uh_ref[:, 6 * D_HGRN:7 * D_HGRN] = _sigmoid(proj(base + 4 * D_HGRN, D_HGRN))
    um_ref[...] = proj(base + 5 * D_HGRN, D_XMEM).astype(BF16)


def _in_proj(x2d, w_in_bf, lb_f, lb_b):
    t = x2d.shape[0]
    tm = 256
    return pl.pallas_call(
        _in_proj_kernel,
        grid=(t // tm,),
        in_specs=[
            pl.BlockSpec((tm, D_MODEL), lambda i: (i, 0)),
            pl.BlockSpec((D_MODEL, D_IN_PROJ), lambda i: (0, 0)),
            pl.BlockSpec((N_LAYERS + 1, D_HGRN), lambda i: (0, 0)),
            pl.BlockSpec((N_LAYERS + 1, D_HGRN), lambda i: (0, 0)),
        ],
        out_specs=[
            pl.BlockSpec((tm, 3 * D_DIFF), lambda i: (i, 0)),
            pl.BlockSpec((tm, 7 * D_HGRN), lambda i: (i, 0)),
            pl.BlockSpec((tm, D_XMEM), lambda i: (i, 0)),
        ],
        out_shape=[
            jax.ShapeDtypeStruct((t, 3 * D_DIFF), BF16),
            jax.ShapeDtypeStruct((t, 7 * D_HGRN), F32),
            jax.ShapeDtypeStruct((t, D_XMEM), BF16),
        ],
        compiler_params=pltpu.CompilerParams(
            dimension_semantics=("arbitrary",), vmem_limit_bytes=VMEM_LIMIT),
        name="in_proj",
    )(x2d, w_in_bf, lb_f, lb_b)


def _diff_attn_kernel(slopes_ref, q_ref, k_ref, v_ref, nw_ref, lq1_ref, lk1_ref,
                      lq2_ref, lk2_ref, o_ref, *, tq, tk, seq, lam_init):
    h = pl.program_id(1)
    i = pl.program_id(2)
    slope = slopes_ref[h]
    q = q_ref[...]
    lane = lax.broadcasted_iota(I32, (tq, 2 * DH), 1)
    zero = jnp.zeros_like(q)
    q1 = jnp.where(lane < DH, q, zero)
    q2 = jnp.where(lane >= DH, q, zero)
    qpos = i * tq + lax.broadcasted_iota(I32, (tq, tk), 0)
    kidx = lax.broadcasted_iota(I32, (tq, tk), 1)

    def step(j, carry):
        m1, l1, a1, m2, l2, a2 = carry
        k0 = pl.multiple_of(j * tk, tk)
        kc = k_ref[pl.ds(k0, tk), :]
        vc = v_ref[pl.ds(k0, tk), :]
        bias = -slope * jnp.abs(qpos - (k0 + kidx)).astype(F32)

        def one(qm, m, l, a):
            s = _nt_dot(qm, kc) + bias
            m_new = jnp.maximum(m, jnp.max(s, axis=-1, keepdims=True))
            p = jnp.exp(s - m_new)
            r = jnp.exp(m - m_new)
            l = r * l + jnp.sum(p, axis=-1, keepdims=True)
            a = r * a + _dot(p.astype(BF16), vc)
            return m_new, l, a

        m1, l1, a1 = one(q1, m1, l1, a1)
        m2, l2, a2 = one(q2, m2, l2, a2)
        return m1, l1, a1, m2, l2, a2

    neg = jnp.full((tq, 1), -jnp.inf, F32)
    z1 = jnp.zeros((tq, 1), F32)
    za = jnp.zeros((tq, 2 * DH), F32)
    m1, l1, a1, m2, l2, a2 = lax.fori_loop(0, seq // tk, step, (neg, z1, za, neg, z1, za))

    lam = (jnp.exp(jnp.sum(lq1_ref[...] * lk1_ref[...], axis=-1, keepdims=True))
           - jnp.exp(jnp.sum(lq2_ref[...] * lk2_ref[...], axis=-1, keepdims=True))
           + lam_init)
    o = a1 / l1 - lam * (a2 / l2)
    o = o * lax.rsqrt(jnp.mean(o * o, axis=-1, keepdims=True) + NORM_EPS)
    o_ref[...] = (o * nw_ref[...] * (1.0 - lam_init)).astype(o_ref.dtype)


def _diff_attn(ua, slopes, nw, lq1, lk1, lq2, lk2, batch, seq, lam_init):
    tq, tk = 256, 512
    nq = seq // tq
    kern = functools.partial(_diff_attn_kernel, tq=tq, tk=tk, seq=seq, lam_init=lam_init)
    small = lambda b, h, i, s: (0, 0)
    return pl.pallas_call(
        kern,
        grid_spec=pltpu.PrefetchScalarGridSpec(
            num_scalar_prefetch=1,
            grid=(batch, N_HEADS, nq),
            in_specs=[
                pl.BlockSpec((tq, 2 * DH), lambda b, h, i, s: (b * nq + i, h)),
                pl.BlockSpec((seq, 2 * DH), lambda b, h, i, s: (b, N_HEADS + h)),
                pl.BlockSpec((seq, 2 * DH), lambda b, h, i, s: (b, 2 * N_HEADS + h)),
                pl.BlockSpec((1, 2 * DH), small),
                pl.BlockSpec((1, DH), small),
                pl.BlockSpec((1, DH), small),
                pl.BlockSpec((1, DH), small),
                pl.BlockSpec((1, DH), small),
            ],
            out_specs=pl.BlockSpec((tq, 2 * DH), lambda b, h, i, s: (b * nq + i, h)),
        ),
        out_shape=jax.ShapeDtypeStruct((batch * seq, D_DIFF), BF16),
        compiler_params=pltpu.CompilerParams(
            dimension_semantics=("arbitrary", "arbitrary", "arbitrary"),
            vmem_limit_bytes=VMEM_LIMIT),
        name="diff_attn",
    )(slopes, ua, ua, ua, nw, lq1, lk1, lq2, lk2)


def _mem_attn_kernel(q_ref, mem_ref, wkv_ref, o_ref, kv_ref, *, tq, mlen):
    @pl.when(pl.program_id(1) == 0)
    def _():
        kv_ref[...] = _dot(mem_ref[...].astype(BF16), wkv_ref[...]).astype(BF16)

    q = q_ref[...]
    mk = kv_ref[:, 0:D_XMEM]
    mv = kv_ref[:, D_XMEM:2 * D_XMEM]
    qhead = lax.broadcasted_iota(I32, (tq, D_XMEM), 1) >> DH_SHIFT
    vhead = lax.broadcasted_iota(I32, (mlen, D_XMEM), 1) >> DH_SHIFT
    acc = jnp.zeros((tq, D_XMEM), F32)
    for h in range(N_HEADS):
        qh = jnp.where(qhead == h, q, jnp.zeros_like(q))
        s = _nt_dot(qh, mk) * (1.0 / math.sqrt(DH))
        e = jnp.exp(s - jnp.max(s, axis=-1, keepdims=True))
        p = e / jnp.sum(e, axis=-1, keepdims=True)
        vh = jnp.where(vhead == h, mv, jnp.zeros_like(mv))
        acc = acc + _dot(p.astype(BF16), vh)
    o_ref[...] = acc.astype(o_ref.dtype)


def _mem_attn(um, mem2d, wkv_bf, batch, seq, mlen):
    tq = 512
    nq = seq // tq
    kern = functools.partial(_mem_attn_kernel, tq=tq, mlen=mlen)
    return pl.pallas_call(
        kern,
        grid=(batch, nq),
        in_specs=[
            pl.BlockSpec((tq, D_XMEM), lambda b, i: (b * nq + i, 0)),
            pl.BlockSpec((mlen, D_MODEL), lambda b, i: (b, 0)),
            pl.BlockSpec((D_MODEL, 2 * D_XMEM), lambda b, i: (0, 0)),
        ],
        out_specs=pl.BlockSpec((tq, D_XMEM), lambda b, i: (b * nq + i, 0)),
        out_shape=jax.ShapeDtypeStruct((batch * seq, D_XMEM), BF16),
        scratch_shapes=[pltpu.VMEM((mlen, 2 * D_XMEM), BF16)],
        compiler_params=pltpu.CompilerParams(
            dimension_semantics=("arbitrary", "arbitrary"), vmem_limit_bytes=VMEM_LIMIT),
        name="mem_attn",
    )(um, mem2d, wkv_bf)


def _hgrn2_kernel(q_ref, kf_ref, lf_ref, kb_ref, lb_ref, v_ref, g_ref, nw_ref, o_ref,
                  bf_ref, bb_ref, acc_ref, *, seq, rb):
    n_chunks = seq // CHUNK
    w = 2 * DH
    row = lax.broadcasted_iota(I32, (rb, w), 0) & (CHUNK - 1)
    li = lax.broadcasted_iota(I32, (w, w), 0) >> DH_SHIFT
    lj = lax.broadcasted_iota(I32, (w, w), 1) >> DH_SHIFT
    same_head = li == lj
    head_ones = jnp.where(same_head, 1.0, 0.0).astype(BF16)

    def chunk_cumsum(x, reverse):
        for sh in (1, 2, 4, 8):
            if reverse:
                moved = pltpu.roll(x, rb - sh, 0)
                keep = row + sh < CHUNK
            else:
                moved = pltpu.roll(x, sh, 0)
                keep = row >= sh
            x = x + jnp.where(keep, moved, 0.0)
        return x

    def intra(blk, carry):
        r0 = pl.multiple_of(blk * rb, rb)
        rows = pl.ds(r0, rb)
        q = q_ref[rows, :]
        v = v_ref[rows, :]
        acc = jnp.zeros((rb, w), F32)
        for reverse, k_ref, l_ref, b_out in ((False, kf_ref, lf_ref, bf_ref),
                                            (True, kb_ref, lb_ref, bb_ref)):
            k = k_ref[rows, :]
            b = chunk_cumsum(l_ref[rows, :], reverse)
            b_out[rows, :] = b
            for d in range(CHUNK):
                if d == 0:
                    z = q * k
                    vs = v
                else:
                    sh = rb - d if reverse else d
                    keep = (row + d < CHUNK) if reverse else (row >= d)
                    bs = pltpu.roll(b, sh, 0)
                    ks = pltpu.roll(k, sh, 0)
                    vs = pltpu.roll(v, sh, 0)
                    dec = jnp.exp(jnp.where(keep, b - bs, -jnp.inf))
                    z = q * ks * dec
                acc = acc + _dot(z.astype(BF16), head_ones) * vs
        acc_ref[rows, :] = acc
        return carry

    lax.fori_loop(0, seq // rb, intra, 0)

    def inter(n, carry):
        wf, wb = carry
        rf = pl.multiple_of(n * CHUNK, CHUNK)
        rr = pl.multiple_of((n_chunks - 1 - n) * CHUNK, CHUNK)

        def one(state, r0, k_ref, b_ref, edge):
            rows = pl.ds(r0, CHUNK)
            b = b_ref[rows, :]
            b_edge = b_ref[pl.ds(r0 + edge, 1), :]
            qd = q_ref[rows, :] * jnp.exp(b)
            o = _nt_dot(qd.astype(BF16), state.astype(BF16))
            acc_ref[rows, :] = acc_ref[rows, :] + o
            kd = k_ref[rows, :] * jnp.exp(b_edge - b)
            upd = _tn_dot(v_ref[rows, :].astype(BF16), kd.astype(BF16))
            return state * jnp.exp(b_edge) + jnp.where(same_head, upd, 0.0)

        wf = one(wf, rf, kf_ref, bf_ref, CHUNK - 1)
        wb = one(wb, rr, kb_ref, bb_ref, 0)
        return wf, wb

    z = jnp.zeros((w, w), F32)
    lax.fori_loop(0, n_chunks, inter, (z, z))

    def finish(blk, carry):
        rows = pl.ds(pl.multiple_of(blk * rb, rb), rb)
        o = acc_ref[rows, :]
        lane = lax.broadcasted_iota(I32, (rb, w), 1)
        lo = lane < DH
        sq = o * o
        ms_lo = jnp.sum(jnp.where(lo, sq, 0.0), axis=-1, keepdims=True)
        ms_hi = jnp.sum(jnp.where(lo, 0.0, sq), axis=-1, keepdims=True)
        ms = jnp.where(lo, ms_lo, ms_hi) * (1.0 / DH)
        o = o * lax.rsqrt(ms + NORM_EPS) * nw_ref[...] * g_ref[rows, :]
        o_ref[rows, :] = o.astype(o_ref.dtype)
        return carry

    lax.fori_loop(0, seq // rb, finish, 0)


def _hgrn2(uh, nw, batch, seq):
    w = 2 * DH
    npair = D_HGRN // w
    kern = functools.partial(_hgrn2_kernel, seq=seq, rb=128)

    def sec(s):
        return pl.BlockSpec((seq, w), lambda b, p, s=s: (b, s * npair + p))

    return pl.pallas_call(
        kern,
        grid=(batch, npair),
        in_specs=[sec(0), sec(1), sec(2), sec(3), sec(4), sec(5), sec(6),
                  pl.BlockSpec((1, w), lambda b, p: (0, p))],
        out_specs=pl.BlockSpec((seq, w), lambda b, p: (b, p)),
        out_shape=jax.ShapeDtypeStruct((batch * seq, D_HGRN), BF16),
        scratch_shapes=[pltpu.VMEM((seq, w), F32)] * 3,
        compiler_params=pltpu.CompilerParams(
            dimension_semantics=("arbitrary", "arbitrary"), vmem_limit_bytes=VMEM_LIMIT),
        name="hgrn2",
    )(uh, uh, uh, uh, uh, uh, uh, nw)


def _post_mixer_kernel(x_ref, od_ref, oh_ref, om_ref, wo_ref, g_ref, b_ref, rw_ref,
                       rb_ref, h_ref, eidx_ref, gate_ref, rank_ref, cnt_ref, carry_ref,
                       *, tm, alpha):
    i = pl.program_id(0)

    @pl.when(i == 0)
    def _():
        carry_ref[...] = jnp.zeros_like(carry_ref)

    mix = (_dot(od_ref[...], wo_ref[0:D_DIFF, :])
           + _dot(oh_ref[...], wo_ref[D_DIFF:D_DIFF + D_HGRN, :])
           + _dot(om_ref[...], wo_ref[D_DIFF + D_HGRN:, :]))
    h = _layer_norm(alpha * x_ref[...] + mix, g_ref[...], b_ref[...])
    h_ref[...] = h

    h_hi = h.astype(BF16)
    h_lo = (h - h_hi.astype(F32)).astype(BF16)
    rw = rw_ref[...]
    rw_hi = rw.astype(BF16)
    rw_lo = (rw - rw_hi.astype(F32)).astype(BF16)
    logits = _dot(h_hi, rw_hi) + _dot(h_lo, rw_hi) + _dot(h_hi, rw_lo) + rb_ref[...]

    lane = lax.broadcasted_iota(I32, (tm, N_EXPERTS), 1).astype(F32)
    work = logits
    sels, vals, idxs = [], [], []
    for _ in range(TOP_K):
        m = jnp.max(work, axis=-1, keepdims=True)
        idx = jnp.min(jnp.where(work == m, lane, float(N_EXPERTS)), axis=-1, keepdims=True)
        sel = lane == idx
        work = jnp.where(sel, -jnp.inf, work)
        sels.append(sel)
        vals.append(m)
        idxs.append(idx)
    es = [jnp.exp(v - vals[0]) for v in vals]
    den = es[0] + es[1] + es[2] + es[3]

    chosen = jnp.where(sels[0] | sels[1] | sels[2] | sels[3], 1.0, 0.0)
    ti = lax.broadcasted_iota(I32, (tm, tm), 0)
    tj = lax.broadcasted_iota(I32, (tm, tm), 1)
    earlier = jnp.where(tj < ti, 1.0, 0.0).astype(BF16)
    prefix = _dot(earlier, chosen.astype(BF16)) + carry_ref[0:1, 0:N_EXPERTS]

    col = lax.broadcasted_iota(I32, (tm, 8), 1)
    eidx = jnp.zeros((tm, 8), F32)
    gate = jnp.zeros((tm, 8), F32)
    rank = jnp.zeros((tm, 8), F32)
    for k in range(TOP_K):
        rk = jnp.sum(jnp.where(sels[k], prefix, 0.0), axis=-1, keepdims=True)
        eidx = jnp.where(col == k, idxs[k], eidx)
        gate = jnp.where(col == k, es[k] / den, gate)
        rank = jnp.where(col == k, rk, rank)
    eidx_ref[...] = eidx.astype(I32)
    gate_ref[...] = gate
    rank_ref[...] = rank.astype(I32)

    total = carry_ref[0:1, 0:N_EXPERTS] + jnp.sum(chosen, axis=0, keepdims=True)
    carry_ref[0:1, 0:N_EXPERTS] = total
    cnt_ref[...] = carry_ref[...]


def _post_mixer(x2d, od, oh, om, wo_bf, g, b, rw, rb, alpha):
    t = x2d.shape[0]
    tm = 256
    kern = functools.partial(_post_mixer_kernel, tm=tm, alpha=alpha)
    full = lambda i: (0, 0)
    tok = lambda i: (i, 0)
    return pl.pallas_call(
        kern,
        grid=(t // tm,),
        in_specs=[
            pl.BlockSpec((tm, D_MODEL), tok),
            pl.BlockSpec((tm, D_DIFF), tok),
            pl.BlockSpec((tm, D_HGRN), tok),
            pl.BlockSpec((tm, D_XMEM), tok),
            pl.BlockSpec((D_MODEL, D_MODEL), full),
            pl.BlockSpec((1, D_MODEL), full),
            pl.BlockSpec((1, D_MODEL), full),
            pl.BlockSpec((D_MODEL, N_EXPERTS), full),
            pl.BlockSpec((1, N_EXPERTS), full),
        ],
        out_specs=[
            pl.BlockSpec((tm, D_MODEL), tok),
            pl.BlockSpec((tm, 8), tok),
            pl.BlockSpec((tm, 8), tok),
            pl.BlockSpec((tm, 8), tok),
            pl.BlockSpec((8, LANES), full),
        ],
        out_shape=[
            jax.ShapeDtypeStruct((t, D_MODEL), F32),
            jax.ShapeDtypeStruct((t, 8), I32),
            jax.ShapeDtypeStruct((t, 8), F32),
            jax.ShapeDtypeStruct((t, 8), I32),
            jax.ShapeDtypeStruct((8, LANES), F32),
        ],
        scratch_shapes=[pltpu.VMEM((8, LANES), F32)],
        compiler_params=pltpu.CompilerParams(
            dimension_semantics=("arbitrary",), vmem_limit_bytes=VMEM_LIMIT),
        name="post_mixer",
    )(x2d, od, oh, om, wo_bf, g, b, rw, rb)


def _route_meta_kernel(eidx_ref, rank_ref, cnt_ref, dest_ref, be_ref, *, tm, n_blocks):
    cnt = cnt_ref[0:1, :]
    blocks = jnp.floor((cnt + (MOE_BLOCK - 1)) * (1.0 / MOE_BLOCK))
    ei = lax.broadcasted_iota(I32, (LANES, LANES), 0)
    ej = lax.broadcasted_iota(I32, (LANES, LANES), 1)
    before = jnp.where(ei < ej, 1.0, 0.0).astype(BF16)
    b8 = jnp.broadcast_to(blocks, (8, LANES)).astype(BF16)
    pstart = _dot(b8, before)[0:1, :]
    pend = pstart + blocks

    lane = lax.broadcasted_iota(I32, (tm, LANES), 1).astype(F32)
    col = lax.broadcasted_iota(I32, (tm, 8), 1)
    eidx = eidx_ref[...].astype(F32)
    dest = jnp.zeros((tm, 8), F32)
    for k in range(TOP_K):
        ek = jnp.sum(jnp.where(col == k, eidx, 0.0), axis=-1, keepdims=True)
        start = jnp.sum(jnp.where(lane == ek, pstart, 0.0), axis=-1, keepdims=True)
        dest = jnp.where(col == k, start * float(MOE_BLOCK), dest)
    dest_ref[...] = dest.astype(I32) + rank_ref[...]

    bi = lax.broadcasted_iota(I32, (n_blocks, LANES), 0).astype(F32)
    bl = lax.broadcasted_iota(I32, (n_blocks, LANES), 1)
    done = jnp.where((bl < N_EXPERTS) & (pend <= bi), 1.0, 0.0)
    be = jnp.minimum(jnp.sum(done, axis=-1, keepdims=True), float(N_EXPERTS - 1))
    used = jnp.sum(jnp.where(bl[0:1, :] == N_EXPERTS - 1, pend, 0.0), axis=-1, keepdims=True)
    bcol = lax.broadcasted_iota(I32, (n_blocks, 8), 1)
    be_ref[...] = jnp.where(bcol == 0, be, jnp.where(bcol == 1, used, 0.0)).astype(I32)


def _route_meta(eidx, rank, cnt, n_blocks):
    t = eidx.shape[0]
    tm = 2048
    kern = functools.partial(_route_meta_kernel, tm=tm, n_blocks=n_blocks)
    return pl.pallas_call(
        kern,
        grid=(t // tm,),
        in_specs=[
            pl.BlockSpec((tm, 8), lambda i: (i, 0)),
            pl.BlockSpec((tm, 8), lambda i: (i, 0)),
            pl.BlockSpec((8, LANES), lambda i: (0, 0)),
        ],
        out_specs=[
            pl.BlockSpec((tm, 8), lambda i: (i, 0)),
            pl.BlockSpec((n_blocks, 8), lambda i: (0, 0)),
        ],
        out_shape=[
            jax.ShapeDtypeStruct((t, 8), I32),
            jax.ShapeDtypeStruct((n_blocks, 8), I32),
        ],
        compiler_params=pltpu.CompilerParams(
            dimension_semantics=("arbitrary",), vmem_limit_bytes=VMEM_LIMIT),
        name="route_meta",
    )(eidx, rank, cnt)


def _dispatch_kernel(dest_ref, h_ref, xs_in_ref, xs_ref, sem, *, tm):
    del xs_in_ref
    base = pl.program_id(0) * (tm * TOP_K)

    def issue(t, carry):
        for k in range(TOP_K):
            d = dest_ref[base + t * TOP_K + k]
            pltpu.make_async_copy(h_ref.at[pl.ds(t, 1), :], xs_ref.at[pl.ds(d, 1), :],
                                  sem).start()
        return carry

    lax.fori_loop(0, tm, issue, 0)

    def drain(t, carry):
        for k in range(TOP_K):
            pltpu.make_async_copy(h_ref.at[pl.ds(0, 1), :], xs_ref.at[pl.ds(0, 1), :],
                                  sem).wait()
        return carry

    lax.fori_loop(0, tm, drain, 0)


def _dispatch(dest_flat, h, xs_zero):
    t = h.shape[0]
    tm = 256
    kern = functools.partial(_dispatch_kernel, tm=tm)
    return pl.pallas_call(
        kern,
        grid_spec=pltpu.PrefetchScalarGridSpec(
            num_scalar_prefetch=1,
            grid=(t // tm,),
            in_specs=[
                pl.BlockSpec((tm, D_MODEL), lambda i, d: (i, 0)),
                pl.BlockSpec(memory_space=pl.ANY),
            ],
            out_specs=pl.BlockSpec(memory_space=pl.ANY),
            scratch_shapes=[pltpu.SemaphoreType.DMA(())],
        ),
        out_shape=jax.ShapeDtypeStruct(xs_zero.shape, xs_zero.dtype),
        input_output_aliases={2: 0},
        compiler_params=pltpu.CompilerParams(
            dimension_semantics=("arbitrary",), vmem_limit_bytes=VMEM_LIMIT),
        name="dispatch",
    )(dest_flat, h, xs_zero)


def _experts_kernel(be_ref, used_ref, xs_ref, wgu_ref, wdn_ref, bg_ref, bl_ref, bd_ref,
                    perm_ref, ys_ref, wg_ref, wl_ref, wd_ref):
    i = pl.program_id(0)
    prev = be_ref[jnp.maximum(i - 1, 0)]
    fresh = jnp.logical_or(i == 0, be_ref[i] != prev)

    @pl.when(fresh)
    def _():
        perm = perm_ref[...]
        half = LANES
        for c in range(2 * D_EXPERT // (2 * half)):
            slab = wgu_ref[0, :, c * 2 * half:(c + 1) * 2 * half].astype(BF16)
            sep = _dot(slab, perm)
            wg_ref[:, c * half:(c + 1) * half] = sep[:, 0:half].astype(BF16)
            wl_ref[:, c * half:(c + 1) * half] = sep[:, half:2 * half].astype(BF16)
        wd_ref[...] = wdn_ref[0].astype(BF16)

    @pl.when(i < used_ref[0])
    def _():
        xb = xs_ref[...].astype(BF16)
        glu = jnp.minimum(_dot(xb, wg_ref[...]) + bg_ref[0], SWIGLU_LIMIT)
        lin = jnp.clip(_dot(xb, wl_ref[...]) + bl_ref[0], -SWIGLU_LIMIT, SWIGLU_LIMIT)
        act = glu * _sigmoid(SWIGLU_ALPHA * glu) * (lin + 1.0)
        ys_ref[...] = _dot(act.astype(BF16), wd_ref[...]) + bd_ref[0]

    @pl.when(i >= used_ref[0])
    def _():
        ys_ref[...] = jnp.zeros_like(ys_ref)


def _experts(block_e, used, xs, w_gu, w_dn, b_g, b_l, b_d, perm):
    n_rows = xs.shape[0]
    n_blocks = n_rows // MOE_BLOCK
    wmap = lambda i, be, u: (be[i], 0, 0)
    return pl.pallas_call(
        _experts_kernel,
        grid_spec=pltpu.PrefetchScalarGridSpec(
            num_scalar_prefetch=2,
            grid=(n_blocks,),
            in_specs=[
                pl.BlockSpec((MOE_BLOCK, D_MODEL), lambda i, be, u: (i, 0)),
                pl.BlockSpec((1, D_MODEL, 2 * D_EXPERT), wmap),
                pl.BlockSpec((1, D_EXPERT, D_MODEL), wmap),
                pl.BlockSpec((1, 1, D_EXPERT), wmap),
                pl.BlockSpec((1, 1, D_EXPERT), wmap),
                pl.BlockSpec((1, 1, D_MODEL), wmap),
                pl.BlockSpec((2 * LANES, 2 * LANES), lambda i, be, u: (0, 0)),
            ],
            out_specs=pl.BlockSpec((MOE_BLOCK, D_MODEL), lambda i, be, u: (i, 0)),
            scratch_shapes=[
                pltpu.VMEM((D_MODEL, D_EXPERT), BF16),
                pltpu.VMEM((D_MODEL, D_EXPERT), BF16),
                pltpu.VMEM((D_EXPERT, D_MODEL), BF16),
            ],
        ),
        out_shape=jax.ShapeDtypeStruct((n_rows, D_MODEL), F32),
        compiler_params=pltpu.CompilerParams(
            dimension_semantics=("arbitrary",), vmem_limit_bytes=VMEM_LIMIT),
        name="experts",
    )(block_e, used, xs, w_gu, w_dn, b_g, b_l, b_d, perm)


def _combine_kernel(dest_ref, h_ref, gate_ref, g_ref, b_ref, ys_ref, o_ref, buf_ref, sem,
                    *, tm, alpha):
    i = pl.program_id(0)
    n = pl.num_programs(0)

    def issue(tile, slot):
        base = tile * (tm * TOP_K)

        def body(t, carry):
            for k in range(TOP_K):
                d = dest_ref[base + t * TOP_K + k]
                pltpu.make_async_copy(ys_ref.at[pl.ds(d, 1), :],
                                      buf_ref.at[slot, k, pl.ds(t, 1), :],
                                      sem.at[slot]).start()
            return carry

        lax.fori_loop(0, tm, body, 0)

    @pl.when(i == 0)
    def _():
        issue(0, 0)

    @pl.when(i + 1 < n)
    def _():
        issue(i + 1, (i + 1) % 2)

    slot = i % 2

    def drain(t, carry):
        for k in range(TOP_K):
            pltpu.make_async_copy(ys_ref.at[pl.ds(0, 1), :],
                                  buf_ref.at[slot, k, pl.ds(0, 1), :],
                                  sem.at[slot]).wait()
        return carry

    lax.fori_loop(0, tm, drain, 0)

    gate = gate_ref[...]
    col = lax.broadcasted_iota(I32, (tm, 8), 1)
    ffn = jnp.zeros((tm, D_MODEL), F32)
    for k in range(TOP_K):
        gk = jnp.sum(jnp.where(col == k, gate, 0.0), axis=-1, keepdims=True)
        ffn = ffn + gk * buf_ref[slot, k]
    o_ref[...] = _layer_norm(alpha * h_ref[...] + ffn, g_ref[...], b_ref[...])


def _combine(dest_flat, h, gate, g, b, ys, alpha):
    t = h.shape[0]
    tm = 256
    kern = functools.partial(_combine_kernel, tm=tm, alpha=alpha)
    return pl.pallas_call(
        kern,
        grid_spec=pltpu.PrefetchScalarGridSpec(
            num_scalar_prefetch=1,
            grid=(t // tm,),
            in_specs=[
                pl.BlockSpec((tm, D_MODEL), lambda i, d: (i, 0)),
                pl.BlockSpec((tm, 8), lambda i, d: (i, 0)),
                pl.BlockSpec((1, D_MODEL), lambda i, d: (0, 0)),
                pl.BlockSpec((1, D_MODEL), lambda i, d: (0, 0)),
                pl.BlockSpec(memory_space=pl.ANY),
            ],
            out_specs=pl.BlockSpec((tm, D_MODEL), lambda i, d: (i, 0)),
            scratch_shapes=[
                pltpu.VMEM((2, TOP_K, tm, D_MODEL), F32),
                pltpu.SemaphoreType.DMA((2,)),
            ],
        ),
        out_shape=jax.ShapeDtypeStruct((t, D_MODEL), F32),
        compiler_params=pltpu.CompilerParams(
            dimension_semantics=("arbitrary",), vmem_limit_bytes=VMEM_LIMIT),
        name="combine",
    )(dest_flat, h, gate, g, b, ys)


def _column_split_permutation():
    n = 2 * LANES
    p = np.zeros((n, n), np.float32)
    j = np.arange(n)
    p[j, j // 2 + LANES * (j % 2)] = 1.0
    return jnp.asarray(p, dtype=BF16)


def kernel(x, mem, w_in, lam_q1, lam_k1, lam_q2, lam_k2, diff_norm_w, hgrn_lb_fwd,
           hgrn_lb_bwd, hgrn_norm_w, w_mem_kv, w_o, ln1_g, ln1_b, router_w, router_b,
           w_gate_up, b_gate_up, w_down, b_down, ln2_g, ln2_b):
    batch, seq, d = x.shape
    mlen = mem.shape[1]
    assert d == D_MODEL and w_in.shape == (N_LAYERS, D_MODEL, D_IN_PROJ)
    assert hgrn_lb_fwd.shape == (N_LAYERS + 1, D_HGRN)
    t = batch * seq
    alpha = (2.0 * N_LAYERS) ** 0.25
    lam_init = 0.8 - 0.6 * math.exp(-0.3 * 0)
    slopes = jnp.asarray(2.0 ** (-8.0 * np.arange(1, N_HEADS + 1) / N_HEADS), dtype=F32)

    x2d = x.reshape(t, D_MODEL)
    ua, uh, um = _in_proj(x2d, w_in[0].astype(BF16), hgrn_lb_fwd, hgrn_lb_bwd)
    o_diff = _diff_attn(ua, slopes, diff_norm_w, lam_q1, lam_k1, lam_q2, lam_k2,
                        batch, seq, lam_init)
    o_mem = _mem_attn(um, mem.reshape(batch * mlen, D_MODEL), w_mem_kv[0].astype(BF16),
                      batch, seq, mlen)
    o_hgrn = _hgrn2(uh, hgrn_norm_w, batch, seq)

    h1, eidx, gate, rank, cnt = _post_mixer(
        x2d, o_diff, o_hgrn, o_mem, w_o[0].astype(BF16), ln1_g, ln1_b, router_w[0],
        router_b, alpha)

    n_blocks = -(-(t * TOP_K) // MOE_BLOCK) + N_EXPERTS
    dest, meta = _route_meta(eidx, rank, cnt, n_blocks)
    dest_flat = dest[:, 0:TOP_K].reshape(t * TOP_K)
    block_e = meta[:, 0]
    used = meta[0:1, 1]

    xs = _dispatch(dest_flat, h1, jnp.zeros((n_blocks * MOE_BLOCK, D_MODEL), F32))
    b_g = b_gate_up[0][:, 0::2].reshape(N_EXPERTS, 1, D_EXPERT)
    b_l = b_gate_up[0][:, 1::2].reshape(N_EXPERTS, 1, D_EXPERT)
    ys = _experts(block_e, used, xs, w_gate_up[0], w_down[0], b_g, b_l,
                  b_down[0].reshape(N_EXPERTS, 1, D_MODEL), _column_split_permutation())
    out = _combine(dest_flat, h1, gate, ln2_g, ln2_b, ys, alpha)
    return out.reshape(batch, seq, D_MODEL)
```

```python
import functools
import math

import jax
import jax.numpy as jnp
import numpy as np
from jax import lax
from jax.experimental import pallas as pl
from jax.experimental.pallas import tpu as pltpu

F32 = jnp.float32
BF16 = jnp.bfloat16
I32 = jnp.int32

D_MODEL = 1024
N_LAYERS = 1
D_DIFF = 512
D_HGRN = 256
D_XMEM = 256
N_HEADS = 4
DH = 64
DH_SHIFT = 6
D_IN_PROJ = 3 * D_DIFF + 5 * D_HGRN + D_XMEM
CHUNK = 16
N_EXPERTS = 32
TOP_K = 4
D_EXPERT = 1024
MOE_BLOCK = 256
SWIGLU_ALPHA = 1.702
SWIGLU_LIMIT = 7.0
NORM_EPS = 1e-5
LOG2E = math.log2(math.e)
LANES = 128
VMEM_LIMIT = 56 * 1024 * 1024


def _nt_dot(a, b):
    return lax.dot_general(a, b, (((1,), (1,)), ((), ())), preferred_element_type=F32)


def _tn_dot(a, b):
    return lax.dot_general(a, b, (((0,), (0,)), ((), ())), preferred_element_type=F32)


def _dot(a, b):
    return jnp.dot(a, b, preferred_element_type=F32)


def _sigmoid(x):
    return 1.0 / (1.0 + jnp.exp(-x))


def _layer_norm(y, g, b):
    mu = jnp.mean(y, axis=-1, keepdims=True)
    yc = y - mu
    var = jnp.mean(yc * yc, axis=-1, keepdims=True)
    return yc * lax.rsqrt(var + NORM_EPS) * g + b


def _in_proj_kernel(x_ref, w_ref, lbf_ref, lbb_ref, ua_ref, uh_ref, um_ref):
    xb = x_ref[...].astype(BF16)

    def proj(c0, width):
        return _dot(xb, w_ref[:, c0:c0 + width])

    ua_ref[:, 0:D_DIFF] = (proj(0, D_DIFF) * (LOG2E / math.sqrt(DH))).astype(BF16)
    ua_ref[:, D_DIFF:2 * D_DIFF] = proj(D_DIFF, D_DIFF).astype(BF16)
    ua_ref[:, 2 * D_DIFF:3 * D_DIFF] = proj(2 * D_DIFF, D_DIFF).astype(BF16)

    def lower_bound(lb_ref):
        a = lb_ref[0:1, :]
        b = lb_ref[1:2, :]
        m = jnp.maximum(a, b)
        ea = jnp.exp(a - m)
        eb = jnp.exp(b - m)
        return ea / (ea + eb)

    base = 3 * D_DIFF
    hq = proj(base, D_HGRN)
    uh_ref[:, 0:D_HGRN] = hq * _sigmoid(hq)
    for d, lb_ref in enumerate((lbf_ref, lbb_ref)):
        lb = lower_bound(lb_ref)
        f = lb + (1.0 - lb) * _sigmoid(proj(base + (1 + d) * D_HGRN, D_HGRN))
        uh_ref[:, (1 + 2 * d) * D_HGRN:(2 + 2 * d) * D_HGRN] = 1.0 - f
        uh_ref[:, (2 + 2 * d) * D_HGRN:(3 + 2 * d) * D_HGRN] = jnp.log(f)
    uh_ref[:, 5 * D_HGRN:6 * D_HGRN] = proj(base + 3 * D_HGRN, D_HGRN)
    uh_ref[:, 6 * D_HGRN:7 * D_HGRN] = _sigmoid(proj(base + 4 * D_HGRN, D_HGRN))
    um_ref[...] = proj(base + 5 * D_HGRN, D_XMEM).astype(BF16)


def _in_proj(x2d, w_in_bf, lb_f, lb_b):
    t = x2d.shape[0]
    tm = 256
    return pl.pallas_call(
        _in_proj_kernel,
        grid=(t // tm,),
        in_specs=[
            pl.BlockSpec((tm, D_MODEL), lambda i: (i, 0)),
            pl.BlockSpec((D_MODEL, D_IN_PROJ), lambda i: (0, 0)),
            pl.BlockSpec((N_LAYERS + 1, D_HGRN), lambda i: (0, 0)),
            pl.BlockSpec((N_LAYERS + 1, D_HGRN), lambda i: (0, 0)),
        ],
        out_specs=[
            pl.BlockSpec((tm, 3 * D_DIFF), lambda i: (i, 0)),
            pl.BlockSpec((tm, 7 * D_HGRN), lambda i: (i, 0)),
            pl.BlockSpec((tm, D_XMEM), lambda i: (i, 0)),
        ],
        out_shape=[
            jax.ShapeDtypeStruct((t, 3 * D_DIFF), BF16),
            jax.ShapeDtypeStruct((t, 7 * D_HGRN), F32),
            jax.ShapeDtypeStruct((t, D_XMEM), BF16),
        ],
        compiler_params=pltpu.CompilerParams(
            dimension_semantics=("arbitrary",), vmem_limit_bytes=VMEM_LIMIT),
        name="in_proj",
    )(x2d, w_in_bf, lb_f, lb_b)


N_POS_FEATURES = 12


def _alibi_feature_rows(c3, base):
    f = lax.broadcasted_iota(I32, (1, 2 * DH), 1) - base

    def pieces(f0):
        return jnp.where(f == f0, c3[0], jnp.where(f == f0 + 1, c3[1],
                                                   jnp.where(f == f0 + 2, c3[2], 0.0)))

    span = lambda lo: jnp.where((f >= lo) & (f < lo + 3), 1.0, 0.0)
    q_const = 64.0 * pieces(6) + pieces(9)
    k_const = -64.0 * pieces(0) - pieces(3)
    return (q_const, span(0), span(3)), (k_const, span(6), span(9))


def _position_features(rows, pos0, n):
    const, hi_mask, lo_mask = rows
    pos = pos0 + lax.broadcasted_iota(I32, (n, 2 * DH), 0)
    hi = (pos >> DH_SHIFT).astype(F32)
    lo = (pos & (DH - 1)).astype(F32)
    return const + hi_mask * hi + lo_mask * lo


def _diff_attn_kernel(c_ref, q_ref, k_ref, v_ref, nw_ref, lq1_ref, lk1_ref, lq2_ref,
                      lk2_ref, o_ref, ka1_ref, ka2_ref, vt_ref, corr_ref, acc1_ref, acc2_ref,
                      m1_ref, m2_ref, l1_ref, l2_ref, *, t, qs, seq, lam_init):
    h = pl.program_id(1)
    i = pl.program_id(2)
    nk = seq // t
    c3 = (c_ref[h, 0], c_ref[h, 1], c_ref[h, 2])
    lane = lax.broadcasted_iota(I32, (t, 2 * DH), 1)
    first_half = lane < DH
    q_rows1, k_rows1 = _alibi_feature_rows(c3, DH)
    q_rows2, k_rows2 = _alibi_feature_rows(c3, 0)

    @pl.when(i == 0)
    def _():
        def build(r, carry):
            r0 = pl.multiple_of(r * t, t)
            rows = pl.ds(r0, t)
            kblk = k_ref[rows, :]
            ka1_ref[rows, :] = jnp.where(first_half, kblk,
                                         _position_features(k_rows1, r0, t).astype(BF16))
            ka2_ref[rows, :] = jnp.where(first_half,
                                         _position_features(k_rows2, r0, t).astype(BF16), kblk)
            vt_ref[r] = v_ref[rows, :].astype(F32).T.astype(BF16)
            return carry

        lax.fori_loop(0, nk, build, 0)
        kk = lax.broadcasted_iota(I32, (t, t), 0)
        qq = lax.broadcasted_iota(I32, (t, t), 1)
        c = c3[0] + c3[1] + c3[2]
        corr_ref[...] = (2.0 * c) * jnp.minimum(qq - kk, 0).astype(F32)

    q = q_ref[...]
    qf1 = _position_features(q_rows1, i * t, t)
    qf2 = _position_features(q_rows2, i * t, t)
    qa1_before = jnp.where(first_half, q, qf1.astype(BF16))
    qa1_after = jnp.where(first_half, q, (-qf1).astype(BF16))
    qa2_before = jnp.where(first_half, qf2.astype(BF16), q)
    qa2_after = jnp.where(first_half, (-qf2).astype(BF16), q)

    m1_ref[...] = jnp.full(m1_ref.shape, -jnp.inf, F32)
    m2_ref[...] = jnp.full(m2_ref.shape, -jnp.inf, F32)
    l1_ref[...] = jnp.zeros(l1_ref.shape, F32)
    l2_ref[...] = jnp.zeros(l2_ref.shape, F32)
    acc1_ref[...] = jnp.zeros(acc1_ref.shape, F32)
    acc2_ref[...] = jnp.zeros(acc2_ref.shape, F32)

    def chunk(j, qa1, qa2, diagonal):
        rows = pl.ds(pl.multiple_of(j * t, t), t)
        vt = vt_ref[j]
        for qa, ka_ref, acc_ref, m_ref, l_ref in ((qa1, ka1_ref, acc1_ref, m1_ref, l1_ref),
                                                  (qa2, ka2_ref, acc2_ref, m2_ref, l2_ref)):
            ka = ka_ref[rows, :]
            for u in range(t // qs):
                cols = slice(u * qs, (u + 1) * qs)
                s = _nt_dot(ka, qa[cols, :])
                if diagonal:
                    s = s + corr_ref[:, cols]
                m_old = m_ref[:, cols]
                m_new = jnp.maximum(m_old, jnp.max(s, axis=0, keepdims=True))
                p = jnp.exp2(s - m_new)
                r = jnp.exp2(m_old - m_new)
                l_ref[:, cols] = r * l_ref[:, cols] + jnp.sum(p, axis=0, keepdims=True)
                acc_ref[:, cols] = r * acc_ref[:, cols] + _dot(vt, p.astype(BF16))
                m_ref[:, cols] = m_new

    def before(j, carry):
        chunk(j, qa1_before, qa2_before, False)
        return carry

    def after(j, carry):
        chunk(j, qa1_after, qa2_after, False)
        return carry

    lax.fori_loop(0, i, before, 0)
    chunk(i, qa1_before, qa2_before, True)
    lax.fori_loop(i + 1, nk, after, 0)

    lam = (jnp.exp(jnp.sum(lq1_ref[...] * lk1_ref[...], axis=-1, keepdims=True))
           - jnp.exp(jnp.sum(lq2_ref[...] * lk2_ref[...], axis=-1, keepdims=True))
           + lam_init)
    o = acc1_ref[...] / l1_ref[0:1, :] - lam * (acc2_ref[...] / l2_ref[0:1, :])
    o = o * lax.rsqrt(jnp.mean(o * o, axis=0, keepdims=True) + NORM_EPS)
    o_ref[...] = (o.T * nw_ref[...] * (1.0 - lam_init)).astype(o_ref.dtype)


def _bf16_pieces(x):
    x = np.asarray(x, np.float32)
    out = []
    for _ in range(3):
        p = x.astype(BF16).astype(np.float32)
        out.append(p)
        x = x - p
    return np.stack(out, axis=-1)


def _diff_attn(ua, nw, lq1, lk1, lq2, lk2, batch, seq, lam_init):
    t = 512
    nq = seq // t
    slopes = 2.0 ** (-8.0 * np.arange(1, N_HEADS + 1) / N_HEADS)
    c_pieces = jnp.asarray(_bf16_pieces(slopes.astype(np.float32) * np.float32(LOG2E)))
    kern = functools.partial(_diff_attn_kernel, t=t, qs=LANES, seq=seq, lam_init=lam_init)
    small = lambda b, h, i, s: (0, 0)
    return pl.pallas_call(
        kern,
        grid_spec=pltpu.PrefetchScalarGridSpec(
            num_scalar_prefetch=1,
            grid=(batch, N_HEADS, nq),
            in_specs=[
                pl.BlockSpec((t, 2 * DH), lambda b, h, i, s: (b * nq + i, h)),
                pl.BlockSpec((seq, 2 * DH), lambda b, h, i, s: (b, N_HEADS + h)),
                pl.BlockSpec((seq, 2 * DH), lambda b, h, i, s: (b, 2 * N_HEADS + h)),
                pl.BlockSpec((1, 2 * DH), small),
                pl.BlockSpec((1, DH), small),
                pl.BlockSpec((1, DH), small),
                pl.BlockSpec((1, DH), small),
                pl.BlockSpec((1, DH), small),
            ],
            out_specs=pl.BlockSpec((t, 2 * DH), lambda b, h, i, s: (b * nq + i, h)),
            scratch_shapes=[
                pltpu.VMEM((seq, 2 * DH), BF16),
                pltpu.VMEM((seq, 2 * DH), BF16),
                pltpu.VMEM((nq, 2 * DH, t), BF16),
                pltpu.VMEM((t, t), F32),
                pltpu.VMEM((2 * DH, t), F32),
                pltpu.VMEM((2 * DH, t), F32),
                pltpu.VMEM((1, t), F32),
                pltpu.VMEM((1, t), F32),
                pltpu.VMEM((1, t), F32),
                pltpu.VMEM((1, t), F32),
            ],
        ),
        out_shape=jax.ShapeDtypeStruct((batch * seq, D_DIFF), BF16),
        compiler_params=pltpu.CompilerParams(
            dimension_semantics=("arbitrary", "arbitrary", "arbitrary"),
            vmem_limit_bytes=VMEM_LIMIT),
        name="diff_attn",
    )(c_pieces, ua, ua, ua, nw, lq1, lk1, lq2, lk2)


def _mem_attn_kernel(q_ref, mem_ref, wkv_ref, o_ref, kv_ref, *, tq, mlen):
    @pl.when(pl.program_id(1) == 0)
    def _():
        kv_ref[...] = _dot(mem_ref[...].astype(BF16), wkv_ref[...]).astype(BF16)

    q = q_ref[...]
    mk = kv_ref[:, 0:D_XMEM]
    mv = kv_ref[:, D_XMEM:2 * D_XMEM]
    qhead = lax.broadcasted_iota(I32, (tq, D_XMEM), 1) >> DH_SHIFT
    vhead = lax.broadcasted_iota(I32, (mlen, D_XMEM), 1) >> DH_SHIFT
    acc = jnp.zeros((tq, D_XMEM), F32)
    for h in range(N_HEADS):
        qh = jnp.where(qhead == h, q, jnp.zeros_like(q))
        s = _nt_dot(qh, mk) * (1.0 / math.sqrt(DH))
        e = jnp.exp(s - jnp.max(s, axis=-1, keepdims=True))
        p = e / jnp.sum(e, axis=-1, keepdims=True)
        vh = jnp.where(vhead == h, mv, jnp.zeros_like(mv))
        acc = acc + _dot(p.astype(BF16), vh)
    o_ref[...] = acc.astype(o_ref.dtype)


def _mem_attn(um, mem2d, wkv_bf, batch, seq, mlen):
    tq = 512
    nq = seq // tq
    kern = functools.partial(_mem_attn_kernel, tq=tq, mlen=mlen)
    return pl.pallas_call(
        kern,
        grid=(batch, nq),
        in_specs=[
            pl.BlockSpec((tq, D_XMEM), lambda b, i: (b * nq + i, 0)),
            pl.BlockSpec((mlen, D_MODEL), lambda b, i: (b, 0)),
            pl.BlockSpec((D_MODEL, 2 * D_XMEM), lambda b, i: (0, 0)),
        ],
        out_specs=pl.BlockSpec((tq, D_XMEM), lambda b, i: (b * nq + i, 0)),
        out_shape=jax.ShapeDtypeStruct((batch * seq, D_XMEM), BF16),
        scratch_shapes=[pltpu.VMEM((mlen, 2 * D_XMEM), BF16)],
        compiler_params=pltpu.CompilerParams(
            dimension_semantics=("arbitrary", "arbitrary"), vmem_limit_bytes=VMEM_LIMIT),
        name="mem_attn",
    )(um, mem2d, wkv_bf)


def _hgrn2_kernel(q_ref, kf_ref, lf_ref, kb_ref, lb_ref, v_ref, g_ref, nw_ref, o_ref,
                  kfp_ref, kbp_ref, vp_ref, bf_ref, bb_ref, acc_ref, cross_ref, *, seq, rb):
    n_chunks = seq // CHUNK
    w = 2 * DH
    row = lax.broadcasted_iota(I32, (rb, w), 0) & (CHUNK - 1)
    li = lax.broadcasted_iota(I32, (w, w), 0) >> DH_SHIFT
    lj = lax.broadcasted_iota(I32, (w, w), 1) >> DH_SHIFT
    same_head = li == lj
    head_ones = jnp.where(same_head, 1.0, 0.0).astype(BF16)

    def chunk_cumsum(x, reverse):
        for sh in (1, 2, 4, 8):
            if reverse:
                moved = pltpu.roll(x, rb - sh, 0)
                keep = row + sh < CHUNK
            else:
                moved = pltpu.roll(x, sh, 0)
                keep = row >= sh
            x = x + jnp.where(keep, moved, 0.0)
        return x

    halo = jnp.zeros((CHUNK, w), F32)
    for ref in (kfp_ref, kbp_ref, vp_ref, bf_ref, bb_ref):
        ref[0:CHUNK, :] = halo
        ref[seq + CHUNK:seq + 2 * CHUNK, :] = halo

    def prepare(blk, carry):
        r0 = pl.multiple_of(blk * rb, rb)
        rows = pl.ds(r0, rb)
        inner = pl.ds(r0 + CHUNK, rb)
        kfp_ref[inner, :] = kf_ref[rows, :]
        kbp_ref[inner, :] = kb_ref[rows, :]
        vp_ref[inner, :] = v_ref[rows, :]
        bf_ref[inner, :] = chunk_cumsum(lf_ref[rows, :] * LOG2E, False)
        bb_ref[inner, :] = chunk_cumsum(lb_ref[rows, :] * LOG2E, True)
        return carry

    lax.fori_loop(0, seq // rb, prepare, 0)

    def intra(blk):
        r0 = pl.multiple_of(blk * rb, rb)
        rows = pl.ds(r0, rb)
        q = q_ref[rows, :]
        acc = jnp.zeros((rb, w), F32)
        for reverse, kp_ref, b_ref in ((False, kfp_ref, bf_ref), (True, kbp_ref, bb_ref)):
            b = b_ref[pl.ds(r0 + CHUNK, rb), :]
            for d in range(CHUNK):
                src = pl.ds(r0 + CHUNK + (d if reverse else -d), rb)
                ks = kp_ref[src, :]
                vs = vp_ref[src, :]
                if d == 0:
                    z = q * ks
                else:
                    keep = (row + d < CHUNK) if reverse else (row >= d)
                    dec = jnp.exp2(jnp.where(keep, b - b_ref[src, :], -jnp.inf))
                    z = q * ks * dec
                acc = acc + _dot(z.astype(BF16), head_ones) * vs
        acc_ref[rows, :] = acc

    def inter(n, carry):
        wf, wb = carry
        rf = pl.multiple_of(n * CHUNK, CHUNK)
        rr = pl.multiple_of((n_chunks - 1 - n) * CHUNK, CHUNK)

        def one(state, r0, kp_ref, b_ref, edge):
            rows = pl.ds(r0, CHUNK)
            prow = pl.ds(r0 + CHUNK, CHUNK)
            b = b_ref[prow, :]
            b_edge = b_ref[pl.ds(r0 + CHUNK + edge, 1), :]
            qd = q_ref[rows, :] * jnp.exp2(b)
            cross_ref[rows, :] = cross_ref[rows, :] + _nt_dot(qd.astype(BF16),
                                                              state.astype(BF16))
            kd = kp_ref[prow, :] * jnp.exp2(b_edge - b)
            upd = _tn_dot(vp_ref[prow, :].astype(BF16), kd.astype(BF16))
            return state * jnp.exp2(b_edge) + jnp.where(same_head, upd, 0.0)

        wf = one(wf, rf, kfp_ref, bf_ref, CHUNK - 1)
        wb = one(wb, rr, kbp_ref, bb_ref, 0)
        return wf, wb

    cross_ref[...] = jnp.zeros(cross_ref.shape, F32)

    def fused(blk, carry):
        intra(blk)
        for c in range(rb // CHUNK):
            carry = inter(blk * (rb // CHUNK) + c, carry)
        return carry

    z = jnp.zeros((w, w), F32)
    lax.fori_loop(0, seq // rb, fused, (z, z))

    def finish(blk, carry):
        rows = pl.ds(pl.multiple_of(blk * rb, rb), rb)
        o = acc_ref[rows, :] + cross_ref[rows, :]
        lane = lax.broadcasted_iota(I32, (rb, w), 1)
        lo = lane < DH
        sq = o * o
        ms_lo = jnp.sum(jnp.where(lo, sq, 0.0), axis=-1, keepdims=True)
        ms_hi = jnp.sum(jnp.where(lo, 0.0, sq), axis=-1, keepdims=True)
        ms = jnp.where(lo, ms_lo, ms_hi) * (1.0 / DH)
        o = o * lax.rsqrt(ms + NORM_EPS) * nw_ref[...] * g_ref[rows, :]
        o_ref[rows, :] = o.astype(o_ref.dtype)
        return carry

    lax.fori_loop(0, seq // rb, finish, 0)


def _hgrn2(uh, nw, batch, seq):
    w = 2 * DH
    npair = D_HGRN // w
    kern = functools.partial(_hgrn2_kernel, seq=seq, rb=128)

    def sec(s):
        return pl.BlockSpec((seq, w), lambda b, p, s=s: (b, s * npair + p))

    return pl.pallas_call(
        kern,
        grid=(batch, npair),
        in_specs=[sec(0), sec(1), sec(2), sec(3), sec(4), sec(5), sec(6),
                  pl.BlockSpec((1, w), lambda b, p: (0, p))],
        out_specs=pl.BlockSpec((seq, w), lambda b, p: (b, p)),
        out_shape=jax.ShapeDtypeStruct((batch * seq, D_HGRN), BF16),
        scratch_shapes=[pltpu.VMEM((seq + 2 * CHUNK, w), F32)] * 5
        + [pltpu.VMEM((seq, w), F32)] * 2,
        compiler_params=pltpu.CompilerParams(
            dimension_semantics=("arbitrary", "arbitrary"), vmem_limit_bytes=VMEM_LIMIT),
        name="hgrn2",
    )(uh, uh, uh, uh, uh, uh, uh, nw)


def _post_mixer_kernel(x_ref, od_ref, oh_ref, om_ref, wo_ref, g_ref, b_ref, rw_ref,
                       rb_ref, h_ref, eidx_ref, gate_ref, rank_ref, cnt_ref, carry_ref,
                       *, tm, alpha):
    i = pl.program_id(0)

    @pl.when(i == 0)
    def _():
        carry_ref[...] = jnp.zeros_like(carry_ref)

    mix = (_dot(od_ref[...], wo_ref[0:D_DIFF, :])
           + _dot(oh_ref[...], wo_ref[D_DIFF:D_DIFF + D_HGRN, :])
           + _dot(om_ref[...], wo_ref[D_DIFF + D_HGRN:, :]))
    h = _layer_norm(alpha * x_ref[...] + mix, g_ref[...], b_ref[...])
    h_ref[...] = h

    h_hi = h.astype(BF16)
    h_lo = (h - h_hi.astype(F32)).astype(BF16)
    rw = rw_ref[...]
    rw_hi = rw.astype(BF16)
    rw_lo = (rw - rw_hi.astype(F32)).astype(BF16)
    logits = _dot(h_hi, rw_hi) + _dot(h_lo, rw_hi) + _dot(h_hi, rw_lo) + rb_ref[...]

    lane = lax.broadcasted_iota(I32, (tm, N_EXPERTS), 1).astype(F32)
    work = logits
    sels, vals, idxs = [], [], []
    for _ in range(TOP_K):
        m = jnp.max(work, axis=-1, keepdims=True)
        idx = jnp.min(jnp.where(work == m, lane, float(N_EXPERTS)), axis=-1, keepdims=True)
        sel = lane == idx
        work = jnp.where(sel, -jnp.inf, work)
        sels.append(sel)
        vals.append(m)
        idxs.append(idx)
    es = [jnp.exp(v - vals[0]) for v in vals]
    den = es[0] + es[1] + es[2] + es[3]

    chosen = jnp.where(sels[0] | sels[1] | sels[2] | sels[3], 1.0, 0.0)
    ti = lax.broadcasted_iota(I32, (tm, tm), 0)
    tj = lax.broadcasted_iota(I32, (tm, tm), 1)
    earlier = jnp.where(tj < ti, 1.0, 0.0).astype(BF16)
    prefix = _dot(earlier, chosen.astype(BF16)) + carry_ref[0:1, 0:N_EXPERTS]

    col = lax.broadcasted_iota(I32, (tm, 8), 1)
    eidx = jnp.zeros((tm, 8), F32)
    gate = jnp.zeros((tm, 8), F32)
    rank = jnp.zeros((tm, 8), F32)
    for k in range(TOP_K):
        rk = jnp.sum(jnp.where(sels[k], prefix, 0.0), axis=-1, keepdims=True)
        eidx = jnp.where(col == k, idxs[k], eidx)
        gate = jnp.where(col == k, es[k] / den, gate)
        rank = jnp.where(col == k, rk, rank)
    eidx_ref[...] = eidx.astype(I32)
    gate_ref[...] = gate
    rank_ref[...] = rank.astype(I32)

    total = carry_ref[0:1, 0:N_EXPERTS] + jnp.sum(chosen, axis=0, keepdims=True)
    carry_ref[0:1, 0:N_EXPERTS] = total
    cnt_ref[...] = carry_ref[...]


def _post_mixer(x2d, od, oh, om, wo_bf, g, b, rw, rb, alpha):
    t = x2d.shape[0]
    tm = 256
    kern = functools.partial(_post_mixer_kernel, tm=tm, alpha=alpha)
    full = lambda i: (0, 0)
    tok = lambda i: (i, 0)
    return pl.pallas_call(
        kern,
        grid=(t // tm,),
        in_specs=[
            pl.BlockSpec((tm, D_MODEL), tok),
            pl.BlockSpec((tm, D_DIFF), tok),
            pl.BlockSpec((tm, D_HGRN), tok),
            pl.BlockSpec((tm, D_XMEM), tok),
            pl.BlockSpec((D_MODEL, D_MODEL), full),
            pl.BlockSpec((1, D_MODEL), full),
            pl.BlockSpec((1, D_MODEL), full),
            pl.BlockSpec((D_MODEL, N_EXPERTS), full),
            pl.BlockSpec((1, N_EXPERTS), full),
        ],
        out_specs=[
            pl.BlockSpec((tm, D_MODEL), tok),
            pl.BlockSpec((tm, 8), tok),
            pl.BlockSpec((tm, 8), tok),
            pl.BlockSpec((tm, 8), tok),
            pl.BlockSpec((8, LANES), full),
        ],
        out_shape=[
            jax.ShapeDtypeStruct((t, D_MODEL), F32),
            jax.ShapeDtypeStruct((t, 8), I32),
            jax.ShapeDtypeStruct((t, 8), F32),
            jax.ShapeDtypeStruct((t, 8), I32),
            jax.ShapeDtypeStruct((8, LANES), F32),
        ],
        scratch_shapes=[pltpu.VMEM((8, LANES), F32)],
        compiler_params=pltpu.CompilerParams(
            dimension_semantics=("arbitrary",), vmem_limit_bytes=VMEM_LIMIT),
        name="post_mixer",
    )(x2d, od, oh, om, wo_bf, g, b, rw, rb)


def _route_meta_kernel(eidx_ref, rank_ref, cnt_ref, dest_ref, be_ref, *, tm, n_blocks):
    cnt = cnt_ref[0:1, :]
    blocks = jnp.floor((cnt + (MOE_BLOCK - 1)) * (1.0 / MOE_BLOCK))
    ei = lax.broadcasted_iota(I32, (LANES, LANES), 0)
    ej = lax.broadcasted_iota(I32, (LANES, LANES), 1)
    before = jnp.where(ei < ej, 1.0, 0.0).astype(BF16)
    b8 = jnp.broadcast_to(blocks, (8, LANES)).astype(BF16)
    pstart = _dot(b8, before)[0:1, :]
    pend = pstart + blocks

    lane = lax.broadcasted_iota(I32, (tm, LANES), 1).astype(F32)
    col = lax.broadcasted_iota(I32, (tm, 8), 1)
    eidx = eidx_ref[...].astype(F32)
    dest = jnp.zeros((tm, 8), F32)
    for k in range(TOP_K):
        ek = jnp.sum(jnp.where(col == k, eidx, 0.0), axis=-1, keepdims=True)
        start = jnp.sum(jnp.where(lane == ek, pstart, 0.0), axis=-1, keepdims=True)
        dest = jnp.where(col == k, start * float(MOE_BLOCK), dest)
    dest_ref[...] = dest.astype(I32) + rank_ref[...]

    bi = lax.broadcasted_iota(I32, (n_blocks, LANES), 0).astype(F32)
    bl = lax.broadcasted_iota(I32, (n_blocks, LANES), 1)
    done = jnp.where((bl < N_EXPERTS) & (pend <= bi), 1.0, 0.0)
    be = jnp.minimum(jnp.sum(done, axis=-1, keepdims=True), float(N_EXPERTS - 1))
    used = jnp.sum(jnp.where(bl[0:1, :] == N_EXPERTS - 1, pend, 0.0), axis=-1, keepdims=True)
    bcol = lax.broadcasted_iota(I32, (n_blocks, 8), 1)
    be_ref[...] = jnp.where(bcol == 0, be, jnp.where(bcol == 1, used, 0.0)).astype(I32)


def _route_meta(eidx, rank, cnt, n_blocks):
    t = eidx.shape[0]
    tm = 2048
    kern = functools.partial(_route_meta_kernel, tm=tm, n_blocks=n_blocks)
    return pl.pallas_call(
        kern,
        grid=(t // tm,),
        in_specs=[
            pl.BlockSpec((tm, 8), lambda i: (i, 0)),
            pl.BlockSpec((tm, 8), lambda i: (i, 0)),
            pl.BlockSpec((8, LANES), lambda i: (0, 0)),
        ],
        out_specs=[
            pl.BlockSpec((tm, 8), lambda i: (i, 0)),
            pl.BlockSpec((n_blocks, 8), lambda i: (0, 0)),
        ],
        out_shape=[
            jax.ShapeDtypeStruct((t, 8), I32),
            jax.ShapeDtypeStruct((n_blocks, 8), I32),
        ],
        compiler_params=pltpu.CompilerParams(
            dimension_semantics=("arbitrary",), vmem_limit_bytes=VMEM_LIMIT),
        name="route_meta",
    )(eidx, rank, cnt)


def _dispatch_kernel(dest_ref, h_ref, xs_in_ref, xs_ref, sem, *, tm):
    del xs_in_ref
    base = pl.program_id(0) * (tm * TOP_K)

    def issue(t, carry):
        for k in range(TOP_K):
            d = dest_ref[base + t * TOP_K + k]
            pltpu.make_async_copy(h_ref.at[pl.ds(t, 1), :], xs_ref.at[pl.ds(d, 1), :],
                                  sem).start(priority=k % 2)
        return carry

    lax.fori_loop(0, tm, issue, 0, unroll=4)

    for k in range(TOP_K):
        pltpu.make_async_copy(h_ref, xs_ref.at[pl.ds(0, tm), :], sem).wait()


def _dispatch(dest_flat, h, xs_zero):
    t = h.shape[0]
    tm = 256
    kern = functools.partial(_dispatch_kernel, tm=tm)
    return pl.pallas_call(
        kern,
        grid_spec=pltpu.PrefetchScalarGridSpec(
            num_scalar_prefetch=1,
            grid=(t // tm,),
            in_specs=[
                pl.BlockSpec((tm, D_MODEL), lambda i, d: (i, 0)),
                pl.BlockSpec(memory_space=pl.ANY),
            ],
            out_specs=pl.BlockSpec(memory_space=pl.ANY),
            scratch_shapes=[pltpu.SemaphoreType.DMA(())],
        ),
        out_shape=jax.ShapeDtypeStruct(xs_zero.shape, xs_zero.dtype),
        input_output_aliases={2: 0},
        compiler_params=pltpu.CompilerParams(
            dimension_semantics=("arbitrary",), vmem_limit_bytes=VMEM_LIMIT),
        name="dispatch",
    )(dest_flat, h, xs_zero)


def _experts_kernel(be_ref, used_ref, xs_ref, wgu_hbm, wdn_hbm, bg_ref, bl_ref, bd_ref,
                    perm_ref, ys_ref, wgu_st, wdn_st, wg_ref, wl_ref, wd_ref, sem):
    i = pl.program_id(0)
    used = used_ref[0]
    e = be_ref[i]
    prev = be_ref[jnp.maximum(i - 1, 0)]
    fresh = jnp.logical_and(i < used, jnp.logical_or(i == 0, e != prev))

    def weight_copies(ex):
        return (pltpu.make_async_copy(wgu_hbm.at[ex], wgu_st, sem.at[0]),
                pltpu.make_async_copy(wdn_hbm.at[ex], wdn_st, sem.at[1]))

    @pl.when(jnp.logical_and(i == 0, used > 0))
    def _():
        for cp in weight_copies(e):
            cp.start()

    @pl.when(fresh)
    def _():
        for cp in weight_copies(e):
            cp.wait()
        perm = perm_ref[...]
        half = LANES
        for c in range(2 * D_EXPERT // (2 * half)):
            slab = wgu_st[:, c * 2 * half:(c + 1) * 2 * half].astype(BF16)
            sep = _dot(slab, perm)
            wg_ref[:, c * half:(c + 1) * half] = sep[:, 0:half].astype(BF16)
            wl_ref[:, c * half:(c + 1) * half] = sep[:, half:2 * half].astype(BF16)
        wd_ref[...] = wdn_st[...].astype(BF16)

        def same_expert(j):
            return jnp.logical_and(j < used, be_ref[jnp.minimum(j, used - 1)] == e)

        nxt = lax.while_loop(same_expert, lambda j: j + 1, i + 1)

        @pl.when(nxt < used)
        def _():
            for cp in weight_copies(be_ref[nxt]):
                cp.start()

    @pl.when(i < used_ref[0])
    def _():
        xb = xs_ref[...].astype(BF16)
        glu = jnp.minimum(_dot(xb, wg_ref[...]) + bg_ref[0], SWIGLU_LIMIT)
        lin = jnp.clip(_dot(xb, wl_ref[...]) + bl_ref[0], -SWIGLU_LIMIT, SWIGLU_LIMIT)
        act = glu * _sigmoid(SWIGLU_ALPHA * glu) * (lin + 1.0)
        ys_ref[...] = _dot(act.astype(BF16), wd_ref[...]) + bd_ref[0]

    @pl.when(i >= used_ref[0])
    def _():
        ys_ref[...] = jnp.zeros_like(ys_ref)


def _experts(block_e, used, xs, w_gu, w_dn, b_g, b_l, b_d, perm):
    n_rows = xs.shape[0]
    n_blocks = n_rows // MOE_BLOCK
    wmap = lambda i, be, u: (be[i], 0, 0)
    return pl.pallas_call(
        _experts_kernel,
        grid_spec=pltpu.PrefetchScalarGridSpec(
            num_scalar_prefetch=2,
            grid=(n_blocks,),
            in_specs=[
                pl.BlockSpec((MOE_BLOCK, D_MODEL), lambda i, be, u: (i, 0)),
                pl.BlockSpec(memory_space=pl.ANY),
                pl.BlockSpec(memory_space=pl.ANY),
                pl.BlockSpec((1, 1, D_EXPERT), wmap),
                pl.BlockSpec((1, 1, D_EXPERT), wmap),
                pl.BlockSpec((1, 1, D_MODEL), wmap),
                pl.BlockSpec((2 * LANES, 2 * LANES), lambda i, be, u: (0, 0)),
            ],
            out_specs=pl.BlockSpec((MOE_BLOCK, D_MODEL), lambda i, be, u: (i, 0)),
            scratch_shapes=[
                pltpu.VMEM((D_MODEL, 2 * D_EXPERT), F32),
                pltpu.VMEM((D_EXPERT, D_MODEL), F32),
                pltpu.VMEM((D_MODEL, D_EXPERT), BF16),
                pltpu.VMEM((D_MODEL, D_EXPERT), BF16),
                pltpu.VMEM((D_EXPERT, D_MODEL), BF16),
                pltpu.SemaphoreType.DMA((2,)),
            ],
        ),
        out_shape=jax.ShapeDtypeStruct((n_rows, D_MODEL), F32),
        compiler_params=pltpu.CompilerParams(
            dimension_semantics=("arbitrary",), vmem_limit_bytes=VMEM_LIMIT),
        name="experts",
    )(block_e, used, xs, w_gu, w_dn, b_g, b_l, b_d, perm)


def _combine_kernel(dest_ref, h_ref, gate_ref, g_ref, b_ref, ys_ref, o_ref, buf_ref, sem,
                    *, tm, alpha):
    i = pl.program_id(0)
    n = pl.num_programs(0)

    def issue(tile, slot):
        base = tile * (tm * TOP_K)

        def body(t, carry):
            for k in range(TOP_K):
                d = dest_ref[base + t * TOP_K + k]
                pltpu.make_async_copy(ys_ref.at[pl.ds(d, 1), :],
                                      buf_ref.at[slot, k, pl.ds(t, 1), :],
                                      sem.at[slot]).start(priority=k % 2)
            return carry

        lax.fori_loop(0, tm, body, 0, unroll=4)

    @pl.when(i == 0)
    def _():
        issue(0, 0)

    @pl.when(i + 1 < n)
    def _():
        issue(i + 1, (i + 1) % 2)

    slot = i % 2

    for k in range(TOP_K):
        pltpu.make_async_copy(ys_ref.at[pl.ds(0, tm), :], buf_ref.at[slot, k],
                              sem.at[slot]).wait()

    gate = gate_ref[...]
    col = lax.broadcasted_iota(I32, (tm, 8), 1)
    ffn = jnp.zeros((tm, D_MODEL), F32)
    for k in range(TOP_K):
        gk = jnp.sum(jnp.where(col == k, gate, 0.0), axis=-1, keepdims=True)
        ffn = ffn + gk * buf_ref[slot, k]
    o_ref[...] = _layer_norm(alpha * h_ref[...] + ffn, g_ref[...], b_ref[...])


def _combine(dest_flat, h, gate, g, b, ys, alpha):
    t = h.shape[0]
    tm = 256
    kern = functools.partial(_combine_kernel, tm=tm, alpha=alpha)
    return pl.pallas_call(
        kern,
        grid_spec=pltpu.PrefetchScalarGridSpec(
            num_scalar_prefetch=1,
            grid=(t // tm,),
            in_specs=[
                pl.BlockSpec((tm, D_MODEL), lambda i, d: (i, 0)),
                pl.BlockSpec((tm, 8), lambda i, d: (i, 0)),
                pl.BlockSpec((1, D_MODEL), lambda i, d: (0, 0)),
                pl.BlockSpec((1, D_MODEL), lambda i, d: (0, 0)),
                pl.BlockSpec(memory_space=pl.ANY),
            ],
            out_specs=pl.BlockSpec((tm, D_MODEL), lambda i, d: (i, 0)),
            scratch_shapes=[
                pltpu.VMEM((2, TOP_K, tm, D_MODEL), F32),
                pltpu.SemaphoreType.DMA((2,)),
            ],
        ),
        out_shape=jax.ShapeDtypeStruct((t, D_MODEL), F32),
        compiler_params=pltpu.CompilerParams(
            dimension_semantics=("arbitrary",), vmem_limit_bytes=VMEM_LIMIT),
        name="combine",
    )(dest_flat, h, gate, g, b, ys)


def _column_split_permutation():
    n = 2 * LANES
    p = np.zeros((n, n), np.float32)
    j = np.arange(n)
    p[j, j // 2 + LANES * (j % 2)] = 1.0
    return jnp.asarray(p, dtype=BF16)


def kernel(x, mem, w_in, lam_q1, lam_k1, lam_q2, lam_k2, diff_norm_w, hgrn_lb_fwd,
           hgrn_lb_bwd, hgrn_norm_w, w_mem_kv, w_o, ln1_g, ln1_b, router_w, router_b,
           w_gate_up, b_gate_up, w_down, b_down, ln2_g, ln2_b):
    batch, seq, d = x.shape
    mlen = mem.shape[1]
    assert d == D_MODEL and w_in.shape == (N_LAYERS, D_MODEL, D_IN_PROJ)
    assert hgrn_lb_fwd.shape == (N_LAYERS + 1, D_HGRN)
    t = batch * seq
    alpha = (2.0 * N_LAYERS) ** 0.25
    lam_init = 0.8 - 0.6 * math.exp(-0.3 * 0)

    x2d = x.reshape(t, D_MODEL)
    ua, uh, um = _in_proj(x2d, w_in[0].astype(BF16), hgrn_lb_fwd, hgrn_lb_bwd)
    o_diff = _diff_attn(ua, diff_norm_w, lam_q1, lam_k1, lam_q2, lam_k2,
                        batch, seq, lam_init)
    o_mem = _mem_attn(um, mem.reshape(batch * mlen, D_MODEL), w_mem_kv[0].astype(BF16),
                      batch, seq, mlen)
    o_hgrn = _hgrn2(uh, hgrn_norm_w, batch, seq)

    h1, eidx, gate, rank, cnt = _post_mixer(
        x2d, o_diff, o_hgrn, o_mem, w_o[0].astype(BF16), ln1_g, ln1_b, router_w[0],
        router_b, alpha)

    n_blocks = -(-(t * TOP_K) // MOE_BLOCK) + N_EXPERTS
    dest, meta = _route_meta(eidx, rank, cnt, n_blocks)
    dest_flat = dest[:, 0:TOP_K].reshape(t * TOP_K)
    block_e = meta[:, 0]
    used = meta[0:1, 1]

    xs = _dispatch(dest_flat, h1, jnp.zeros((n_blocks * MOE_BLOCK, D_MODEL), F32))
    b_g = b_gate_up[0][:, 0::2].reshape(N_EXPERTS, 1, D_EXPERT)
    b_l = b_gate_up[0][:, 1::2].reshape(N_EXPERTS, 1, D_EXPERT)
    ys = _experts(block_e, used, xs, w_gate_up[0], w_down[0], b_g, b_l,
                  b_down[0].reshape(N_EXPERTS, 1, D_MODEL), _column_split_permutation())
    out = _combine(dest_flat, h1, gate, ln2_g, ln2_b, ys, alpha)
    return out.reshape(batch, seq, D_MODEL)
```

```python
import functools
import math

import jax
import jax.numpy as jnp
import numpy as np
from jax import lax
from jax.experimental import pallas as pl
from jax.experimental.pallas import tpu as pltpu

F32 = jnp.float32
BF16 = jnp.bfloat16
I32 = jnp.int32

D_MODEL = 1024
N_LAYERS = 1
D_DIFF = 512
D_HGRN = 256
D_XMEM = 256
N_HEADS = 4
DH = 64
DH_SHIFT = 6
D_IN_PROJ = 3 * D_DIFF + 5 * D_HGRN + D_XMEM
CHUNK = 16
N_EXPERTS = 32
TOP_K = 4
D_EXPERT = 1024
MOE_BLOCK = 256
SWIGLU_ALPHA = 1.702
SWIGLU_LIMIT = 7.0
NORM_EPS = 1e-5
LOG2E = math.log2(math.e)
LANES = 128
VMEM_LIMIT = 56 * 1024 * 1024


def _nt_dot(a, b):
    return lax.dot_general(a, b, (((1,), (1,)), ((), ())), preferred_element_type=F32)


def _tn_dot(a, b):
    return lax.dot_general(a, b, (((0,), (0,)), ((), ())), preferred_element_type=F32)


def _dot(a, b):
    return jnp.dot(a, b, preferred_element_type=F32)


def _sigmoid(x):
    return 1.0 / (1.0 + jnp.exp(-x))


def _layer_norm(y, g, b):
    mu = jnp.mean(y, axis=-1, keepdims=True)
    yc = y - mu
    var = jnp.mean(yc * yc, axis=-1, keepdims=True)
    return yc * lax.rsqrt(var + NORM_EPS) * g + b


def _in_proj_kernel(x_ref, w_ref, lbf_ref, lbb_ref, ua_ref, uh_ref, um_ref):
    xb = x_ref[...].astype(BF16)

    def proj(c0, width):
        return _dot(xb, w_ref[:, c0:c0 + width])

    ua_ref[:, 0:D_DIFF] = (proj(0, D_DIFF) * (LOG2E / math.sqrt(DH))).astype(BF16)
    ua_ref[:, D_DIFF:2 * D_DIFF] = proj(D_DIFF, D_DIFF).astype(BF16)
    ua_ref[:, 2 * D_DIFF:3 * D_DIFF] = proj(2 * D_DIFF, D_DIFF).astype(BF16)

    def lower_bound(lb_ref):
        a = lb_ref[0:1, :]
        b = lb_ref[1:2, :]
        m = jnp.maximum(a, b)
        ea = jnp.exp(a - m)
        eb = jnp.exp(b - m)
        return ea / (ea + eb)

    base = 3 * D_DIFF
    hq = proj(base, D_HGRN)
    uh_ref[:, 0:D_HGRN] = hq * _sigmoid(hq)
    for d, lb_ref in enumerate((lbf_ref, lbb_ref)):
        lb = lower_bound(lb_ref)
        f = lb + (1.0 - lb) * _sigmoid(proj(base + (1 + d) * D_HGRN, D_HGRN))
        uh_ref[:, (1 + 2 * d) * D_HGRN:(2 + 2 * d) * D_HGRN] = 1.0 - f
        uh_ref[:, (2 + 2 * d) * D_HGRN:(3 + 2 * d) * D_HGRN] = jnp.log(f)
    uh_ref[:, 5 * D_HGRN:6 * D_HGRN] = proj(base + 3 * D_HGRN, D_HGRN)
    uh_ref[:, 6 * D_HGRN:7 * D_HGRN] = _sigmoid(proj(base + 4 * D_HGRN, D_HGRN))
    um_ref[...] = proj(base + 5 * D_HGRN, D_XMEM).astype(BF16)


def _in_proj(x2d, w_in_bf, lb_f, lb_b):
    t = x2d.shape[0]
    tm = 256
    return pl.pallas_call(
        _in_proj_kernel,
        grid=(t // tm,),
        in_specs=[
            pl.BlockSpec((tm, D_MODEL), lambda i: (i, 0)),
            pl.BlockSpec((D_MODEL, D_IN_PROJ), lambda i: (0, 0)),
            pl.BlockSpec((N_LAYERS + 1, D_HGRN), lambda i: (0, 0)),
            pl.BlockSpec((N_LAYERS + 1, D_HGRN), lambda i: (0, 0)),
        ],
        out_specs=[
            pl.BlockSpec((tm, 3 * D_DIFF), lambda i: (i, 0)),
            pl.BlockSpec((tm, 7 * D_HGRN), lambda i: (i, 0)),
            pl.BlockSpec((tm, D_XMEM), lambda i: (i, 0)),
        ],
        out_shape=[
            jax.ShapeDtypeStruct((t, 3 * D_DIFF), BF16),
            jax.ShapeDtypeStruct((t, 7 * D_HGRN), F32),
            jax.ShapeDtypeStruct((t, D_XMEM), BF16),
        ],
        compiler_params=pltpu.CompilerParams(
            dimension_semantics=("arbitrary",), vmem_limit_bytes=VMEM_LIMIT),
        name="in_proj",
    )(x2d, w_in_bf, lb_f, lb_b)


N_POS_FEATURES = 12


def _alibi_feature_rows(c3, base):
    f = lax.broadcasted_iota(I32, (1, 2 * DH), 1) - base

    def pieces(f0):
        return jnp.where(f == f0, c3[0], jnp.where(f == f0 + 1, c3[1],
                                                   jnp.where(f == f0 + 2, c3[2], 0.0)))

    span = lambda lo: jnp.where((f >= lo) & (f < lo + 3), 1.0, 0.0)
    q_const = 64.0 * pieces(6) + pieces(9)
    k_const = -64.0 * pieces(0) - pieces(3)
    return (q_const, span(0), span(3)), (k_const, span(6), span(9))


def _position_features(rows, pos0, n):
    const, hi_mask, lo_mask = rows
    pos = pos0 + lax.broadcasted_iota(I32, (n, 2 * DH), 0)
    hi = (pos >> DH_SHIFT).astype(F32)
    lo = (pos & (DH - 1)).astype(F32)
    return const + hi_mask * hi + lo_mask * lo


def _diff_attn_kernel(c_ref, q_ref, k_ref, v_ref, nw_ref, lq1_ref, lk1_ref, lq2_ref,
                      lk2_ref, o_ref, ka1_ref, ka2_ref, corr_ref, vt_ref, acc_ref, m_ref,
                      l_ref, *, t, qs, seq, lam_init):
    h = pl.program_id(1)
    i = pl.program_id(2)
    nk = seq // t
    c3 = (c_ref[h, 0], c_ref[h, 1], c_ref[h, 2])
    lane = lax.broadcasted_iota(I32, (t, 2 * DH), 1)
    first_half = lane < DH
    q_rows1, k_rows1 = _alibi_feature_rows(c3, DH)
    q_rows2, k_rows2 = _alibi_feature_rows(c3, 0)

    @pl.when(i == 0)
    def _():
        def build(r, carry):
            r0 = pl.multiple_of(r * t, t)
            rows = pl.ds(r0, t)
            kblk = k_ref[rows, :]
            ka1_ref[rows, :] = jnp.where(first_half, kblk,
                                         _position_features(k_rows1, r0, t).astype(BF16))
            ka2_ref[rows, :] = jnp.where(first_half,
                                         _position_features(k_rows2, r0, t).astype(BF16), kblk)
            vt_ref[r] = v_ref[rows, :].astype(F32).T.astype(BF16)
            return carry

        lax.fori_loop(0, nk, build, 0)
        kk = lax.broadcasted_iota(I32, (t, t), 0)
        qq = lax.broadcasted_iota(I32, (t, t), 1)
        c = c3[0] + c3[1] + c3[2]
        corr_ref[...] = (2.0 * c) * jnp.minimum(qq - kk, 0).astype(F32)

    q = q_ref[...]
    qf1 = _position_features(q_rows1, i * t, t)
    qf2 = _position_features(q_rows2, i * t, t)
    qa1_before = jnp.where(first_half, q, qf1.astype(BF16))
    qa1_after = jnp.where(first_half, q, (-qf1).astype(BF16))
    qa2_before = jnp.where(first_half, qf2.astype(BF16), q)
    qa2_after = jnp.where(first_half, (-qf2).astype(BF16), q)

    m_ref[...] = jnp.full(m_ref.shape, -jnp.inf, F32)
    l_ref[...] = jnp.zeros(l_ref.shape, F32)
    acc_ref[...] = jnp.zeros(acc_ref.shape, F32)

    def chunk(j, qa1, qa2, diagonal):
        rows = pl.ds(pl.multiple_of(j * t, t), t)
        vt = vt_ref[j]
        for mp, (qa, ka_ref) in enumerate(((qa1, ka1_ref), (qa2, ka2_ref))):
            ka = ka_ref[rows, :]
            for u in range(t // qs):
                qrows = slice(u * qs, (u + 1) * qs)
                cols = slice(mp * t + u * qs, mp * t + (u + 1) * qs)
                s = _nt_dot(ka, qa[qrows, :])
                if diagonal:
                    s = s + corr_ref[:, qrows]
                m_old = m_ref[:, cols]
                m_new = jnp.maximum(m_old, jnp.max(s, axis=0, keepdims=True))
                p = jnp.exp2(s - m_new)
                r = jnp.exp2(m_old - m_new)
                l_ref[:, cols] = r * l_ref[:, cols] + jnp.sum(p, axis=0, keepdims=True)
                acc_ref[mp, :, qrows] = (r * acc_ref[mp, :, qrows]
                                         + _dot(vt, p.astype(BF16)))
                m_ref[:, cols] = m_new

    chunk(i, qa1_before, qa2_before, True)
    for jj in range(nk - 1):
        j = jj + (jj >= i).astype(I32)
        keys_first = j < i
        chunk(j, jnp.where(keys_first, qa1_before, qa1_after),
              jnp.where(keys_first, qa2_before, qa2_after), False)

    lam = (jnp.exp(jnp.sum(lq1_ref[...] * lk1_ref[...], axis=-1, keepdims=True))
           - jnp.exp(jnp.sum(lq2_ref[...] * lk2_ref[...], axis=-1, keepdims=True))
           + lam_init)
    o = acc_ref[0] / l_ref[:, 0:t] - lam * (acc_ref[1] / l_ref[:, t:2 * t])
    o = o * lax.rsqrt(jnp.mean(o * o, axis=0, keepdims=True) + NORM_EPS)
    o_ref[...] = (o.T * nw_ref[...] * (1.0 - lam_init)).astype(o_ref.dtype)


def _bf16_pieces(x):
    x = np.asarray(x, np.float32)
    out = []
    for _ in range(3):
        p = x.astype(BF16).astype(np.float32)
        out.append(p)
        x = x - p
    return np.stack(out, axis=-1)


def _diff_attn(ua, nw, lq1, lk1, lq2, lk2, batch, seq, lam_init):
    t = 512
    nq = seq // t
    slopes = 2.0 ** (-8.0 * np.arange(1, N_HEADS + 1) / N_HEADS)
    c_pieces = jnp.asarray(_bf16_pieces(slopes.astype(np.float32) * np.float32(LOG2E)))
    kern = functools.partial(_diff_attn_kernel, t=t, qs=LANES, seq=seq, lam_init=lam_init)
    small = lambda b, h, i, s: (0, 0)
    return pl.pallas_call(
        kern,
        grid_spec=pltpu.PrefetchScalarGridSpec(
            num_scalar_prefetch=1,
            grid=(batch, N_HEADS, nq),
            in_specs=[
                pl.BlockSpec((t, 2 * DH), lambda b, h, i, s: (b * nq + i, h)),
                pl.BlockSpec((seq, 2 * DH), lambda b, h, i, s: (b, N_HEADS + h)),
                pl.BlockSpec((seq, 2 * DH), lambda b, h, i, s: (b, 2 * N_HEADS + h)),
                pl.BlockSpec((1, 2 * DH), small),
                pl.BlockSpec((1, DH), small),
                pl.BlockSpec((1, DH), small),
                pl.BlockSpec((1, DH), small),
                pl.BlockSpec((1, DH), small),
            ],
            out_specs=pl.BlockSpec((t, 2 * DH), lambda b, h, i, s: (b * nq + i, h)),
            scratch_shapes=[
                pltpu.VMEM((seq, 2 * DH), BF16),
                pltpu.VMEM((seq, 2 * DH), BF16),
                pltpu.VMEM((t, t), F32),
                pltpu.VMEM((nq, 2 * DH, t), BF16),
                pltpu.VMEM((2, 2 * DH, t), F32),
                pltpu.VMEM((1, 2 * t), F32),
                pltpu.VMEM((1, 2 * t), F32),
            ],
        ),
        out_shape=jax.ShapeDtypeStruct((batch * seq, D_DIFF), BF16),
        compiler_params=pltpu.CompilerParams(
            dimension_semantics=("arbitrary", "arbitrary", "arbitrary"),
            vmem_limit_bytes=VMEM_LIMIT),
        name="diff_attn",
    )(c_pieces, ua, ua, ua, nw, lq1, lk1, lq2, lk2)


def _mem_attn_kernel(q_ref, mem_ref, wkv_ref, o_ref, kv_ref, *, tq, mlen):
    @pl.when(pl.program_id(1) == 0)
    def _():
        kv_ref[...] = _dot(mem_ref[...].astype(BF16), wkv_ref[...]).astype(BF16)

    q = q_ref[...]
    mk = kv_ref[:, 0:D_XMEM]
    mv = kv_ref[:, D_XMEM:2 * D_XMEM]
    qhead = lax.broadcasted_iota(I32, (tq, D_XMEM), 1) >> DH_SHIFT
    vhead = lax.broadcasted_iota(I32, (mlen, D_XMEM), 1) >> DH_SHIFT
    acc = jnp.zeros((tq, D_XMEM), F32)
    for h in range(N_HEADS):
        qh = jnp.where(qhead == h, q, jnp.zeros_like(q))
        s = _nt_dot(qh, mk) * (1.0 / math.sqrt(DH))
        e = jnp.exp(s - jnp.max(s, axis=-1, keepdims=True))
        p = e / jnp.sum(e, axis=-1, keepdims=True)
        vh = jnp.where(vhead == h, mv, jnp.zeros_like(mv))
        acc = acc + _dot(p.astype(BF16), vh)
    o_ref[...] = acc.astype(o_ref.dtype)


def _mem_attn(um, mem2d, wkv_bf, batch, seq, mlen):
    tq = 512
    nq = seq // tq
    kern = functools.partial(_mem_attn_kernel, tq=tq, mlen=mlen)
    return pl.pallas_call(
        kern,
        grid=(batch, nq),
        in_specs=[
            pl.BlockSpec((tq, D_XMEM), lambda b, i: (b * nq + i, 0)),
            pl.BlockSpec((mlen, D_MODEL), lambda b, i: (b, 0)),
            pl.BlockSpec((D_MODEL, 2 * D_XMEM), lambda b, i: (0, 0)),
        ],
        out_specs=pl.BlockSpec((tq, D_XMEM), lambda b, i: (b * nq + i, 0)),
        out_shape=jax.ShapeDtypeStruct((batch * seq, D_XMEM), BF16),
        scratch_shapes=[pltpu.VMEM((mlen, 2 * D_XMEM), BF16)],
        compiler_params=pltpu.CompilerParams(
            dimension_semantics=("arbitrary", "arbitrary"), vmem_limit_bytes=VMEM_LIMIT),
        name="mem_attn",
    )(um, mem2d, wkv_bf)


def _hgrn2_kernel(q_ref, kf_ref, lf_ref, kb_ref, lb_ref, v_ref, g_ref, nw_ref, o_ref,
                  kfp_ref, kbp_ref, vp_ref, bf_ref, bb_ref, acc_ref, cross_ref, *, seq, rb):
    n_chunks = seq // CHUNK
    w = 2 * DH
    row = lax.broadcasted_iota(I32, (rb, w), 0) & (CHUNK - 1)
    li = lax.broadcasted_iota(I32, (w, w), 0) >> DH_SHIFT
    lj = lax.broadcasted_iota(I32, (w, w), 1) >> DH_SHIFT
    same_head = li == lj
    head_ones = jnp.where(same_head, 1.0, 0.0).astype(BF16)

    def chunk_cumsum(x, reverse):
        for sh in (1, 2, 4, 8):
            if reverse:
                moved = pltpu.roll(x, rb - sh, 0)
                keep = row + sh < CHUNK
            else:
                moved = pltpu.roll(x, sh, 0)
                keep = row >= sh
            x = x + jnp.where(keep, moved, 0.0)
        return x

    halo = jnp.zeros((CHUNK, w), F32)
    for ref in (kfp_ref, kbp_ref, vp_ref, bf_ref, bb_ref):
        ref[0:CHUNK, :] = halo
        ref[seq + CHUNK:seq + 2 * CHUNK, :] = halo

    def prepare(blk, carry):
        r0 = pl.multiple_of(blk * rb, rb)
        rows = pl.ds(r0, rb)
        inner = pl.ds(r0 + CHUNK, rb)
        kfp_ref[inner, :] = kf_ref[rows, :]
        kbp_ref[inner, :] = kb_ref[rows, :]
        vp_ref[inner, :] = v_ref[rows, :]
        bf_ref[inner, :] = chunk_cumsum(lf_ref[rows, :] * LOG2E, False)
        bb_ref[inner, :] = chunk_cumsum(lb_ref[rows, :] * LOG2E, True)
        return carry

    lax.fori_loop(0, seq // rb, prepare, 0)

    def intra(blk):
        r0 = pl.multiple_of(blk * rb, rb)
        rows = pl.ds(r0, rb)
        q = q_ref[rows, :]
        acc = jnp.zeros((rb, w), F32)
        for reverse, kp_ref, b_ref in ((False, kfp_ref, bf_ref), (True, kbp_ref, bb_ref)):
            b = b_ref[pl.ds(r0 + CHUNK, rb), :]
            for d in range(CHUNK):
                src = pl.ds(r0 + CHUNK + (d if reverse else -d), rb)
                ks = kp_ref[src, :]
                vs = vp_ref[src, :]
                if d == 0:
                    z = q * ks
                else:
                    keep = (row + d < CHUNK) if reverse else (row >= d)
                    dec = jnp.exp2(jnp.where(keep, b - b_ref[src, :], -jnp.inf))
                    z = q * ks * dec
                acc = acc + _dot(z.astype(BF16), head_ones) * vs
        acc_ref[rows, :] = acc

    def inter(n, carry):
        wf, wb = carry
        rf = pl.multiple_of(n * CHUNK, CHUNK)
        rr = pl.multiple_of((n_chunks - 1 - n) * CHUNK, CHUNK)

        def one(state, r0, kp_ref, b_ref, edge):
            rows = pl.ds(r0, CHUNK)
            prow = pl.ds(r0 + CHUNK, CHUNK)
            b = b_ref[prow, :]
            b_edge = b_ref[pl.ds(r0 + CHUNK + edge, 1), :]
            qd = q_ref[rows, :] * jnp.exp2(b)
            cross_ref[rows, :] = cross_ref[rows, :] + _nt_dot(qd.astype(BF16),
                                                              state.astype(BF16))
            kd = kp_ref[prow, :] * jnp.exp2(b_edge - b)
            upd = _tn_dot(vp_ref[prow, :].astype(BF16), kd.astype(BF16))
            return state * jnp.exp2(b_edge) + jnp.where(same_head, upd, 0.0)

        wf = one(wf, rf, kfp_ref, bf_ref, CHUNK - 1)
        wb = one(wb, rr, kbp_ref, bb_ref, 0)
        return wf, wb

    cross_ref[...] = jnp.zeros(cross_ref.shape, F32)

    def fused(blk, carry):
        intra(blk)
        for c in range(rb // CHUNK):
            carry = inter(blk * (rb // CHUNK) + c, carry)
        return carry

    z = jnp.zeros((w, w), F32)
    lax.fori_loop(0, seq // rb, fused, (z, z))

    def finish(blk, carry):
        rows = pl.ds(pl.multiple_of(blk * rb, rb), rb)
        o = acc_ref[rows, :] + cross_ref[rows, :]
        lane = lax.broadcasted_iota(I32, (rb, w), 1)
        lo = lane < DH
        sq = o * o
        ms_lo = jnp.sum(jnp.where(lo, sq, 0.0), axis=-1, keepdims=True)
        ms_hi = jnp.sum(jnp.where(lo, 0.0, sq), axis=-1, keepdims=True)
        ms = jnp.where(lo, ms_lo, ms_hi) * (1.0 / DH)
        o = o * lax.rsqrt(ms + NORM_EPS) * nw_ref[...] * g_ref[rows, :]
        o_ref[rows, :] = o.astype(o_ref.dtype)
        return carry

    lax.fori_loop(0, seq // rb, finish, 0)


def _hgrn2(uh, nw, batch, seq):
    w = 2 * DH
    npair = D_HGRN // w
    kern = functools.partial(_hgrn2_kernel, seq=seq, rb=128)

    def sec(s):
        return pl.BlockSpec((seq, w), lambda b, p, s=s: (b, s * npair + p))

    return pl.pallas_call(
        kern,
        grid=(batch, npair),
        in_specs=[sec(0), sec(1), sec(2), sec(3), sec(4), sec(5), sec(6),
                  pl.BlockSpec((1, w), lambda b, p: (0, p))],
        out_specs=pl.BlockSpec((seq, w), lambda b, p: (b, p)),
        out_shape=jax.ShapeDtypeStruct((batch * seq, D_HGRN), BF16),
        scratch_shapes=[pltpu.VMEM((seq + 2 * CHUNK, w), F32)] * 5
        + [pltpu.VMEM((seq, w), F32)] * 2,
        compiler_params=pltpu.CompilerParams(
            dimension_semantics=("arbitrary", "arbitrary"), vmem_limit_bytes=VMEM_LIMIT),
        name="hgrn2",
    )(uh, uh, uh, uh, uh, uh, uh, nw)


def _post_mixer_kernel(x_ref, od_ref, oh_ref, om_ref, wo_ref, g_ref, b_ref, rw_ref,
                       rb_ref, h_ref, eidx_ref, gate_ref, rank_ref, cnt_ref, carry_ref,
                       *, tm, alpha):
    i = pl.program_id(0)

    @pl.when(i == 0)
    def _():
        carry_ref[...] = jnp.zeros_like(carry_ref)

    mix = (_dot(od_ref[...], wo_ref[0:D_DIFF, :])
           + _dot(oh_ref[...], wo_ref[D_DIFF:D_DIFF + D_HGRN, :])
           + _dot(om_ref[...], wo_ref[D_DIFF + D_HGRN:, :]))
    h = _layer_norm(alpha * x_ref[...] + mix, g_ref[...], b_ref[...])
    h_ref[...] = h

    h_hi = h.astype(BF16)
    h_lo = (h - h_hi.astype(F32)).astype(BF16)
    rw = rw_ref[...]
    rw_hi = rw.astype(BF16)
    rw_lo = (rw - rw_hi.astype(F32)).astype(BF16)
    logits = _dot(h_hi, rw_hi) + _dot(h_lo, rw_hi) + _dot(h_hi, rw_lo) + rb_ref[...]

    lane = lax.broadcasted_iota(I32, (tm, N_EXPERTS), 1).astype(F32)
    work = logits
    sels, vals, idxs = [], [], []
    for _ in range(TOP_K):
        m = jnp.max(work, axis=-1, keepdims=True)
        idx = jnp.min(jnp.where(work == m, lane, float(N_EXPERTS)), axis=-1, keepdims=True)
        sel = lane == idx
        work = jnp.where(sel, -jnp.inf, work)
        sels.append(sel)
        vals.append(m)
        idxs.append(idx)
    es = [jnp.exp(v - vals[0]) for v in vals]
    den = es[0] + es[1] + es[2] + es[3]

    chosen = jnp.where(sels[0] | sels[1] | sels[2] | sels[3], 1.0, 0.0)
    ti = lax.broadcasted_iota(I32, (tm, tm), 0)
    tj = lax.broadcasted_iota(I32, (tm, tm), 1)
    earlier = jnp.where(tj < ti, 1.0, 0.0).astype(BF16)
    prefix = _dot(earlier, chosen.astype(BF16)) + carry_ref[0:1, 0:N_EXPERTS]

    col = lax.broadcasted_iota(I32, (tm, 8), 1)
    eidx = jnp.zeros((tm, 8), F32)
    gate = jnp.zeros((tm, 8), F32)
    rank = jnp.zeros((tm, 8), F32)
    for k in range(TOP_K):
        rk = jnp.sum(jnp.where(sels[k], prefix, 0.0), axis=-1, keepdims=True)
        eidx = jnp.where(col == k, idxs[k], eidx)
        gate = jnp.where(col == k, es[k] / den, gate)
        rank = jnp.where(col == k, rk, rank)
    eidx_ref[...] = eidx.astype(I32)
    gate_ref[...] = gate
    rank_ref[...] = rank.astype(I32)

    total = carry_ref[0:1, 0:N_EXPERTS] + jnp.sum(chosen, axis=0, keepdims=True)
    carry_ref[0:1, 0:N_EXPERTS] = total
    cnt_ref[...] = carry_ref[...]


def _post_mixer(x2d, od, oh, om, wo_bf, g, b, rw, rb, alpha):
    t = x2d.shape[0]
    tm = 256
    kern = functools.partial(_post_mixer_kernel, tm=tm, alpha=alpha)
    full = lambda i: (0, 0)
    tok = lambda i: (i, 0)
    return pl.pallas_call(
        kern,
        grid=(t // tm,),
        in_specs=[
            pl.BlockSpec((tm, D_MODEL), tok),
            pl.BlockSpec((tm, D_DIFF), tok),
            pl.BlockSpec((tm, D_HGRN), tok),
            pl.BlockSpec((tm, D_XMEM), tok),
            pl.BlockSpec((D_MODEL, D_MODEL), full),
            pl.BlockSpec((1, D_MODEL), full),
            pl.BlockSpec((1, D_MODEL), full),
            pl.BlockSpec((D_MODEL, N_EXPERTS), full),
            pl.BlockSpec((1, N_EXPERTS), full),
        ],
        out_specs=[
            pl.BlockSpec((tm, D_MODEL), tok),
            pl.BlockSpec((tm, 8), tok),
            pl.BlockSpec((tm, 8), tok),
            pl.BlockSpec((tm, 8), tok),
            pl.BlockSpec((8, LANES), full),
        ],
        out_shape=[
            jax.ShapeDtypeStruct((t, D_MODEL), F32),
            jax.ShapeDtypeStruct((t, 8), I32),
            jax.ShapeDtypeStruct((t, 8), F32),
            jax.ShapeDtypeStruct((t, 8), I32),
            jax.ShapeDtypeStruct((8, LANES), F32),
        ],
        scratch_shapes=[pltpu.VMEM((8, LANES), F32)],
        compiler_params=pltpu.CompilerParams(
            dimension_semantics=("arbitrary",), vmem_limit_bytes=VMEM_LIMIT),
        name="post_mixer",
    )(x2d, od, oh, om, wo_bf, g, b, rw, rb)


def _route_meta_kernel(eidx_ref, rank_ref, cnt_ref, dest_ref, be_ref, *, tm, n_blocks):
    cnt = cnt_ref[0:1, :]
    blocks = jnp.floor((cnt + (MOE_BLOCK - 1)) * (1.0 / MOE_BLOCK))
    ei = lax.broadcasted_iota(I32, (LANES, LANES), 0)
    ej = lax.broadcasted_iota(I32, (LANES, LANES), 1)
    before = jnp.where(ei < ej, 1.0, 0.0).astype(BF16)
    b8 = jnp.broadcast_to(blocks, (8, LANES)).astype(BF16)
    pstart = _dot(b8, before)[0:1, :]
    pend = pstart + blocks

    lane = lax.broadcasted_iota(I32, (tm, LANES), 1).astype(F32)
    col = lax.broadcasted_iota(I32, (tm, 8), 1)
    eidx = eidx_ref[...].astype(F32)
    dest = jnp.zeros((tm, 8), F32)
    for k in range(TOP_K):
        ek = jnp.sum(jnp.where(col == k, eidx, 0.0), axis=-1, keepdims=True)
        start = jnp.sum(jnp.where(lane == ek, pstart, 0.0), axis=-1, keepdims=True)
        dest = jnp.where(col == k, start * float(MOE_BLOCK), dest)
    dest_ref[...] = dest.astype(I32) + rank_ref[...]

    bi = lax.broadcasted_iota(I32, (n_blocks, LANES), 0).astype(F32)
    bl = lax.broadcasted_iota(I32, (n_blocks, LANES), 1)
    done = jnp.where((bl < N_EXPERTS) & (pend <= bi), 1.0, 0.0)
    be = jnp.minimum(jnp.sum(done, axis=-1, keepdims=True), float(N_EXPERTS - 1))
    used = jnp.sum(jnp.where(bl[0:1, :] == N_EXPERTS - 1, pend, 0.0), axis=-1, keepdims=True)
    bcol = lax.broadcasted_iota(I32, (n_blocks, 8), 1)
    be_ref[...] = jnp.where(bcol == 0, be, jnp.where(bcol == 1, used, 0.0)).astype(I32)


def _route_meta(eidx, rank, cnt, n_blocks):
    t = eidx.shape[0]
    tm = 2048
    kern = functools.partial(_route_meta_kernel, tm=tm, n_blocks=n_blocks)
    return pl.pallas_call(
        kern,
        grid=(t // tm,),
        in_specs=[
            pl.BlockSpec((tm, 8), lambda i: (i, 0)),
            pl.BlockSpec((tm, 8), lambda i: (i, 0)),
            pl.BlockSpec((8, LANES), lambda i: (0, 0)),
        ],
        out_specs=[
            pl.BlockSpec((tm, 8), lambda i: (i, 0)),
            pl.BlockSpec((n_blocks, 8), lambda i: (0, 0)),
        ],
        out_shape=[
            jax.ShapeDtypeStruct((t, 8), I32),
            jax.ShapeDtypeStruct((n_blocks, 8), I32),
        ],
        compiler_params=pltpu.CompilerParams(
            dimension_semantics=("arbitrary",), vmem_limit_bytes=VMEM_LIMIT),
        name="route_meta",
    )(eidx, rank, cnt)


def _dispatch_kernel(dest_ref, h_ref, xs_in_ref, xs_ref, sem, *, tm):
    del xs_in_ref
    base = pl.program_id(0) * (tm * TOP_K)

    def issue(t, carry):
        for k in range(TOP_K):
            d = dest_ref[base + t * TOP_K + k]
            pltpu.make_async_copy(h_ref.at[pl.ds(t, 1), :], xs_ref.at[pl.ds(d, 1), :],
                                  sem).start(priority=k % 2)
        return carry

    lax.fori_loop(0, tm, issue, 0, unroll=4)

    for k in range(TOP_K):
        pltpu.make_async_copy(h_ref, xs_ref.at[pl.ds(0, tm), :], sem).wait()


def _dispatch(dest_flat, h, xs_zero):
    t = h.shape[0]
    tm = 256
    kern = functools.partial(_dispatch_kernel, tm=tm)
    return pl.pallas_call(
        kern,
        grid_spec=pltpu.PrefetchScalarGridSpec(
            num_scalar_prefetch=1,
            grid=(t // tm,),
            in_specs=[
                pl.BlockSpec((tm, D_MODEL), lambda i, d: (i, 0)),
                pl.BlockSpec(memory_space=pl.ANY),
            ],
            out_specs=pl.BlockSpec(memory_space=pl.ANY),
            scratch_shapes=[pltpu.SemaphoreType.DMA(())],
        ),
        out_shape=jax.ShapeDtypeStruct(xs_zero.shape, xs_zero.dtype),
        input_output_aliases={2: 0},
        compiler_params=pltpu.CompilerParams(
            dimension_semantics=("arbitrary",), vmem_limit_bytes=VMEM_LIMIT),
        name="dispatch",
    )(dest_flat, h, xs_zero)


def _experts_kernel(be_ref, used_ref, xs_ref, wgu_hbm, wdn_hbm, bg_ref, bl_ref, bd_ref,
                    perm_ref, ys_ref, wgu_st, wdn_st, wg_ref, wl_ref, wd_ref, sem):
    i = pl.program_id(0)
    used = used_ref[0]
    e = be_ref[i]
    prev = be_ref[jnp.maximum(i - 1, 0)]
    fresh = jnp.logical_and(i < used, jnp.logical_or(i == 0, e != prev))

    def weight_copies(ex):
        return (pltpu.make_async_copy(wgu_hbm.at[ex], wgu_st, sem.at[0]),
                pltpu.make_async_copy(wdn_hbm.at[ex], wdn_st, sem.at[1]))

    @pl.when(jnp.logical_and(i == 0, used > 0))
    def _():
        for cp in weight_copies(e):
            cp.start()

    @pl.when(fresh)
    def _():
        for cp in weight_copies(e):
            cp.wait()
        perm = perm_ref[...]
        half = LANES
        for c in range(2 * D_EXPERT // (2 * half)):
            slab = wgu_st[:, c * 2 * half:(c + 1) * 2 * half].astype(BF16)
            sep = _dot(slab, perm)
            wg_ref[:, c * half:(c + 1) * half] = sep[:, 0:half].astype(BF16)
            wl_ref[:, c * half:(c + 1) * half] = sep[:, half:2 * half].astype(BF16)
        wd_ref[...] = wdn_st[...].astype(BF16)

        def same_expert(j):
            return jnp.logical_and(j < used, be_ref[jnp.minimum(j, used - 1)] == e)

        nxt = lax.while_loop(same_expert, lambda j: j + 1, i + 1)

        @pl.when(nxt < used)
        def _():
            for cp in weight_copies(be_ref[nxt]):
                cp.start()

    @pl.when(i < used_ref[0])
    def _():
        xb = xs_ref[...].astype(BF16)
        glu = jnp.minimum(_dot(xb, wg_ref[...]) + bg_ref[0], SWIGLU_LIMIT)
        lin = jnp.clip(_dot(xb, wl_ref[...]) + bl_ref[0], -SWIGLU_LIMIT, SWIGLU_LIMIT)
        act = glu * _sigmoid(SWIGLU_ALPHA * glu) * (lin + 1.0)
        ys_ref[...] = _dot(act.astype(BF16), wd_ref[...]) + bd_ref[0]

    @pl.when(i >= used_ref[0])
    def _():
        ys_ref[...] = jnp.zeros_like(ys_ref)


def _experts(block_e, used, xs, w_gu, w_dn, b_g, b_l, b_d, perm):
    n_rows = xs.shape[0]
    n_blocks = n_rows // MOE_BLOCK
    wmap = lambda i, be, u: (be[i], 0, 0)
    return pl.pallas_call(
        _experts_kernel,
        grid_spec=pltpu.PrefetchScalarGridSpec(
            num_scalar_prefetch=2,
            grid=(n_blocks,),
            in_specs=[
                pl.BlockSpec((MOE_BLOCK, D_MODEL), lambda i, be, u: (i, 0)),
                pl.BlockSpec(memory_space=pl.ANY),
                pl.BlockSpec(memory_space=pl.ANY),
                pl.BlockSpec((1, 1, D_EXPERT), wmap),
                pl.BlockSpec((1, 1, D_EXPERT), wmap),
                pl.BlockSpec((1, 1, D_MODEL), wmap),
                pl.BlockSpec((2 * LANES, 2 * LANES), lambda i, be, u: (0, 0)),
            ],
            out_specs=pl.BlockSpec((MOE_BLOCK, D_MODEL), lambda i, be, u: (i, 0)),
            scratch_shapes=[
                pltpu.VMEM((D_MODEL, 2 * D_EXPERT), F32),
                pltpu.VMEM((D_EXPERT, D_MODEL), F32),
                pltpu.VMEM((D_MODEL, D_EXPERT), BF16),
                pltpu.VMEM((D_MODEL, D_EXPERT), BF16),
                pltpu.VMEM((D_EXPERT, D_MODEL), BF16),
                pltpu.SemaphoreType.DMA((2,)),
            ],
        ),
        out_shape=jax.ShapeDtypeStruct((n_rows, D_MODEL), F32),
        compiler_params=pltpu.CompilerParams(
            dimension_semantics=("arbitrary",), vmem_limit_bytes=VMEM_LIMIT),
        name="experts",
    )(block_e, used, xs, w_gu, w_dn, b_g, b_l, b_d, perm)


def _combine_kernel(dest_ref, h_ref, gate_ref, g_ref, b_ref, ys_ref, o_ref, buf_ref, sem,
                    *, tm, alpha):
    i = pl.program_id(0)
    n = pl.num_programs(0)

    def issue(tile, slot):
        base = tile * (tm * TOP_K)

        def body(t, carry):
            for k in range(TOP_K):
                d = dest_ref[base + t * TOP_K + k]
                pltpu.make_async_copy(ys_ref.at[pl.ds(d, 1), :],
                                      buf_ref.at[slot, k, pl.ds(t, 1), :],
                                      sem.at[slot]).start(priority=k % 2)
            return carry

        lax.fori_loop(0, tm, body, 0, unroll=4)

    @pl.when(i == 0)
    def _():
        issue(0, 0)

    @pl.when(i + 1 < n)
    def _():
        issue(i + 1, (i + 1) % 2)

    slot = i % 2

    for k in range(TOP_K):
        pltpu.make_async_copy(ys_ref.at[pl.ds(0, tm), :], buf_ref.at[slot, k],
                              sem.at[slot]).wait()

    gate = gate_ref[...]
    col = lax.broadcasted_iota(I32, (tm, 8), 1)
    ffn = jnp.zeros((tm, D_MODEL), F32)
    for k in range(TOP_K):
        gk = jnp.sum(jnp.where(col == k, gate, 0.0), axis=-1, keepdims=True)
        ffn = ffn + gk * buf_ref[slot, k]
    o_ref[...] = _layer_norm(alpha * h_ref[...] + ffn, g_ref[...], b_ref[...])


def _combine(dest_flat, h, gate, g, b, ys, alpha):
    t = h.shape[0]
    tm = 256
    kern = functools.partial(_combine_kernel, tm=tm, alpha=alpha)
    return pl.pallas_call(
        kern,
        grid_spec=pltpu.PrefetchScalarGridSpec(
            num_scalar_prefetch=1,
            grid=(t // tm,),
            in_specs=[
                pl.BlockSpec((tm, D_MODEL), lambda i, d: (i, 0)),
                pl.BlockSpec((tm, 8), lambda i, d: (i, 0)),
                pl.BlockSpec((1, D_MODEL), lambda i, d: (0, 0)),
                pl.BlockSpec((1, D_MODEL), lambda i, d: (0, 0)),
                pl.BlockSpec(memory_space=pl.ANY),
            ],
            out_specs=pl.BlockSpec((tm, D_MODEL), lambda i, d: (i, 0)),
            scratch_shapes=[
                pltpu.VMEM((2, TOP_K, tm, D_MODEL), F32),
                pltpu.SemaphoreType.DMA((2,)),
            ],
        ),
        out_shape=jax.ShapeDtypeStruct((t, D_MODEL), F32),
        compiler_params=pltpu.CompilerParams(
            dimension_semantics=("arbitrary",), vmem_limit_bytes=VMEM_LIMIT),
        name="combine",
    )(dest_flat, h, gate, g, b, ys)


def _column_split_permutation():
    n = 2 * LANES
    p = np.zeros((n, n), np.float32)
    j = np.arange(n)
    p[j, j // 2 + LANES * (j % 2)] = 1.0
    return jnp.asarray(p, dtype=BF16)


def kernel(x, mem, w_in, lam_q1, lam_k1, lam_q2, lam_k2, diff_norm_w, hgrn_lb_fwd,
           hgrn_lb_bwd, hgrn_norm_w, w_mem_kv, w_o, ln1_g, ln1_b, router_w, router_b,
           w_gate_up, b_gate_up, w_down, b_down, ln2_g, ln2_b):
    batch, seq, d = x.shape
    mlen = mem.shape[1]
    assert d == D_MODEL and w_in.shape == (N_LAYERS, D_MODEL, D_IN_PROJ)
    assert hgrn_lb_fwd.shape == (N_LAYERS + 1, D_HGRN)
    t = batch * seq
    alpha = (2.0 * N_LAYERS) ** 0.25
    lam_init = 0.8 - 0.6 * math.exp(-0.3 * 0)

    x2d = x.reshape(t, D_MODEL)
    ua, uh, um = _in_proj(x2d, w_in[0].astype(BF16), hgrn_lb_fwd, hgrn_lb_bwd)
    o_diff = _diff_attn(ua, diff_norm_w, lam_q1, lam_k1, lam_q2, lam_k2,
                        batch, seq, lam_init)
    o_mem = _mem_attn(um, mem.reshape(batch * mlen, D_MODEL), w_mem_kv[0].astype(BF16),
                      batch, seq, mlen)
    o_hgrn = _hgrn2(uh, hgrn_norm_w, batch, seq)

    h1, eidx, gate, rank, cnt = _post_mixer(
        x2d, o_diff, o_hgrn, o_mem, w_o[0].astype(BF16), ln1_g, ln1_b, router_w[0],
        router_b, alpha)

    n_blocks = -(-(t * TOP_K) // MOE_BLOCK) + N_EXPERTS
    dest, meta = _route_meta(eidx, rank, cnt, n_blocks)
    dest_flat = dest[:, 0:TOP_K].reshape(t * TOP_K)
    block_e = meta[:, 0]
    used = meta[0:1, 1]

    xs = _dispatch(dest_flat, h1, jnp.zeros((n_blocks * MOE_BLOCK, D_MODEL), F32))
    b_g = b_gate_up[0][:, 0::2].reshape(N_EXPERTS, 1, D_EXPERT)
    b_l = b_gate_up[0][:, 1::2].reshape(N_EXPERTS, 1, D_EXPERT)
    ys = _experts(block_e, used, xs, w_gate_up[0], w_down[0], b_g, b_l,
                  b_down[0].reshape(N_EXPERTS, 1, D_MODEL), _column_split_permutation())
    out = _combine(dest_flat, h1, gate, ln2_g, ln2_b, ys, alpha)
    return out.reshape(batch, seq, D_MODEL)
```

```python
import functools
import math

import jax
import jax.numpy as jnp
import numpy as np
from jax import lax
from jax.experimental import pallas as pl
from jax.experimental.pallas import tpu as pltpu
from jax.experimental.pallas import tpu_sc as plsc

F32 = jnp.float32
BF16 = jnp.bfloat16
I32 = jnp.int32

D_MODEL = 1024
N_LAYERS = 1
D_DIFF = 512
D_HGRN = 256
D_XMEM = 256
N_HEADS = 4
DH = 64
DH_SHIFT = 6
D_IN_PROJ = 3 * D_DIFF + 5 * D_HGRN + D_XMEM
CHUNK = 16
N_EXPERTS = 32
TOP_K = 4
D_EXPERT = 1024
MOE_BLOCK = 256
SWIGLU_ALPHA = 1.702
SWIGLU_LIMIT = 7.0
NORM_EPS = 1e-5
LOG2E = math.log2(math.e)
LANES = 128
VMEM_LIMIT = 56 * 1024 * 1024


def _nt_dot(a, b):
    return lax.dot_general(a, b, (((1,), (1,)), ((), ())), preferred_element_type=F32)


def _tn_dot(a, b):
    return lax.dot_general(a, b, (((0,), (0,)), ((), ())), preferred_element_type=F32)


def _dot(a, b):
    return jnp.dot(a, b, preferred_element_type=F32)


def _sigmoid(x):
    return 1.0 / (1.0 + jnp.exp(-x))


def _layer_norm(y, g, b):
    mu = jnp.mean(y, axis=-1, keepdims=True)
    yc = y - mu
    var = jnp.mean(yc * yc, axis=-1, keepdims=True)
    return yc * lax.rsqrt(var + NORM_EPS) * g + b


def _in_proj_kernel(x_ref, w_ref, lbf_ref, lbb_ref, ua_ref, uh_ref, um_ref):
    xb = x_ref[...].astype(BF16)

    def proj(c0, width):
        return _dot(xb, w_ref[:, c0:c0 + width])

    ua_ref[:, 0:D_DIFF] = (proj(0, D_DIFF) * (LOG2E / math.sqrt(DH))).astype(BF16)
    ua_ref[:, D_DIFF:2 * D_DIFF] = proj(D_DIFF, D_DIFF).astype(BF16)
    ua_ref[:, 2 * D_DIFF:3 * D_DIFF] = proj(2 * D_DIFF, D_DIFF).astype(BF16)

    def lower_bound(lb_ref):
        a = lb_ref[0:1, :]
        b = lb_ref[1:2, :]
        m = jnp.maximum(a, b)
        ea = jnp.exp(a - m)
        eb = jnp.exp(b - m)
        return ea / (ea + eb)

    base = 3 * D_DIFF
    hq = proj(base, D_HGRN)
    uh_ref[:, 0:D_HGRN] = hq * _sigmoid(hq)
    for d, lb_ref in enumerate((lbf_ref, lbb_ref)):
        lb = lower_bound(lb_ref)
        f = lb + (1.0 - lb) * _sigmoid(proj(base + (1 + d) * D_HGRN, D_HGRN))
        uh_ref[:, (1 + 2 * d) * D_HGRN:(2 + 2 * d) * D_HGRN] = 1.0 - f
        uh_ref[:, (2 + 2 * d) * D_HGRN:(3 + 2 * d) * D_HGRN] = jnp.log(f)
    uh_ref[:, 5 * D_HGRN:6 * D_HGRN] = proj(base + 3 * D_HGRN, D_HGRN)
    uh_ref[:, 6 * D_HGRN:7 * D_HGRN] = _sigmoid(proj(base + 4 * D_HGRN, D_HGRN))
    um_ref[...] = proj(base + 5 * D_HGRN, D_XMEM).astype(BF16)


def _in_proj(x2d, w_in_bf, lb_f, lb_b):
    t = x2d.shape[0]
    tm = 256
    return pl.pallas_call(
        _in_proj_kernel,
        grid=(t // tm,),
        in_specs=[
            pl.BlockSpec((tm, D_MODEL), lambda i: (i, 0)),
            pl.BlockSpec((D_MODEL, D_IN_PROJ), lambda i: (0, 0)),
            pl.BlockSpec((N_LAYERS + 1, D_HGRN), lambda i: (0, 0)),
            pl.BlockSpec((N_LAYERS + 1, D_HGRN), lambda i: (0, 0)),
        ],
        out_specs=[
            pl.BlockSpec((tm, 3 * D_DIFF), lambda i: (i, 0)),
            pl.BlockSpec((tm, 7 * D_HGRN), lambda i: (i, 0)),
            pl.BlockSpec((tm, D_XMEM), lambda i: (i, 0)),
        ],
        out_shape=[
            jax.ShapeDtypeStruct((t, 3 * D_DIFF), BF16),
            jax.ShapeDtypeStruct((t, 7 * D_HGRN), F32),
            jax.ShapeDtypeStruct((t, D_XMEM), BF16),
        ],
        compiler_params=pltpu.CompilerParams(
            dimension_semantics=("arbitrary",), vmem_limit_bytes=VMEM_LIMIT),
        name="in_proj",
    )(x2d, w_in_bf, lb_f, lb_b)


N_POS_FEATURES = 12


def _alibi_feature_rows(c3, base):
    f = lax.broadcasted_iota(I32, (1, 2 * DH), 1) - base

    def pieces(f0):
        return jnp.where(f == f0, c3[0], jnp.where(f == f0 + 1, c3[1],
                                                   jnp.where(f == f0 + 2, c3[2], 0.0)))

    span = lambda lo: jnp.where((f >= lo) & (f < lo + 3), 1.0, 0.0)
    q_const = 64.0 * pieces(6) + pieces(9)
    k_const = -64.0 * pieces(0) - pieces(3)
    return (q_const, span(0), span(3)), (k_const, span(6), span(9))


def _position_features(rows, pos0, n):
    const, hi_mask, lo_mask = rows
    pos = pos0 + lax.broadcasted_iota(I32, (n, 2 * DH), 0)
    hi = (pos >> DH_SHIFT).astype(F32)
    lo = (pos & (DH - 1)).astype(F32)
    return const + hi_mask * hi + lo_mask * lo


def _diff_attn_kernel(c_ref, q_ref, k_ref, v_ref, nw_ref, lq1_ref, lk1_ref, lq2_ref,
                      lk2_ref, o_ref, ka1_ref, ka2_ref, corr_ref, vt_ref, acc_ref, m_ref,
                      l_ref, *, t, qs, seq, lam_init):
    h = pl.program_id(1)
    i = pl.program_id(2)
    nk = seq // t
    c3 = (c_ref[h, 0], c_ref[h, 1], c_ref[h, 2])
    lane = lax.broadcasted_iota(I32, (t, 2 * DH), 1)
    first_half = lane < DH
    q_rows1, k_rows1 = _alibi_feature_rows(c3, DH)
    q_rows2, k_rows2 = _alibi_feature_rows(c3, 0)

    @pl.when(i == 0)
    def _():
        def build(r, carry):
            r0 = pl.multiple_of(r * t, t)
            rows = pl.ds(r0, t)
            kblk = k_ref[rows, :]
            ka1_ref[rows, :] = jnp.where(first_half, kblk,
                                         _position_features(k_rows1, r0, t).astype(BF16))
            ka2_ref[rows, :] = jnp.where(first_half,
                                         _position_features(k_rows2, r0, t).astype(BF16), kblk)
            vt_ref[r] = v_ref[rows, :].astype(F32).T.astype(BF16)
            return carry

        lax.fori_loop(0, nk, build, 0)
        kk = lax.broadcasted_iota(I32, (t, t), 0)
        qq = lax.broadcasted_iota(I32, (t, t), 1)
        c = c3[0] + c3[1] + c3[2]
        corr_ref[...] = (2.0 * c) * jnp.minimum(qq - kk, 0).astype(F32)

    q = q_ref[...]
    qf1 = _position_features(q_rows1, i * t, t)
    qf2 = _position_features(q_rows2, i * t, t)
    qa1_before = jnp.where(first_half, q, qf1.astype(BF16))
    qa1_after = jnp.where(first_half, q, (-qf1).astype(BF16))
    qa2_before = jnp.where(first_half, qf2.astype(BF16), q)
    qa2_after = jnp.where(first_half, (-qf2).astype(BF16), q)

    m_ref[...] = jnp.full(m_ref.shape, -jnp.inf, F32)
    l_ref[...] = jnp.zeros(l_ref.shape, F32)
    acc_ref[...] = jnp.zeros(acc_ref.shape, F32)

    def chunk(j, qa1, qa2, diagonal):
        rows = pl.ds(pl.multiple_of(j * t, t), t)
        vt = vt_ref[j]
        for mp, (qa, ka_ref) in enumerate(((qa1, ka1_ref), (qa2, ka2_ref))):
            ka = ka_ref[rows, :]
            for u in range(t // qs):
                qrows = slice(u * qs, (u + 1) * qs)
                cols = slice(mp * t + u * qs, mp * t + (u + 1) * qs)
                s = _nt_dot(ka, qa[qrows, :])
                if diagonal:
                    s = s + corr_ref[:, qrows]
                m_old = m_ref[:, cols]
                m_new = jnp.maximum(m_old, jnp.max(s, axis=0, keepdims=True))
                p = jnp.exp2(s - m_new)
                r = jnp.exp2(m_old - m_new)
                l_ref[:, cols] = r * l_ref[:, cols] + jnp.sum(p, axis=0, keepdims=True)
                acc_ref[mp, :, qrows] = (r * acc_ref[mp, :, qrows]
                                         + _dot(vt, p.astype(BF16)))
                m_ref[:, cols] = m_new

    chunk(i, qa1_before, qa2_before, True)
    for jj in range(nk - 1):
        j = jj + (jj >= i).astype(I32)
        keys_first = j < i
        chunk(j, jnp.where(keys_first, qa1_before, qa1_after),
              jnp.where(keys_first, qa2_before, qa2_after), False)

    lam = (jnp.exp(jnp.sum(lq1_ref[...] * lk1_ref[...], axis=-1, keepdims=True))
           - jnp.exp(jnp.sum(lq2_ref[...] * lk2_ref[...], axis=-1, keepdims=True))
           + lam_init)
    o = acc_ref[0] / l_ref[:, 0:t] - lam * (acc_ref[1] / l_ref[:, t:2 * t])
    o = o * lax.rsqrt(jnp.mean(o * o, axis=0, keepdims=True) + NORM_EPS)
    o_ref[...] = (o.T * nw_ref[...] * (1.0 - lam_init)).astype(o_ref.dtype)


def _bf16_pieces(x):
    x = np.asarray(x, np.float32)
    out = []
    for _ in range(3):
        p = x.astype(BF16).astype(np.float32)
        out.append(p)
        x = x - p
    return np.stack(out, axis=-1)


def _diff_attn(ua, nw, lq1, lk1, lq2, lk2, batch, seq, lam_init):
    t = 512
    nq = seq // t
    slopes = 2.0 ** (-8.0 * np.arange(1, N_HEADS + 1) / N_HEADS)
    c_pieces = jnp.asarray(_bf16_pieces(slopes.astype(np.float32) * np.float32(LOG2E)))
    kern = functools.partial(_diff_attn_kernel, t=t, qs=LANES, seq=seq, lam_init=lam_init)
    small = lambda b, h, i, s: (0, 0)
    return pl.pallas_call(
        kern,
        grid_spec=pltpu.PrefetchScalarGridSpec(
            num_scalar_prefetch=1,
            grid=(batch, N_HEADS, nq),
            in_specs=[
                pl.BlockSpec((t, 2 * DH), lambda b, h, i, s: (b * nq + i, h)),
                pl.BlockSpec((seq, 2 * DH), lambda b, h, i, s: (b, N_HEADS + h)),
                pl.BlockSpec((seq, 2 * DH), lambda b, h, i, s: (b, 2 * N_HEADS + h)),
                pl.BlockSpec((1, 2 * DH), small),
                pl.BlockSpec((1, DH), small),
                pl.BlockSpec((1, DH), small),
                pl.BlockSpec((1, DH), small),
                pl.BlockSpec((1, DH), small),
            ],
            out_specs=pl.BlockSpec((t, 2 * DH), lambda b, h, i, s: (b * nq + i, h)),
            scratch_shapes=[
                pltpu.VMEM((seq, 2 * DH), BF16),
                pltpu.VMEM((seq, 2 * DH), BF16),
                pltpu.VMEM((t, t), F32),
                pltpu.VMEM((nq, 2 * DH, t), BF16),
                pltpu.VMEM((2, 2 * DH, t), F32),
                pltpu.VMEM((1, 2 * t), F32),
                pltpu.VMEM((1, 2 * t), F32),
            ],
        ),
        out_shape=jax.ShapeDtypeStruct((batch * seq, D_DIFF), BF16),
        compiler_params=pltpu.CompilerParams(
            dimension_semantics=("arbitrary", "arbitrary", "arbitrary"),
            vmem_limit_bytes=VMEM_LIMIT),
        name="diff_attn",
    )(c_pieces, ua, ua, ua, nw, lq1, lk1, lq2, lk2)


def _mem_attn_kernel(q_ref, mem_ref, wkv_ref, o_ref, kv_ref, *, tq, mlen):
    @pl.when(pl.program_id(1) == 0)
    def _():
        kv_ref[...] = _dot(mem_ref[...].astype(BF16), wkv_ref[...]).astype(BF16)

    q = q_ref[...]
    mk = kv_ref[:, 0:D_XMEM]
    mv = kv_ref[:, D_XMEM:2 * D_XMEM]
    qhead = lax.broadcasted_iota(I32, (tq, D_XMEM), 1) >> DH_SHIFT
    vhead = lax.broadcasted_iota(I32, (mlen, D_XMEM), 1) >> DH_SHIFT
    acc = jnp.zeros((tq, D_XMEM), F32)
    for h in range(N_HEADS):
        qh = jnp.where(qhead == h, q, jnp.zeros_like(q))
        s = _nt_dot(qh, mk) * (1.0 / math.sqrt(DH))
        e = jnp.exp(s - jnp.max(s, axis=-1, keepdims=True))
        p = e / jnp.sum(e, axis=-1, keepdims=True)
        vh = jnp.where(vhead == h, mv, jnp.zeros_like(mv))
        acc = acc + _dot(p.astype(BF16), vh)
    o_ref[...] = acc.astype(o_ref.dtype)


def _mem_attn(um, mem2d, wkv_bf, batch, seq, mlen):
    tq = 512
    nq = seq // tq
    kern = functools.partial(_mem_attn_kernel, tq=tq, mlen=mlen)
    return pl.pallas_call(
        kern,
        grid=(batch, nq),
        in_specs=[
            pl.BlockSpec((tq, D_XMEM), lambda b, i: (b * nq + i, 0)),
            pl.BlockSpec((mlen, D_MODEL), lambda b, i: (b, 0)),
            pl.BlockSpec((D_MODEL, 2 * D_XMEM), lambda b, i: (0, 0)),
        ],
        out_specs=pl.BlockSpec((tq, D_XMEM), lambda b, i: (b * nq + i, 0)),
        out_shape=jax.ShapeDtypeStruct((batch * seq, D_XMEM), BF16),
        scratch_shapes=[pltpu.VMEM((mlen, 2 * D_XMEM), BF16)],
        compiler_params=pltpu.CompilerParams(
            dimension_semantics=("arbitrary", "arbitrary"), vmem_limit_bytes=VMEM_LIMIT),
        name="mem_attn",
    )(um, mem2d, wkv_bf)


def _hgrn2_kernel(q_ref, kf_ref, lf_ref, kb_ref, lb_ref, v_ref, g_ref, nw_ref, o_ref,
                  kfp_ref, kbp_ref, vp_ref, bf_ref, bb_ref, acc_ref, cross_ref, *, seq, rb):
    n_chunks = seq // CHUNK
    w = 2 * DH
    row = lax.broadcasted_iota(I32, (rb, w), 0) & (CHUNK - 1)
    li = lax.broadcasted_iota(I32, (w, w), 0) >> DH_SHIFT
    lj = lax.broadcasted_iota(I32, (w, w), 1) >> DH_SHIFT
    same_head = li == lj
    head_ones = jnp.where(same_head, 1.0, 0.0).astype(BF16)

    def chunk_cumsum(x, reverse):
        for sh in (1, 2, 4, 8):
            if reverse:
                moved = pltpu.roll(x, rb - sh, 0)
                keep = row + sh < CHUNK
            else:
                moved = pltpu.roll(x, sh, 0)
                keep = row >= sh
            x = x + jnp.where(keep, moved, 0.0)
        return x

    halo = jnp.zeros((CHUNK, w), F32)
    for ref in (kfp_ref, kbp_ref, vp_ref, bf_ref, bb_ref):
        ref[0:CHUNK, :] = halo
        ref[seq + CHUNK:seq + 2 * CHUNK, :] = halo

    def prepare(blk, carry):
        r0 = pl.multiple_of(blk * rb, rb)
        rows = pl.ds(r0, rb)
        inner = pl.ds(r0 + CHUNK, rb)
        kfp_ref[inner, :] = kf_ref[rows, :]
        kbp_ref[inner, :] = kb_ref[rows, :]
        vp_ref[inner, :] = v_ref[rows, :]
        bf_ref[inner, :] = chunk_cumsum(lf_ref[rows, :] * LOG2E, False)
        bb_ref[inner, :] = chunk_cumsum(lb_ref[rows, :] * LOG2E, True)
        return carry

    lax.fori_loop(0, seq // rb, prepare, 0)

    def intra(blk):
        r0 = pl.multiple_of(blk * rb, rb)
        rows = pl.ds(r0, rb)
        q = q_ref[rows, :]
        acc = jnp.zeros((rb, w), F32)
        for reverse, kp_ref, b_ref in ((False, kfp_ref, bf_ref), (True, kbp_ref, bb_ref)):
            b = b_ref[pl.ds(r0 + CHUNK, rb), :]
            for d in range(CHUNK):
                src = pl.ds(r0 + CHUNK + (d if reverse else -d), rb)
                ks = kp_ref[src, :]
                vs = vp_ref[src, :]
                if d == 0:
                    z = q * ks
                else:
                    keep = (row + d < CHUNK) if reverse else (row >= d)
                    dec = jnp.exp2(jnp.where(keep, b - b_ref[src, :], -jnp.inf))
                    z = q * ks * dec
                acc = acc + _dot(z.astype(BF16), head_ones) * vs
        acc_ref[rows, :] = acc

    def inter(n, carry):
        wf, wb = carry
        rf = pl.multiple_of(n * CHUNK, CHUNK)
        rr = pl.multiple_of((n_chunks - 1 - n) * CHUNK, CHUNK)

        def one(state, r0, kp_ref, b_ref, edge):
            rows = pl.ds(r0, CHUNK)
            prow = pl.ds(r0 + CHUNK, CHUNK)
            b = b_ref[prow, :]
            b_edge = b_ref[pl.ds(r0 + CHUNK + edge, 1), :]
            qd = q_ref[rows, :] * jnp.exp2(b)
            cross_ref[rows, :] = cross_ref[rows, :] + _nt_dot(qd.astype(BF16),
                                                              state.astype(BF16))
            kd = kp_ref[prow, :] * jnp.exp2(b_edge - b)
            upd = _tn_dot(vp_ref[prow, :].astype(BF16), kd.astype(BF16))
            return state * jnp.exp2(b_edge) + jnp.where(same_head, upd, 0.0)

        wf = one(wf, rf, kfp_ref, bf_ref, CHUNK - 1)
        wb = one(wb, rr, kbp_ref, bb_ref, 0)
        return wf, wb

    cross_ref[...] = jnp.zeros(cross_ref.shape, F32)

    def fused(blk, carry):
        intra(blk)
        for c in range(rb // CHUNK):
            carry = inter(blk * (rb // CHUNK) + c, carry)
        return carry

    z = jnp.zeros((w, w), F32)
    lax.fori_loop(0, seq // rb, fused, (z, z))

    def finish(blk, carry):
        rows = pl.ds(pl.multiple_of(blk * rb, rb), rb)
        o = acc_ref[rows, :] + cross_ref[rows, :]
        lane = lax.broadcasted_iota(I32, (rb, w), 1)
        lo = lane < DH
        sq = o * o
        ms_lo = jnp.sum(jnp.where(lo, sq, 0.0), axis=-1, keepdims=True)
        ms_hi = jnp.sum(jnp.where(lo, 0.0, sq), axis=-1, keepdims=True)
        ms = jnp.where(lo, ms_lo, ms_hi) * (1.0 / DH)
        o = o * lax.rsqrt(ms + NORM_EPS) * nw_ref[...] * g_ref[rows, :]
        o_ref[rows, :] = o.astype(o_ref.dtype)
        return carry

    lax.fori_loop(0, seq // rb, finish, 0)


def _hgrn2(uh, nw, batch, seq):
    w = 2 * DH
    npair = D_HGRN // w
    kern = functools.partial(_hgrn2_kernel, seq=seq, rb=128)

    def sec(s):
        return pl.BlockSpec((seq, w), lambda b, p, s=s: (b, s * npair + p))

    return pl.pallas_call(
        kern,
        grid=(batch, npair),
        in_specs=[sec(0), sec(1), sec(2), sec(3), sec(4), sec(5), sec(6),
                  pl.BlockSpec((1, w), lambda b, p: (0, p))],
        out_specs=pl.BlockSpec((seq, w), lambda b, p: (b, p)),
        out_shape=jax.ShapeDtypeStruct((batch * seq, D_HGRN), BF16),
        scratch_shapes=[pltpu.VMEM((seq + 2 * CHUNK, w), F32)] * 5
        + [pltpu.VMEM((seq, w), F32)] * 2,
        compiler_params=pltpu.CompilerParams(
            dimension_semantics=("arbitrary", "arbitrary"), vmem_limit_bytes=VMEM_LIMIT),
        name="hgrn2",
    )(uh, uh, uh, uh, uh, uh, uh, nw)


def _post_mixer_kernel(x_ref, od_ref, oh_ref, om_ref, wo_ref, g_ref, b_ref, rw_ref,
                       rb_ref, h_ref, eidx_ref, gate_ref, rank_ref, cnt_ref, carry_ref,
                       *, tm, alpha):
    i = pl.program_id(0)

    @pl.when(i == 0)
    def _():
        carry_ref[...] = jnp.zeros_like(carry_ref)

    mix = (_dot(od_ref[...], wo_ref[0:D_DIFF, :])
           + _dot(oh_ref[...], wo_ref[D_DIFF:D_DIFF + D_HGRN, :])
           + _dot(om_ref[...], wo_ref[D_DIFF + D_HGRN:, :]))
    h = _layer_norm(alpha * x_ref[...] + mix, g_ref[...], b_ref[...])
    h_ref[...] = h

    h_hi = h.astype(BF16)
    h_lo = (h - h_hi.astype(F32)).astype(BF16)
    rw = rw_ref[...]
    rw_hi = rw.astype(BF16)
    rw_lo = (rw - rw_hi.astype(F32)).astype(BF16)
    logits = _dot(h_hi, rw_hi) + _dot(h_lo, rw_hi) + _dot(h_hi, rw_lo) + rb_ref[...]

    lane = lax.broadcasted_iota(I32, (tm, N_EXPERTS), 1).astype(F32)
    work = logits
    sels, vals, idxs = [], [], []
    for _ in range(TOP_K):
        m = jnp.max(work, axis=-1, keepdims=True)
        idx = jnp.min(jnp.where(work == m, lane, float(N_EXPERTS)), axis=-1, keepdims=True)
        sel = lane == idx
        work = jnp.where(sel, -jnp.inf, work)
        sels.append(sel)
        vals.append(m)
        idxs.append(idx)
    es = [jnp.exp(v - vals[0]) for v in vals]
    den = es[0] + es[1] + es[2] + es[3]

    chosen = jnp.where(sels[0] | sels[1] | sels[2] | sels[3], 1.0, 0.0)
    ti = lax.broadcasted_iota(I32, (tm, tm), 0)
    tj = lax.broadcasted_iota(I32, (tm, tm), 1)
    earlier = jnp.where(tj < ti, 1.0, 0.0).astype(BF16)
    prefix = _dot(earlier, chosen.astype(BF16)) + carry_ref[0:1, 0:N_EXPERTS]

    col = lax.broadcasted_iota(I32, (tm, 8), 1)
    eidx = jnp.zeros((tm, 8), F32)
    gate = jnp.zeros((tm, 8), F32)
    rank = jnp.zeros((tm, 8), F32)
    for k in range(TOP_K):
        rk = jnp.sum(jnp.where(sels[k], prefix, 0.0), axis=-1, keepdims=True)
        eidx = jnp.where(col == k, idxs[k], eidx)
        gate = jnp.where(col == k, es[k] / den, gate)
        rank = jnp.where(col == k, rk, rank)
    eidx_ref[...] = eidx.astype(I32)
    gate_ref[...] = gate
    rank_ref[...] = rank.astype(I32)

    total = carry_ref[0:1, 0:N_EXPERTS] + jnp.sum(chosen, axis=0, keepdims=True)
    carry_ref[0:1, 0:N_EXPERTS] = total
    cnt_ref[...] = carry_ref[...]


def _post_mixer(x2d, od, oh, om, wo_bf, g, b, rw, rb, alpha):
    t = x2d.shape[0]
    tm = 256
    kern = functools.partial(_post_mixer_kernel, tm=tm, alpha=alpha)
    full = lambda i: (0, 0)
    tok = lambda i: (i, 0)
    return pl.pallas_call(
        kern,
        grid=(t // tm,),
        in_specs=[
            pl.BlockSpec((tm, D_MODEL), tok),
            pl.BlockSpec((tm, D_DIFF), tok),
            pl.BlockSpec((tm, D_HGRN), tok),
            pl.BlockSpec((tm, D_XMEM), tok),
            pl.BlockSpec((D_MODEL, D_MODEL), full),
            pl.BlockSpec((1, D_MODEL), full),
            pl.BlockSpec((1, D_MODEL), full),
            pl.BlockSpec((D_MODEL, N_EXPERTS), full),
            pl.BlockSpec((1, N_EXPERTS), full),
        ],
        out_specs=[
            pl.BlockSpec((tm, D_MODEL), tok),
            pl.BlockSpec((tm, 8), tok),
            pl.BlockSpec((tm, 8), tok),
            pl.BlockSpec((tm, 8), tok),
            pl.BlockSpec((8, LANES), full),
        ],
        out_shape=[
            jax.ShapeDtypeStruct((t, D_MODEL), F32),
            jax.ShapeDtypeStruct((t, 8), I32),
            jax.ShapeDtypeStruct((t, 8), F32),
            jax.ShapeDtypeStruct((t, 8), I32),
            jax.ShapeDtypeStruct((8, LANES), F32),
        ],
        scratch_shapes=[pltpu.VMEM((8, LANES), F32)],
        compiler_params=pltpu.CompilerParams(
            dimension_semantics=("arbitrary",), vmem_limit_bytes=VMEM_LIMIT),
        name="post_mixer",
    )(x2d, od, oh, om, wo_bf, g, b, rw, rb)


def _route_meta_kernel(eidx_ref, rank_ref, cnt_ref, dest_ref, be_ref, *, tm, n_blocks):
    cnt = cnt_ref[0:1, :]
    blocks = jnp.floor((cnt + (MOE_BLOCK - 1)) * (1.0 / MOE_BLOCK))
    ei = lax.broadcasted_iota(I32, (LANES, LANES), 0)
    ej = lax.broadcasted_iota(I32, (LANES, LANES), 1)
    before = jnp.where(ei < ej, 1.0, 0.0).astype(BF16)
    b8 = jnp.broadcast_to(blocks, (8, LANES)).astype(BF16)
    pstart = _dot(b8, before)[0:1, :]
    pend = pstart + blocks

    lane = lax.broadcasted_iota(I32, (tm, LANES), 1).astype(F32)
    col = lax.broadcasted_iota(I32, (tm, 8), 1)
    eidx = eidx_ref[...].astype(F32)
    dest = jnp.zeros((tm, 8), F32)
    for k in range(TOP_K):
        ek = jnp.sum(jnp.where(col == k, eidx, 0.0), axis=-1, keepdims=True)
        start = jnp.sum(jnp.where(lane == ek, pstart, 0.0), axis=-1, keepdims=True)
        dest = jnp.where(col == k, start * float(MOE_BLOCK), dest)
    dest_ref[...] = dest.astype(I32) + rank_ref[...]

    bi = lax.broadcasted_iota(I32, (n_blocks, LANES), 0).astype(F32)
    bl = lax.broadcasted_iota(I32, (n_blocks, LANES), 1)
    done = jnp.where((bl < N_EXPERTS) & (pend <= bi), 1.0, 0.0)
    be = jnp.minimum(jnp.sum(done, axis=-1, keepdims=True), float(N_EXPERTS - 1))
    used = jnp.sum(jnp.where(bl[0:1, :] == N_EXPERTS - 1, pend, 0.0), axis=-1, keepdims=True)
    bcol = lax.broadcasted_iota(I32, (n_blocks, 8), 1)
    be_ref[...] = jnp.where(bcol == 0, be, jnp.where(bcol == 1, used, 0.0)).astype(I32)


def _route_meta(eidx, rank, cnt, n_blocks):
    t = eidx.shape[0]
    tm = 2048
    kern = functools.partial(_route_meta_kernel, tm=tm, n_blocks=n_blocks)
    return pl.pallas_call(
        kern,
        grid=(t // tm,),
        in_specs=[
            pl.BlockSpec((tm, 8), lambda i: (i, 0)),
            pl.BlockSpec((tm, 8), lambda i: (i, 0)),
            pl.BlockSpec((8, LANES), lambda i: (0, 0)),
        ],
        out_specs=[
            pl.BlockSpec((tm, 8), lambda i: (i, 0)),
            pl.BlockSpec((n_blocks, 8), lambda i: (0, 0)),
        ],
        out_shape=[
            jax.ShapeDtypeStruct((t, 8), I32),
            jax.ShapeDtypeStruct((n_blocks, 8), I32),
        ],
        compiler_params=pltpu.CompilerParams(
            dimension_semantics=("arbitrary",), vmem_limit_bytes=VMEM_LIMIT),
        name="route_meta",
    )(eidx, rank, cnt)


SC_WINDOW = LANES
N_QUARTERS = 4
QUARTER = D_MODEL // N_QUARTERS


def _sc_mesh():
    return plsc.VectorSubcoreMesh(core_axis_name="c", subcore_axis_name="s")


def _dispatch(dest_km, h, n_rows):
    t = h.shape[0]

    @functools.partial(
        pl.kernel, out_type=[jax.ShapeDtypeStruct((n_rows, QUARTER), h.dtype)] * N_QUARTERS,
        mesh=_sc_mesh(), scratch_types=[])
    def scatter_rows(h_hbm, d_hbm, *xs_hbm):
        for q in range(N_QUARTERS):
            def body(x_vmem, i_vmem, q=q):
                for k in range(TOP_K):
                    pltpu.sync_copy(x_vmem, xs_hbm[q].at[i_vmem.at[k]])

            pltpu.emit_pipeline(
                body,
                grid=(t // SC_WINDOW,),
                in_specs=[pl.BlockSpec((SC_WINDOW, QUARTER), lambda i, q=q: (i, q)),
                          pl.BlockSpec((TOP_K, SC_WINDOW), lambda i: (0, i))],
                out_specs=[],
                core_axis_name=("c", "s"),
                dimension_semantics=(pltpu.PARALLEL,),
            )(h_hbm, d_hbm)

    return scatter_rows(h, dest_km)


def _gather_back(dest_km, ys):
    t = dest_km.shape[1]

    @functools.partial(
        pl.kernel,
        out_type=[jax.ShapeDtypeStruct((t, TOP_K * QUARTER), ys[0].dtype)] * N_QUARTERS,
        mesh=_sc_mesh(), scratch_types=[])
    def gather_rows(d_hbm, *refs):
        ys_hbm, out_hbm = refs[:N_QUARTERS], refs[N_QUARTERS:]
        for q in range(N_QUARTERS):
            for k in range(TOP_K):
                def body(i_vmem, o_vmem, q=q, k=k):
                    pltpu.sync_copy(ys_hbm[q].at[i_vmem.at[k]], o_vmem)

                pltpu.emit_pipeline(
                    body,
                    grid=(t // SC_WINDOW,),
                    in_specs=[pl.BlockSpec((TOP_K, SC_WINDOW), lambda i: (0, i))],
                    out_specs=[pl.BlockSpec((SC_WINDOW, QUARTER), lambda i, k=k: (i, k))],
                    core_axis_name=("c", "s"),
                    dimension_semantics=(pltpu.PARALLEL,),
                )(d_hbm, out_hbm[q])

    return gather_rows(dest_km, *ys)


def _experts_kernel(be_ref, used_ref, *refs):
    xs_refs = refs[:N_QUARTERS]
    wgu_hbm, wdn_hbm, bg_ref, bl_ref, bd_ref, perm_ref = refs[N_QUARTERS:N_QUARTERS + 6]
    ys_refs = refs[N_QUARTERS + 6:2 * N_QUARTERS + 6]
    wgu_st, wdn_st, wg_ref, wl_ref, wd_ref, sem = refs[2 * N_QUARTERS + 6:]
    i = pl.program_id(0)
    used = used_ref[0]
    e = be_ref[i]
    prev = be_ref[jnp.maximum(i - 1, 0)]
    fresh = jnp.logical_and(i < used, jnp.logical_or(i == 0, e != prev))

    def weight_copies(ex):
        return (pltpu.make_async_copy(wgu_hbm.at[ex], wgu_st, sem.at[0]),
                pltpu.make_async_copy(wdn_hbm.at[ex], wdn_st, sem.at[1]))

    @pl.when(jnp.logical_and(i == 0, used > 0))
    def _():
        for cp in weight_copies(e):
            cp.start()

    @pl.when(fresh)
    def _():
        for cp in weight_copies(e):
            cp.wait()
        perm = perm_ref[...]
        half = LANES
        for c in range(2 * D_EXPERT // (2 * half)):
            slab = wgu_st[:, c * 2 * half:(c + 1) * 2 * half].astype(BF16)
            sep = _dot(slab, perm)
            wg_ref[:, c * half:(c + 1) * half] = sep[:, 0:half].astype(BF16)
            wl_ref[:, c * half:(c + 1) * half] = sep[:, half:2 * half].astype(BF16)
        wd_ref[...] = wdn_st[...].astype(BF16)

        def same_expert(j):
            return jnp.logical_and(j < used, be_ref[jnp.minimum(j, used - 1)] == e)

        nxt = lax.while_loop(same_expert, lambda j: j + 1, i + 1)

        @pl.when(nxt < used)
        def _():
            for cp in weight_copies(be_ref[nxt]):
                cp.start()

    @pl.when(i < used_ref[0])
    def _():
        xb = jnp.concatenate([r[...] for r in xs_refs], axis=1).astype(BF16)
        glu = jnp.minimum(_dot(xb, wg_ref[...]) + bg_ref[0], SWIGLU_LIMIT)
        lin = jnp.clip(_dot(xb, wl_ref[...]) + bl_ref[0], -SWIGLU_LIMIT, SWIGLU_LIMIT)
        act = glu * _sigmoid(SWIGLU_ALPHA * glu) * (lin + 1.0)
        y = _dot(act.astype(BF16), wd_ref[...]) + bd_ref[0]
        for q, ys_ref in enumerate(ys_refs):
            ys_ref[...] = y[:, q * QUARTER:(q + 1) * QUARTER]

    @pl.when(i >= used_ref[0])
    def _():
        for ys_ref in ys_refs:
            ys_ref[...] = jnp.zeros_like(ys_ref)


def _experts(block_e, used, xs, w_gu, w_dn, b_g, b_l, b_d, perm):
    n_rows = xs[0].shape[0]
    n_blocks = n_rows // MOE_BLOCK
    wmap = lambda i, be, u: (be[i], 0, 0)
    rows = pl.BlockSpec((MOE_BLOCK, QUARTER), lambda i, be, u: (i, 0))
    return pl.pallas_call(
        _experts_kernel,
        grid_spec=pltpu.PrefetchScalarGridSpec(
            num_scalar_prefetch=2,
            grid=(n_blocks,),
            in_specs=[rows] * N_QUARTERS + [
                pl.BlockSpec(memory_space=pl.ANY),
                pl.BlockSpec(memory_space=pl.ANY),
                pl.BlockSpec((1, 1, D_EXPERT), wmap),
                pl.BlockSpec((1, 1, D_EXPERT), wmap),
                pl.BlockSpec((1, 1, D_MODEL), wmap),
                pl.BlockSpec((2 * LANES, 2 * LANES), lambda i, be, u: (0, 0)),
            ],
            out_specs=[rows] * N_QUARTERS,
            scratch_shapes=[
                pltpu.VMEM((D_MODEL, 2 * D_EXPERT), F32),
                pltpu.VMEM((D_EXPERT, D_MODEL), F32),
                pltpu.VMEM((D_MODEL, D_EXPERT), BF16),
                pltpu.VMEM((D_MODEL, D_EXPERT), BF16),
                pltpu.VMEM((D_EXPERT, D_MODEL), BF16),
                pltpu.SemaphoreType.DMA((2,)),
            ],
        ),
        out_shape=[jax.ShapeDtypeStruct((n_rows, QUARTER), F32)] * N_QUARTERS,
        compiler_params=pltpu.CompilerParams(
            dimension_semantics=("arbitrary",), vmem_limit_bytes=VMEM_LIMIT),
        name="experts",
    )(block_e, used, *xs, w_gu, w_dn, b_g, b_l, b_d, perm)


def _combine_kernel(h_ref, gate_ref, g_ref, b_ref, *refs, tm, alpha):
    yg_refs, o_ref = refs[:N_QUARTERS], refs[N_QUARTERS]
    gate = gate_ref[...]
    col = lax.broadcasted_iota(I32, (tm, 8), 1)
    gks = [jnp.sum(jnp.where(col == k, gate, 0.0), axis=-1, keepdims=True)
           for k in range(TOP_K)]
    parts = []
    for yg_ref in yg_refs:
        part = gks[0] * yg_ref[:, 0:QUARTER]
        for k in range(1, TOP_K):
            part = part + gks[k] * yg_ref[:, k * QUARTER:(k + 1) * QUARTER]
        parts.append(part)
    ffn = jnp.concatenate(parts, axis=1)
    o_ref[...] = _layer_norm(alpha * h_ref[...] + ffn, g_ref[...], b_ref[...])


def _combine(h, gate, g, b, yg, alpha):
    t = h.shape[0]
    tm = 256
    kern = functools.partial(_combine_kernel, tm=tm, alpha=alpha)
    tok = lambda i: (i, 0)
    return pl.pallas_call(
        kern,
        grid=(t // tm,),
        in_specs=[
            pl.BlockSpec((tm, D_MODEL), tok),
            pl.BlockSpec((tm, 8), tok),
            pl.BlockSpec((1, D_MODEL), lambda i: (0, 0)),
            pl.BlockSpec((1, D_MODEL), lambda i: (0, 0)),
        ] + [pl.BlockSpec((tm, TOP_K * QUARTER), tok)] * N_QUARTERS,
        out_specs=pl.BlockSpec((tm, D_MODEL), tok),
        out_shape=jax.ShapeDtypeStruct((t, D_MODEL), F32),
        compiler_params=pltpu.CompilerParams(
            dimension_semantics=("arbitrary",), vmem_limit_bytes=VMEM_LIMIT),
        name="combine",
    )(h, gate, g, b, *yg)


def _column_split_permutation():
    n = 2 * LANES
    p = np.zeros((n, n), np.float32)
    j = np.arange(n)
    p[j, j // 2 + LANES * (j % 2)] = 1.0
    return jnp.asarray(p, dtype=BF16)


def kernel(x, mem, w_in, lam_q1, lam_k1, lam_q2, lam_k2, diff_norm_w, hgrn_lb_fwd,
           hgrn_lb_bwd, hgrn_norm_w, w_mem_kv, w_o, ln1_g, ln1_b, router_w, router_b,
           w_gate_up, b_gate_up, w_down, b_down, ln2_g, ln2_b):
    batch, seq, d = x.shape
    mlen = mem.shape[1]
    assert d == D_MODEL and w_in.shape == (N_LAYERS, D_MODEL, D_IN_PROJ)
    assert hgrn_lb_fwd.shape == (N_LAYERS + 1, D_HGRN)
    t = batch * seq
    alpha = (2.0 * N_LAYERS) ** 0.25
    lam_init = 0.8 - 0.6 * math.exp(-0.3 * 0)

    x2d = x.reshape(t, D_MODEL)
    ua, uh, um = _in_proj(x2d, w_in[0].astype(BF16), hgrn_lb_fwd, hgrn_lb_bwd)
    o_diff = _diff_attn(ua, diff_norm_w, lam_q1, lam_k1, lam_q2, lam_k2,
                        batch, seq, lam_init)
    o_mem = _mem_attn(um, mem.reshape(batch * mlen, D_MODEL), w_mem_kv[0].astype(BF16),
                      batch, seq, mlen)
    o_hgrn = _hgrn2(uh, hgrn_norm_w, batch, seq)

    h1, eidx, gate, rank, cnt = _post_mixer(
        x2d, o_diff, o_hgrn, o_mem, w_o[0].astype(BF16), ln1_g, ln1_b, router_w[0],
        router_b, alpha)

    n_blocks = -(-(t * TOP_K) // MOE_BLOCK) + N_EXPERTS
    dest, meta = _route_meta(eidx, rank, cnt, n_blocks)
    dest_km = dest[:, 0:TOP_K].T
    block_e = meta[:, 0]
    used = meta[0:1, 1]

    xs = _dispatch(dest_km, h1, n_blocks * MOE_BLOCK)
    b_g = b_gate_up[0][:, 0::2].reshape(N_EXPERTS, 1, D_EXPERT)
    b_l = b_gate_up[0][:, 1::2].reshape(N_EXPERTS, 1, D_EXPERT)
    ys = _experts(block_e, used, xs, w_gate_up[0], w_down[0], b_g, b_l,
                  b_down[0].reshape(N_EXPERTS, 1, D_MODEL), _column_split_permutation())
    out = _combine(h1, gate, ln2_g, ln2_b, _gather_back(dest_km, ys), alpha)
    return out.reshape(batch, seq, D_MODEL)
```

```python
import functools
import math

import jax
import jax.numpy as jnp
import numpy as np
from jax import lax
from jax.experimental import pallas as pl
from jax.experimental.pallas import tpu as pltpu
from jax.experimental.pallas import tpu_sc as plsc

F32 = jnp.float32
BF16 = jnp.bfloat16
I32 = jnp.int32
U32 = jnp.uint32

D_MODEL = 1024
N_LAYERS = 1
D_DIFF = 512
D_HGRN = 256
D_XMEM = 256
N_HEADS = 4
DH = 64
DH_SHIFT = 6
D_IN_PROJ = 3 * D_DIFF + 5 * D_HGRN + D_XMEM
CHUNK = 16
N_EXPERTS = 32
TOP_K = 4
D_EXPERT = 1024
MOE_BLOCK = 256
SWIGLU_ALPHA = 1.702
SWIGLU_LIMIT = 7.0
NORM_EPS = 1e-5
LOG2E = math.log2(math.e)
LANES = 128
VMEM_LIMIT = 56 * 1024 * 1024


def _nt_dot(a, b):
    return lax.dot_general(a, b, (((1,), (1,)), ((), ())), preferred_element_type=F32)


def _tn_dot(a, b):
    return lax.dot_general(a, b, (((0,), (0,)), ((), ())), preferred_element_type=F32)


def _dot(a, b):
    return jnp.dot(a, b, preferred_element_type=F32)


def _sigmoid(x):
    return 1.0 / (1.0 + jnp.exp(-x))


def _layer_norm(y, g, b):
    mu = jnp.mean(y, axis=-1, keepdims=True)
    yc = y - mu
    var = jnp.mean(yc * yc, axis=-1, keepdims=True)
    return yc * lax.rsqrt(var + NORM_EPS) * g + b


def _in_proj_kernel(x_ref, w_ref, lbf_ref, lbb_ref, ua_ref, uh_ref, um_ref):
    xb = x_ref[...].astype(BF16)

    def proj(c0, width):
        return _dot(xb, w_ref[:, c0:c0 + width])

    ua_ref[:, 0:D_DIFF] = (proj(0, D_DIFF) * (LOG2E / math.sqrt(DH))).astype(BF16)
    ua_ref[:, D_DIFF:2 * D_DIFF] = proj(D_DIFF, D_DIFF).astype(BF16)
    ua_ref[:, 2 * D_DIFF:3 * D_DIFF] = proj(2 * D_DIFF, D_DIFF).astype(BF16)

    def lower_bound(lb_ref):
        a = lb_ref[0:1, :]
        b = lb_ref[1:2, :]
        m = jnp.maximum(a, b)
        ea = jnp.exp(a - m)
        eb = jnp.exp(b - m)
        return ea / (ea + eb)

    base = 3 * D_DIFF
    hq = proj(base, D_HGRN)
    uh_ref[:, 0:D_HGRN] = hq * _sigmoid(hq)
    for d, lb_ref in enumerate((lbf_ref, lbb_ref)):
        lb = lower_bound(lb_ref)
        f = lb + (1.0 - lb) * _sigmoid(proj(base + (1 + d) * D_HGRN, D_HGRN))
        uh_ref[:, (1 + 2 * d) * D_HGRN:(2 + 2 * d) * D_HGRN] = 1.0 - f
        uh_ref[:, (2 + 2 * d) * D_HGRN:(3 + 2 * d) * D_HGRN] = jnp.log(f)
    uh_ref[:, 5 * D_HGRN:6 * D_HGRN] = proj(base + 3 * D_HGRN, D_HGRN)
    uh_ref[:, 6 * D_HGRN:7 * D_HGRN] = _sigmoid(proj(base + 4 * D_HGRN, D_HGRN))
    um_ref[...] = proj(base + 5 * D_HGRN, D_XMEM).astype(BF16)


def _in_proj(x2d, w_in_bf, lb_f, lb_b):
    t = x2d.shape[0]
    tm = 256
    return pl.pallas_call(
        _in_proj_kernel,
        grid=(t // tm,),
        in_specs=[
            pl.BlockSpec((tm, D_MODEL), lambda i: (i, 0)),
            pl.BlockSpec((D_MODEL, D_IN_PROJ), lambda i: (0, 0)),
            pl.BlockSpec((N_LAYERS + 1, D_HGRN), lambda i: (0, 0)),
            pl.BlockSpec((N_LAYERS + 1, D_HGRN), lambda i: (0, 0)),
        ],
        out_specs=[
            pl.BlockSpec((tm, 3 * D_DIFF), lambda i: (i, 0)),
            pl.BlockSpec((tm, 7 * D_HGRN), lambda i: (i, 0)),
            pl.BlockSpec((tm, D_XMEM), lambda i: (i, 0)),
        ],
        out_shape=[
            jax.ShapeDtypeStruct((t, 3 * D_DIFF), BF16),
            jax.ShapeDtypeStruct((t, 7 * D_HGRN), F32),
            jax.ShapeDtypeStruct((t, D_XMEM), BF16),
        ],
        compiler_params=pltpu.CompilerParams(
            dimension_semantics=("arbitrary",), vmem_limit_bytes=VMEM_LIMIT),
        name="in_proj",
    )(x2d, w_in_bf, lb_f, lb_b)


N_POS_FEATURES = 12


def _alibi_feature_rows(c3, base):
    f = lax.broadcasted_iota(I32, (1, 2 * DH), 1) - base

    def pieces(f0):
        return jnp.where(f == f0, c3[0], jnp.where(f == f0 + 1, c3[1],
                                                   jnp.where(f == f0 + 2, c3[2], 0.0)))

    span = lambda lo: jnp.where((f >= lo) & (f < lo + 3), 1.0, 0.0)
    q_const = 64.0 * pieces(6) + pieces(9)
    k_const = -64.0 * pieces(0) - pieces(3)
    return (q_const, span(0), span(3)), (k_const, span(6), span(9))


def _position_features(rows, pos0, n):
    const, hi_mask, lo_mask = rows
    pos = pos0 + lax.broadcasted_iota(I32, (n, 2 * DH), 0)
    hi = (pos >> DH_SHIFT).astype(F32)
    lo = (pos & (DH - 1)).astype(F32)
    return const + hi_mask * hi + lo_mask * lo


def _diff_attn_kernel(c_ref, q_ref, k_ref, v_ref, nw_ref, lq1_ref, lk1_ref, lq2_ref,
                      lk2_ref, o_ref, ka1_ref, ka2_ref, corr_ref, vt_ref, acc_ref, m_ref,
                      l_ref, *, t, qs, seq, lam_init):
    h = pl.program_id(1)
    i = pl.program_id(2)
    nk = seq // t
    c3 = (c_ref[h, 0], c_ref[h, 1], c_ref[h, 2])
    lane = lax.broadcasted_iota(I32, (t, 2 * DH), 1)
    first_half = lane < DH
    q_rows1, k_rows1 = _alibi_feature_rows(c3, DH)
    q_rows2, k_rows2 = _alibi_feature_rows(c3, 0)

    @pl.when(i == 0)
    def _():
        def build(r, carry):
            r0 = pl.multiple_of(r * t, t)
            rows = pl.ds(r0, t)
            kblk = k_ref[rows, :]
            ka1_ref[rows, :] = jnp.where(first_half, kblk,
                                         _position_features(k_rows1, r0, t).astype(BF16))
            ka2_ref[rows, :] = jnp.where(first_half,
                                         _position_features(k_rows2, r0, t).astype(BF16), kblk)
            vt_ref[r] = v_ref[rows, :].astype(F32).T.astype(BF16)
            return carry

        lax.fori_loop(0, nk, build, 0)
        kk = lax.broadcasted_iota(I32, (t, t), 0)
        qq = lax.broadcasted_iota(I32, (t, t), 1)
        c = c3[0] + c3[1] + c3[2]
        corr_ref[...] = (2.0 * c) * jnp.minimum(qq - kk, 0).astype(F32)

    q = q_ref[...]
    qf1 = _position_features(q_rows1, i * t, t)
    qf2 = _position_features(q_rows2, i * t, t)
    qa1_before = jnp.where(first_half, q, qf1.astype(BF16))
    qa1_after = jnp.where(first_half, q, (-qf1).astype(BF16))
    qa2_before = jnp.where(first_half, qf2.astype(BF16), q)
    qa2_after = jnp.where(first_half, (-qf2).astype(BF16), q)

    m_ref[...] = jnp.full(m_ref.shape, -jnp.inf, F32)
    l_ref[...] = jnp.zeros(l_ref.shape, F32)
    acc_ref[...] = jnp.zeros(acc_ref.shape, F32)

    def chunk(j, qa1, qa2, diagonal):
        rows = pl.ds(pl.multiple_of(j * t, t), t)
        vt = vt_ref[j]
        for mp, (qa, ka_ref) in enumerate(((qa1, ka1_ref), (qa2, ka2_ref))):
            ka = ka_ref[rows, :]
            for u in range(t // qs):
                qrows = slice(u * qs, (u + 1) * qs)
                cols = slice(mp * t + u * qs, mp * t + (u + 1) * qs)
                s = _nt_dot(ka, qa[qrows, :])
                if diagonal:
                    s = s + corr_ref[:, qrows]
                m_old = m_ref[:, cols]
                m_new = jnp.maximum(m_old, jnp.max(s, axis=0, keepdims=True))
                p = jnp.exp2(s - m_new)
                r = jnp.exp2(m_old - m_new)
                l_ref[:, cols] = r * l_ref[:, cols] + jnp.sum(p, axis=0, keepdims=True)
                acc_ref[mp, :, qrows] = (r * acc_ref[mp, :, qrows]
                                         + _dot(vt, p.astype(BF16)))
                m_ref[:, cols] = m_new

    chunk(i, qa1_before, qa2_before, True)
    for jj in range(nk - 1):
        j = jj + (jj >= i).astype(I32)
        keys_first = j < i
        chunk(j, jnp.where(keys_first, qa1_before, qa1_after),
              jnp.where(keys_first, qa2_before, qa2_after), False)

    lam = (jnp.exp(jnp.sum(lq1_ref[...] * lk1_ref[...], axis=-1, keepdims=True))
           - jnp.exp(jnp.sum(lq2_ref[...] * lk2_ref[...], axis=-1, keepdims=True))
           + lam_init)
    o = acc_ref[0] / l_ref[:, 0:t] - lam * (acc_ref[1] / l_ref[:, t:2 * t])
    o = o * lax.rsqrt(jnp.mean(o * o, axis=0, keepdims=True) + NORM_EPS)
    o_ref[...] = (o.T * nw_ref[...] * (1.0 - lam_init)).astype(o_ref.dtype)


def _bf16_pieces(x):
    x = np.asarray(x, np.float32)
    out = []
    for _ in range(3):
        p = x.astype(BF16).astype(np.float32)
        out.append(p)
        x = x - p
    return np.stack(out, axis=-1)


def _diff_attn(ua, nw, lq1, lk1, lq2, lk2, batch, seq, lam_init):
    t = 512
    nq = seq // t
    slopes = 2.0 ** (-8.0 * np.arange(1, N_HEADS + 1) / N_HEADS)
    c_pieces = jnp.asarray(_bf16_pieces(slopes.astype(np.float32) * np.float32(LOG2E)))
    kern = functools.partial(_diff_attn_kernel, t=t, qs=LANES, seq=seq, lam_init=lam_init)
    small = lambda b, h, i, s: (0, 0)
    return pl.pallas_call(
        kern,
        grid_spec=pltpu.PrefetchScalarGridSpec(
            num_scalar_prefetch=1,
            grid=(batch, N_HEADS, nq),
            in_specs=[
                pl.BlockSpec((t, 2 * DH), lambda b, h, i, s: (b * nq + i, h)),
                pl.BlockSpec((seq, 2 * DH), lambda b, h, i, s: (b, N_HEADS + h)),
                pl.BlockSpec((seq, 2 * DH), lambda b, h, i, s: (b, 2 * N_HEADS + h)),
                pl.BlockSpec((1, 2 * DH), small),
                pl.BlockSpec((1, DH), small),
                pl.BlockSpec((1, DH), small),
                pl.BlockSpec((1, DH), small),
                pl.BlockSpec((1, DH), small),
            ],
            out_specs=pl.BlockSpec((t, 2 * DH), lambda b, h, i, s: (b * nq + i, h)),
            scratch_shapes=[
                pltpu.VMEM((seq, 2 * DH), BF16),
                pltpu.VMEM((seq, 2 * DH), BF16),
                pltpu.VMEM((t, t), F32),
                pltpu.VMEM((nq, 2 * DH, t), BF16),
                pltpu.VMEM((2, 2 * DH, t), F32),
                pltpu.VMEM((1, 2 * t), F32),
                pltpu.VMEM((1, 2 * t), F32),
            ],
        ),
        out_shape=jax.ShapeDtypeStruct((batch * seq, D_DIFF), BF16),
        compiler_params=pltpu.CompilerParams(
            dimension_semantics=("arbitrary", "arbitrary", "arbitrary"),
            vmem_limit_bytes=VMEM_LIMIT),
        name="diff_attn",
    )(c_pieces, ua, ua, ua, nw, lq1, lk1, lq2, lk2)


def _mem_attn_kernel(q_ref, mem_ref, wkv_ref, o_ref, kv_ref, *, tq, mlen):
    @pl.when(pl.program_id(1) == 0)
    def _():
        kv_ref[...] = _dot(mem_ref[...].astype(BF16), wkv_ref[...]).astype(BF16)

    q = q_ref[...]
    mk = kv_ref[:, 0:D_XMEM]
    mv = kv_ref[:, D_XMEM:2 * D_XMEM]
    qhead = lax.broadcasted_iota(I32, (tq, D_XMEM), 1) >> DH_SHIFT
    vhead = lax.broadcasted_iota(I32, (mlen, D_XMEM), 1) >> DH_SHIFT
    acc = jnp.zeros((tq, D_XMEM), F32)
    for h in range(N_HEADS):
        qh = jnp.where(qhead == h, q, jnp.zeros_like(q))
        s = _nt_dot(qh, mk) * (1.0 / math.sqrt(DH))
        e = jnp.exp(s - jnp.max(s, axis=-1, keepdims=True))
        p = e / jnp.sum(e, axis=-1, keepdims=True)
        vh = jnp.where(vhead == h, mv, jnp.zeros_like(mv))
        acc = acc + _dot(p.astype(BF16), vh)
    o_ref[...] = acc.astype(o_ref.dtype)


def _mem_attn(um, mem2d, wkv_bf, batch, seq, mlen):
    tq = 512
    nq = seq // tq
    kern = functools.partial(_mem_attn_kernel, tq=tq, mlen=mlen)
    return pl.pallas_call(
        kern,
        grid=(batch, nq),
        in_specs=[
            pl.BlockSpec((tq, D_XMEM), lambda b, i: (b * nq + i, 0)),
            pl.BlockSpec((mlen, D_MODEL), lambda b, i: (b, 0)),
            pl.BlockSpec((D_MODEL, 2 * D_XMEM), lambda b, i: (0, 0)),
        ],
        out_specs=pl.BlockSpec((tq, D_XMEM), lambda b, i: (b * nq + i, 0)),
        out_shape=jax.ShapeDtypeStruct((batch * seq, D_XMEM), BF16),
        scratch_shapes=[pltpu.VMEM((mlen, 2 * D_XMEM), BF16)],
        compiler_params=pltpu.CompilerParams(
            dimension_semantics=("arbitrary", "arbitrary"), vmem_limit_bytes=VMEM_LIMIT),
        name="mem_attn",
    )(um, mem2d, wkv_bf)


def _hgrn2_kernel(q_ref, kf_ref, lf_ref, kb_ref, lb_ref, v_ref, g_ref, nw_ref, o_ref,
                  kfp_ref, kbp_ref, vp_ref, bf_ref, bb_ref, acc_ref, cross_ref, *, seq, rb):
    n_chunks = seq // CHUNK
    w = 2 * DH
    row = lax.broadcasted_iota(I32, (rb, w), 0) & (CHUNK - 1)
    li = lax.broadcasted_iota(I32, (w, w), 0) >> DH_SHIFT
    lj = lax.broadcasted_iota(I32, (w, w), 1) >> DH_SHIFT
    same_head = li == lj
    head_ones = jnp.where(same_head, 1.0, 0.0).astype(BF16)

    def chunk_cumsum(x, reverse):
        for sh in (1, 2, 4, 8):
            if reverse:
                moved = pltpu.roll(x, rb - sh, 0)
                keep = row + sh < CHUNK
            else:
                moved = pltpu.roll(x, sh, 0)
                keep = row >= sh
            x = x + jnp.where(keep, moved, 0.0)
        return x

    halo = jnp.zeros((CHUNK, w), F32)
    for ref in (kfp_ref, kbp_ref, vp_ref, bf_ref, bb_ref):
        ref[0:CHUNK, :] = halo
        ref[seq + CHUNK:seq + 2 * CHUNK, :] = halo

    def prepare(blk, carry):
        r0 = pl.multiple_of(blk * rb, rb)
        rows = pl.ds(r0, rb)
        inner = pl.ds(r0 + CHUNK, rb)
        kfp_ref[inner, :] = kf_ref[rows, :]
        kbp_ref[inner, :] = kb_ref[rows, :]
        vp_ref[inner, :] = v_ref[rows, :]
        bf_ref[inner, :] = chunk_cumsum(lf_ref[rows, :] * LOG2E, False)
        bb_ref[inner, :] = chunk_cumsum(lb_ref[rows, :] * LOG2E, True)
        return carry

    lax.fori_loop(0, seq // rb, prepare, 0)

    def intra(blk):
        r0 = pl.multiple_of(blk * rb, rb)
        rows = pl.ds(r0, rb)
        q = q_ref[rows, :]
        acc = jnp.zeros((rb, w), F32)
        for reverse, kp_ref, b_ref in ((False, kfp_ref, bf_ref), (True, kbp_ref, bb_ref)):
            b = b_ref[pl.ds(r0 + CHUNK, rb), :]
            for d in range(CHUNK):
                src = pl.ds(r0 + CHUNK + (d if reverse else -d), rb)
                ks = kp_ref[src, :]
                vs = vp_ref[src, :]
                if d == 0:
                    z = q * ks
                else:
                    keep = (row + d < CHUNK) if reverse else (row >= d)
                    dec = jnp.exp2(jnp.where(keep, b - b_ref[src, :], -jnp.inf))
                    z = q * ks * dec
                acc = acc + _dot(z.astype(BF16), head_ones) * vs
        acc_ref[rows, :] = acc

    def inter(n, carry):
        wf, wb = carry
        rf = pl.multiple_of(n * CHUNK, CHUNK)
        rr = pl.multiple_of((n_chunks - 1 - n) * CHUNK, CHUNK)

        def one(state, r0, kp_ref, b_ref, edge):
            rows = pl.ds(r0, CHUNK)
            prow = pl.ds(r0 + CHUNK, CHUNK)
            b = b_ref[prow, :]
            b_edge = b_ref[pl.ds(r0 + CHUNK + edge, 1), :]
            qd = q_ref[rows, :] * jnp.exp2(b)
            cross_ref[rows, :] = cross_ref[rows, :] + _nt_dot(qd.astype(BF16),
                                                              state.astype(BF16))
            kd = kp_ref[prow, :] * jnp.exp2(b_edge - b)
            upd = _tn_dot(vp_ref[prow, :].astype(BF16), kd.astype(BF16))
            return state * jnp.exp2(b_edge) + jnp.where(same_head, upd, 0.0)

        wf = one(wf, rf, kfp_ref, bf_ref, CHUNK - 1)
        wb = one(wb, rr, kbp_ref, bb_ref, 0)
        return wf, wb

    cross_ref[...] = jnp.zeros(cross_ref.shape, F32)

    def fused(blk, carry):
        intra(blk)
        for c in range(rb // CHUNK):
            carry = inter(blk * (rb // CHUNK) + c, carry)
        return carry

    z = jnp.zeros((w, w), F32)
    lax.fori_loop(0, seq // rb, fused, (z, z))

    def finish(blk, carry):
        rows = pl.ds(pl.multiple_of(blk * rb, rb), rb)
        o = acc_ref[rows, :] + cross_ref[rows, :]
        lane = lax.broadcasted_iota(I32, (rb, w), 1)
        lo = lane < DH
        sq = o * o
        ms_lo = jnp.sum(jnp.where(lo, sq, 0.0), axis=-1, keepdims=True)
        ms_hi = jnp.sum(jnp.where(lo, 0.0, sq), axis=-1, keepdims=True)
        ms = jnp.where(lo, ms_lo, ms_hi) * (1.0 / DH)
        o = o * lax.rsqrt(ms + NORM_EPS) * nw_ref[...] * g_ref[rows, :]
        o_ref[rows, :] = o.astype(o_ref.dtype)
        return carry

    lax.fori_loop(0, seq // rb, finish, 0)


def _hgrn2(uh, nw, batch, seq):
    w = 2 * DH
    npair = D_HGRN // w
    kern = functools.partial(_hgrn2_kernel, seq=seq, rb=128)

    def sec(s):
        return pl.BlockSpec((seq, w), lambda b, p, s=s: (b, s * npair + p))

    return pl.pallas_call(
        kern,
        grid=(batch, npair),
        in_specs=[sec(0), sec(1), sec(2), sec(3), sec(4), sec(5), sec(6),
                  pl.BlockSpec((1, w), lambda b, p: (0, p))],
        out_specs=pl.BlockSpec((seq, w), lambda b, p: (b, p)),
        out_shape=jax.ShapeDtypeStruct((batch * seq, D_HGRN), BF16),
        scratch_shapes=[pltpu.VMEM((seq + 2 * CHUNK, w), F32)] * 5
        + [pltpu.VMEM((seq, w), F32)] * 2,
        compiler_params=pltpu.CompilerParams(
            dimension_semantics=("arbitrary", "arbitrary"), vmem_limit_bytes=VMEM_LIMIT),
        name="hgrn2",
    )(uh, uh, uh, uh, uh, uh, uh, nw)


def _rows_to_block(rows, tm):
    r = lax.broadcasted_iota(I32, (8, tm), 0)
    out = jnp.zeros((8, tm), rows[0].dtype)
    for k, row in enumerate(rows):
        out = jnp.where(r == k, row, out)
    return out


def _pack_bf16_pairs(x):
    m = x.shape[1] // 2
    u = lax.bitcast_convert_type(x, U32)
    r = u + jnp.uint32(0x7FFF) + ((u >> 16) & jnp.uint32(1))
    return (r[:, 0:m] >> 16) | (r[:, m:2 * m] & jnp.uint32(0xFFFF0000))


def _unpack_bf16_pairs(w):
    lo = lax.bitcast_convert_type(w << 16, F32).astype(BF16)
    hi = lax.bitcast_convert_type(w & jnp.uint32(0xFFFF0000), F32).astype(BF16)
    return jnp.concatenate([lo, hi], axis=1)


def _post_mixer_kernel(x_ref, od_ref, oh_ref, om_ref, wo_ref, g_ref, b_ref, rwt_ref,
                       rbt_ref, h_ref, hp_ref, eidx_ref, gate_ref, rank_ref, cnt_ref,
                       carry_ref, *, tm, alpha):
    i = pl.program_id(0)

    @pl.when(i == 0)
    def _():
        carry_ref[...] = jnp.zeros_like(carry_ref)

    mix = (_dot(od_ref[...], wo_ref[0:D_DIFF, :])
           + _dot(oh_ref[...], wo_ref[D_DIFF:D_DIFF + D_HGRN, :])
           + _dot(om_ref[...], wo_ref[D_DIFF + D_HGRN:, :]))
    h = _layer_norm(alpha * x_ref[...] + mix, g_ref[...], b_ref[...])
    h_ref[...] = h

    h_hi = h.astype(BF16)
    hp_ref[...] = _pack_bf16_pairs(h)
    h_lo = (h - h_hi.astype(F32)).astype(BF16)
    rwt = rwt_ref[...]
    rwt_hi = rwt.astype(BF16)
    rwt_lo = (rwt - rwt_hi.astype(F32)).astype(BF16)
    logits = (_nt_dot(rwt_hi, h_hi) + _nt_dot(rwt_hi, h_lo) + _nt_dot(rwt_lo, h_hi)
              + rbt_ref[...])

    erow = lax.broadcasted_iota(I32, (N_EXPERTS, tm), 0).astype(F32)
    work = logits
    sels, vals, idxs = [], [], []
    for _ in range(TOP_K):
        m = jnp.max(work, axis=0, keepdims=True)
        idx = jnp.min(jnp.where(work == m, erow, float(N_EXPERTS)), axis=0, keepdims=True)
        sel = erow == idx
        work = jnp.where(sel, -jnp.inf, work)
        sels.append(sel)
        vals.append(m)
        idxs.append(idx)
    es = [jnp.exp(v - vals[0]) for v in vals]
    den = es[0] + es[1] + es[2] + es[3]

    chosen = jnp.where(sels[0] | sels[1] | sels[2] | sels[3], 1.0, 0.0)
    ti = lax.broadcasted_iota(I32, (tm, tm), 0)
    tj = lax.broadcasted_iota(I32, (tm, tm), 1)
    earlier = jnp.where(ti < tj, 1.0, 0.0).astype(BF16)
    prefix = _dot(chosen.astype(BF16), earlier) + carry_ref[:, 0:1]

    ranks = [jnp.sum(jnp.where(sels[k], prefix, 0.0), axis=0, keepdims=True)
             for k in range(TOP_K)]
    eidx_ref[...] = _rows_to_block(idxs, tm).astype(I32)
    gate_ref[...] = _rows_to_block([e / den for e in es], tm)
    rank_ref[...] = _rows_to_block(ranks, tm).astype(I32)

    total = carry_ref[:, 0:1] + jnp.sum(chosen, axis=1, keepdims=True)
    carry_ref[...] = jnp.broadcast_to(total, carry_ref.shape)
    cnt_ref[...] = carry_ref[...]


def _post_mixer(x2d, od, oh, om, wo_bf, g, b, rwt, rbt, alpha):
    t = x2d.shape[0]
    tm = 512
    kern = functools.partial(_post_mixer_kernel, tm=tm, alpha=alpha)
    full = lambda i: (0, 0)
    tok = lambda i: (i, 0)
    per_tok = lambda i: (0, i)
    return pl.pallas_call(
        kern,
        grid=(t // tm,),
        in_specs=[
            pl.BlockSpec((tm, D_MODEL), tok),
            pl.BlockSpec((tm, D_DIFF), tok),
            pl.BlockSpec((tm, D_HGRN), tok),
            pl.BlockSpec((tm, D_XMEM), tok),
            pl.BlockSpec((D_MODEL, D_MODEL), full),
            pl.BlockSpec((1, D_MODEL), full),
            pl.BlockSpec((1, D_MODEL), full),
            pl.BlockSpec((N_EXPERTS, D_MODEL), full),
            pl.BlockSpec((N_EXPERTS, 1), full),
        ],
        out_specs=[
            pl.BlockSpec((tm, D_MODEL), tok),
            pl.BlockSpec((tm, D_MODEL // 2), tok),
            pl.BlockSpec((8, tm), per_tok),
            pl.BlockSpec((8, tm), per_tok),
            pl.BlockSpec((8, tm), per_tok),
            pl.BlockSpec((N_EXPERTS, LANES), full),
        ],
        out_shape=[
            jax.ShapeDtypeStruct((t, D_MODEL), F32),
            jax.ShapeDtypeStruct((t, D_MODEL // 2), U32),
            jax.ShapeDtypeStruct((8, t), I32),
            jax.ShapeDtypeStruct((8, t), F32),
            jax.ShapeDtypeStruct((8, t), I32),
            jax.ShapeDtypeStruct((N_EXPERTS, LANES), F32),
        ],
        scratch_shapes=[pltpu.VMEM((N_EXPERTS, LANES), F32)],
        compiler_params=pltpu.CompilerParams(
            dimension_semantics=("arbitrary",), vmem_limit_bytes=VMEM_LIMIT),
        name="post_mixer",
    )(x2d, od, oh, om, wo_bf, g, b, rwt, rbt)


def _route_meta_kernel(eidx_ref, rank_ref, cnt_ref, dest_ref, be_ref, *, tm, nb_lanes):
    cnt = cnt_ref[...]
    blocks = jnp.floor((cnt + (MOE_BLOCK - 1)) * (1.0 / MOE_BLOCK))
    ei = lax.broadcasted_iota(I32, (N_EXPERTS, N_EXPERTS), 0)
    ej = lax.broadcasted_iota(I32, (N_EXPERTS, N_EXPERTS), 1)
    below = jnp.where(ej < ei, 1.0, 0.0).astype(BF16)
    pstart = _dot(below, blocks.astype(BF16))[:, 0:1]
    pend = pstart + blocks[:, 0:1]

    erow = lax.broadcasted_iota(I32, (N_EXPERTS, tm), 0).astype(F32)
    eidx = eidx_ref[...].astype(F32)
    starts = [jnp.sum(jnp.where(erow == eidx[k:k + 1, :], pstart, 0.0), axis=0, keepdims=True)
              for k in range(TOP_K)]
    dest_ref[...] = ((_rows_to_block(starts, tm) * float(MOE_BLOCK)).astype(I32)
                     + rank_ref[...])

    bi = lax.broadcasted_iota(I32, (N_EXPERTS, nb_lanes), 1).astype(F32)
    done = jnp.where(pend <= bi, 1.0, 0.0)
    be = jnp.minimum(jnp.sum(done, axis=0, keepdims=True), float(N_EXPERTS - 1))
    used = jnp.broadcast_to(pend[N_EXPERTS - 1:N_EXPERTS, :], (1, nb_lanes))
    be_ref[...] = _rows_to_block([be, used], nb_lanes).astype(I32)


def _route_meta(eidx, rank, cnt, n_blocks):
    t = eidx.shape[1]
    tm = 2048
    nb_lanes = -(-n_blocks // LANES) * LANES
    kern = functools.partial(_route_meta_kernel, tm=tm, nb_lanes=nb_lanes)
    return pl.pallas_call(
        kern,
        grid=(t // tm,),
        in_specs=[
            pl.BlockSpec((8, tm), lambda i: (0, i)),
            pl.BlockSpec((8, tm), lambda i: (0, i)),
            pl.BlockSpec((N_EXPERTS, LANES), lambda i: (0, 0)),
        ],
        out_specs=[
            pl.BlockSpec((8, tm), lambda i: (0, i)),
            pl.BlockSpec((8, nb_lanes), lambda i: (0, 0)),
        ],
        out_shape=[
            jax.ShapeDtypeStruct((8, t), I32),
            jax.ShapeDtypeStruct((8, nb_lanes), I32),
        ],
        compiler_params=pltpu.CompilerParams(
            dimension_semantics=("arbitrary",), vmem_limit_bytes=VMEM_LIMIT),
        name="route_meta",
    )(eidx, rank, cnt)


SC_WINDOW = LANES
N_QUARTERS = 4
QUARTER = D_MODEL // N_QUARTERS
N_PACKED = D_MODEL // 2 // QUARTER


def _sc_mesh():
    return plsc.VectorSubcoreMesh(core_axis_name="c", subcore_axis_name="s")


def _dispatch(dest_km, hp, n_rows):
    t = hp.shape[0]

    @functools.partial(
        pl.kernel, out_type=[jax.ShapeDtypeStruct((n_rows, QUARTER), hp.dtype)] * N_PACKED,
        mesh=_sc_mesh(), scratch_types=[])
    def scatter_rows(h_hbm, d_hbm, *xs_hbm):
        for q in range(N_PACKED):
            def body(x_vmem, i_vmem, q=q):
                for k in range(TOP_K):
                    pltpu.sync_copy(x_vmem, xs_hbm[q].at[i_vmem.at[k]])

            pltpu.emit_pipeline(
                body,
                grid=(t // SC_WINDOW,),
                in_specs=[pl.BlockSpec((SC_WINDOW, QUARTER), lambda i, q=q: (i, q)),
                          pl.BlockSpec((TOP_K, SC_WINDOW), lambda i: (0, i))],
                out_specs=[],
                core_axis_name=("c", "s"),
                dimension_semantics=(pltpu.PARALLEL,),
            )(h_hbm, d_hbm)

    return scatter_rows(hp, dest_km)


def _gather_back(dest_km, ys):
    t = dest_km.shape[1]
    n = len(ys)

    @functools.partial(
        pl.kernel,
        out_type=[jax.ShapeDtypeStruct((t, TOP_K * QUARTER), ys[0].dtype)] * n,
        mesh=_sc_mesh(), scratch_types=[])
    def gather_rows(d_hbm, *refs):
        ys_hbm, out_hbm = refs[:n], refs[n:]
        for q in range(n):
            for k in range(TOP_K):
                def body(i_vmem, o_vmem, q=q, k=k):
                    pltpu.sync_copy(ys_hbm[q].at[i_vmem.at[k]], o_vmem)

                pltpu.emit_pipeline(
                    body,
                    grid=(t // SC_WINDOW,),
                    in_specs=[pl.BlockSpec((TOP_K, SC_WINDOW), lambda i: (0, i))],
                    out_specs=[pl.BlockSpec((SC_WINDOW, QUARTER), lambda i, k=k: (i, k))],
                    core_axis_name=("c", "s"),
                    dimension_semantics=(pltpu.PARALLEL,),
                )(d_hbm, out_hbm[q])

    return gather_rows(dest_km, *ys)


def _experts_kernel(be_ref, used_ref, *refs):
    xs_refs = refs[:N_PACKED]
    wgu_hbm, wdn_hbm, bg_ref, bl_ref, bd_ref, perm_ref = refs[N_PACKED:N_PACKED + 6]
    ys_refs = refs[N_PACKED + 6:2 * N_PACKED + 6]
    wgu_st, wdn_st, wg_ref, wl_ref, wd_ref, sem = refs[2 * N_PACKED + 6:]
    i = pl.program_id(0)
    used = used_ref[0]
    e = be_ref[i]
    prev = be_ref[jnp.maximum(i - 1, 0)]
    fresh = jnp.logical_and(i < used, jnp.logical_or(i == 0, e != prev))

    def weight_copies(ex):
        return (pltpu.make_async_copy(wgu_hbm.at[ex], wgu_st, sem.at[0]),
                pltpu.make_async_copy(wdn_hbm.at[ex], wdn_st, sem.at[1]))

    @pl.when(jnp.logical_and(i == 0, used > 0))
    def _():
        for cp in weight_copies(e):
            cp.start()

    @pl.when(fresh)
    def _():
        for cp in weight_copies(e):
            cp.wait()
        perm = perm_ref[...]
        half = LANES
        for c in range(2 * D_EXPERT // (2 * half)):
            slab = wgu_st[:, c * 2 * half:(c + 1) * 2 * half].astype(BF16)
            sep = _dot(slab, perm)
            wg_ref[:, c * half:(c + 1) * half] = sep[:, 0:half].astype(BF16)
            wl_ref[:, c * half:(c + 1) * half] = sep[:, half:2 * half].astype(BF16)
        wd_ref[...] = wdn_st[...].astype(BF16)

        def same_expert(j):
            return jnp.logical_and(j < used, be_ref[jnp.minimum(j, used - 1)] == e)

        nxt = lax.while_loop(same_expert, lambda j: j + 1, i + 1)

        @pl.when(nxt < used)
        def _():
            for cp in weight_copies(be_ref[nxt]):
                cp.start()

    @pl.when(i < used_ref[0])
    def _():
        xb = _unpack_bf16_pairs(jnp.concatenate([r[...] for r in xs_refs], axis=1))
        glu = jnp.minimum(_dot(xb, wg_ref[...]) + bg_ref[0], SWIGLU_LIMIT)
        lin = jnp.clip(_dot(xb, wl_ref[...]) + bl_ref[0], -SWIGLU_LIMIT, SWIGLU_LIMIT)
        act = glu * _sigmoid(SWIGLU_ALPHA * glu) * (lin + 1.0)
        y = _pack_bf16_pairs(_dot(act.astype(BF16), wd_ref[...]) + bd_ref[0])
        for q, ys_ref in enumerate(ys_refs):
            ys_ref[...] = y[:, q * QUARTER:(q + 1) * QUARTER]

    @pl.when(i >= used_ref[0])
    def _():
        for ys_ref in ys_refs:
            ys_ref[...] = jnp.zeros_like(ys_ref)


def _experts(block_e, used, xs, w_gu, w_dn, b_g, b_l, b_d, perm):
    n_rows = xs[0].shape[0]
    n_blocks = n_rows // MOE_BLOCK
    wmap = lambda i, be, u: (be[i], 0, 0)
    rows = pl.BlockSpec((MOE_BLOCK, QUARTER), lambda i, be, u: (i, 0))
    return pl.pallas_call(
        _experts_kernel,
        grid_spec=pltpu.PrefetchScalarGridSpec(
            num_scalar_prefetch=2,
            grid=(n_blocks,),
            in_specs=[rows] * N_PACKED + [
                pl.BlockSpec(memory_space=pl.ANY),
                pl.BlockSpec(memory_space=pl.ANY),
                pl.BlockSpec((1, 1, D_EXPERT), wmap),
                pl.BlockSpec((1, 1, D_EXPERT), wmap),
                pl.BlockSpec((1, 1, D_MODEL), wmap),
                pl.BlockSpec((2 * LANES, 2 * LANES), lambda i, be, u: (0, 0)),
            ],
            out_specs=[rows] * N_PACKED,
            scratch_shapes=[
                pltpu.VMEM((D_MODEL, 2 * D_EXPERT), F32),
                pltpu.VMEM((D_EXPERT, D_MODEL), F32),
                pltpu.VMEM((D_MODEL, D_EXPERT), BF16),
                pltpu.VMEM((D_MODEL, D_EXPERT), BF16),
                pltpu.VMEM((D_EXPERT, D_MODEL), BF16),
                pltpu.SemaphoreType.DMA((2,)),
            ],
        ),
        out_shape=[jax.ShapeDtypeStruct((n_rows, QUARTER), U32)] * N_PACKED,
        compiler_params=pltpu.CompilerParams(
            dimension_semantics=("arbitrary",), vmem_limit_bytes=VMEM_LIMIT),
        name="experts",
    )(block_e, used, *xs, w_gu, w_dn, b_g, b_l, b_d, perm)


def _combine_kernel(h_ref, gate_ref, g_ref, b_ref, *refs, tm, alpha):
    yg_refs, o_ref = refs[:N_PACKED], refs[N_PACKED]
    gate = jnp.transpose(gate_ref[...])
    gks = [gate[:, k:k + 1] for k in range(TOP_K)]
    lows, highs = [], []
    for yg_ref in yg_refs:
        lo = jnp.zeros((tm, QUARTER), F32)
        hi = jnp.zeros((tm, QUARTER), F32)
        for k in range(TOP_K):
            w = yg_ref[:, k * QUARTER:(k + 1) * QUARTER]
            lo = lo + gks[k] * lax.bitcast_convert_type(w << 16, F32)
            hi = hi + gks[k] * lax.bitcast_convert_type(w & jnp.uint32(0xFFFF0000), F32)
        lows.append(lo)
        highs.append(hi)
    ffn = jnp.concatenate(lows + highs, axis=1)
    o_ref[...] = _layer_norm(alpha * h_ref[...] + ffn, g_ref[...], b_ref[...])


def _combine(h, gate, g, b, yg, alpha):
    t = h.shape[0]
    tm = 256
    kern = functools.partial(_combine_kernel, tm=tm, alpha=alpha)
    tok = lambda i: (i, 0)
    return pl.pallas_call(
        kern,
        grid=(t // tm,),
        in_specs=[
            pl.BlockSpec((tm, D_MODEL), tok),
            pl.BlockSpec((8, tm), lambda i: (0, i)),
            pl.BlockSpec((1, D_MODEL), lambda i: (0, 0)),
            pl.BlockSpec((1, D_MODEL), lambda i: (0, 0)),
        ] + [pl.BlockSpec((tm, TOP_K * QUARTER), tok)] * N_PACKED,
        out_specs=pl.BlockSpec((tm, D_MODEL), tok),
        out_shape=jax.ShapeDtypeStruct((t, D_MODEL), F32),
        compiler_params=pltpu.CompilerParams(
            dimension_semantics=("arbitrary",), vmem_limit_bytes=VMEM_LIMIT),
        name="combine",
    )(h, gate, g, b, *yg)


def _column_split_permutation():
    n = 2 * LANES
    p = np.zeros((n, n), np.float32)
    j = np.arange(n)
    p[j, j // 2 + LANES * (j % 2)] = 1.0
    return jnp.asarray(p, dtype=BF16)


def kernel(x, mem, w_in, lam_q1, lam_k1, lam_q2, lam_k2, diff_norm_w, hgrn_lb_fwd,
           hgrn_lb_bwd, hgrn_norm_w, w_mem_kv, w_o, ln1_g, ln1_b, router_w, router_b,
           w_gate_up, b_gate_up, w_down, b_down, ln2_g, ln2_b):
    batch, seq, d = x.shape
    mlen = mem.shape[1]
    assert d == D_MODEL and w_in.shape == (N_LAYERS, D_MODEL, D_IN_PROJ)
    assert hgrn_lb_fwd.shape == (N_LAYERS + 1, D_HGRN)
    t = batch * seq
    alpha = (2.0 * N_LAYERS) ** 0.25
    lam_init = 0.8 - 0.6 * math.exp(-0.3 * 0)

    x2d = x.reshape(t, D_MODEL)
    ua, uh, um = _in_proj(x2d, w_in[0].astype(BF16), hgrn_lb_fwd, hgrn_lb_bwd)
    o_diff = _diff_attn(ua, diff_norm_w, lam_q1, lam_k1, lam_q2, lam_k2,
                        batch, seq, lam_init)
    o_mem = _mem_attn(um, mem.reshape(batch * mlen, D_MODEL), w_mem_kv[0].astype(BF16),
                      batch, seq, mlen)
    o_hgrn = _hgrn2(uh, hgrn_norm_w, batch, seq)

    h1, h1b, eidx, gate, rank, cnt = _post_mixer(
        x2d, o_diff, o_hgrn, o_mem, w_o[0].astype(BF16), ln1_g, ln1_b, router_w[0].T,
        router_b.reshape(N_EXPERTS, 1), alpha)

    n_blocks = -(-(t * TOP_K) // MOE_BLOCK) + N_EXPERTS
    dest, meta = _route_meta(eidx, rank, cnt, n_blocks)
    dest_km = dest[0:TOP_K]
    block_e = meta[0, 0:n_blocks]
    used = meta[1, 0:1]

    xs = _dispatch(dest_km, h1b, n_blocks * MOE_BLOCK)
    b_g = b_gate_up[0][:, 0::2].reshape(N_EXPERTS, 1, D_EXPERT)
    b_l = b_gate_up[0][:, 1::2].reshape(N_EXPERTS, 1, D_EXPERT)
    ys = _experts(block_e, used, xs, w_gate_up[0], w_down[0], b_g, b_l,
                  b_down[0].reshape(N_EXPERTS, 1, D_MODEL), _column_split_permutation())
    out = _combine(h1, gate, ln2_g, ln2_b, _gather_back(dest_km, ys), alpha)
    return out.reshape(batch, seq, D_MODEL)
```

```python
import functools
import math

import jax
import jax.numpy as jnp
import numpy as np
from jax import lax
from jax.experimental import pallas as pl
from jax.experimental.pallas import tpu as pltpu
from jax.experimental.pallas import tpu_sc as plsc

F32 = jnp.float32
BF16 = jnp.bfloat16
I32 = jnp.int32
U32 = jnp.uint32

D_MODEL = 1024
N_LAYERS = 1
D_DIFF = 512
D_HGRN = 256
D_XMEM = 256
N_HEADS = 4
DH = 64
DH_SHIFT = 6
D_IN_PROJ = 3 * D_DIFF + 5 * D_HGRN + D_XMEM
CHUNK = 16
N_EXPERTS = 32
TOP_K = 4
D_EXPERT = 1024
MOE_BLOCK = 256
SWIGLU_ALPHA = 1.702
SWIGLU_LIMIT = 7.0
NORM_EPS = 1e-5
LOG2E = math.log2(math.e)
LANES = 128
VMEM_LIMIT = 56 * 1024 * 1024


def _nt_dot(a, b):
    return lax.dot_general(a, b, (((1,), (1,)), ((), ())), preferred_element_type=F32)


def _tn_dot(a, b):
    return lax.dot_general(a, b, (((0,), (0,)), ((), ())), preferred_element_type=F32)


def _dot(a, b):
    return jnp.dot(a, b, preferred_element_type=F32)


def _sigmoid(x):
    return 1.0 / (1.0 + jnp.exp(-x))


def _layer_norm(y, g, b):
    mu = jnp.mean(y, axis=-1, keepdims=True)
    yc = y - mu
    var = jnp.mean(yc * yc, axis=-1, keepdims=True)
    return yc * lax.rsqrt(var + NORM_EPS) * g + b


def _in_proj_kernel(x_ref, w_ref, lbf_ref, lbb_ref, ua_ref, uh_ref, um_ref):
    xb = x_ref[...].astype(BF16)

    def proj(c0, width):
        return _dot(xb, w_ref[:, c0:c0 + width])

    ua_ref[:, 0:D_DIFF] = (proj(0, D_DIFF) * (LOG2E / math.sqrt(DH))).astype(BF16)
    ua_ref[:, D_DIFF:2 * D_DIFF] = proj(D_DIFF, D_DIFF).astype(BF16)
    ua_ref[:, 2 * D_DIFF:3 * D_DIFF] = proj(2 * D_DIFF, D_DIFF).astype(BF16)

    def lower_bound(lb_ref):
        a = lb_ref[0:1, :]
        b = lb_ref[1:2, :]
        m = jnp.maximum(a, b)
        ea = jnp.exp(a - m)
        eb = jnp.exp(b - m)
        return ea / (ea + eb)

    base = 3 * D_DIFF
    hq = proj(base, D_HGRN)
    uh_ref[:, 0:D_HGRN] = hq * _sigmoid(hq)
    for d, lb_ref in enumerate((lbf_ref, lbb_ref)):
        lb = lower_bound(lb_ref)
        f = lb + (1.0 - lb) * _sigmoid(proj(base + (1 + d) * D_HGRN, D_HGRN))
        uh_ref[:, (1 + 2 * d) * D_HGRN:(2 + 2 * d) * D_HGRN] = 1.0 - f
        uh_ref[:, (2 + 2 * d) * D_HGRN:(3 + 2 * d) * D_HGRN] = jnp.log(f)
    uh_ref[:, 5 * D_HGRN:6 * D_HGRN] = proj(base + 3 * D_HGRN, D_HGRN)
    uh_ref[:, 6 * D_HGRN:7 * D_HGRN] = _sigmoid(proj(base + 4 * D_HGRN, D_HGRN))
    um_ref[...] = proj(base + 5 * D_HGRN, D_XMEM).astype(BF16)


def _in_proj(x2d, w_in_bf, lb_f, lb_b):
    t = x2d.shape[0]
    tm = 256
    return pl.pallas_call(
        _in_proj_kernel,
        grid=(t // tm,),
        in_specs=[
            pl.BlockSpec((tm, D_MODEL), lambda i: (i, 0)),
            pl.BlockSpec((D_MODEL, D_IN_PROJ), lambda i: (0, 0)),
            pl.BlockSpec((N_LAYERS + 1, D_HGRN), lambda i: (0, 0)),
            pl.BlockSpec((N_LAYERS + 1, D_HGRN), lambda i: (0, 0)),
        ],
        out_specs=[
            pl.BlockSpec((tm, 3 * D_DIFF), lambda i: (i, 0)),
            pl.BlockSpec((tm, 7 * D_HGRN), lambda i: (i, 0)),
            pl.BlockSpec((tm, D_XMEM), lambda i: (i, 0)),
        ],
        out_shape=[
            jax.ShapeDtypeStruct((t, 3 * D_DIFF), BF16),
            jax.ShapeDtypeStruct((t, 7 * D_HGRN), F32),
            jax.ShapeDtypeStruct((t, D_XMEM), BF16),
        ],
        compiler_params=pltpu.CompilerParams(
            dimension_semantics=("arbitrary",), vmem_limit_bytes=VMEM_LIMIT),
        name="in_proj",
    )(x2d, w_in_bf, lb_f, lb_b)


N_POS_FEATURES = 12


def _bf16_pieces(x):
    x = np.asarray(x, np.float32)
    out = []
    for _ in range(3):
        p = x.astype(BF16).astype(np.float32)
        out.append(p)
        x = x - p
    return np.stack(out, axis=-1)


def _alibi_tables(seq, t):
    slopes = (2.0 ** (-8.0 * np.arange(1, N_HEADS + 1) / N_HEADS)).astype(np.float32)
    c3 = _bf16_pieces(slopes * np.float32(LOG2E))
    pos = np.arange(seq)
    hi = (pos >> DH_SHIFT).astype(np.float32)
    lo = (pos & (DH - 1)).astype(np.float32)
    qf = np.zeros((N_HEADS, 2, seq, 2 * DH), np.float32)
    kf = np.zeros((N_HEADS, 2, seq, 2 * DH), np.float32)
    for m, base in enumerate((DH, 0)):
        for j in range(3):
            qf[:, m, :, base + j] = hi
            qf[:, m, :, base + 3 + j] = lo
            qf[:, m, :, base + 6 + j] = 64.0 * c3[:, j, None]
            qf[:, m, :, base + 9 + j] = c3[:, j, None]
            kf[:, m, :, base + j] = -64.0 * c3[:, j, None]
            kf[:, m, :, base + 3 + j] = -c3[:, j, None]
            kf[:, m, :, base + 6 + j] = hi
            kf[:, m, :, base + 9 + j] = lo
    c = c3.sum(axis=-1)
    kk = np.arange(t)[:, None]
    qq = np.arange(t)[None, :]
    corr = 2.0 * c[:, None, None] * np.minimum(qq - kk, 0).astype(np.float32)
    return (jnp.asarray(qf, dtype=BF16), jnp.asarray(kf, dtype=BF16),
            jnp.asarray(corr, dtype=F32))


def _diff_attn_kernel(q_ref, k_ref, v_ref, qf_ref, kf_ref, corr_ref, nw_ref, lq1_ref,
                      lk1_ref, lq2_ref, lk2_ref, o_ref, ka1_ref, ka2_ref, vt_ref, acc_ref,
                      m_ref, l_ref, *, t, qs, seq, lam_init):
    i = pl.program_id(2)
    nk = seq // t
    lane = lax.broadcasted_iota(I32, (t, 2 * DH), 1)
    first_half = lane < DH

    @pl.when(i == 0)
    def _():
        def build(r, carry):
            rows = pl.ds(pl.multiple_of(r * t, t), t)
            kblk = k_ref[rows, :]
            ka1_ref[rows, :] = jnp.where(first_half, kblk, kf_ref[0, 0, rows, :])
            ka2_ref[rows, :] = jnp.where(first_half, kf_ref[0, 1, rows, :], kblk)
            vt_ref[r] = v_ref[rows, :].astype(F32).T.astype(BF16)
            return carry

        lax.fori_loop(0, nk, build, 0)

    q = q_ref[...]
    qf1 = qf_ref[0, 0]
    qf2 = qf_ref[0, 1]
    qa1_before = jnp.where(first_half, q, qf1)
    qa1_after = jnp.where(first_half, q, -qf1)
    qa2_before = jnp.where(first_half, qf2, q)
    qa2_after = jnp.where(first_half, -qf2, q)

    m_ref[...] = jnp.full(m_ref.shape, -jnp.inf, F32)
    l_ref[...] = jnp.zeros(l_ref.shape, F32)
    acc_ref[...] = jnp.zeros(acc_ref.shape, F32)

    def chunk(j, qa1, qa2, diagonal):
        rows = pl.ds(pl.multiple_of(j * t, t), t)
        vt = vt_ref[j]
        for mp, (qa, ka_ref) in enumerate(((qa1, ka1_ref), (qa2, ka2_ref))):
            ka = ka_ref[rows, :]
            for u in range(t // qs):
                qrows = slice(u * qs, (u + 1) * qs)
                cols = slice(mp * t + u * qs, mp * t + (u + 1) * qs)
                s = _nt_dot(ka, qa[qrows, :])
                if diagonal:
                    s = s + corr_ref[0, :, qrows]
                m_old = m_ref[:, cols]
                m_new = jnp.maximum(m_old, jnp.max(s, axis=0, keepdims=True))
                p = jnp.exp2(s - m_new)
                r = jnp.exp2(m_old - m_new)
                l_ref[:, cols] = r * l_ref[:, cols] + jnp.sum(p, axis=0, keepdims=True)
                acc_ref[mp, :, qrows] = (r * acc_ref[mp, :, qrows]
                                         + _dot(vt, p.astype(BF16)))
                m_ref[:, cols] = m_new

    chunk(i, qa1_before, qa2_before, True)
    for jj in range(nk - 1):
        j = jj + (jj >= i).astype(I32)
        keys_first = j < i
        chunk(j, jnp.where(keys_first, qa1_before, qa1_after),
              jnp.where(keys_first, qa2_before, qa2_after), False)

    lam = (jnp.exp(jnp.sum(lq1_ref[...] * lk1_ref[...], axis=-1, keepdims=True))
           - jnp.exp(jnp.sum(lq2_ref[...] * lk2_ref[...], axis=-1, keepdims=True))
           + lam_init)
    o = acc_ref[0] / l_ref[:, 0:t] - lam * (acc_ref[1] / l_ref[:, t:2 * t])
    o = o * lax.rsqrt(jnp.mean(o * o, axis=0, keepdims=True) + NORM_EPS)
    o_ref[...] = (o.T * nw_ref[...] * (1.0 - lam_init)).astype(o_ref.dtype)


def _diff_attn(ua, nw, lq1, lk1, lq2, lk2, batch, seq, lam_init):
    t = 512
    nq = seq // t
    qf, kf, corr = _alibi_tables(seq, t)
    kern = functools.partial(_diff_attn_kernel, t=t, qs=LANES, seq=seq, lam_init=lam_init)
    small = lambda b, h, i: (0, 0)
    return pl.pallas_call(
        kern,
        grid_spec=pltpu.PrefetchScalarGridSpec(
            num_scalar_prefetch=0,
            grid=(batch, N_HEADS, nq),
            in_specs=[
                pl.BlockSpec((t, 2 * DH), lambda b, h, i: (b * nq + i, h)),
                pl.BlockSpec((seq, 2 * DH), lambda b, h, i: (b, N_HEADS + h)),
                pl.BlockSpec((seq, 2 * DH), lambda b, h, i: (b, 2 * N_HEADS + h)),
                pl.BlockSpec((1, 2, t, 2 * DH), lambda b, h, i: (h, 0, i, 0)),
                pl.BlockSpec((1, 2, seq, 2 * DH), lambda b, h, i: (h, 0, 0, 0)),
                pl.BlockSpec((1, t, t), lambda b, h, i: (h, 0, 0)),
                pl.BlockSpec((1, 2 * DH), small),
                pl.BlockSpec((1, DH), small),
                pl.BlockSpec((1, DH), small),
                pl.BlockSpec((1, DH), small),
                pl.BlockSpec((1, DH), small),
            ],
            out_specs=pl.BlockSpec((t, 2 * DH), lambda b, h, i: (b * nq + i, h)),
            scratch_shapes=[
                pltpu.VMEM((seq, 2 * DH), BF16),
                pltpu.VMEM((seq, 2 * DH), BF16),
                pltpu.VMEM((nq, 2 * DH, t), BF16),
                pltpu.VMEM((2, 2 * DH, t), F32),
                pltpu.VMEM((1, 2 * t), F32),
                pltpu.VMEM((1, 2 * t), F32),
            ],
        ),
        out_shape=jax.ShapeDtypeStruct((batch * seq, D_DIFF), BF16),
        compiler_params=pltpu.CompilerParams(
            dimension_semantics=("arbitrary", "arbitrary", "arbitrary"),
            vmem_limit_bytes=VMEM_LIMIT),
        name="diff_attn",
    )(ua, ua, ua, qf, kf, corr, nw, lq1, lk1, lq2, lk2)


def _mem_attn_kernel(q_ref, mem_ref, wkv_ref, o_ref, kv_ref, *, tq, mlen):
    @pl.when(pl.program_id(1) == 0)
    def _():
        kv_ref[...] = _dot(mem_ref[...].astype(BF16), wkv_ref[...]).astype(BF16)

    q = q_ref[...]
    mk = kv_ref[:, 0:D_XMEM]
    mv = kv_ref[:, D_XMEM:2 * D_XMEM]
    qhead = lax.broadcasted_iota(I32, (tq, D_XMEM), 1) >> DH_SHIFT
    vhead = lax.broadcasted_iota(I32, (mlen, D_XMEM), 1) >> DH_SHIFT
    acc = jnp.zeros((tq, D_XMEM), F32)
    for h in range(N_HEADS):
        qh = jnp.where(qhead == h, q, jnp.zeros_like(q))
        s = _nt_dot(qh, mk) * (1.0 / math.sqrt(DH))
        e = jnp.exp(s - jnp.max(s, axis=-1, keepdims=True))
        p = e / jnp.sum(e, axis=-1, keepdims=True)
        vh = jnp.where(vhead == h, mv, jnp.zeros_like(mv))
        acc = acc + _dot(p.astype(BF16), vh)
    o_ref[...] = acc.astype(o_ref.dtype)


def _mem_attn(um, mem2d, wkv_bf, batch, seq, mlen):
    tq = 512
    nq = seq // tq
    kern = functools.partial(_mem_attn_kernel, tq=tq, mlen=mlen)
    return pl.pallas_call(
        kern,
        grid=(batch, nq),
        in_specs=[
            pl.BlockSpec((tq, D_XMEM), lambda b, i: (b * nq + i, 0)),
            pl.BlockSpec((mlen, D_MODEL), lambda b, i: (b, 0)),
            pl.BlockSpec((D_MODEL, 2 * D_XMEM), lambda b, i: (0, 0)),
        ],
        out_specs=pl.BlockSpec((tq, D_XMEM), lambda b, i: (b * nq + i, 0)),
        out_shape=jax.ShapeDtypeStruct((batch * seq, D_XMEM), BF16),
        scratch_shapes=[pltpu.VMEM((mlen, 2 * D_XMEM), BF16)],
        compiler_params=pltpu.CompilerParams(
            dimension_semantics=("arbitrary", "arbitrary"), vmem_limit_bytes=VMEM_LIMIT),
        name="mem_attn",
    )(um, mem2d, wkv_bf)


def _hgrn2_kernel(q_ref, kf_ref, lf_ref, kb_ref, lb_ref, v_ref, g_ref, nw_ref, o_ref,
                  kfp_ref, kbp_ref, vp_ref, bf_ref, bb_ref, acc_ref, cross_ref, *, seq, rb):
    n_chunks = seq // CHUNK
    w = 2 * DH
    row = lax.broadcasted_iota(I32, (rb, w), 0) & (CHUNK - 1)
    li = lax.broadcasted_iota(I32, (w, w), 0) >> DH_SHIFT
    lj = lax.broadcasted_iota(I32, (w, w), 1) >> DH_SHIFT
    same_head = li == lj
    head_ones = jnp.where(same_head, 1.0, 0.0).astype(BF16)

    def chunk_cumsum(x, reverse):
        for sh in (1, 2, 4, 8):
            if reverse:
                moved = pltpu.roll(x, rb - sh, 0)
                keep = row + sh < CHUNK
            else:
                moved = pltpu.roll(x, sh, 0)
                keep = row >= sh
            x = x + jnp.where(keep, moved, 0.0)
        return x

    halo = jnp.zeros((CHUNK, w), F32)
    for ref in (kfp_ref, kbp_ref, vp_ref, bf_ref, bb_ref):
        ref[0:CHUNK, :] = halo
        ref[seq + CHUNK:seq + 2 * CHUNK, :] = halo

    def prepare(blk, carry):
        r0 = pl.multiple_of(blk * rb, rb)
        rows = pl.ds(r0, rb)
        inner = pl.ds(r0 + CHUNK, rb)
        kfp_ref[inner, :] = kf_ref[rows, :]
        kbp_ref[inner, :] = kb_ref[rows, :]
        vp_ref[inner, :] = v_ref[rows, :]
        bf_ref[inner, :] = chunk_cumsum(lf_ref[rows, :] * LOG2E, False)
        bb_ref[inner, :] = chunk_cumsum(lb_ref[rows, :] * LOG2E, True)
        return carry

    lax.fori_loop(0, seq // rb, prepare, 0)

    def intra(blk):
        r0 = pl.multiple_of(blk * rb, rb)
        rows = pl.ds(r0, rb)
        q = q_ref[rows, :]
        acc = jnp.zeros((rb, w), F32)
        for reverse, kp_ref, b_ref in ((False, kfp_ref, bf_ref), (True, kbp_ref, bb_ref)):
            b = b_ref[pl.ds(r0 + CHUNK, rb), :]
            for d in range(CHUNK):
                src = pl.ds(r0 + CHUNK + (d if reverse else -d), rb)
                ks = kp_ref[src, :]
                vs = vp_ref[src, :]
                if d == 0:
                    z = q * ks
                else:
                    keep = (row + d < CHUNK) if reverse else (row >= d)
                    dec = jnp.exp2(jnp.where(keep, b - b_ref[src, :], -jnp.inf))
                    z = q * ks * dec
                acc = acc + _dot(z.astype(BF16), head_ones) * vs
        acc_ref[rows, :] = acc

    def inter(n, carry):
        wf, wb = carry
        rf = pl.multiple_of(n * CHUNK, CHUNK)
        rr = pl.multiple_of((n_chunks - 1 - n) * CHUNK, CHUNK)

        def one(state, r0, kp_ref, b_ref, edge):
            rows = pl.ds(r0, CHUNK)
            prow = pl.ds(r0 + CHUNK, CHUNK)
            b = b_ref[prow, :]
            b_edge = b_ref[pl.ds(r0 + CHUNK + edge, 1), :]
            qd = q_ref[rows, :] * jnp.exp2(b)
            cross_ref[rows, :] = cross_ref[rows, :] + _nt_dot(qd.astype(BF16),
                                                              state.astype(BF16))
            kd = kp_ref[prow, :] * jnp.exp2(b_edge - b)
            upd = _tn_dot(vp_ref[prow, :].astype(BF16), kd.astype(BF16))
            return state * jnp.exp2(b_edge) + jnp.where(same_head, upd, 0.0)

        wf = one(wf, rf, kfp_ref, bf_ref, CHUNK - 1)
        wb = one(wb, rr, kbp_ref, bb_ref, 0)
        return wf, wb

    cross_ref[...] = jnp.zeros(cross_ref.shape, F32)

    def fused(blk, carry):
        intra(blk)
        for c in range(rb // CHUNK):
            carry = inter(blk * (rb // CHUNK) + c, carry)
        return carry

    z = jnp.zeros((w, w), F32)
    lax.fori_loop(0, seq // rb, fused, (z, z))

    def finish(blk, carry):
        rows = pl.ds(pl.multiple_of(blk * rb, rb), rb)
        o = acc_ref[rows, :] + cross_ref[rows, :]
        lane = lax.broadcasted_iota(I32, (rb, w), 1)
        lo = lane < DH
        sq = o * o
        ms_lo = jnp.sum(jnp.where(lo, sq, 0.0), axis=-1, keepdims=True)
        ms_hi = jnp.sum(jnp.where(lo, 0.0, sq), axis=-1, keepdims=True)
        ms = jnp.where(lo, ms_lo, ms_hi) * (1.0 / DH)
        o = o * lax.rsqrt(ms + NORM_EPS) * nw_ref[...] * g_ref[rows, :]
        o_ref[rows, :] = o.astype(o_ref.dtype)
        return carry

    lax.fori_loop(0, seq // rb, finish, 0)


def _hgrn2(uh, nw, batch, seq):
    w = 2 * DH
    npair = D_HGRN // w
    kern = functools.partial(_hgrn2_kernel, seq=seq, rb=128)

    def sec(s):
        return pl.BlockSpec((seq, w), lambda b, p, s=s: (b, s * npair + p))

    return pl.pallas_call(
        kern,
        grid=(batch, npair),
        in_specs=[sec(0), sec(1), sec(2), sec(3), sec(4), sec(5), sec(6),
                  pl.BlockSpec((1, w), lambda b, p: (0, p))],
        out_specs=pl.BlockSpec((seq, w), lambda b, p: (b, p)),
        out_shape=jax.ShapeDtypeStruct((batch * seq, D_HGRN), BF16),
        scratch_shapes=[pltpu.VMEM((seq + 2 * CHUNK, w), F32)] * 5
        + [pltpu.VMEM((seq, w), F32)] * 2,
        compiler_params=pltpu.CompilerParams(
            dimension_semantics=("arbitrary", "arbitrary"), vmem_limit_bytes=VMEM_LIMIT),
        name="hgrn2",
    )(uh, uh, uh, uh, uh, uh, uh, nw)


def _rows_to_block(rows, tm):
    r = lax.broadcasted_iota(I32, (8, tm), 0)
    out = jnp.zeros((8, tm), rows[0].dtype)
    for k, row in enumerate(rows):
        out = jnp.where(r == k, row, out)
    return out


def _pack_bf16_pairs(x):
    m = x.shape[1] // 2
    u = lax.bitcast_convert_type(x, U32)
    r = u + jnp.uint32(0x7FFF) + ((u >> 16) & jnp.uint32(1))
    return (r[:, 0:m] >> 16) | (r[:, m:2 * m] & jnp.uint32(0xFFFF0000))


def _unpack_bf16_pairs(w):
    lo = lax.bitcast_convert_type(w << 16, F32).astype(BF16)
    hi = lax.bitcast_convert_type(w & jnp.uint32(0xFFFF0000), F32).astype(BF16)
    return jnp.concatenate([lo, hi], axis=1)


def _post_mixer_kernel(x_ref, od_ref, oh_ref, om_ref, wo_ref, g_ref, b_ref, rwt_ref,
                       rbt_ref, h_ref, hp_ref, eidx_ref, gate_ref, rank_ref, cnt_ref,
                       carry_ref, *, tm, alpha):
    i = pl.program_id(0)

    @pl.when(i == 0)
    def _():
        carry_ref[...] = jnp.zeros_like(carry_ref)

    mix = (_dot(od_ref[...], wo_ref[0:D_DIFF, :])
           + _dot(oh_ref[...], wo_ref[D_DIFF:D_DIFF + D_HGRN, :])
           + _dot(om_ref[...], wo_ref[D_DIFF + D_HGRN:, :]))
    h = _layer_norm(alpha * x_ref[...] + mix, g_ref[...], b_ref[...])
    h_ref[...] = h

    h_hi = h.astype(BF16)
    hp_ref[...] = _pack_bf16_pairs(h)
    h_lo = (h - h_hi.astype(F32)).astype(BF16)
    rwt = rwt_ref[...]
    rwt_hi = rwt.astype(BF16)
    rwt_lo = (rwt - rwt_hi.astype(F32)).astype(BF16)
    logits = (_nt_dot(rwt_hi, h_hi) + _nt_dot(rwt_hi, h_lo) + _nt_dot(rwt_lo, h_hi)
              + rbt_ref[...])

    erow = lax.broadcasted_iota(I32, (N_EXPERTS, tm), 0).astype(F32)
    work = logits
    sels, vals, idxs = [], [], []
    for _ in range(TOP_K):
        m = jnp.max(work, axis=0, keepdims=True)
        idx = jnp.min(jnp.where(work == m, erow, float(N_EXPERTS)), axis=0, keepdims=True)
        sel = erow == idx
        work = jnp.where(sel, -jnp.inf, work)
        sels.append(sel)
        vals.append(m)
        idxs.append(idx)
    es = [jnp.exp(v - vals[0]) for v in vals]
    den = es[0] + es[1] + es[2] + es[3]

    chosen = jnp.where(sels[0] | sels[1] | sels[2] | sels[3], 1.0, 0.0)
    ti = lax.broadcasted_iota(I32, (tm, tm), 0)
    tj = lax.broadcasted_iota(I32, (tm, tm), 1)
    earlier = jnp.where(ti < tj, 1.0, 0.0).astype(BF16)
    prefix = _dot(chosen.astype(BF16), earlier) + carry_ref[:, 0:1]

    ranks = [jnp.sum(jnp.where(sels[k], prefix, 0.0), axis=0, keepdims=True)
             for k in range(TOP_K)]
    eidx_ref[...] = _rows_to_block(idxs, tm).astype(I32)
    gate_ref[...] = _rows_to_block([e / den for e in es], tm)
    rank_ref[...] = _rows_to_block(ranks, tm).astype(I32)

    total = carry_ref[:, 0:1] + jnp.sum(chosen, axis=1, keepdims=True)
    carry_ref[...] = jnp.broadcast_to(total, carry_ref.shape)
    cnt_ref[...] = carry_ref[...]


def _post_mixer(x2d, od, oh, om, wo_bf, g, b, rwt, rbt, alpha):
    t = x2d.shape[0]
    tm = 512
    kern = functools.partial(_post_mixer_kernel, tm=tm, alpha=alpha)
    full = lambda i: (0, 0)
    tok = lambda i: (i, 0)
    per_tok = lambda i: (0, i)
    return pl.pallas_call(
        kern,
        grid=(t // tm,),
        in_specs=[
            pl.BlockSpec((tm, D_MODEL), tok),
            pl.BlockSpec((tm, D_DIFF), tok),
            pl.BlockSpec((tm, D_HGRN), tok),
            pl.BlockSpec((tm, D_XMEM), tok),
            pl.BlockSpec((D_MODEL, D_MODEL), full),
            pl.BlockSpec((1, D_MODEL), full),
            pl.BlockSpec((1, D_MODEL), full),
            pl.BlockSpec((N_EXPERTS, D_MODEL), full),
            pl.BlockSpec((N_EXPERTS, 1), full),
        ],
        out_specs=[
            pl.BlockSpec((tm, D_MODEL), tok),
            pl.BlockSpec((tm, D_MODEL // 2), tok),
            pl.BlockSpec((8, tm), per_tok),
            pl.BlockSpec((8, tm), per_tok),
            pl.BlockSpec((8, tm), per_tok),
            pl.BlockSpec((N_EXPERTS, LANES), full),
        ],
        out_shape=[
            jax.ShapeDtypeStruct((t, D_MODEL), F32),
            jax.ShapeDtypeStruct((t, D_MODEL // 2), U32),
            jax.ShapeDtypeStruct((8, t), I32),
            jax.ShapeDtypeStruct((8, t), F32),
            jax.ShapeDtypeStruct((8, t), I32),
            jax.ShapeDtypeStruct((N_EXPERTS, LANES), F32),
        ],
        scratch_shapes=[pltpu.VMEM((N_EXPERTS, LANES), F32)],
        compiler_params=pltpu.CompilerParams(
            dimension_semantics=("arbitrary",), vmem_limit_bytes=VMEM_LIMIT),
        name="post_mixer",
    )(x2d, od, oh, om, wo_bf, g, b, rwt, rbt)


def _route_meta_kernel(eidx_ref, rank_ref, cnt_ref, dest_ref, be_ref, *, tm, nb_lanes):
    cnt = cnt_ref[...]
    blocks = jnp.floor((cnt + (MOE_BLOCK - 1)) * (1.0 / MOE_BLOCK))
    ei = lax.broadcasted_iota(I32, (N_EXPERTS, N_EXPERTS), 0)
    ej = lax.broadcasted_iota(I32, (N_EXPERTS, N_EXPERTS), 1)
    below = jnp.where(ej < ei, 1.0, 0.0).astype(BF16)
    pstart = _dot(below, blocks.astype(BF16))[:, 0:1]
    pend = pstart + blocks[:, 0:1]

    erow = lax.broadcasted_iota(I32, (N_EXPERTS, tm), 0).astype(F32)
    eidx = eidx_ref[...].astype(F32)
    starts = [jnp.sum(jnp.where(erow == eidx[k:k + 1, :], pstart, 0.0), axis=0, keepdims=True)
              for k in range(TOP_K)]
    dest_ref[...] = ((_rows_to_block(starts, tm) * float(MOE_BLOCK)).astype(I32)
                     + rank_ref[...])

    bi = lax.broadcasted_iota(I32, (N_EXPERTS, nb_lanes), 1).astype(F32)
    done = jnp.where(pend <= bi, 1.0, 0.0)
    be = jnp.minimum(jnp.sum(done, axis=0, keepdims=True), float(N_EXPERTS - 1))
    used = jnp.broadcast_to(pend[N_EXPERTS - 1:N_EXPERTS, :], (1, nb_lanes))
    be_ref[...] = _rows_to_block([be, used], nb_lanes).astype(I32)


def _route_meta(eidx, rank, cnt, n_blocks):
    t = eidx.shape[1]
    tm = 2048
    nb_lanes = -(-n_blocks // LANES) * LANES
    kern = functools.partial(_route_meta_kernel, tm=tm, nb_lanes=nb_lanes)
    return pl.pallas_call(
        kern,
        grid=(t // tm,),
        in_specs=[
            pl.BlockSpec((8, tm), lambda i: (0, i)),
            pl.BlockSpec((8, tm), lambda i: (0, i)),
            pl.BlockSpec((N_EXPERTS, LANES), lambda i: (0, 0)),
        ],
        out_specs=[
            pl.BlockSpec((8, tm), lambda i: (0, i)),
            pl.BlockSpec((8, nb_lanes), lambda i: (0, 0)),
        ],
        out_shape=[
            jax.ShapeDtypeStruct((8, t), I32),
            jax.ShapeDtypeStruct((8, nb_lanes), I32),
        ],
        compiler_params=pltpu.CompilerParams(
            dimension_semantics=("arbitrary",), vmem_limit_bytes=VMEM_LIMIT),
        name="route_meta",
    )(eidx, rank, cnt)


SC_WINDOW = LANES
N_QUARTERS = 4
QUARTER = D_MODEL // N_QUARTERS
N_PACKED = D_MODEL // 2 // QUARTER


def _sc_mesh():
    return plsc.VectorSubcoreMesh(core_axis_name="c", subcore_axis_name="s")


def _dispatch(dest_km, hp, n_rows):
    t = hp.shape[0]

    @functools.partial(
        pl.kernel, out_type=[jax.ShapeDtypeStruct((n_rows, QUARTER), hp.dtype)] * N_PACKED,
        mesh=_sc_mesh(), scratch_types=[])
    def scatter_rows(h_hbm, d_hbm, *xs_hbm):
        for q in range(N_PACKED):
            def body(x_vmem, i_vmem, q=q):
                for k in range(TOP_K):
                    pltpu.sync_copy(x_vmem, xs_hbm[q].at[i_vmem.at[k]])

            pltpu.emit_pipeline(
                body,
                grid=(t // SC_WINDOW,),
                in_specs=[pl.BlockSpec((SC_WINDOW, QUARTER), lambda i, q=q: (i, q)),
                          pl.BlockSpec((TOP_K, SC_WINDOW), lambda i: (0, i))],
                out_specs=[],
                core_axis_name=("c", "s"),
                dimension_semantics=(pltpu.PARALLEL,),
            )(h_hbm, d_hbm)

    return scatter_rows(hp, dest_km)


def _gather_back(dest_km, ys):
    t = dest_km.shape[1]
    n = len(ys)

    @functools.partial(
        pl.kernel,
        out_type=[jax.ShapeDtypeStruct((t, TOP_K * QUARTER), ys[0].dtype)] * n,
        mesh=_sc_mesh(), scratch_types=[])
    def gather_rows(d_hbm, *refs):
        ys_hbm, out_hbm = refs[:n], refs[n:]
        for q in range(n):
            for k in range(TOP_K):
                def body(i_vmem, o_vmem, q=q, k=k):
                    pltpu.sync_copy(ys_hbm[q].at[i_vmem.at[k]], o_vmem)

                pltpu.emit_pipeline(
                    body,
                    grid=(t // SC_WINDOW,),
                    in_specs=[pl.BlockSpec((TOP_K, SC_WINDOW), lambda i: (0, i))],
                    out_specs=[pl.BlockSpec((SC_WINDOW, QUARTER), lambda i, k=k: (i, k))],
                    core_axis_name=("c", "s"),
                    dimension_semantics=(pltpu.PARALLEL,),
                )(d_hbm, out_hbm[q])

    return gather_rows(dest_km, *ys)


def _experts_kernel(be_ref, used_ref, *refs):
    xs_refs = refs[:N_PACKED]
    wgu_hbm, wdn_hbm, bg_ref, bl_ref, bd_ref, perm_ref = refs[N_PACKED:N_PACKED + 6]
    ys_refs = refs[N_PACKED + 6:2 * N_PACKED + 6]
    wgu_st, wdn_st, wg_ref, wl_ref, wd_ref, sem = refs[2 * N_PACKED + 6:]
    i = pl.program_id(0)
    used = used_ref[0]
    e = be_ref[i]
    prev = be_ref[jnp.maximum(i - 1, 0)]
    fresh = jnp.logical_and(i < used, jnp.logical_or(i == 0, e != prev))

    def weight_copies(ex):
        return (pltpu.make_async_copy(wgu_hbm.at[ex], wgu_st, sem.at[0]),
                pltpu.make_async_copy(wdn_hbm.at[ex], wdn_st, sem.at[1]))

    @pl.when(jnp.logical_and(i == 0, used > 0))
    def _():
        for cp in weight_copies(e):
            cp.start()

    @pl.when(fresh)
    def _():
        for cp in weight_copies(e):
            cp.wait()
        perm = perm_ref[...]
        half = LANES
        for c in range(2 * D_EXPERT // (2 * half)):
            slab = wgu_st[:, c * 2 * half:(c + 1) * 2 * half].astype(BF16)
            sep = _dot(slab, perm)
            wg_ref[:, c * half:(c + 1) * half] = sep[:, 0:half].astype(BF16)
            wl_ref[:, c * half:(c + 1) * half] = sep[:, half:2 * half].astype(BF16)
        wd_ref[...] = wdn_st[...].astype(BF16)

        def same_expert(j):
            return jnp.logical_and(j < used, be_ref[jnp.minimum(j, used - 1)] == e)

        nxt = lax.while_loop(same_expert, lambda j: j + 1, i + 1)

        @pl.when(nxt < used)
        def _():
            for cp in weight_copies(be_ref[nxt]):
                cp.start()

    @pl.when(i < used_ref[0])
    def _():
        xb = _unpack_bf16_pairs(jnp.concatenate([r[...] for r in xs_refs], axis=1))
        glu = jnp.minimum(_dot(xb, wg_ref[...]) + bg_ref[0], SWIGLU_LIMIT)
        lin = jnp.clip(_dot(xb, wl_ref[...]) + bl_ref[0], -SWIGLU_LIMIT, SWIGLU_LIMIT)
        act = glu * _sigmoid(SWIGLU_ALPHA * glu) * (lin + 1.0)
        y = _pack_bf16_pairs(_dot(act.astype(BF16), wd_ref[...]) + bd_ref[0])
        for q, ys_ref in enumerate(ys_refs):
            ys_ref[...] = y[:, q * QUARTER:(q + 1) * QUARTER]

    @pl.when(i >= used_ref[0])
    def _():
        for ys_ref in ys_refs:
            ys_ref[...] = jnp.zeros_like(ys_ref)


def _experts(block_e, used, xs, w_gu, w_dn, b_g, b_l, b_d, perm):
    n_rows = xs[0].shape[0]
    n_blocks = n_rows // MOE_BLOCK
    wmap = lambda i, be, u: (be[i], 0, 0)
    rows = pl.BlockSpec((MOE_BLOCK, QUARTER), lambda i, be, u: (i, 0))
    return pl.pallas_call(
        _experts_kernel,
        grid_spec=pltpu.PrefetchScalarGridSpec(
            num_scalar_prefetch=2,
            grid=(n_blocks,),
            in_specs=[rows] * N_PACKED + [
                pl.BlockSpec(memory_space=pl.ANY),
                pl.BlockSpec(memory_space=pl.ANY),
                pl.BlockSpec((1, 1, D_EXPERT), wmap),
                pl.BlockSpec((1, 1, D_EXPERT), wmap),
                pl.BlockSpec((1, 1, D_MODEL), wmap),
                pl.BlockSpec((2 * LANES, 2 * LANES), lambda i, be, u: (0, 0)),
            ],
            out_specs=[rows] * N_PACKED,
            scratch_shapes=[
                pltpu.VMEM((D_MODEL, 2 * D_EXPERT), F32),
                pltpu.VMEM((D_EXPERT, D_MODEL), F32),
                pltpu.VMEM((D_MODEL, D_EXPERT), BF16),
                pltpu.VMEM((D_MODEL, D_EXPERT), BF16),
                pltpu.VMEM((D_EXPERT, D_MODEL), BF16),
                pltpu.SemaphoreType.DMA((2,)),
            ],
        ),
        out_shape=[jax.ShapeDtypeStruct((n_rows, QUARTER), U32)] * N_PACKED,
        compiler_params=pltpu.CompilerParams(
            dimension_semantics=("arbitrary",), vmem_limit_bytes=VMEM_LIMIT),
        name="experts",
    )(block_e, used, *xs, w_gu, w_dn, b_g, b_l, b_d, perm)


def _combine_kernel(h_ref, gate_ref, g_ref, b_ref, *refs, tm, alpha):
    yg_refs, o_ref = refs[:N_PACKED], refs[N_PACKED]
    gate = jnp.transpose(gate_ref[...])
    gks = [gate[:, k:k + 1] for k in range(TOP_K)]
    lows, highs = [], []
    for yg_ref in yg_refs:
        lo = jnp.zeros((tm, QUARTER), F32)
        hi = jnp.zeros((tm, QUARTER), F32)
        for k in range(TOP_K):
            w = yg_ref[:, k * QUARTER:(k + 1) * QUARTER]
            lo = lo + gks[k] * lax.bitcast_convert_type(w << 16, F32)
            hi = hi + gks[k] * lax.bitcast_convert_type(w & jnp.uint32(0xFFFF0000), F32)
        lows.append(lo)
        highs.append(hi)
    ffn = jnp.concatenate(lows + highs, axis=1)
    o_ref[...] = _layer_norm(alpha * h_ref[...] + ffn, g_ref[...], b_ref[...])


def _combine(h, gate, g, b, yg, alpha):
    t = h.shape[0]
    tm = 256
    kern = functools.partial(_combine_kernel, tm=tm, alpha=alpha)
    tok = lambda i: (i, 0)
    return pl.pallas_call(
        kern,
        grid=(t // tm,),
        in_specs=[
            pl.BlockSpec((tm, D_MODEL), tok),
            pl.BlockSpec((8, tm), lambda i: (0, i)),
            pl.BlockSpec((1, D_MODEL), lambda i: (0, 0)),
            pl.BlockSpec((1, D_MODEL), lambda i: (0, 0)),
        ] + [pl.BlockSpec((tm, TOP_K * QUARTER), tok)] * N_PACKED,
        out_specs=pl.BlockSpec((tm, D_MODEL), tok),
        out_shape=jax.ShapeDtypeStruct((t, D_MODEL), F32),
        compiler_params=pltpu.CompilerParams(
            dimension_semantics=("arbitrary",), vmem_limit_bytes=VMEM_LIMIT),
        name="combine",
    )(h, gate, g, b, *yg)


def _column_split_permutation():
    n = 2 * LANES
    p = np.zeros((n, n), np.float32)
    j = np.arange(n)
    p[j, j // 2 + LANES * (j % 2)] = 1.0
    return jnp.asarray(p, dtype=BF16)


def kernel(x, mem, w_in, lam_q1, lam_k1, lam_q2, lam_k2, diff_norm_w, hgrn_lb_fwd,
           hgrn_lb_bwd, hgrn_norm_w, w_mem_kv, w_o, ln1_g, ln1_b, router_w, router_b,
           w_gate_up, b_gate_up, w_down, b_down, ln2_g, ln2_b):
    batch, seq, d = x.shape
    mlen = mem.shape[1]
    assert d == D_MODEL and w_in.shape == (N_LAYERS, D_MODEL, D_IN_PROJ)
    assert hgrn_lb_fwd.shape == (N_LAYERS + 1, D_HGRN)
    t = batch * seq
    alpha = (2.0 * N_LAYERS) ** 0.25
    lam_init = 0.8 - 0.6 * math.exp(-0.3 * 0)

    x2d = x.reshape(t, D_MODEL)
    ua, uh, um = _in_proj(x2d, w_in[0].astype(BF16), hgrn_lb_fwd, hgrn_lb_bwd)
    o_diff = _diff_attn(ua, diff_norm_w, lam_q1, lam_k1, lam_q2, lam_k2,
                        batch, seq, lam_init)
    o_mem = _mem_attn(um, mem.reshape(batch * mlen, D_MODEL), w_mem_kv[0].astype(BF16),
                      batch, seq, mlen)
    o_hgrn = _hgrn2(uh, hgrn_norm_w, batch, seq)

    h1, h1b, eidx, gate, rank, cnt = _post_mixer(
        x2d, o_diff, o_hgrn, o_mem, w_o[0].astype(BF16), ln1_g, ln1_b, router_w[0].T,
        router_b.reshape(N_EXPERTS, 1), alpha)

    n_blocks = -(-(t * TOP_K) // MOE_BLOCK) + N_EXPERTS
    dest, meta = _route_meta(eidx, rank, cnt, n_blocks)
    dest_km = dest[0:TOP_K]
    block_e = meta[0, 0:n_blocks]
    used = meta[1, 0:1]

    xs = _dispatch(dest_km, h1b, n_blocks * MOE_BLOCK)
    b_g = b_gate_up[0][:, 0::2].reshape(N_EXPERTS, 1, D_EXPERT)
    b_l = b_gate_up[0][:, 1::2].reshape(N_EXPERTS, 1, D_EXPERT)
    ys = _experts(block_e, used, xs, w_gate_up[0], w_down[0], b_g, b_l,
                  b_down[0].reshape(N_EXPERTS, 1, D_MODEL), _column_split_permutation())
    out = _combine(h1, gate, ln2_g, ln2_b, _gather_back(dest_km, ys), alpha)
    return out.reshape(batch, seq, D_MODEL)
```

```python
import functools
import math

import jax
import jax.numpy as jnp
import numpy as np
from jax import lax
from jax.experimental import pallas as pl
from jax.experimental.pallas import tpu as pltpu
from jax.experimental.pallas import tpu_sc as plsc

F32 = jnp.float32
BF16 = jnp.bfloat16
I32 = jnp.int32
U32 = jnp.uint32

D_MODEL = 1024
N_LAYERS = 1
D_DIFF = 512
D_HGRN = 256
D_XMEM = 256
N_HEADS = 4
DH = 64
DH_SHIFT = 6
D_IN_PROJ = 3 * D_DIFF + 5 * D_HGRN + D_XMEM
CHUNK = 16
N_EXPERTS = 32
TOP_K = 4
D_EXPERT = 1024
MOE_BLOCK = 512
SWIGLU_ALPHA = 1.702
SWIGLU_LIMIT = 7.0
NORM_EPS = 1e-5
LOG2E = math.log2(math.e)
LANES = 128
VMEM_LIMIT = 56 * 1024 * 1024


def _nt_dot(a, b):
    return lax.dot_general(a, b, (((1,), (1,)), ((), ())), preferred_element_type=F32)


def _tn_dot(a, b):
    return lax.dot_general(a, b, (((0,), (0,)), ((), ())), preferred_element_type=F32)


def _dot(a, b):
    return jnp.dot(a, b, preferred_element_type=F32)


def _sigmoid(x):
    return 1.0 / (1.0 + jnp.exp(-x))


def _layer_norm(y, g, b):
    mu = jnp.mean(y, axis=-1, keepdims=True)
    yc = y - mu
    var = jnp.mean(yc * yc, axis=-1, keepdims=True)
    return yc * lax.rsqrt(var + NORM_EPS) * g + b


def _in_proj_kernel(x_ref, w_ref, lbf_ref, lbb_ref, ua_ref, uh_ref, um_ref):
    xb = x_ref[...].astype(BF16)

    def proj(c0, width):
        return _dot(xb, w_ref[:, c0:c0 + width])

    ua_ref[:, 0:D_DIFF] = (proj(0, D_DIFF) * (LOG2E / math.sqrt(DH))).astype(BF16)
    ua_ref[:, D_DIFF:2 * D_DIFF] = proj(D_DIFF, D_DIFF).astype(BF16)
    ua_ref[:, 2 * D_DIFF:3 * D_DIFF] = proj(2 * D_DIFF, D_DIFF).astype(BF16)

    def lower_bound(lb_ref):
        a = lb_ref[0:1, :]
        b = lb_ref[1:2, :]
        m = jnp.maximum(a, b)
        ea = jnp.exp(a - m)
        eb = jnp.exp(b - m)
        return ea / (ea + eb)

    base = 3 * D_DIFF
    hq = proj(base, D_HGRN)
    uh_ref[:, 0:D_HGRN] = hq * _sigmoid(hq)
    for d, lb_ref in enumerate((lbf_ref, lbb_ref)):
        lb = lower_bound(lb_ref)
        f = lb + (1.0 - lb) * _sigmoid(proj(base + (1 + d) * D_HGRN, D_HGRN))
        uh_ref[:, (1 + 2 * d) * D_HGRN:(2 + 2 * d) * D_HGRN] = 1.0 - f
        uh_ref[:, (2 + 2 * d) * D_HGRN:(3 + 2 * d) * D_HGRN] = jnp.log(f)
    uh_ref[:, 5 * D_HGRN:6 * D_HGRN] = proj(base + 3 * D_HGRN, D_HGRN)
    uh_ref[:, 6 * D_HGRN:7 * D_HGRN] = _sigmoid(proj(base + 4 * D_HGRN, D_HGRN))
    um_ref[...] = proj(base + 5 * D_HGRN, D_XMEM).astype(BF16)


def _in_proj(x2d, w_in_bf, lb_f, lb_b):
    t = x2d.shape[0]
    tm = 256
    return pl.pallas_call(
        _in_proj_kernel,
        grid=(t // tm,),
        in_specs=[
            pl.BlockSpec((tm, D_MODEL), lambda i: (i, 0)),
            pl.BlockSpec((D_MODEL, D_IN_PROJ), lambda i: (0, 0)),
            pl.BlockSpec((N_LAYERS + 1, D_HGRN), lambda i: (0, 0)),
            pl.BlockSpec((N_LAYERS + 1, D_HGRN), lambda i: (0, 0)),
        ],
        out_specs=[
            pl.BlockSpec((tm, 3 * D_DIFF), lambda i: (i, 0)),
            pl.BlockSpec((tm, 7 * D_HGRN), lambda i: (i, 0)),
            pl.BlockSpec((tm, D_XMEM), lambda i: (i, 0)),
        ],
        out_shape=[
            jax.ShapeDtypeStruct((t, 3 * D_DIFF), BF16),
            jax.ShapeDtypeStruct((t, 7 * D_HGRN), F32),
            jax.ShapeDtypeStruct((t, D_XMEM), BF16),
        ],
        compiler_params=pltpu.CompilerParams(
            dimension_semantics=("arbitrary",), vmem_limit_bytes=VMEM_LIMIT),
        name="in_proj",
    )(x2d, w_in_bf, lb_f, lb_b)


N_POS_FEATURES = 12


def _bf16_pieces(x):
    x = np.asarray(x, np.float32)
    out = []
    for _ in range(3):
        p = x.astype(BF16).astype(np.float32)
        out.append(p)
        x = x - p
    return np.stack(out, axis=-1)


def _alibi_tables(seq, t):
    slopes = (2.0 ** (-8.0 * np.arange(1, N_HEADS + 1) / N_HEADS)).astype(np.float32)
    c3 = _bf16_pieces(slopes * np.float32(LOG2E))
    pos = np.arange(seq)
    hi = (pos >> DH_SHIFT).astype(np.float32)
    lo = (pos & (DH - 1)).astype(np.float32)
    qf = np.zeros((N_HEADS, 2, seq, 2 * DH), np.float32)
    kf = np.zeros((N_HEADS, 2, seq, 2 * DH), np.float32)
    for m, base in enumerate((DH, 0)):
        for j in range(3):
            qf[:, m, :, base + j] = hi
            qf[:, m, :, base + 3 + j] = lo
            qf[:, m, :, base + 6 + j] = 64.0 * c3[:, j, None]
            qf[:, m, :, base + 9 + j] = c3[:, j, None]
            kf[:, m, :, base + j] = -64.0 * c3[:, j, None]
            kf[:, m, :, base + 3 + j] = -c3[:, j, None]
            kf[:, m, :, base + 6 + j] = hi
            kf[:, m, :, base + 9 + j] = lo
    c = c3.sum(axis=-1)
    kk = np.arange(t)[:, None]
    qq = np.arange(t)[None, :]
    corr = 2.0 * c[:, None, None] * np.minimum(qq - kk, 0).astype(np.float32)
    return (jnp.asarray(qf, dtype=BF16), jnp.asarray(kf, dtype=BF16),
            jnp.asarray(corr, dtype=F32))


def _diff_attn_kernel(q_ref, k_ref, v_ref, qf_ref, kf_ref, corr_ref, nw_ref, lq1_ref,
                      lk1_ref, lq2_ref, lk2_ref, o_ref, ka1_ref, ka2_ref, vt_ref, acc_ref,
                      m_ref, l_ref, *, t, qs, seq, lam_init):
    i = pl.program_id(2)
    nk = seq // t
    lane = lax.broadcasted_iota(I32, (t, 2 * DH), 1)
    first_half = lane < DH

    @pl.when(i == 0)
    def _():
        def build(r, carry):
            rows = pl.ds(pl.multiple_of(r * t, t), t)
            kblk = k_ref[rows, :]
            ka1_ref[rows, :] = jnp.where(first_half, kblk, kf_ref[0, 0, rows, :])
            ka2_ref[rows, :] = jnp.where(first_half, kf_ref[0, 1, rows, :], kblk)
            vt_ref[r] = v_ref[rows, :].astype(F32).T.astype(BF16)
            return carry

        lax.fori_loop(0, nk, build, 0)

    q = q_ref[...]
    qf1 = qf_ref[0, 0]
    qf2 = qf_ref[0, 1]
    qa1_before = jnp.where(first_half, q, qf1)
    qa1_after = jnp.where(first_half, q, -qf1)
    qa2_before = jnp.where(first_half, qf2, q)
    qa2_after = jnp.where(first_half, -qf2, q)

    m_ref[...] = jnp.full(m_ref.shape, -jnp.inf, F32)
    l_ref[...] = jnp.zeros(l_ref.shape, F32)
    acc_ref[...] = jnp.zeros(acc_ref.shape, F32)

    def chunk(j, qa1, qa2, diagonal):
        rows = pl.ds(pl.multiple_of(j * t, t), t)
        vt = vt_ref[j]
        for mp, (qa, ka_ref) in enumerate(((qa1, ka1_ref), (qa2, ka2_ref))):
            ka = ka_ref[rows, :]
            for u in range(t // qs):
                qrows = slice(u * qs, (u + 1) * qs)
                cols = slice(mp * t + u * qs, mp * t + (u + 1) * qs)
                s = _nt_dot(ka, qa[qrows, :])
                if diagonal:
                    s = s + corr_ref[0, :, qrows]
                m_old = m_ref[:, cols]
                m_new = jnp.maximum(m_old, jnp.max(s, axis=0, keepdims=True))
                p = jnp.exp2(s - m_new)
                r = jnp.exp2(m_old - m_new)
                l_ref[:, cols] = r * l_ref[:, cols] + jnp.sum(p, axis=0, keepdims=True)
                acc_ref[mp, :, qrows] = (r * acc_ref[mp, :, qrows]
                                         + _dot(vt, p.astype(BF16)))
                m_ref[:, cols] = m_new

    chunk(i, qa1_before, qa2_before, True)
    for jj in range(nk - 1):
        j = jj + (jj >= i).astype(I32)
        keys_first = j < i
        chunk(j, jnp.where(keys_first, qa1_before, qa1_after),
              jnp.where(keys_first, qa2_before, qa2_after), False)

    lam = (jnp.exp(jnp.sum(lq1_ref[...] * lk1_ref[...], axis=-1, keepdims=True))
           - jnp.exp(jnp.sum(lq2_ref[...] * lk2_ref[...], axis=-1, keepdims=True))
           + lam_init)
    o = acc_ref[0] / l_ref[:, 0:t] - lam * (acc_ref[1] / l_ref[:, t:2 * t])
    o = o * lax.rsqrt(jnp.mean(o * o, axis=0, keepdims=True) + NORM_EPS)
    o_ref[...] = (o.T * nw_ref[...] * (1.0 - lam_init)).astype(o_ref.dtype)


def _diff_attn(ua, nw, lq1, lk1, lq2, lk2, batch, seq, lam_init):
    t = 512
    nq = seq // t
    qf, kf, corr = _alibi_tables(seq, t)
    kern = functools.partial(_diff_attn_kernel, t=t, qs=LANES, seq=seq, lam_init=lam_init)
    small = lambda b, h, i: (0, 0)
    return pl.pallas_call(
        kern,
        grid_spec=pltpu.PrefetchScalarGridSpec(
            num_scalar_prefetch=0,
            grid=(batch, N_HEADS, nq),
            in_specs=[
                pl.BlockSpec((t, 2 * DH), lambda b, h, i: (b * nq + i, h)),
                pl.BlockSpec((seq, 2 * DH), lambda b, h, i: (b, N_HEADS + h)),
                pl.BlockSpec((seq, 2 * DH), lambda b, h, i: (b, 2 * N_HEADS + h)),
                pl.BlockSpec((1, 2, t, 2 * DH), lambda b, h, i: (h, 0, i, 0)),
                pl.BlockSpec((1, 2, seq, 2 * DH), lambda b, h, i: (h, 0, 0, 0)),
                pl.BlockSpec((1, t, t), lambda b, h, i: (h, 0, 0)),
                pl.BlockSpec((1, 2 * DH), small),
                pl.BlockSpec((1, DH), small),
                pl.BlockSpec((1, DH), small),
                pl.BlockSpec((1, DH), small),
                pl.BlockSpec((1, DH), small),
            ],
            out_specs=pl.BlockSpec((t, 2 * DH), lambda b, h, i: (b * nq + i, h)),
            scratch_shapes=[
                pltpu.VMEM((seq, 2 * DH), BF16),
                pltpu.VMEM((seq, 2 * DH), BF16),
                pltpu.VMEM((nq, 2 * DH, t), BF16),
                pltpu.VMEM((2, 2 * DH, t), F32),
                pltpu.VMEM((1, 2 * t), F32),
                pltpu.VMEM((1, 2 * t), F32),
            ],
        ),
        out_shape=jax.ShapeDtypeStruct((batch * seq, D_DIFF), BF16),
        compiler_params=pltpu.CompilerParams(
            dimension_semantics=("arbitrary", "arbitrary", "arbitrary"),
            vmem_limit_bytes=VMEM_LIMIT),
        name="diff_attn",
    )(ua, ua, ua, qf, kf, corr, nw, lq1, lk1, lq2, lk2)


def _mem_attn_kernel(q_ref, mem_ref, wkv_ref, o_ref, kv_ref, *, tq, mlen):
    @pl.when(pl.program_id(1) == 0)
    def _():
        kv_ref[...] = _dot(mem_ref[...].astype(BF16), wkv_ref[...]).astype(BF16)

    q = q_ref[...]
    mk = kv_ref[:, 0:D_XMEM]
    mv = kv_ref[:, D_XMEM:2 * D_XMEM]
    qhead = lax.broadcasted_iota(I32, (tq, D_XMEM), 1) >> DH_SHIFT
    vhead = lax.broadcasted_iota(I32, (mlen, D_XMEM), 1) >> DH_SHIFT
    acc = jnp.zeros((tq, D_XMEM), F32)
    for h in range(N_HEADS):
        qh = jnp.where(qhead == h, q, jnp.zeros_like(q))
        s = _nt_dot(qh, mk) * (1.0 / math.sqrt(DH))
        e = jnp.exp(s - jnp.max(s, axis=-1, keepdims=True))
        p = e / jnp.sum(e, axis=-1, keepdims=True)
        vh = jnp.where(vhead == h, mv, jnp.zeros_like(mv))
        acc = acc + _dot(p.astype(BF16), vh)
    o_ref[...] = acc.astype(o_ref.dtype)


def _mem_attn(um, mem2d, wkv_bf, batch, seq, mlen):
    tq = 512
    nq = seq // tq
    kern = functools.partial(_mem_attn_kernel, tq=tq, mlen=mlen)
    return pl.pallas_call(
        kern,
        grid=(batch, nq),
        in_specs=[
            pl.BlockSpec((tq, D_XMEM), lambda b, i: (b * nq + i, 0)),
            pl.BlockSpec((mlen, D_MODEL), lambda b, i: (b, 0)),
            pl.BlockSpec((D_MODEL, 2 * D_XMEM), lambda b, i: (0, 0)),
        ],
        out_specs=pl.BlockSpec((tq, D_XMEM), lambda b, i: (b * nq + i, 0)),
        out_shape=jax.ShapeDtypeStruct((batch * seq, D_XMEM), BF16),
        scratch_shapes=[pltpu.VMEM((mlen, 2 * D_XMEM), BF16)],
        compiler_params=pltpu.CompilerParams(
            dimension_semantics=("arbitrary", "arbitrary"), vmem_limit_bytes=VMEM_LIMIT),
        name="mem_attn",
    )(um, mem2d, wkv_bf)


def _hgrn2_kernel(q_ref, kf_ref, lf_ref, kb_ref, lb_ref, v_ref, g_ref, nw_ref, o_ref,
                  kfp_ref, kbp_ref, vp_ref, bf_ref, bb_ref, acc_ref, cross_ref, *, seq, rb):
    n_chunks = seq // CHUNK
    w = 2 * DH
    row = lax.broadcasted_iota(I32, (rb, w), 0) & (CHUNK - 1)
    li = lax.broadcasted_iota(I32, (w, w), 0) >> DH_SHIFT
    lj = lax.broadcasted_iota(I32, (w, w), 1) >> DH_SHIFT
    same_head = li == lj
    head_ones = jnp.where(same_head, 1.0, 0.0).astype(BF16)

    def chunk_cumsum(x, reverse):
        for sh in (1, 2, 4, 8):
            if reverse:
                moved = pltpu.roll(x, rb - sh, 0)
                keep = row + sh < CHUNK
            else:
                moved = pltpu.roll(x, sh, 0)
                keep = row >= sh
            x = x + jnp.where(keep, moved, 0.0)
        return x

    halo = jnp.zeros((CHUNK, w), F32)
    for ref in (kfp_ref, kbp_ref, vp_ref, bf_ref, bb_ref):
        ref[0:CHUNK, :] = halo
        ref[seq + CHUNK:seq + 2 * CHUNK, :] = halo

    def prepare(blk, carry):
        r0 = pl.multiple_of(blk * rb, rb)
        rows = pl.ds(r0, rb)
        inner = pl.ds(r0 + CHUNK, rb)
        kfp_ref[inner, :] = kf_ref[rows, :]
        kbp_ref[inner, :] = kb_ref[rows, :]
        vp_ref[inner, :] = v_ref[rows, :]
        bf_ref[inner, :] = chunk_cumsum(lf_ref[rows, :] * LOG2E, False)
        bb_ref[inner, :] = chunk_cumsum(lb_ref[rows, :] * LOG2E, True)
        return carry

    lax.fori_loop(0, seq // rb, prepare, 0)

    def intra(blk):
        r0 = pl.multiple_of(blk * rb, rb)
        rows = pl.ds(r0, rb)
        q = q_ref[rows, :]
        acc = jnp.zeros((rb, w), F32)
        for reverse, kp_ref, b_ref in ((False, kfp_ref, bf_ref), (True, kbp_ref, bb_ref)):
            b = b_ref[pl.ds(r0 + CHUNK, rb), :]
            for d in range(CHUNK):
                src = pl.ds(r0 + CHUNK + (d if reverse else -d), rb)
                ks = kp_ref[src, :]
                vs = vp_ref[src, :]
                if d == 0:
                    z = q * ks
                else:
                    keep = (row + d < CHUNK) if reverse else (row >= d)
                    dec = jnp.exp2(jnp.where(keep, b - b_ref[src, :], -jnp.inf))
                    z = q * ks * dec
                acc = acc + _dot(z.astype(BF16), head_ones) * vs
        acc_ref[rows, :] = acc

    def inter(n, carry):
        wf, wb = carry
        rf = pl.multiple_of(n * CHUNK, CHUNK)
        rr = pl.multiple_of((n_chunks - 1 - n) * CHUNK, CHUNK)

        def one(state, r0, kp_ref, b_ref, edge):
            rows = pl.ds(r0, CHUNK)
            prow = pl.ds(r0 + CHUNK, CHUNK)
            b = b_ref[prow, :]
            b_edge = b_ref[pl.ds(r0 + CHUNK + edge, 1), :]
            qd = q_ref[rows, :] * jnp.exp2(b)
            cross_ref[rows, :] = cross_ref[rows, :] + _nt_dot(qd.astype(BF16),
                                                              state.astype(BF16))
            kd = kp_ref[prow, :] * jnp.exp2(b_edge - b)
            upd = _tn_dot(vp_ref[prow, :].astype(BF16), kd.astype(BF16))
            return state * jnp.exp2(b_edge) + jnp.where(same_head, upd, 0.0)

        wf = one(wf, rf, kfp_ref, bf_ref, CHUNK - 1)
        wb = one(wb, rr, kbp_ref, bb_ref, 0)
        return wf, wb

    cross_ref[...] = jnp.zeros(cross_ref.shape, F32)

    def fused(blk, carry):
        intra(blk)
        for c in range(rb // CHUNK):
            carry = inter(blk * (rb // CHUNK) + c, carry)
        return carry

    z = jnp.zeros((w, w), F32)
    lax.fori_loop(0, seq // rb, fused, (z, z))

    def finish(blk, carry):
        rows = pl.ds(pl.multiple_of(blk * rb, rb), rb)
        o = acc_ref[rows, :] + cross_ref[rows, :]
        lane = lax.broadcasted_iota(I32, (rb, w), 1)
        lo = lane < DH
        sq = o * o
        ms_lo = jnp.sum(jnp.where(lo, sq, 0.0), axis=-1, keepdims=True)
        ms_hi = jnp.sum(jnp.where(lo, 0.0, sq), axis=-1, keepdims=True)
        ms = jnp.where(lo, ms_lo, ms_hi) * (1.0 / DH)
        o = o * lax.rsqrt(ms + NORM_EPS) * nw_ref[...] * g_ref[rows, :]
        o_ref[rows, :] = o.astype(o_ref.dtype)
        return carry

    lax.fori_loop(0, seq // rb, finish, 0)


def _hgrn2(uh, nw, batch, seq):
    w = 2 * DH
    npair = D_HGRN // w
    kern = functools.partial(_hgrn2_kernel, seq=seq, rb=128)

    def sec(s):
        return pl.BlockSpec((seq, w), lambda b, p, s=s: (b, s * npair + p))

    return pl.pallas_call(
        kern,
        grid=(batch, npair),
        in_specs=[sec(0), sec(1), sec(2), sec(3), sec(4), sec(5), sec(6),
                  pl.BlockSpec((1, w), lambda b, p: (0, p))],
        out_specs=pl.BlockSpec((seq, w), lambda b, p: (b, p)),
        out_shape=jax.ShapeDtypeStruct((batch * seq, D_HGRN), BF16),
        scratch_shapes=[pltpu.VMEM((seq + 2 * CHUNK, w), F32)] * 5
        + [pltpu.VMEM((seq, w), F32)] * 2,
        compiler_params=pltpu.CompilerParams(
            dimension_semantics=("arbitrary", "arbitrary"), vmem_limit_bytes=VMEM_LIMIT),
        name="hgrn2",
    )(uh, uh, uh, uh, uh, uh, uh, nw)


def _rows_to_block(rows, tm):
    r = lax.broadcasted_iota(I32, (8, tm), 0)
    out = jnp.zeros((8, tm), rows[0].dtype)
    for k, row in enumerate(rows):
        out = jnp.where(r == k, row, out)
    return out


def _pack_bf16_pairs(x):
    m = x.shape[1] // 2
    u = lax.bitcast_convert_type(x, U32)
    r = u + jnp.uint32(0x7FFF) + ((u >> 16) & jnp.uint32(1))
    return (r[:, 0:m] >> 16) | (r[:, m:2 * m] & jnp.uint32(0xFFFF0000))


def _unpack_bf16_pairs(w):
    lo = lax.bitcast_convert_type(w << 16, F32).astype(BF16)
    hi = lax.bitcast_convert_type(w & jnp.uint32(0xFFFF0000), F32).astype(BF16)
    return jnp.concatenate([lo, hi], axis=1)


def _post_mixer_kernel(x_ref, od_ref, oh_ref, om_ref, wo_ref, g_ref, b_ref, rwt_ref,
                       rbt_ref, h_ref, hp_ref, eidx_ref, gate_ref, rank_ref, cnt_ref,
                       carry_ref, *, tm, alpha):
    i = pl.program_id(0)

    @pl.when(i == 0)
    def _():
        carry_ref[...] = jnp.zeros_like(carry_ref)

    mix = (_dot(od_ref[...], wo_ref[0:D_DIFF, :])
           + _dot(oh_ref[...], wo_ref[D_DIFF:D_DIFF + D_HGRN, :])
           + _dot(om_ref[...], wo_ref[D_DIFF + D_HGRN:, :]))
    h = _layer_norm(alpha * x_ref[...] + mix, g_ref[...], b_ref[...])
    h_ref[...] = h

    h_hi = h.astype(BF16)
    hp_ref[...] = _pack_bf16_pairs(h)
    h_lo = (h - h_hi.astype(F32)).astype(BF16)
    rwt = rwt_ref[...]
    rwt_hi = rwt.astype(BF16)
    rwt_lo = (rwt - rwt_hi.astype(F32)).astype(BF16)
    logits = (_nt_dot(rwt_hi, h_hi) + _nt_dot(rwt_hi, h_lo) + _nt_dot(rwt_lo, h_hi)
              + rbt_ref[...])

    erow = lax.broadcasted_iota(I32, (N_EXPERTS, tm), 0).astype(F32)
    work = logits
    sels, vals, idxs = [], [], []
    for _ in range(TOP_K):
        m = jnp.max(work, axis=0, keepdims=True)
        idx = jnp.min(jnp.where(work == m, erow, float(N_EXPERTS)), axis=0, keepdims=True)
        sel = erow == idx
        work = jnp.where(sel, -jnp.inf, work)
        sels.append(sel)
        vals.append(m)
        idxs.append(idx)
    es = [jnp.exp(v - vals[0]) for v in vals]
    den = es[0] + es[1] + es[2] + es[3]

    chosen = jnp.where(sels[0] | sels[1] | sels[2] | sels[3], 1.0, 0.0)
    ti = lax.broadcasted_iota(I32, (tm, tm), 0)
    tj = lax.broadcasted_iota(I32, (tm, tm), 1)
    earlier = jnp.where(ti < tj, 1.0, 0.0).astype(BF16)
    prefix = _dot(chosen.astype(BF16), earlier) + carry_ref[:, 0:1]

    ranks = [jnp.sum(jnp.where(sels[k], prefix, 0.0), axis=0, keepdims=True)
             for k in range(TOP_K)]
    eidx_ref[...] = _rows_to_block(idxs, tm).astype(I32)
    gate_ref[...] = _rows_to_block([e / den for e in es], tm)
    rank_ref[...] = _rows_to_block(ranks, tm).astype(I32)

    total = carry_ref[:, 0:1] + jnp.sum(chosen, axis=1, keepdims=True)
    carry_ref[...] = jnp.broadcast_to(total, carry_ref.shape)
    cnt_ref[...] = carry_ref[...]


def _post_mixer(x2d, od, oh, om, wo_bf, g, b, rwt, rbt, alpha):
    t = x2d.shape[0]
    tm = 512
    kern = functools.partial(_post_mixer_kernel, tm=tm, alpha=alpha)
    full = lambda i: (0, 0)
    tok = lambda i: (i, 0)
    per_tok = lambda i: (0, i)
    return pl.pallas_call(
        kern,
        grid=(t // tm,),
        in_specs=[
            pl.BlockSpec((tm, D_MODEL), tok),
            pl.BlockSpec((tm, D_DIFF), tok),
            pl.BlockSpec((tm, D_HGRN), tok),
            pl.BlockSpec((tm, D_XMEM), tok),
            pl.BlockSpec((D_MODEL, D_MODEL), full),
            pl.BlockSpec((1, D_MODEL), full),
            pl.BlockSpec((1, D_MODEL), full),
            pl.BlockSpec((N_EXPERTS, D_MODEL), full),
            pl.BlockSpec((N_EXPERTS, 1), full),
        ],
        out_specs=[
            pl.BlockSpec((tm, D_MODEL), tok),
            pl.BlockSpec((tm, D_MODEL // 2), tok),
            pl.BlockSpec((8, tm), per_tok),
            pl.BlockSpec((8, tm), per_tok),
            pl.BlockSpec((8, tm), per_tok),
            pl.BlockSpec((N_EXPERTS, LANES), full),
        ],
        out_shape=[
            jax.ShapeDtypeStruct((t, D_MODEL), F32),
            jax.ShapeDtypeStruct((t, D_MODEL // 2), U32),
            jax.ShapeDtypeStruct((8, t), I32),
            jax.ShapeDtypeStruct((8, t), F32),
            jax.ShapeDtypeStruct((8, t), I32),
            jax.ShapeDtypeStruct((N_EXPERTS, LANES), F32),
        ],
        scratch_shapes=[pltpu.VMEM((N_EXPERTS, LANES), F32)],
        compiler_params=pltpu.CompilerParams(
            dimension_semantics=("arbitrary",), vmem_limit_bytes=VMEM_LIMIT),
        name="post_mixer",
    )(x2d, od, oh, om, wo_bf, g, b, rwt, rbt)


def _route_meta_kernel(eidx_ref, rank_ref, cnt_ref, dest_ref, be_ref, *, tm, nb_lanes):
    cnt = cnt_ref[...]
    blocks = jnp.floor((cnt + (MOE_BLOCK - 1)) * (1.0 / MOE_BLOCK))
    ei = lax.broadcasted_iota(I32, (N_EXPERTS, N_EXPERTS), 0)
    ej = lax.broadcasted_iota(I32, (N_EXPERTS, N_EXPERTS), 1)
    below = jnp.where(ej < ei, 1.0, 0.0).astype(BF16)
    pstart = _dot(below, blocks.astype(BF16))[:, 0:1]
    pend = pstart + blocks[:, 0:1]

    erow = lax.broadcasted_iota(I32, (N_EXPERTS, tm), 0).astype(F32)
    eidx = eidx_ref[...].astype(F32)
    starts = [jnp.sum(jnp.where(erow == eidx[k:k + 1, :], pstart, 0.0), axis=0, keepdims=True)
              for k in range(TOP_K)]
    dest_ref[...] = ((_rows_to_block(starts, tm) * float(MOE_BLOCK)).astype(I32)
                     + rank_ref[...])

    bi = lax.broadcasted_iota(I32, (N_EXPERTS, nb_lanes), 1).astype(F32)
    done = jnp.where(pend <= bi, 1.0, 0.0)
    be = jnp.minimum(jnp.sum(done, axis=0, keepdims=True), float(N_EXPERTS - 1))
    used = jnp.broadcast_to(pend[N_EXPERTS - 1:N_EXPERTS, :], (1, nb_lanes))
    be_ref[...] = _rows_to_block([be, used], nb_lanes).astype(I32)


def _route_meta(eidx, rank, cnt, n_blocks):
    t = eidx.shape[1]
    tm = 2048
    nb_lanes = -(-n_blocks // LANES) * LANES
    kern = functools.partial(_route_meta_kernel, tm=tm, nb_lanes=nb_lanes)
    return pl.pallas_call(
        kern,
        grid=(t // tm,),
        in_specs=[
            pl.BlockSpec((8, tm), lambda i: (0, i)),
            pl.BlockSpec((8, tm), lambda i: (0, i)),
            pl.BlockSpec((N_EXPERTS, LANES), lambda i: (0, 0)),
        ],
        out_specs=[
            pl.BlockSpec((8, tm), lambda i: (0, i)),
            pl.BlockSpec((8, nb_lanes), lambda i: (0, 0)),
        ],
        out_shape=[
            jax.ShapeDtypeStruct((8, t), I32),
            jax.ShapeDtypeStruct((8, nb_lanes), I32),
        ],
        compiler_params=pltpu.CompilerParams(
            dimension_semantics=("arbitrary",), vmem_limit_bytes=VMEM_LIMIT),
        name="route_meta",
    )(eidx, rank, cnt)


SC_WINDOW = LANES
N_QUARTERS = 4
QUARTER = D_MODEL // N_QUARTERS
N_PACKED = D_MODEL // 2 // QUARTER


def _sc_mesh():
    return plsc.VectorSubcoreMesh(core_axis_name="c", subcore_axis_name="s")


def _dispatch(dest_km, hp, n_rows):
    t = hp.shape[0]

    @functools.partial(
        pl.kernel, out_type=[jax.ShapeDtypeStruct((n_rows, QUARTER), hp.dtype)] * N_PACKED,
        mesh=_sc_mesh(), scratch_types=[])
    def scatter_rows(h_hbm, d_hbm, *xs_hbm):
        for q in range(N_PACKED):
            def body(x_vmem, i_vmem, q=q):
                for k in range(TOP_K):
                    pltpu.sync_copy(x_vmem, xs_hbm[q].at[i_vmem.at[k]])

            pltpu.emit_pipeline(
                body,
                grid=(t // SC_WINDOW,),
                in_specs=[pl.BlockSpec((SC_WINDOW, QUARTER), lambda i, q=q: (i, q)),
                          pl.BlockSpec((TOP_K, SC_WINDOW), lambda i: (0, i))],
                out_specs=[],
                core_axis_name=("c", "s"),
                dimension_semantics=(pltpu.PARALLEL,),
            )(h_hbm, d_hbm)

    return scatter_rows(hp, dest_km)


def _gather_back(dest_km, ys):
    t = dest_km.shape[1]
    n = len(ys)

    @functools.partial(
        pl.kernel,
        out_type=[jax.ShapeDtypeStruct((t, TOP_K * QUARTER), ys[0].dtype)] * n,
        mesh=_sc_mesh(), scratch_types=[])
    def gather_rows(d_hbm, *refs):
        ys_hbm, out_hbm = refs[:n], refs[n:]
        for q in range(n):
            for k in range(TOP_K):
                def body(i_vmem, o_vmem, q=q, k=k):
                    pltpu.sync_copy(ys_hbm[q].at[i_vmem.at[k]], o_vmem)

                pltpu.emit_pipeline(
                    body,
                    grid=(t // SC_WINDOW,),
                    in_specs=[pl.BlockSpec((TOP_K, SC_WINDOW), lambda i: (0, i))],
                    out_specs=[pl.BlockSpec((SC_WINDOW, QUARTER), lambda i, k=k: (i, k))],
                    core_axis_name=("c", "s"),
                    dimension_semantics=(pltpu.PARALLEL,),
                )(d_hbm, out_hbm[q])

    return gather_rows(dest_km, *ys)


def _experts_kernel(be_ref, used_ref, *refs):
    xs_refs = refs[:N_PACKED]
    wgu_hbm, wdn_hbm, bg_ref, bl_ref, bd_ref, perm_ref = refs[N_PACKED:N_PACKED + 6]
    ys_refs = refs[N_PACKED + 6:2 * N_PACKED + 6]
    wgu_st, wdn_st, wg_ref, wl_ref, wd_ref, sem = refs[2 * N_PACKED + 6:]
    i = pl.program_id(0)
    used = used_ref[0]
    e = be_ref[i]
    prev = be_ref[jnp.maximum(i - 1, 0)]
    fresh = jnp.logical_and(i < used, jnp.logical_or(i == 0, e != prev))

    def weight_copies(ex):
        return (pltpu.make_async_copy(wgu_hbm.at[ex], wgu_st, sem.at[0]),
                pltpu.make_async_copy(wdn_hbm.at[ex], wdn_st, sem.at[1]))

    @pl.when(jnp.logical_and(i == 0, used > 0))
    def _():
        for cp in weight_copies(e):
            cp.start()

    @pl.when(fresh)
    def _():
        for cp in weight_copies(e):
            cp.wait()
        perm = perm_ref[...]
        half = LANES
        for c in range(2 * D_EXPERT // (2 * half)):
            slab = wgu_st[:, c * 2 * half:(c + 1) * 2 * half].astype(BF16)
            sep = _dot(slab, perm)
            wg_ref[:, c * half:(c + 1) * half] = sep[:, 0:half].astype(BF16)
            wl_ref[:, c * half:(c + 1) * half] = sep[:, half:2 * half].astype(BF16)
        wd_ref[...] = wdn_st[...].astype(BF16)

        def same_expert(j):
            return jnp.logical_and(j < used, be_ref[jnp.minimum(j, used - 1)] == e)

        nxt = lax.while_loop(same_expert, lambda j: j + 1, i + 1)

        @pl.when(nxt < used)
        def _():
            for cp in weight_copies(be_ref[nxt]):
                cp.start()

    @pl.when(i < used_ref[0])
    def _():
        xb = _unpack_bf16_pairs(jnp.concatenate([r[...] for r in xs_refs], axis=1))
        glu = jnp.minimum(_dot(xb, wg_ref[...]) + bg_ref[0], SWIGLU_LIMIT)
        lin = jnp.clip(_dot(xb, wl_ref[...]) + bl_ref[0], -SWIGLU_LIMIT, SWIGLU_LIMIT)
        act = glu * _sigmoid(SWIGLU_ALPHA * glu) * (lin + 1.0)
        y = _pack_bf16_pairs(_dot(act.astype(BF16), wd_ref[...]) + bd_ref[0])
        for q, ys_ref in enumerate(ys_refs):
            ys_ref[...] = y[:, q * QUARTER:(q + 1) * QUARTER]

    @pl.when(i >= used_ref[0])
    def _():
        for ys_ref in ys_refs:
            ys_ref[...] = jnp.zeros_like(ys_ref)


def _experts(block_e, used, xs, w_gu, w_dn, b_g, b_l, b_d, perm):
    n_rows = xs[0].shape[0]
    n_blocks = n_rows // MOE_BLOCK
    wmap = lambda i, be, u: (be[i], 0, 0)
    rows = pl.BlockSpec((MOE_BLOCK, QUARTER), lambda i, be, u: (i, 0))
    return pl.pallas_call(
        _experts_kernel,
        grid_spec=pltpu.PrefetchScalarGridSpec(
            num_scalar_prefetch=2,
            grid=(n_blocks,),
            in_specs=[rows] * N_PACKED + [
                pl.BlockSpec(memory_space=pl.ANY),
                pl.BlockSpec(memory_space=pl.ANY),
                pl.BlockSpec((1, 1, D_EXPERT), wmap),
                pl.BlockSpec((1, 1, D_EXPERT), wmap),
                pl.BlockSpec((1, 1, D_MODEL), wmap),
                pl.BlockSpec((2 * LANES, 2 * LANES), lambda i, be, u: (0, 0)),
            ],
            out_specs=[rows] * N_PACKED,
            scratch_shapes=[
                pltpu.VMEM((D_MODEL, 2 * D_EXPERT), F32),
                pltpu.VMEM((D_EXPERT, D_MODEL), F32),
                pltpu.VMEM((D_MODEL, D_EXPERT), BF16),
                pltpu.VMEM((D_MODEL, D_EXPERT), BF16),
                pltpu.VMEM((D_EXPERT, D_MODEL), BF16),
                pltpu.SemaphoreType.DMA((2,)),
            ],
        ),
        out_shape=[jax.ShapeDtypeStruct((n_rows, QUARTER), U32)] * N_PACKED,
        compiler_params=pltpu.CompilerParams(
            dimension_semantics=("arbitrary",), vmem_limit_bytes=VMEM_LIMIT),
        name="experts",
    )(block_e, used, *xs, w_gu, w_dn, b_g, b_l, b_d, perm)


def _combine_kernel(h_ref, gate_ref, g_ref, b_ref, *refs, tm, alpha):
    yg_refs, o_ref = refs[:N_PACKED], refs[N_PACKED]
    gate = jnp.transpose(gate_ref[...])
    gks = [gate[:, k:k + 1] for k in range(TOP_K)]
    lows, highs = [], []
    for yg_ref in yg_refs:
        lo = jnp.zeros((tm, QUARTER), F32)
        hi = jnp.zeros((tm, QUARTER), F32)
        for k in range(TOP_K):
            w = yg_ref[:, k * QUARTER:(k + 1) * QUARTER]
            lo = lo + gks[k] * lax.bitcast_convert_type(w << 16, F32)
            hi = hi + gks[k] * lax.bitcast_convert_type(w & jnp.uint32(0xFFFF0000), F32)
        lows.append(lo)
        highs.append(hi)
    ffn = jnp.concatenate(lows + highs, axis=1)
    o_ref[...] = _layer_norm(alpha * h_ref[...] + ffn, g_ref[...], b_ref[...])


def _combine(h, gate, g, b, yg, alpha):
    t = h.shape[0]
    tm = 256
    kern = functools.partial(_combine_kernel, tm=tm, alpha=alpha)
    tok = lambda i: (i, 0)
    return pl.pallas_call(
        kern,
        grid=(t // tm,),
        in_specs=[
            pl.BlockSpec((tm, D_MODEL), tok),
            pl.BlockSpec((8, tm), lambda i: (0, i)),
            pl.BlockSpec((1, D_MODEL), lambda i: (0, 0)),
            pl.BlockSpec((1, D_MODEL), lambda i: (0, 0)),
        ] + [pl.BlockSpec((tm, TOP_K * QUARTER), tok)] * N_PACKED,
        out_specs=pl.BlockSpec((tm, D_MODEL), tok),
        out_shape=jax.ShapeDtypeStruct((t, D_MODEL), F32),
        compiler_params=pltpu.CompilerParams(
            dimension_semantics=("arbitrary",), vmem_limit_bytes=VMEM_LIMIT),
        name="combine",
    )(h, gate, g, b, *yg)


def _column_split_permutation():
    n = 2 * LANES
    p = np.zeros((n, n), np.float32)
    j = np.arange(n)
    p[j, j // 2 + LANES * (j % 2)] = 1.0
    return jnp.asarray(p, dtype=BF16)


def kernel(x, mem, w_in, lam_q1, lam_k1, lam_q2, lam_k2, diff_norm_w, hgrn_lb_fwd,
           hgrn_lb_bwd, hgrn_norm_w, w_mem_kv, w_o, ln1_g, ln1_b, router_w, router_b,
           w_gate_up, b_gate_up, w_down, b_down, ln2_g, ln2_b):
    batch, seq, d = x.shape
    mlen = mem.shape[1]
    assert d == D_MODEL and w_in.shape == (N_LAYERS, D_MODEL, D_IN_PROJ)
    assert hgrn_lb_fwd.shape == (N_LAYERS + 1, D_HGRN)
    t = batch * seq
    alpha = (2.0 * N_LAYERS) ** 0.25
    lam_init = 0.8 - 0.6 * math.exp(-0.3 * 0)

    x2d = x.reshape(t, D_MODEL)
    ua, uh, um = _in_proj(x2d, w_in[0].astype(BF16), hgrn_lb_fwd, hgrn_lb_bwd)
    o_diff = _diff_attn(ua, diff_norm_w, lam_q1, lam_k1, lam_q2, lam_k2,
                        batch, seq, lam_init)
    o_mem = _mem_attn(um, mem.reshape(batch * mlen, D_MODEL), w_mem_kv[0].astype(BF16),
                      batch, seq, mlen)
    o_hgrn = _hgrn2(uh, hgrn_norm_w, batch, seq)

    h1, h1b, eidx, gate, rank, cnt = _post_mixer(
        x2d, o_diff, o_hgrn, o_mem, w_o[0].astype(BF16), ln1_g, ln1_b, router_w[0].T,
        router_b.reshape(N_EXPERTS, 1), alpha)

    n_blocks = -(-(t * TOP_K) // MOE_BLOCK) + N_EXPERTS
    dest, meta = _route_meta(eidx, rank, cnt, n_blocks)
    dest_km = dest[0:TOP_K]
    block_e = meta[0, 0:n_blocks]
    used = meta[1, 0:1]

    xs = _dispatch(dest_km, h1b, n_blocks * MOE_BLOCK)
    b_g = b_gate_up[0][:, 0::2].reshape(N_EXPERTS, 1, D_EXPERT)
    b_l = b_gate_up[0][:, 1::2].reshape(N_EXPERTS, 1, D_EXPERT)
    ys = _experts(block_e, used, xs, w_gate_up[0], w_down[0], b_g, b_l,
                  b_down[0].reshape(N_EXPERTS, 1, D_MODEL), _column_split_permutation())
    out = _combine(h1, gate, ln2_g, ln2_b, _gather_back(dest_km, ys), alpha)
    return out.reshape(batch, seq, D_MODEL)
```

```python
import functools
import math

import jax
import jax.numpy as jnp
import numpy as np
from jax import lax
from jax.experimental import pallas as pl
from jax.experimental.pallas import tpu as pltpu
from jax.experimental.pallas import tpu_sc as plsc

F32 = jnp.float32
BF16 = jnp.bfloat16
I32 = jnp.int32
U32 = jnp.uint32

D_MODEL = 1024
N_LAYERS = 1
D_DIFF = 512
D_HGRN = 256
D_XMEM = 256
N_HEADS = 4
DH = 64
DH_SHIFT = 6
D_IN_PROJ = 3 * D_DIFF + 5 * D_HGRN + D_XMEM
CHUNK = 16
N_EXPERTS = 32
TOP_K = 4
D_EXPERT = 1024
MOE_BLOCK = 512
SWIGLU_ALPHA = 1.702
SWIGLU_LIMIT = 7.0
NORM_EPS = 1e-5
LOG2E = math.log2(math.e)
LANES = 128
VMEM_LIMIT = 56 * 1024 * 1024


def _nt_dot(a, b):
    return lax.dot_general(a, b, (((1,), (1,)), ((), ())), preferred_element_type=F32)


def _tn_dot(a, b):
    return lax.dot_general(a, b, (((0,), (0,)), ((), ())), preferred_element_type=F32)


def _dot(a, b):
    return jnp.dot(a, b, preferred_element_type=F32)


def _sigmoid(x):
    return 1.0 / (1.0 + jnp.exp(-x))


def _layer_norm(y, g, b):
    mu = jnp.mean(y, axis=-1, keepdims=True)
    yc = y - mu
    var = jnp.mean(yc * yc, axis=-1, keepdims=True)
    return yc * lax.rsqrt(var + NORM_EPS) * g + b


def _in_proj_kernel(x_ref, w_ref, lbf_ref, lbb_ref, ua_ref, uh_ref, um_ref):
    xb = x_ref[...].astype(BF16)

    def proj(c0, width):
        return _dot(xb, w_ref[:, c0:c0 + width])

    ua_ref[:, 0:D_DIFF] = (proj(0, D_DIFF) * (LOG2E / math.sqrt(DH))).astype(BF16)
    ua_ref[:, D_DIFF:2 * D_DIFF] = proj(D_DIFF, D_DIFF).astype(BF16)
    ua_ref[:, 2 * D_DIFF:3 * D_DIFF] = proj(2 * D_DIFF, D_DIFF).astype(BF16)

    def lower_bound(lb_ref):
        a = lb_ref[0:1, :]
        b = lb_ref[1:2, :]
        m = jnp.maximum(a, b)
        ea = jnp.exp(a - m)
        eb = jnp.exp(b - m)
        return ea / (ea + eb)

    base = 3 * D_DIFF
    hq = proj(base, D_HGRN)
    uh_ref[:, 0:D_HGRN] = hq * _sigmoid(hq)
    for d, lb_ref in enumerate((lbf_ref, lbb_ref)):
        lb = lower_bound(lb_ref)
        f = lb + (1.0 - lb) * _sigmoid(proj(base + (1 + d) * D_HGRN, D_HGRN))
        uh_ref[:, (1 + 2 * d) * D_HGRN:(2 + 2 * d) * D_HGRN] = 1.0 - f
        uh_ref[:, (2 + 2 * d) * D_HGRN:(3 + 2 * d) * D_HGRN] = jnp.log(f)
    uh_ref[:, 5 * D_HGRN:6 * D_HGRN] = proj(base + 3 * D_HGRN, D_HGRN)
    uh_ref[:, 6 * D_HGRN:7 * D_HGRN] = _sigmoid(proj(base + 4 * D_HGRN, D_HGRN))
    um_ref[...] = proj(base + 5 * D_HGRN, D_XMEM).astype(BF16)


def _in_proj(x2d, w_in_bf, lb_f, lb_b):
    t = x2d.shape[0]
    tm = 1024
    return pl.pallas_call(
        _in_proj_kernel,
        grid=(t // tm,),
        in_specs=[
            pl.BlockSpec((tm, D_MODEL), lambda i: (i, 0)),
            pl.BlockSpec((D_MODEL, D_IN_PROJ), lambda i: (0, 0)),
            pl.BlockSpec((N_LAYERS + 1, D_HGRN), lambda i: (0, 0)),
            pl.BlockSpec((N_LAYERS + 1, D_HGRN), lambda i: (0, 0)),
        ],
        out_specs=[
            pl.BlockSpec((tm, 3 * D_DIFF), lambda i: (i, 0)),
            pl.BlockSpec((tm, 7 * D_HGRN), lambda i: (i, 0)),
            pl.BlockSpec((tm, D_XMEM), lambda i: (i, 0)),
        ],
        out_shape=[
            jax.ShapeDtypeStruct((t, 3 * D_DIFF), BF16),
            jax.ShapeDtypeStruct((t, 7 * D_HGRN), F32),
            jax.ShapeDtypeStruct((t, D_XMEM), BF16),
        ],
        compiler_params=pltpu.CompilerParams(
            dimension_semantics=("arbitrary",), vmem_limit_bytes=VMEM_LIMIT),
        name="in_proj",
    )(x2d, w_in_bf, lb_f, lb_b)


N_POS_FEATURES = 12


def _bf16_pieces(x):
    x = np.asarray(x, np.float32)
    out = []
    for _ in range(3):
        p = x.astype(BF16).astype(np.float32)
        out.append(p)
        x = x - p
    return np.stack(out, axis=-1)


def _alibi_tables(seq, t):
    slopes = (2.0 ** (-8.0 * np.arange(1, N_HEADS + 1) / N_HEADS)).astype(np.float32)
    c3 = _bf16_pieces(slopes * np.float32(LOG2E))
    pos = np.arange(seq)
    hi = (pos >> DH_SHIFT).astype(np.float32)
    lo = (pos & (DH - 1)).astype(np.float32)
    qf = np.zeros((N_HEADS, 2, seq, 2 * DH), np.float32)
    kf = np.zeros((N_HEADS, 2, seq, 2 * DH), np.float32)
    for m, base in enumerate((DH, 0)):
        for j in range(3):
            qf[:, m, :, base + j] = hi
            qf[:, m, :, base + 3 + j] = lo
            qf[:, m, :, base + 6 + j] = 64.0 * c3[:, j, None]
            qf[:, m, :, base + 9 + j] = c3[:, j, None]
            kf[:, m, :, base + j] = -64.0 * c3[:, j, None]
            kf[:, m, :, base + 3 + j] = -c3[:, j, None]
            kf[:, m, :, base + 6 + j] = hi
            kf[:, m, :, base + 9 + j] = lo
    c = c3.sum(axis=-1)
    kk = np.arange(t)[:, None]
    qq = np.arange(t)[None, :]
    corr = 2.0 * c[:, None, None] * np.minimum(qq - kk, 0).astype(np.float32)
    return (jnp.asarray(qf, dtype=BF16), jnp.asarray(kf, dtype=BF16),
            jnp.asarray(corr, dtype=F32))


def _diff_attn_kernel(q_ref, k_ref, v_ref, qf_ref, kf_ref, corr_ref, nw_ref, lq1_ref,
                      lk1_ref, lq2_ref, lk2_ref, o_ref, ka1_ref, ka2_ref, vt_ref, acc_ref,
                      m_ref, l_ref, *, t, qs, seq, lam_init):
    i = pl.program_id(2)
    nk = seq // t
    lane = lax.broadcasted_iota(I32, (t, 2 * DH), 1)
    first_half = lane < DH

    @pl.when(i == 0)
    def _():
        def build(r, carry):
            rows = pl.ds(pl.multiple_of(r * t, t), t)
            kblk = k_ref[rows, :]
            ka1_ref[rows, :] = jnp.where(first_half, kblk, kf_ref[0, 0, rows, :])
            ka2_ref[rows, :] = jnp.where(first_half, kf_ref[0, 1, rows, :], kblk)
            vt_ref[r] = v_ref[rows, :].astype(F32).T.astype(BF16)
            return carry

        lax.fori_loop(0, nk, build, 0)

    q = q_ref[...]
    qf1 = qf_ref[0, 0]
    qf2 = qf_ref[0, 1]
    qa1_before = jnp.where(first_half, q, qf1)
    qa1_after = jnp.where(first_half, q, -qf1)
    qa2_before = jnp.where(first_half, qf2, q)
    qa2_after = jnp.where(first_half, -qf2, q)

    m_ref[...] = jnp.full(m_ref.shape, -jnp.inf, F32)
    l_ref[...] = jnp.zeros(l_ref.shape, F32)
    acc_ref[...] = jnp.zeros(acc_ref.shape, F32)

    def chunk(j, qa1, qa2, diagonal):
        rows = pl.ds(pl.multiple_of(j * t, t), t)
        vt = vt_ref[j]
        for mp, (qa, ka_ref) in enumerate(((qa1, ka1_ref), (qa2, ka2_ref))):
            ka = ka_ref[rows, :]
            for u in range(t // qs):
                qrows = slice(u * qs, (u + 1) * qs)
                cols = slice(mp * t + u * qs, mp * t + (u + 1) * qs)
                s = _nt_dot(ka, qa[qrows, :])
                if diagonal:
                    s = s + corr_ref[0, :, qrows]
                m_old = m_ref[:, cols]
                m_new = jnp.maximum(m_old, jnp.max(s, axis=0, keepdims=True))
                p = jnp.exp2(s - m_new)
                r = jnp.exp2(m_old - m_new)
                l_ref[:, cols] = r * l_ref[:, cols] + jnp.sum(p, axis=0, keepdims=True)
                acc_ref[mp, :, qrows] = (r * acc_ref[mp, :, qrows]
                                         + _dot(vt, p.astype(BF16)))
                m_ref[:, cols] = m_new

    chunk(i, qa1_before, qa2_before, True)
    for jj in range(nk - 1):
        j = jj + (jj >= i).astype(I32)
        keys_first = j < i
        chunk(j, jnp.where(keys_first, qa1_before, qa1_after),
              jnp.where(keys_first, qa2_before, qa2_after), False)

    lam = (jnp.exp(jnp.sum(lq1_ref[...] * lk1_ref[...], axis=-1, keepdims=True))
           - jnp.exp(jnp.sum(lq2_ref[...] * lk2_ref[...], axis=-1, keepdims=True))
           + lam_init)
    o = acc_ref[0] / l_ref[:, 0:t] - lam * (acc_ref[1] / l_ref[:, t:2 * t])
    o = o * lax.rsqrt(jnp.mean(o * o, axis=0, keepdims=True) + NORM_EPS)
    o_ref[...] = (o.T * nw_ref[...] * (1.0 - lam_init)).astype(o_ref.dtype)


def _diff_attn(ua, nw, lq1, lk1, lq2, lk2, batch, seq, lam_init):
    t = 512
    nq = seq // t
    qf, kf, corr = _alibi_tables(seq, t)
    kern = functools.partial(_diff_attn_kernel, t=t, qs=LANES, seq=seq, lam_init=lam_init)
    small = lambda b, h, i: (0, 0)
    return pl.pallas_call(
        kern,
        grid_spec=pltpu.PrefetchScalarGridSpec(
            num_scalar_prefetch=0,
            grid=(batch, N_HEADS, nq),
            in_specs=[
                pl.BlockSpec((t, 2 * DH), lambda b, h, i: (b * nq + i, h)),
                pl.BlockSpec((seq, 2 * DH), lambda b, h, i: (b, N_HEADS + h)),
                pl.BlockSpec((seq, 2 * DH), lambda b, h, i: (b, 2 * N_HEADS + h)),
                pl.BlockSpec((1, 2, t, 2 * DH), lambda b, h, i: (h, 0, i, 0)),
                pl.BlockSpec((1, 2, seq, 2 * DH), lambda b, h, i: (h, 0, 0, 0)),
                pl.BlockSpec((1, t, t), lambda b, h, i: (h, 0, 0)),
                pl.BlockSpec((1, 2 * DH), small),
                pl.BlockSpec((1, DH), small),
                pl.BlockSpec((1, DH), small),
                pl.BlockSpec((1, DH), small),
                pl.BlockSpec((1, DH), small),
            ],
            out_specs=pl.BlockSpec((t, 2 * DH), lambda b, h, i: (b * nq + i, h)),
            scratch_shapes=[
                pltpu.VMEM((seq, 2 * DH), BF16),
                pltpu.VMEM((seq, 2 * DH), BF16),
                pltpu.VMEM((nq, 2 * DH, t), BF16),
                pltpu.VMEM((2, 2 * DH, t), F32),
                pltpu.VMEM((1, 2 * t), F32),
                pltpu.VMEM((1, 2 * t), F32),
            ],
        ),
        out_shape=jax.ShapeDtypeStruct((batch * seq, D_DIFF), BF16),
        compiler_params=pltpu.CompilerParams(
            dimension_semantics=("arbitrary", "arbitrary", "arbitrary"),
            vmem_limit_bytes=VMEM_LIMIT),
        name="diff_attn",
    )(ua, ua, ua, qf, kf, corr, nw, lq1, lk1, lq2, lk2)


def _mem_attn_kernel(q_ref, mem_ref, wkv_ref, o_ref, kv_ref, *, tq, mlen):
    @pl.when(pl.program_id(1) == 0)
    def _():
        kv_ref[...] = _dot(mem_ref[...].astype(BF16), wkv_ref[...]).astype(BF16)

    q = q_ref[...]
    mk = kv_ref[:, 0:D_XMEM]
    mv = kv_ref[:, D_XMEM:2 * D_XMEM]
    qhead = lax.broadcasted_iota(I32, (tq, D_XMEM), 1) >> DH_SHIFT
    vhead = lax.broadcasted_iota(I32, (mlen, D_XMEM), 1) >> DH_SHIFT
    acc = jnp.zeros((tq, D_XMEM), F32)
    for h in range(N_HEADS):
        qh = jnp.where(qhead == h, q, jnp.zeros_like(q))
        s = _nt_dot(qh, mk) * (1.0 / math.sqrt(DH))
        e = jnp.exp(s - jnp.max(s, axis=-1, keepdims=True))
        p = e / jnp.sum(e, axis=-1, keepdims=True)
        vh = jnp.where(vhead == h, mv, jnp.zeros_like(mv))
        acc = acc + _dot(p.astype(BF16), vh)
    o_ref[...] = acc.astype(o_ref.dtype)


def _mem_attn(um, mem2d, wkv_bf, batch, seq, mlen):
    tq = 1024
    nq = seq // tq
    kern = functools.partial(_mem_attn_kernel, tq=tq, mlen=mlen)
    return pl.pallas_call(
        kern,
        grid=(batch, nq),
        in_specs=[
            pl.BlockSpec((tq, D_XMEM), lambda b, i: (b * nq + i, 0)),
            pl.BlockSpec((mlen, D_MODEL), lambda b, i: (b, 0)),
            pl.BlockSpec((D_MODEL, 2 * D_XMEM), lambda b, i: (0, 0)),
        ],
        out_specs=pl.BlockSpec((tq, D_XMEM), lambda b, i: (b * nq + i, 0)),
        out_shape=jax.ShapeDtypeStruct((batch * seq, D_XMEM), BF16),
        scratch_shapes=[pltpu.VMEM((mlen, 2 * D_XMEM), BF16)],
        compiler_params=pltpu.CompilerParams(
            dimension_semantics=("arbitrary", "arbitrary"), vmem_limit_bytes=VMEM_LIMIT),
        name="mem_attn",
    )(um, mem2d, wkv_bf)


def _hgrn2_kernel(q_ref, kf_ref, lf_ref, kb_ref, lb_ref, v_ref, g_ref, nw_ref, o_ref,
                  kfp_ref, kbp_ref, vp_ref, bf_ref, bb_ref, acc_ref, cross_ref, *, seq, rb):
    n_chunks = seq // CHUNK
    w = 2 * DH
    row = lax.broadcasted_iota(I32, (rb, w), 0) & (CHUNK - 1)
    li = lax.broadcasted_iota(I32, (w, w), 0) >> DH_SHIFT
    lj = lax.broadcasted_iota(I32, (w, w), 1) >> DH_SHIFT
    same_head = li == lj
    head_ones = jnp.where(same_head, 1.0, 0.0).astype(BF16)

    def chunk_cumsum(x, reverse):
        for sh in (1, 2, 4, 8):
            if reverse:
                moved = pltpu.roll(x, rb - sh, 0)
                keep = row + sh < CHUNK
            else:
                moved = pltpu.roll(x, sh, 0)
                keep = row >= sh
            x = x + jnp.where(keep, moved, 0.0)
        return x

    halo = jnp.zeros((CHUNK, w), F32)
    for ref in (kfp_ref, kbp_ref, vp_ref, bf_ref, bb_ref):
        ref[0:CHUNK, :] = halo
        ref[seq + CHUNK:seq + 2 * CHUNK, :] = halo

    def prepare(blk, carry):
        r0 = pl.multiple_of(blk * rb, rb)
        rows = pl.ds(r0, rb)
        inner = pl.ds(r0 + CHUNK, rb)
        kfp_ref[inner, :] = kf_ref[rows, :]
        kbp_ref[inner, :] = kb_ref[rows, :]
        vp_ref[inner, :] = v_ref[rows, :]
        bf_ref[inner, :] = chunk_cumsum(lf_ref[rows, :] * LOG2E, False)
        bb_ref[inner, :] = chunk_cumsum(lb_ref[rows, :] * LOG2E, True)
        return carry

    lax.fori_loop(0, seq // rb, prepare, 0)

    def intra(blk):
        r0 = pl.multiple_of(blk * rb, rb)
        rows = pl.ds(r0, rb)
        q = q_ref[rows, :]
        acc = jnp.zeros((rb, w), F32)
        for reverse, kp_ref, b_ref in ((False, kfp_ref, bf_ref), (True, kbp_ref, bb_ref)):
            b = b_ref[pl.ds(r0 + CHUNK, rb), :]
            for d in range(CHUNK):
                src = pl.ds(r0 + CHUNK + (d if reverse else -d), rb)
                ks = kp_ref[src, :]
                vs = vp_ref[src, :]
                if d == 0:
                    z = q * ks
                else:
                    keep = (row + d < CHUNK) if reverse else (row >= d)
                    dec = jnp.exp2(jnp.where(keep, b - b_ref[src, :], -jnp.inf))
                    z = q * ks * dec
                acc = acc + _dot(z.astype(BF16), head_ones) * vs
        acc_ref[rows, :] = acc

    def inter(n, carry):
        wf, wb = carry
        rf = pl.multiple_of(n * CHUNK, CHUNK)
        rr = pl.multiple_of((n_chunks - 1 - n) * CHUNK, CHUNK)

        def one(state, r0, kp_ref, b_ref, edge):
            rows = pl.ds(r0, CHUNK)
            prow = pl.ds(r0 + CHUNK, CHUNK)
            b = b_ref[prow, :]
            b_edge = b_ref[pl.ds(r0 + CHUNK + edge, 1), :]
            qd = q_ref[rows, :] * jnp.exp2(b)
            cross_ref[rows, :] = cross_ref[rows, :] + _nt_dot(qd.astype(BF16),
                                                              state.astype(BF16))
            kd = kp_ref[prow, :] * jnp.exp2(b_edge - b)
            upd = _tn_dot(vp_ref[prow, :].astype(BF16), kd.astype(BF16))
            return state * jnp.exp2(b_edge) + jnp.where(same_head, upd, 0.0)

        wf = one(wf, rf, kfp_ref, bf_ref, CHUNK - 1)
        wb = one(wb, rr, kbp_ref, bb_ref, 0)
        return wf, wb

    cross_ref[...] = jnp.zeros(cross_ref.shape, F32)

    def fused(blk, carry):
        intra(blk)
        for c in range(rb // CHUNK):
            carry = inter(blk * (rb // CHUNK) + c, carry)
        return carry

    z = jnp.zeros((w, w), F32)
    lax.fori_loop(0, seq // rb, fused, (z, z))

    def finish(blk, carry):
        rows = pl.ds(pl.multiple_of(blk * rb, rb), rb)
        o = acc_ref[rows, :] + cross_ref[rows, :]
        lane = lax.broadcasted_iota(I32, (rb, w), 1)
        lo = lane < DH
        sq = o * o
        ms_lo = jnp.sum(jnp.where(lo, sq, 0.0), axis=-1, keepdims=True)
        ms_hi = jnp.sum(jnp.where(lo, 0.0, sq), axis=-1, keepdims=True)
        ms = jnp.where(lo, ms_lo, ms_hi) * (1.0 / DH)
        o = o * lax.rsqrt(ms + NORM_EPS) * nw_ref[...] * g_ref[rows, :]
        o_ref[rows, :] = o.astype(o_ref.dtype)
        return carry

    lax.fori_loop(0, seq // rb, finish, 0)


def _hgrn2(uh, nw, batch, seq):
    w = 2 * DH
    npair = D_HGRN // w
    kern = functools.partial(_hgrn2_kernel, seq=seq, rb=256)

    def sec(s):
        return pl.BlockSpec((seq, w), lambda b, p, s=s: (b, s * npair + p))

    return pl.pallas_call(
        kern,
        grid=(batch, npair),
        in_specs=[sec(0), sec(1), sec(2), sec(3), sec(4), sec(5), sec(6),
                  pl.BlockSpec((1, w), lambda b, p: (0, p))],
        out_specs=pl.BlockSpec((seq, w), lambda b, p: (b, p)),
        out_shape=jax.ShapeDtypeStruct((batch * seq, D_HGRN), BF16),
        scratch_shapes=[pltpu.VMEM((seq + 2 * CHUNK, w), F32)] * 5
        + [pltpu.VMEM((seq, w), F32)] * 2,
        compiler_params=pltpu.CompilerParams(
            dimension_semantics=("arbitrary", "arbitrary"), vmem_limit_bytes=VMEM_LIMIT),
        name="hgrn2",
    )(uh, uh, uh, uh, uh, uh, uh, nw)


def _rows_to_block(rows, tm):
    r = lax.broadcasted_iota(I32, (8, tm), 0)
    out = jnp.zeros((8, tm), rows[0].dtype)
    for k, row in enumerate(rows):
        out = jnp.where(r == k, row, out)
    return out


def _pack_bf16_pairs(x):
    m = x.shape[1] // 2
    u = lax.bitcast_convert_type(x, U32)
    r = u + jnp.uint32(0x7FFF) + ((u >> 16) & jnp.uint32(1))
    return (r[:, 0:m] >> 16) | (r[:, m:2 * m] & jnp.uint32(0xFFFF0000))


def _unpack_bf16_pairs(w):
    lo = lax.bitcast_convert_type(w << 16, F32).astype(BF16)
    hi = lax.bitcast_convert_type(w & jnp.uint32(0xFFFF0000), F32).astype(BF16)
    return jnp.concatenate([lo, hi], axis=1)


def _post_mixer_kernel(x_ref, od_ref, oh_ref, om_ref, wo_ref, g_ref, b_ref, rwt_ref,
                       rbt_ref, h_ref, hp_ref, eidx_ref, gate_ref, rank_ref, cnt_ref,
                       carry_ref, *, tm, alpha):
    i = pl.program_id(0)

    @pl.when(i == 0)
    def _():
        carry_ref[...] = jnp.zeros_like(carry_ref)

    mix = (_dot(od_ref[...], wo_ref[0:D_DIFF, :])
           + _dot(oh_ref[...], wo_ref[D_DIFF:D_DIFF + D_HGRN, :])
           + _dot(om_ref[...], wo_ref[D_DIFF + D_HGRN:, :]))
    h = _layer_norm(alpha * x_ref[...] + mix, g_ref[...], b_ref[...])
    h_ref[...] = h

    h_hi = h.astype(BF16)
    hp_ref[...] = _pack_bf16_pairs(h)
    h_lo = (h - h_hi.astype(F32)).astype(BF16)
    rwt = rwt_ref[...]
    rwt_hi = rwt.astype(BF16)
    rwt_lo = (rwt - rwt_hi.astype(F32)).astype(BF16)
    logits = (_nt_dot(rwt_hi, h_hi) + _nt_dot(rwt_hi, h_lo) + _nt_dot(rwt_lo, h_hi)
              + rbt_ref[...])

    erow = lax.broadcasted_iota(I32, (N_EXPERTS, tm), 0).astype(F32)
    work = logits
    sels, vals, idxs = [], [], []
    for _ in range(TOP_K):
        m = jnp.max(work, axis=0, keepdims=True)
        idx = jnp.min(jnp.where(work == m, erow, float(N_EXPERTS)), axis=0, keepdims=True)
        sel = erow == idx
        work = jnp.where(sel, -jnp.inf, work)
        sels.append(sel)
        vals.append(m)
        idxs.append(idx)
    es = [jnp.exp(v - vals[0]) for v in vals]
    den = es[0] + es[1] + es[2] + es[3]

    chosen = jnp.where(sels[0] | sels[1] | sels[2] | sels[3], 1.0, 0.0)
    ti = lax.broadcasted_iota(I32, (tm, tm), 0)
    tj = lax.broadcasted_iota(I32, (tm, tm), 1)
    earlier = jnp.where(ti < tj, 1.0, 0.0).astype(BF16)
    prefix = _dot(chosen.astype(BF16), earlier) + carry_ref[:, 0:1]

    ranks = [jnp.sum(jnp.where(sels[k], prefix, 0.0), axis=0, keepdims=True)
             for k in range(TOP_K)]
    eidx_ref[...] = _rows_to_block(idxs, tm).astype(I32)
    gate_ref[...] = _rows_to_block([e / den for e in es], tm)
    rank_ref[...] = _rows_to_block(ranks, tm).astype(I32)

    total = carry_ref[:, 0:1] + jnp.sum(chosen, axis=1, keepdims=True)
    carry_ref[...] = jnp.broadcast_to(total, carry_ref.shape)
    cnt_ref[...] = carry_ref[...]


def _post_mixer(x2d, od, oh, om, wo_bf, g, b, rwt, rbt, alpha):
    t = x2d.shape[0]
    tm = 512
    kern = functools.partial(_post_mixer_kernel, tm=tm, alpha=alpha)
    full = lambda i: (0, 0)
    tok = lambda i: (i, 0)
    per_tok = lambda i: (0, i)
    return pl.pallas_call(
        kern,
        grid=(t // tm,),
        in_specs=[
            pl.BlockSpec((tm, D_MODEL), tok),
            pl.BlockSpec((tm, D_DIFF), tok),
            pl.BlockSpec((tm, D_HGRN), tok),
            pl.BlockSpec((tm, D_XMEM), tok),
            pl.BlockSpec((D_MODEL, D_MODEL), full),
            pl.BlockSpec((1, D_MODEL), full),
            pl.BlockSpec((1, D_MODEL), full),
            pl.BlockSpec((N_EXPERTS, D_MODEL), full),
            pl.BlockSpec((N_EXPERTS, 1), full),
        ],
        out_specs=[
            pl.BlockSpec((tm, D_MODEL), tok),
            pl.BlockSpec((tm, D_MODEL // 2), tok),
            pl.BlockSpec((8, tm), per_tok),
            pl.BlockSpec((8, tm), per_tok),
            pl.BlockSpec((8, tm), per_tok),
            pl.BlockSpec((N_EXPERTS, LANES), full),
        ],
        out_shape=[
            jax.ShapeDtypeStruct((t, D_MODEL), F32),
            jax.ShapeDtypeStruct((t, D_MODEL // 2), U32),
            jax.ShapeDtypeStruct((8, t), I32),
            jax.ShapeDtypeStruct((8, t), F32),
            jax.ShapeDtypeStruct((8, t), I32),
            jax.ShapeDtypeStruct((N_EXPERTS, LANES), F32),
        ],
        scratch_shapes=[pltpu.VMEM((N_EXPERTS, LANES), F32)],
        compiler_params=pltpu.CompilerParams(
            dimension_semantics=("arbitrary",), vmem_limit_bytes=VMEM_LIMIT),
        name="post_mixer",
    )(x2d, od, oh, om, wo_bf, g, b, rwt, rbt)


def _route_meta_kernel(eidx_ref, rank_ref, cnt_ref, dest_ref, be_ref, *, tm, nb_lanes):
    cnt = cnt_ref[...]
    blocks = jnp.floor((cnt + (MOE_BLOCK - 1)) * (1.0 / MOE_BLOCK))
    ei = lax.broadcasted_iota(I32, (N_EXPERTS, N_EXPERTS), 0)
    ej = lax.broadcasted_iota(I32, (N_EXPERTS, N_EXPERTS), 1)
    below = jnp.where(ej < ei, 1.0, 0.0).astype(BF16)
    pstart = _dot(below, blocks.astype(BF16))[:, 0:1]
    pend = pstart + blocks[:, 0:1]

    erow = lax.broadcasted_iota(I32, (N_EXPERTS, tm), 0).astype(F32)
    eidx = eidx_ref[...].astype(F32)
    starts = [jnp.sum(jnp.where(erow == eidx[k:k + 1, :], pstart, 0.0), axis=0, keepdims=True)
              for k in range(TOP_K)]
    dest_ref[...] = ((_rows_to_block(starts, tm) * float(MOE_BLOCK)).astype(I32)
                     + rank_ref[...])

    bi = lax.broadcasted_iota(I32, (N_EXPERTS, nb_lanes), 1).astype(F32)
    done = jnp.where(pend <= bi, 1.0, 0.0)
    be = jnp.minimum(jnp.sum(done, axis=0, keepdims=True), float(N_EXPERTS - 1))
    used = jnp.broadcast_to(pend[N_EXPERTS - 1:N_EXPERTS, :], (1, nb_lanes))
    be_ref[...] = _rows_to_block([be, used], nb_lanes).astype(I32)


def _route_meta(eidx, rank, cnt, n_blocks):
    t = eidx.shape[1]
    tm = 2048
    nb_lanes = -(-n_blocks // LANES) * LANES
    kern = functools.partial(_route_meta_kernel, tm=tm, nb_lanes=nb_lanes)
    return pl.pallas_call(
        kern,
        grid=(t // tm,),
        in_specs=[
            pl.BlockSpec((8, tm), lambda i: (0, i)),
            pl.BlockSpec((8, tm), lambda i: (0, i)),
            pl.BlockSpec((N_EXPERTS, LANES), lambda i: (0, 0)),
        ],
        out_specs=[
            pl.BlockSpec((8, tm), lambda i: (0, i)),
            pl.BlockSpec((8, nb_lanes), lambda i: (0, 0)),
        ],
        out_shape=[
            jax.ShapeDtypeStruct((8, t), I32),
            jax.ShapeDtypeStruct((8, nb_lanes), I32),
        ],
        compiler_params=pltpu.CompilerParams(
            dimension_semantics=("arbitrary",), vmem_limit_bytes=VMEM_LIMIT),
        name="route_meta",
    )(eidx, rank, cnt)


SC_WINDOW = LANES
N_QUARTERS = 4
QUARTER = D_MODEL // N_QUARTERS
N_PACKED = D_MODEL // 2 // QUARTER


def _sc_mesh():
    return plsc.VectorSubcoreMesh(core_axis_name="c", subcore_axis_name="s")


def _dispatch(dest_km, hp, n_rows):
    t = hp.shape[0]

    @functools.partial(
        pl.kernel, out_type=[jax.ShapeDtypeStruct((n_rows, QUARTER), hp.dtype)] * N_PACKED,
        mesh=_sc_mesh(), scratch_types=[])
    def scatter_rows(h_hbm, d_hbm, *xs_hbm):
        for q in range(N_PACKED):
            def body(x_vmem, i_vmem, q=q):
                for k in range(TOP_K):
                    pltpu.sync_copy(x_vmem, xs_hbm[q].at[i_vmem.at[k]])

            pltpu.emit_pipeline(
                body,
                grid=(t // SC_WINDOW,),
                in_specs=[pl.BlockSpec((SC_WINDOW, QUARTER), lambda i, q=q: (i, q)),
                          pl.BlockSpec((TOP_K, SC_WINDOW), lambda i: (0, i))],
                out_specs=[],
                core_axis_name=("c", "s"),
                dimension_semantics=(pltpu.PARALLEL,),
            )(h_hbm, d_hbm)

    return scatter_rows(hp, dest_km)


def _gather_back(dest_km, ys):
    t = dest_km.shape[1]
    n = len(ys)

    @functools.partial(
        pl.kernel,
        out_type=[jax.ShapeDtypeStruct((t, TOP_K * QUARTER), ys[0].dtype)] * n,
        mesh=_sc_mesh(), scratch_types=[])
    def gather_rows(d_hbm, *refs):
        ys_hbm, out_hbm = refs[:n], refs[n:]
        for q in range(n):
            for k in range(TOP_K):
                def body(i_vmem, o_vmem, q=q, k=k):
                    pltpu.sync_copy(ys_hbm[q].at[i_vmem.at[k]], o_vmem)

                pltpu.emit_pipeline(
                    body,
                    grid=(t // SC_WINDOW,),
                    in_specs=[pl.BlockSpec((TOP_K, SC_WINDOW), lambda i: (0, i))],
                    out_specs=[pl.BlockSpec((SC_WINDOW, QUARTER), lambda i, k=k: (i, k))],
                    core_axis_name=("c", "s"),
                    dimension_semantics=(pltpu.PARALLEL,),
                )(d_hbm, out_hbm[q])

    return gather_rows(dest_km, *ys)


def _experts_kernel(be_ref, used_ref, *refs):
    xs_refs = refs[:N_PACKED]
    wgu_hbm, wdn_hbm, bg_ref, bl_ref, bd_ref, perm_ref = refs[N_PACKED:N_PACKED + 6]
    ys_refs = refs[N_PACKED + 6:2 * N_PACKED + 6]
    wgu_st, wdn_st, wg_ref, wl_ref, wd_ref, sem = refs[2 * N_PACKED + 6:]
    i = pl.program_id(0)
    used = used_ref[0]
    e = be_ref[i]
    prev = be_ref[jnp.maximum(i - 1, 0)]
    fresh = jnp.logical_and(i < used, jnp.logical_or(i == 0, e != prev))

    def weight_copies(ex):
        return (pltpu.make_async_copy(wgu_hbm.at[ex], wgu_st, sem.at[0]),
                pltpu.make_async_copy(wdn_hbm.at[ex], wdn_st, sem.at[1]))

    @pl.when(jnp.logical_and(i == 0, used > 0))
    def _():
        for cp in weight_copies(e):
            cp.start()

    @pl.when(fresh)
    def _():
        for cp in weight_copies(e):
            cp.wait()
        perm = perm_ref[...]
        half = LANES
        for c in range(2 * D_EXPERT // (2 * half)):
            slab = wgu_st[:, c * 2 * half:(c + 1) * 2 * half].astype(BF16)
            sep = _dot(slab, perm)
            wg_ref[:, c * half:(c + 1) * half] = sep[:, 0:half].astype(BF16)
            wl_ref[:, c * half:(c + 1) * half] = sep[:, half:2 * half].astype(BF16)
        wd_ref[...] = wdn_st[...].astype(BF16)

        def same_expert(j):
            return jnp.logical_and(j < used, be_ref[jnp.minimum(j, used - 1)] == e)

        nxt = lax.while_loop(same_expert, lambda j: j + 1, i + 1)

        @pl.when(nxt < used)
        def _():
            for cp in weight_copies(be_ref[nxt]):
                cp.start()

    @pl.when(i < used_ref[0])
    def _():
        xb = _unpack_bf16_pairs(jnp.concatenate([r[...] for r in xs_refs], axis=1))
        glu = jnp.minimum(_dot(xb, wg_ref[...]) + bg_ref[0], SWIGLU_LIMIT)
        lin = jnp.clip(_dot(xb, wl_ref[...]) + bl_ref[0], -SWIGLU_LIMIT, SWIGLU_LIMIT)
        act = glu * _sigmoid(SWIGLU_ALPHA * glu) * (lin + 1.0)
        y = _pack_bf16_pairs(_dot(act.astype(BF16), wd_ref[...]) + bd_ref[0])
        for q, ys_ref in enumerate(ys_refs):
            ys_ref[...] = y[:, q * QUARTER:(q + 1) * QUARTER]

    @pl.when(i >= used_ref[0])
    def _():
        for ys_ref in ys_refs:
            ys_ref[...] = jnp.zeros_like(ys_ref)


def _experts(block_e, used, xs, w_gu, w_dn, b_g, b_l, b_d, perm):
    n_rows = xs[0].shape[0]
    n_blocks = n_rows // MOE_BLOCK
    wmap = lambda i, be, u: (be[i], 0, 0)
    rows = pl.BlockSpec((MOE_BLOCK, QUARTER), lambda i, be, u: (i, 0))
    return pl.pallas_call(
        _experts_kernel,
        grid_spec=pltpu.PrefetchScalarGridSpec(
            num_scalar_prefetch=2,
            grid=(n_blocks,),
            in_specs=[rows] * N_PACKED + [
                pl.BlockSpec(memory_space=pl.ANY),
                pl.BlockSpec(memory_space=pl.ANY),
                pl.BlockSpec((1, 1, D_EXPERT), wmap),
                pl.BlockSpec((1, 1, D_EXPERT), wmap),
                pl.BlockSpec((1, 1, D_MODEL), wmap),
                pl.BlockSpec((2 * LANES, 2 * LANES), lambda i, be, u: (0, 0)),
            ],
            out_specs=[rows] * N_PACKED,
            scratch_shapes=[
                pltpu.VMEM((D_MODEL, 2 * D_EXPERT), F32),
                pltpu.VMEM((D_EXPERT, D_MODEL), F32),
                pltpu.VMEM((D_MODEL, D_EXPERT), BF16),
                pltpu.VMEM((D_MODEL, D_EXPERT), BF16),
                pltpu.VMEM((D_EXPERT, D_MODEL), BF16),
                pltpu.SemaphoreType.DMA((2,)),
            ],
        ),
        out_shape=[jax.ShapeDtypeStruct((n_rows, QUARTER), U32)] * N_PACKED,
        compiler_params=pltpu.CompilerParams(
            dimension_semantics=("arbitrary",), vmem_limit_bytes=VMEM_LIMIT),
        name="experts",
    )(block_e, used, *xs, w_gu, w_dn, b_g, b_l, b_d, perm)


def _combine_kernel(h_ref, gate_ref, g_ref, b_ref, *refs, tm, alpha):
    yg_refs, o_ref = refs[:N_PACKED], refs[N_PACKED]
    gate = jnp.transpose(gate_ref[...])
    gks = [gate[:, k:k + 1] for k in range(TOP_K)]
    lows, highs = [], []
    for yg_ref in yg_refs:
        lo = jnp.zeros((tm, QUARTER), F32)
        hi = jnp.zeros((tm, QUARTER), F32)
        for k in range(TOP_K):
            w = yg_ref[:, k * QUARTER:(k + 1) * QUARTER]
            lo = lo + gks[k] * lax.bitcast_convert_type(w << 16, F32)
            hi = hi + gks[k] * lax.bitcast_convert_type(w & jnp.uint32(0xFFFF0000), F32)
        lows.append(lo)
        highs.append(hi)
    ffn = jnp.concatenate(lows + highs, axis=1)
    o_ref[...] = _layer_norm(alpha * h_ref[...] + ffn, g_ref[...], b_ref[...])


def _combine(h, gate, g, b, yg, alpha):
    t = h.shape[0]
    tm = 256
    kern = functools.partial(_combine_kernel, tm=tm, alpha=alpha)
    tok = lambda i: (i, 0)
    return pl.pallas_call(
        kern,
        grid=(t // tm,),
        in_specs=[
            pl.BlockSpec((tm, D_MODEL), tok),
            pl.BlockSpec((8, tm), lambda i: (0, i)),
            pl.BlockSpec((1, D_MODEL), lambda i: (0, 0)),
            pl.BlockSpec((1, D_MODEL), lambda i: (0, 0)),
        ] + [pl.BlockSpec((tm, TOP_K * QUARTER), tok)] * N_PACKED,
        out_specs=pl.BlockSpec((tm, D_MODEL), tok),
        out_shape=jax.ShapeDtypeStruct((t, D_MODEL), F32),
        compiler_params=pltpu.CompilerParams(
            dimension_semantics=("arbitrary",), vmem_limit_bytes=VMEM_LIMIT),
        name="combine",
    )(h, gate, g, b, *yg)


def _column_split_permutation():
    n = 2 * LANES
    p = np.zeros((n, n), np.float32)
    j = np.arange(n)
    p[j, j // 2 + LANES * (j % 2)] = 1.0
    return jnp.asarray(p, dtype=BF16)


def kernel(x, mem, w_in, lam_q1, lam_k1, lam_q2, lam_k2, diff_norm_w, hgrn_lb_fwd,
           hgrn_lb_bwd, hgrn_norm_w, w_mem_kv, w_o, ln1_g, ln1_b, router_w, router_b,
           w_gate_up, b_gate_up, w_down, b_down, ln2_g, ln2_b):
    batch, seq, d = x.shape
    mlen = mem.shape[1]
    assert d == D_MODEL and w_in.shape == (N_LAYERS, D_MODEL, D_IN_PROJ)
    assert hgrn_lb_fwd.shape == (N_LAYERS + 1, D_HGRN)
    t = batch * seq
    alpha = (2.0 * N_LAYERS) ** 0.25
    lam_init = 0.8 - 0.6 * math.exp(-0.3 * 0)

    x2d = x.reshape(t, D_MODEL)
    ua, uh, um = _in_proj(x2d, w_in[0].astype(BF16), hgrn_lb_fwd, hgrn_lb_bwd)
    o_diff = _diff_attn(ua, diff_norm_w, lam_q1, lam_k1, lam_q2, lam_k2,
                        batch, seq, lam_init)
    o_mem = _mem_attn(um, mem.reshape(batch * mlen, D_MODEL), w_mem_kv[0].astype(BF16),
                      batch, seq, mlen)
    o_hgrn = _hgrn2(uh, hgrn_norm_w, batch, seq)

    h1, h1b, eidx, gate, rank, cnt = _post_mixer(
        x2d, o_diff, o_hgrn, o_mem, w_o[0].astype(BF16), ln1_g, ln1_b, router_w[0].T,
        router_b.reshape(N_EXPERTS, 1), alpha)

    n_blocks = -(-(t * TOP_K) // MOE_BLOCK) + N_EXPERTS
    dest, meta = _route_meta(eidx, rank, cnt, n_blocks)
    dest_km = dest[0:TOP_K]
    block_e = meta[0, 0:n_blocks]
    used = meta[1, 0:1]

    xs = _dispatch(dest_km, h1b, n_blocks * MOE_BLOCK)
    b_g = b_gate_up[0][:, 0::2].reshape(N_EXPERTS, 1, D_EXPERT)
    b_l = b_gate_up[0][:, 1::2].reshape(N_EXPERTS, 1, D_EXPERT)
    ys = _experts(block_e, used, xs, w_gate_up[0], w_down[0], b_g, b_l,
                  b_down[0].reshape(N_EXPERTS, 1, D_MODEL), _column_split_permutation())
    out = _combine(h1, gate, ln2_g, ln2_b, _gather_back(dest_km, ys), alpha)
    return out.reshape(batch, seq, D_MODEL)
```

```python
import functools
import math

import jax
import jax.numpy as jnp
import numpy as np
from jax import lax
from jax.experimental import pallas as pl
from jax.experimental.pallas import tpu as pltpu
from jax.experimental.pallas import tpu_sc as plsc

F32 = jnp.float32
BF16 = jnp.bfloat16
I32 = jnp.int32
U32 = jnp.uint32

D_MODEL = 1024
N_LAYERS = 1
D_DIFF = 512
D_HGRN = 256
D_XMEM = 256
N_HEADS = 4
DH = 64
DH_SHIFT = 6
D_IN_PROJ = 3 * D_DIFF + 5 * D_HGRN + D_XMEM
CHUNK = 16
N_EXPERTS = 32
TOP_K = 4
D_EXPERT = 1024
MOE_BLOCK = 512
SWIGLU_ALPHA = 1.702
SWIGLU_LIMIT = 7.0
NORM_EPS = 1e-5
LOG2E = math.log2(math.e)
LANES = 128
VMEM_LIMIT = 56 * 1024 * 1024


def _nt_dot(a, b):
    return lax.dot_general(a, b, (((1,), (1,)), ((), ())), preferred_element_type=F32)


def _tn_dot(a, b):
    return lax.dot_general(a, b, (((0,), (0,)), ((), ())), preferred_element_type=F32)


def _dot(a, b):
    return jnp.dot(a, b, preferred_element_type=F32)


def _sigmoid(x):
    return 1.0 / (1.0 + jnp.exp(-x))


def _layer_norm(y, g, b):
    mu = jnp.mean(y, axis=-1, keepdims=True)
    yc = y - mu
    var = jnp.mean(yc * yc, axis=-1, keepdims=True)
    return yc * lax.rsqrt(var + NORM_EPS) * g + b


def _in_proj_kernel(x_ref, w_ref, lbf_ref, lbb_ref, ua_ref, uh_ref, um_ref):
    xb = x_ref[...].astype(BF16)

    def proj(c0, width):
        return _dot(xb, w_ref[:, c0:c0 + width])

    ua_ref[:, 0:D_DIFF] = (proj(0, D_DIFF) * (LOG2E / math.sqrt(DH))).astype(BF16)
    ua_ref[:, D_DIFF:2 * D_DIFF] = proj(D_DIFF, D_DIFF).astype(BF16)
    ua_ref[:, 2 * D_DIFF:3 * D_DIFF] = proj(2 * D_DIFF, D_DIFF).astype(BF16)

    def lower_bound(lb_ref):
        a = lb_ref[0:1, :]
        b = lb_ref[1:2, :]
        m = jnp.maximum(a, b)
        ea = jnp.exp(a - m)
        eb = jnp.exp(b - m)
        return ea / (ea + eb)

    base = 3 * D_DIFF
    hq = proj(base, D_HGRN)
    uh_ref[:, 0:D_HGRN] = hq * _sigmoid(hq)
    for d, lb_ref in enumerate((lbf_ref, lbb_ref)):
        lb = lower_bound(lb_ref)
        f = lb + (1.0 - lb) * _sigmoid(proj(base + (1 + d) * D_HGRN, D_HGRN))
        uh_ref[:, (1 + 2 * d) * D_HGRN:(2 + 2 * d) * D_HGRN] = 1.0 - f
        uh_ref[:, (2 + 2 * d) * D_HGRN:(3 + 2 * d) * D_HGRN] = jnp.log(f)
    uh_ref[:, 5 * D_HGRN:6 * D_HGRN] = proj(base + 3 * D_HGRN, D_HGRN)
    uh_ref[:, 6 * D_HGRN:7 * D_HGRN] = _sigmoid(proj(base + 4 * D_HGRN, D_HGRN))
    um_ref[...] = proj(base + 5 * D_HGRN, D_XMEM).astype(BF16)


def _in_proj(x2d, w_in_bf, lb_f, lb_b):
    t = x2d.shape[0]
    tm = 1024
    return pl.pallas_call(
        _in_proj_kernel,
        grid=(t // tm,),
        in_specs=[
            pl.BlockSpec((tm, D_MODEL), lambda i: (i, 0)),
            pl.BlockSpec((D_MODEL, D_IN_PROJ), lambda i: (0, 0)),
            pl.BlockSpec((N_LAYERS + 1, D_HGRN), lambda i: (0, 0)),
            pl.BlockSpec((N_LAYERS + 1, D_HGRN), lambda i: (0, 0)),
        ],
        out_specs=[
            pl.BlockSpec((tm, 3 * D_DIFF), lambda i: (i, 0)),
            pl.BlockSpec((tm, 7 * D_HGRN), lambda i: (i, 0)),
            pl.BlockSpec((tm, D_XMEM), lambda i: (i, 0)),
        ],
        out_shape=[
            jax.ShapeDtypeStruct((t, 3 * D_DIFF), BF16),
            jax.ShapeDtypeStruct((t, 7 * D_HGRN), F32),
            jax.ShapeDtypeStruct((t, D_XMEM), BF16),
        ],
        compiler_params=pltpu.CompilerParams(
            dimension_semantics=("arbitrary",), vmem_limit_bytes=VMEM_LIMIT),
        name="in_proj",
    )(x2d, w_in_bf, lb_f, lb_b)


N_POS_FEATURES = 12


def _bf16_pieces(x):
    x = np.asarray(x, np.float32)
    out = []
    for _ in range(3):
        p = x.astype(BF16).astype(np.float32)
        out.append(p)
        x = x - p
    return np.stack(out, axis=-1)


def _alibi_tables(seq, t):
    slopes = (2.0 ** (-8.0 * np.arange(1, N_HEADS + 1) / N_HEADS)).astype(np.float32)
    c3 = _bf16_pieces(slopes * np.float32(LOG2E))
    pos = np.arange(seq)
    hi = (pos >> DH_SHIFT).astype(np.float32)
    lo = (pos & (DH - 1)).astype(np.float32)
    qf = np.zeros((N_HEADS, 2, seq, 2 * DH), np.float32)
    kf = np.zeros((N_HEADS, 2, seq, 2 * DH), np.float32)
    for m, base in enumerate((DH, 0)):
        for j in range(3):
            qf[:, m, :, base + j] = hi
            qf[:, m, :, base + 3 + j] = lo
            qf[:, m, :, base + 6 + j] = 64.0 * c3[:, j, None]
            qf[:, m, :, base + 9 + j] = c3[:, j, None]
            kf[:, m, :, base + j] = -64.0 * c3[:, j, None]
            kf[:, m, :, base + 3 + j] = -c3[:, j, None]
            kf[:, m, :, base + 6 + j] = hi
            kf[:, m, :, base + 9 + j] = lo
    c = c3.sum(axis=-1)
    kk = np.arange(t)[:, None]
    qq = np.arange(t)[None, :]
    corr = 2.0 * c[:, None, None] * np.minimum(qq - kk, 0).astype(np.float32)
    return (jnp.asarray(qf, dtype=BF16), jnp.asarray(kf, dtype=BF16),
            jnp.asarray(corr, dtype=F32))


def _diff_attn_kernel(q_ref, k_ref, v_ref, qf_ref, kf_ref, corr_ref, nw_ref, lq1_ref,
                      lk1_ref, lq2_ref, lk2_ref, o_ref, ka1_ref, ka2_ref, vt_ref, acc_ref,
                      m_ref, l_ref, *, t, qs, seq, lam_init):
    i = pl.program_id(2)
    nk = seq // t
    lane = lax.broadcasted_iota(I32, (t, 2 * DH), 1)
    first_half = lane < DH

    @pl.when(i == 0)
    def _():
        def build(r, carry):
            rows = pl.ds(pl.multiple_of(r * t, t), t)
            kblk = k_ref[rows, :]
            ka1_ref[rows, :] = jnp.where(first_half, kblk, kf_ref[0, 0, rows, :])
            ka2_ref[rows, :] = jnp.where(first_half, kf_ref[0, 1, rows, :], kblk)
            vt_ref[r] = v_ref[rows, :].astype(F32).T.astype(BF16)
            return carry

        lax.fori_loop(0, nk, build, 0)

    q = q_ref[...]
    qf1 = qf_ref[0, 0]
    qf2 = qf_ref[0, 1]
    qa1_before = jnp.where(first_half, q, qf1)
    qa1_after = jnp.where(first_half, q, -qf1)
    qa2_before = jnp.where(first_half, qf2, q)
    qa2_after = jnp.where(first_half, -qf2, q)

    m_ref[...] = jnp.full(m_ref.shape, -jnp.inf, F32)
    l_ref[...] = jnp.zeros(l_ref.shape, F32)
    acc_ref[...] = jnp.zeros(acc_ref.shape, F32)

    tiles = [(i, qa1_before, qa2_before, True)]
    for jj in range(nk - 1):
        j = jj + (jj >= i).astype(I32)
        keys_first = j < i
        tiles.append((j, jnp.where(keys_first, qa1_before, qa1_after),
                      jnp.where(keys_first, qa2_before, qa2_after), False))
    units = []
    for j, qa1, qa2, diagonal in tiles:
        rows = pl.ds(pl.multiple_of(j * t, t), t)
        for mp, (qa, ka_ref) in enumerate(((qa1, ka1_ref), (qa2, ka2_ref))):
            for u in range(t // qs):
                units.append((j, rows, mp, qa, ka_ref, u, diagonal))

    def scores(unit):
        j, rows, mp, qa, ka_ref, u, diagonal = unit
        s = _nt_dot(ka_ref[rows, :], qa[u * qs:(u + 1) * qs, :])
        if diagonal:
            s = s + corr_ref[0, :, u * qs:(u + 1) * qs]
        return s

    def absorb(unit, s):
        j, rows, mp, qa, ka_ref, u, diagonal = unit
        qrows = slice(u * qs, (u + 1) * qs)
        cols = slice(mp * t + u * qs, mp * t + (u + 1) * qs)
        m_old = m_ref[:, cols]
        m_new = jnp.maximum(m_old, jnp.max(s, axis=0, keepdims=True))
        p = jnp.exp2(s - m_new)
        r = jnp.exp2(m_old - m_new)
        l_ref[:, cols] = r * l_ref[:, cols] + jnp.sum(p, axis=0, keepdims=True)
        acc_ref[mp, :, qrows] = r * acc_ref[mp, :, qrows] + _dot(vt_ref[j], p.astype(BF16))
        m_ref[:, cols] = m_new

    ahead = 2
    pending = [scores(unit) for unit in units[:ahead]]
    for n, unit in enumerate(units):
        if n + ahead < len(units):
            pending.append(scores(units[n + ahead]))
        absorb(unit, pending.pop(0))

    lam = (jnp.exp(jnp.sum(lq1_ref[...] * lk1_ref[...], axis=-1, keepdims=True))
           - jnp.exp(jnp.sum(lq2_ref[...] * lk2_ref[...], axis=-1, keepdims=True))
           + lam_init)
    o = acc_ref[0] / l_ref[:, 0:t] - lam * (acc_ref[1] / l_ref[:, t:2 * t])
    o = o * lax.rsqrt(jnp.mean(o * o, axis=0, keepdims=True) + NORM_EPS)
    o_ref[...] = (o.T * nw_ref[...] * (1.0 - lam_init)).astype(o_ref.dtype)


def _diff_attn(ua, nw, lq1, lk1, lq2, lk2, batch, seq, lam_init):
    t = 512
    nq = seq // t
    qf, kf, corr = _alibi_tables(seq, t)
    kern = functools.partial(_diff_attn_kernel, t=t, qs=LANES, seq=seq, lam_init=lam_init)
    small = lambda b, h, i: (0, 0)
    return pl.pallas_call(
        kern,
        grid_spec=pltpu.PrefetchScalarGridSpec(
            num_scalar_prefetch=0,
            grid=(batch, N_HEADS, nq),
            in_specs=[
                pl.BlockSpec((t, 2 * DH), lambda b, h, i: (b * nq + i, h)),
                pl.BlockSpec((seq, 2 * DH), lambda b, h, i: (b, N_HEADS + h)),
                pl.BlockSpec((seq, 2 * DH), lambda b, h, i: (b, 2 * N_HEADS + h)),
                pl.BlockSpec((1, 2, t, 2 * DH), lambda b, h, i: (h, 0, i, 0)),
                pl.BlockSpec((1, 2, seq, 2 * DH), lambda b, h, i: (h, 0, 0, 0)),
                pl.BlockSpec((1, t, t), lambda b, h, i: (h, 0, 0)),
                pl.BlockSpec((1, 2 * DH), small),
                pl.BlockSpec((1, DH), small),
                pl.BlockSpec((1, DH), small),
                pl.BlockSpec((1, DH), small),
                pl.BlockSpec((1, DH), small),
            ],
            out_specs=pl.BlockSpec((t, 2 * DH), lambda b, h, i: (b * nq + i, h)),
            scratch_shapes=[
                pltpu.VMEM((seq, 2 * DH), BF16),
                pltpu.VMEM((seq, 2 * DH), BF16),
                pltpu.VMEM((nq, 2 * DH, t), BF16),
                pltpu.VMEM((2, 2 * DH, t), F32),
                pltpu.VMEM((1, 2 * t), F32),
                pltpu.VMEM((1, 2 * t), F32),
            ],
        ),
        out_shape=jax.ShapeDtypeStruct((batch * seq, D_DIFF), BF16),
        compiler_params=pltpu.CompilerParams(
            dimension_semantics=("arbitrary", "arbitrary", "arbitrary"),
            vmem_limit_bytes=VMEM_LIMIT),
        name="diff_attn",
    )(ua, ua, ua, qf, kf, corr, nw, lq1, lk1, lq2, lk2)


def _mem_attn_kernel(q_ref, mem_ref, wkv_ref, o_ref, kv_ref, *, tq, mlen):
    @pl.when(pl.program_id(1) == 0)
    def _():
        kv_ref[...] = _dot(mem_ref[...].astype(BF16), wkv_ref[...]).astype(BF16)

    q = q_ref[...]
    mk = kv_ref[:, 0:D_XMEM]
    mv = kv_ref[:, D_XMEM:2 * D_XMEM]
    qhead = lax.broadcasted_iota(I32, (tq, D_XMEM), 1) >> DH_SHIFT
    vhead = lax.broadcasted_iota(I32, (mlen, D_XMEM), 1) >> DH_SHIFT
    acc = jnp.zeros((tq, D_XMEM), F32)
    for h in range(N_HEADS):
        qh = jnp.where(qhead == h, q, jnp.zeros_like(q))
        s = _nt_dot(qh, mk) * (1.0 / math.sqrt(DH))
        e = jnp.exp(s - jnp.max(s, axis=-1, keepdims=True))
        p = e / jnp.sum(e, axis=-1, keepdims=True)
        vh = jnp.where(vhead == h, mv, jnp.zeros_like(mv))
        acc = acc + _dot(p.astype(BF16), vh)
    o_ref[...] = acc.astype(o_ref.dtype)


def _mem_attn(um, mem2d, wkv_bf, batch, seq, mlen):
    tq = 1024
    nq = seq // tq
    kern = functools.partial(_mem_attn_kernel, tq=tq, mlen=mlen)
    return pl.pallas_call(
        kern,
        grid=(batch, nq),
        in_specs=[
            pl.BlockSpec((tq, D_XMEM), lambda b, i: (b * nq + i, 0)),
            pl.BlockSpec((mlen, D_MODEL), lambda b, i: (b, 0)),
            pl.BlockSpec((D_MODEL, 2 * D_XMEM), lambda b, i: (0, 0)),
        ],
        out_specs=pl.BlockSpec((tq, D_XMEM), lambda b, i: (b * nq + i, 0)),
        out_shape=jax.ShapeDtypeStruct((batch * seq, D_XMEM), BF16),
        scratch_shapes=[pltpu.VMEM((mlen, 2 * D_XMEM), BF16)],
        compiler_params=pltpu.CompilerParams(
            dimension_semantics=("arbitrary", "arbitrary"), vmem_limit_bytes=VMEM_LIMIT),
        name="mem_attn",
    )(um, mem2d, wkv_bf)


def _hgrn2_kernel(q_ref, kf_ref, lf_ref, kb_ref, lb_ref, v_ref, g_ref, nw_ref, o_ref,
                  kfp_ref, kbp_ref, vp_ref, bf_ref, bb_ref, acc_ref, cross_ref, *, seq, rb):
    n_chunks = seq // CHUNK
    w = 2 * DH
    row = lax.broadcasted_iota(I32, (rb, w), 0) & (CHUNK - 1)
    li = lax.broadcasted_iota(I32, (w, w), 0) >> DH_SHIFT
    lj = lax.broadcasted_iota(I32, (w, w), 1) >> DH_SHIFT
    same_head = li == lj
    head_ones = jnp.where(same_head, 1.0, 0.0).astype(BF16)

    def chunk_cumsum(x, reverse):
        for sh in (1, 2, 4, 8):
            if reverse:
                moved = pltpu.roll(x, rb - sh, 0)
                keep = row + sh < CHUNK
            else:
                moved = pltpu.roll(x, sh, 0)
                keep = row >= sh
            x = x + jnp.where(keep, moved, 0.0)
        return x

    halo = jnp.zeros((CHUNK, w), F32)
    for ref in (kfp_ref, kbp_ref, vp_ref, bf_ref, bb_ref):
        ref[0:CHUNK, :] = halo
        ref[seq + CHUNK:seq + 2 * CHUNK, :] = halo

    def prepare(blk, carry):
        r0 = pl.multiple_of(blk * rb, rb)
        rows = pl.ds(r0, rb)
        inner = pl.ds(r0 + CHUNK, rb)
        kfp_ref[inner, :] = kf_ref[rows, :]
        kbp_ref[inner, :] = kb_ref[rows, :]
        vp_ref[inner, :] = v_ref[rows, :]
        bf_ref[inner, :] = chunk_cumsum(lf_ref[rows, :] * LOG2E, False)
        bb_ref[inner, :] = chunk_cumsum(lb_ref[rows, :] * LOG2E, True)
        return carry

    lax.fori_loop(0, seq // rb, prepare, 0)

    def intra(blk):
        r0 = pl.multiple_of(blk * rb, rb)
        rows = pl.ds(r0, rb)
        q = q_ref[rows, :]
        acc = jnp.zeros((rb, w), F32)
        for reverse, kp_ref, b_ref in ((False, kfp_ref, bf_ref), (True, kbp_ref, bb_ref)):
            b = b_ref[pl.ds(r0 + CHUNK, rb), :]
            for d in range(CHUNK):
                src = pl.ds(r0 + CHUNK + (d if reverse else -d), rb)
                ks = kp_ref[src, :]
                vs = vp_ref[src, :]
                if d == 0:
                    z = q * ks
                else:
                    keep = (row + d < CHUNK) if reverse else (row >= d)
                    dec = jnp.exp2(jnp.where(keep, b - b_ref[src, :], -jnp.inf))
                    z = q * ks * dec
                acc = acc + _dot(z.astype(BF16), head_ones) * vs
        acc_ref[rows, :] = acc

    def inter(blk, carry):
        per_dir = ([], [])
        for c in range(rb // CHUNK):
            n = blk * (rb // CHUNK) + c
            for d, (r0, kp_ref, b_ref, edge) in enumerate((
                    (n * CHUNK, kfp_ref, bf_ref, CHUNK - 1),
                    ((n_chunks - 1 - n) * CHUNK, kbp_ref, bb_ref, 0))):
                r0 = pl.multiple_of(r0, CHUNK)
                rows = pl.ds(r0, CHUNK)
                prow = pl.ds(r0 + CHUNK, CHUNK)
                b = b_ref[prow, :]
                b_edge = b_ref[pl.ds(r0 + CHUNK + edge, 1), :]
                qd = (q_ref[rows, :] * jnp.exp2(b)).astype(BF16)
                kd = kp_ref[prow, :] * jnp.exp2(b_edge - b)
                upd = _tn_dot(vp_ref[prow, :].astype(BF16), kd.astype(BF16))
                per_dir[d].append((rows, qd, jnp.exp2(b_edge), jnp.where(same_head, upd, 0.0)))
        states = list(carry)
        for d in range(2):
            for rows, qd, decay, upd in per_dir[d]:
                cross_ref[rows, :] = cross_ref[rows, :] + _nt_dot(qd, states[d].astype(BF16))
                states[d] = states[d] * decay + upd
        return tuple(states)

    cross_ref[...] = jnp.zeros(cross_ref.shape, F32)

    def fused(blk, carry):
        intra(blk)
        return inter(blk, carry)

    z = jnp.zeros((w, w), F32)
    lax.fori_loop(0, seq // rb, fused, (z, z))

    def finish(blk, carry):
        rows = pl.ds(pl.multiple_of(blk * rb, rb), rb)
        o = acc_ref[rows, :] + cross_ref[rows, :]
        lane = lax.broadcasted_iota(I32, (rb, w), 1)
        lo = lane < DH
        sq = o * o
        ms_lo = jnp.sum(jnp.where(lo, sq, 0.0), axis=-1, keepdims=True)
        ms_hi = jnp.sum(jnp.where(lo, 0.0, sq), axis=-1, keepdims=True)
        ms = jnp.where(lo, ms_lo, ms_hi) * (1.0 / DH)
        o = o * lax.rsqrt(ms + NORM_EPS) * nw_ref[...] * g_ref[rows, :]
        o_ref[rows, :] = o.astype(o_ref.dtype)
        return carry

    lax.fori_loop(0, seq // rb, finish, 0)


def _hgrn2(uh, nw, batch, seq):
    w = 2 * DH
    npair = D_HGRN // w
    kern = functools.partial(_hgrn2_kernel, seq=seq, rb=256)

    def sec(s):
        return pl.BlockSpec((seq, w), lambda b, p, s=s: (b, s * npair + p))

    return pl.pallas_call(
        kern,
        grid=(batch, npair),
        in_specs=[sec(0), sec(1), sec(2), sec(3), sec(4), sec(5), sec(6),
                  pl.BlockSpec((1, w), lambda b, p: (0, p))],
        out_specs=pl.BlockSpec((seq, w), lambda b, p: (b, p)),
        out_shape=jax.ShapeDtypeStruct((batch * seq, D_HGRN), BF16),
        scratch_shapes=[pltpu.VMEM((seq + 2 * CHUNK, w), F32)] * 5
        + [pltpu.VMEM((seq, w), F32)] * 2,
        compiler_params=pltpu.CompilerParams(
            dimension_semantics=("arbitrary", "arbitrary"), vmem_limit_bytes=VMEM_LIMIT),
        name="hgrn2",
    )(uh, uh, uh, uh, uh, uh, uh, nw)


def _rows_to_block(rows, tm):
    r = lax.broadcasted_iota(I32, (8, tm), 0)
    out = jnp.zeros((8, tm), rows[0].dtype)
    for k, row in enumerate(rows):
        out = jnp.where(r == k, row, out)
    return out


def _pack_bf16_pairs(x):
    m = x.shape[1] // 2
    u = lax.bitcast_convert_type(x, U32)
    r = u + jnp.uint32(0x7FFF) + ((u >> 16) & jnp.uint32(1))
    return (r[:, 0:m] >> 16) | (r[:, m:2 * m] & jnp.uint32(0xFFFF0000))


def _unpack_bf16_pairs(w):
    lo = lax.bitcast_convert_type(w << 16, F32).astype(BF16)
    hi = lax.bitcast_convert_type(w & jnp.uint32(0xFFFF0000), F32).astype(BF16)
    return jnp.concatenate([lo, hi], axis=1)


def _post_mixer_kernel(x_ref, od_ref, oh_ref, om_ref, wo_ref, g_ref, b_ref, rwt_ref,
                       rbt_ref, h_ref, hp_ref, eidx_ref, gate_ref, rank_ref, cnt_ref,
                       carry_ref, *, tm, alpha):
    i = pl.program_id(0)

    @pl.when(i == 0)
    def _():
        carry_ref[...] = jnp.zeros_like(carry_ref)

    mix = (_dot(od_ref[...], wo_ref[0:D_DIFF, :])
           + _dot(oh_ref[...], wo_ref[D_DIFF:D_DIFF + D_HGRN, :])
           + _dot(om_ref[...], wo_ref[D_DIFF + D_HGRN:, :]))
    h = _layer_norm(alpha * x_ref[...] + mix, g_ref[...], b_ref[...])
    h_ref[...] = h

    h_hi = h.astype(BF16)
    hp_ref[...] = _pack_bf16_pairs(h)
    h_lo = (h - h_hi.astype(F32)).astype(BF16)
    rwt = rwt_ref[...]
    rwt_hi = rwt.astype(BF16)
    rwt_lo = (rwt - rwt_hi.astype(F32)).astype(BF16)
    logits = (_nt_dot(rwt_hi, h_hi) + _nt_dot(rwt_hi, h_lo) + _nt_dot(rwt_lo, h_hi)
              + rbt_ref[...])

    erow = lax.broadcasted_iota(I32, (N_EXPERTS, tm), 0).astype(F32)
    work = logits
    sels, vals, idxs = [], [], []
    for _ in range(TOP_K):
        m = jnp.max(work, axis=0, keepdims=True)
        idx = jnp.min(jnp.where(work == m, erow, float(N_EXPERTS)), axis=0, keepdims=True)
        sel = erow == idx
        work = jnp.where(sel, -jnp.inf, work)
        sels.append(sel)
        vals.append(m)
        idxs.append(idx)
    es = [jnp.exp(v - vals[0]) for v in vals]
    den = es[0] + es[1] + es[2] + es[3]

    chosen = jnp.where(sels[0] | sels[1] | sels[2] | sels[3], 1.0, 0.0)
    ti = lax.broadcasted_iota(I32, (tm, tm), 0)
    tj = lax.broadcasted_iota(I32, (tm, tm), 1)
    earlier = jnp.where(ti < tj, 1.0, 0.0).astype(BF16)
    prefix = _dot(chosen.astype(BF16), earlier) + carry_ref[:, 0:1]

    ranks = [jnp.sum(jnp.where(sels[k], prefix, 0.0), axis=0, keepdims=True)
             for k in range(TOP_K)]
    eidx_ref[...] = _rows_to_block(idxs, tm).astype(I32)
    gate_ref[...] = _rows_to_block([e / den for e in es], tm)
    rank_ref[...] = _rows_to_block(ranks, tm).astype(I32)

    total = carry_ref[:, 0:1] + jnp.sum(chosen, axis=1, keepdims=True)
    carry_ref[...] = jnp.broadcast_to(total, carry_ref.shape)
    cnt_ref[...] = carry_ref[...]


def _post_mixer(x2d, od, oh, om, wo_bf, g, b, rwt, rbt, alpha):
    t = x2d.shape[0]
    tm = 512
    kern = functools.partial(_post_mixer_kernel, tm=tm, alpha=alpha)
    full = lambda i: (0, 0)
    tok = lambda i: (i, 0)
    per_tok = lambda i: (0, i)
    return pl.pallas_call(
        kern,
        grid=(t // tm,),
        in_specs=[
            pl.BlockSpec((tm, D_MODEL), tok),
            pl.BlockSpec((tm, D_DIFF), tok),
            pl.BlockSpec((tm, D_HGRN), tok),
            pl.BlockSpec((tm, D_XMEM), tok),
            pl.BlockSpec((D_MODEL, D_MODEL), full),
            pl.BlockSpec((1, D_MODEL), full),
            pl.BlockSpec((1, D_MODEL), full),
            pl.BlockSpec((N_EXPERTS, D_MODEL), full),
            pl.BlockSpec((N_EXPERTS, 1), full),
        ],
        out_specs=[
            pl.BlockSpec((tm, D_MODEL), tok),
            pl.BlockSpec((tm, D_MODEL // 2), tok),
            pl.BlockSpec((8, tm), per_tok),
            pl.BlockSpec((8, tm), per_tok),
            pl.BlockSpec((8, tm), per_tok),
            pl.BlockSpec((N_EXPERTS, LANES), full),
        ],
        out_shape=[
            jax.ShapeDtypeStruct((t, D_MODEL), F32),
            jax.ShapeDtypeStruct((t, D_MODEL // 2), U32),
            jax.ShapeDtypeStruct((8, t), I32),
            jax.ShapeDtypeStruct((8, t), F32),
            jax.ShapeDtypeStruct((8, t), I32),
            jax.ShapeDtypeStruct((N_EXPERTS, LANES), F32),
        ],
        scratch_shapes=[pltpu.VMEM((N_EXPERTS, LANES), F32)],
        compiler_params=pltpu.CompilerParams(
            dimension_semantics=("arbitrary",), vmem_limit_bytes=VMEM_LIMIT),
        name="post_mixer",
    )(x2d, od, oh, om, wo_bf, g, b, rwt, rbt)


def _route_meta_kernel(eidx_ref, rank_ref, cnt_ref, dest_ref, be_ref, *, tm, nb_lanes):
    cnt = cnt_ref[...]
    blocks = jnp.floor((cnt + (MOE_BLOCK - 1)) * (1.0 / MOE_BLOCK))
    ei = lax.broadcasted_iota(I32, (N_EXPERTS, N_EXPERTS), 0)
    ej = lax.broadcasted_iota(I32, (N_EXPERTS, N_EXPERTS), 1)
    below = jnp.where(ej < ei, 1.0, 0.0).astype(BF16)
    pstart = _dot(below, blocks.astype(BF16))[:, 0:1]
    pend = pstart + blocks[:, 0:1]

    erow = lax.broadcasted_iota(I32, (N_EXPERTS, tm), 0).astype(F32)
    eidx = eidx_ref[...].astype(F32)
    starts = [jnp.sum(jnp.where(erow == eidx[k:k + 1, :], pstart, 0.0), axis=0, keepdims=True)
              for k in range(TOP_K)]
    dest_ref[...] = ((_rows_to_block(starts, tm) * float(MOE_BLOCK)).astype(I32)
                     + rank_ref[...])

    bi = lax.broadcasted_iota(I32, (N_EXPERTS, nb_lanes), 1).astype(F32)
    done = jnp.where(pend <= bi, 1.0, 0.0)
    be = jnp.minimum(jnp.sum(done, axis=0, keepdims=True), float(N_EXPERTS - 1))
    used = jnp.broadcast_to(pend[N_EXPERTS - 1:N_EXPERTS, :], (1, nb_lanes))
    be_ref[...] = _rows_to_block([be, used], nb_lanes).astype(I32)


def _route_meta(eidx, rank, cnt, n_blocks):
    t = eidx.shape[1]
    tm = 2048
    nb_lanes = -(-n_blocks // LANES) * LANES
    kern = functools.partial(_route_meta_kernel, tm=tm, nb_lanes=nb_lanes)
    return pl.pallas_call(
        kern,
        grid=(t // tm,),
        in_specs=[
            pl.BlockSpec((8, tm), lambda i: (0, i)),
            pl.BlockSpec((8, tm), lambda i: (0, i)),
            pl.BlockSpec((N_EXPERTS, LANES), lambda i: (0, 0)),
        ],
        out_specs=[
            pl.BlockSpec((8, tm), lambda i: (0, i)),
            pl.BlockSpec((8, nb_lanes), lambda i: (0, 0)),
        ],
        out_shape=[
            jax.ShapeDtypeStruct((8, t), I32),
            jax.ShapeDtypeStruct((8, nb_lanes), I32),
        ],
        compiler_params=pltpu.CompilerParams(
            dimension_semantics=("arbitrary",), vmem_limit_bytes=VMEM_LIMIT),
        name="route_meta",
    )(eidx, rank, cnt)


SC_WINDOW = LANES
N_QUARTERS = 4
QUARTER = D_MODEL // N_QUARTERS
N_PACKED = D_MODEL // 2 // QUARTER


def _sc_mesh():
    return plsc.VectorSubcoreMesh(core_axis_name="c", subcore_axis_name="s")


def _dispatch(dest_km, hp, n_rows):
    t = hp.shape[0]

    @functools.partial(
        pl.kernel, out_type=[jax.ShapeDtypeStruct((n_rows, QUARTER), hp.dtype)] * N_PACKED,
        mesh=_sc_mesh(), scratch_types=[])
    def scatter_rows(h_hbm, d_hbm, *xs_hbm):
        for q in range(N_PACKED):
            def body(x_vmem, i_vmem, q=q):
                for k in range(TOP_K):
                    pltpu.sync_copy(x_vmem, xs_hbm[q].at[i_vmem.at[k]])

            pltpu.emit_pipeline(
                body,
                grid=(t // SC_WINDOW,),
                in_specs=[pl.BlockSpec((SC_WINDOW, QUARTER), lambda i, q=q: (i, q)),
                          pl.BlockSpec((TOP_K, SC_WINDOW), lambda i: (0, i))],
                out_specs=[],
                core_axis_name=("c", "s"),
                dimension_semantics=(pltpu.PARALLEL,),
            )(h_hbm, d_hbm)

    return scatter_rows(hp, dest_km)


def _gather_back(dest_km, ys):
    t = dest_km.shape[1]
    n = len(ys)

    @functools.partial(
        pl.kernel,
        out_type=[jax.ShapeDtypeStruct((t, TOP_K * QUARTER), ys[0].dtype)] * n,
        mesh=_sc_mesh(), scratch_types=[])
    def gather_rows(d_hbm, *refs):
        ys_hbm, out_hbm = refs[:n], refs[n:]
        for q in range(n):
            for k in range(TOP_K):
                def body(i_vmem, o_vmem, q=q, k=k):
                    pltpu.sync_copy(ys_hbm[q].at[i_vmem.at[k]], o_vmem)

                pltpu.emit_pipeline(
                    body,
                    grid=(t // SC_WINDOW,),
                    in_specs=[pl.BlockSpec((TOP_K, SC_WINDOW), lambda i: (0, i))],
                    out_specs=[pl.BlockSpec((SC_WINDOW, QUARTER), lambda i, k=k: (i, k))],
                    core_axis_name=("c", "s"),
                    dimension_semantics=(pltpu.PARALLEL,),
                )(d_hbm, out_hbm[q])

    return gather_rows(dest_km, *ys)


def _experts_kernel(be_ref, used_ref, *refs):
    xs_refs = refs[:N_PACKED]
    wgu_hbm, wdn_hbm, bg_ref, bl_ref, bd_ref, perm_ref = refs[N_PACKED:N_PACKED + 6]
    ys_refs = refs[N_PACKED + 6:2 * N_PACKED + 6]
    wgu_st, wdn_st, wg_ref, wl_ref, wd_ref, sem = refs[2 * N_PACKED + 6:]
    i = pl.program_id(0)
    used = used_ref[0]
    e = be_ref[i]
    prev = be_ref[jnp.maximum(i - 1, 0)]
    fresh = jnp.logical_and(i < used, jnp.logical_or(i == 0, e != prev))

    def weight_copies(ex):
        return (pltpu.make_async_copy(wgu_hbm.at[ex], wgu_st, sem.at[0]),
                pltpu.make_async_copy(wdn_hbm.at[ex], wdn_st, sem.at[1]))

    @pl.when(jnp.logical_and(i == 0, used > 0))
    def _():
        for cp in weight_copies(e):
            cp.start()

    @pl.when(fresh)
    def _():
        for cp in weight_copies(e):
            cp.wait()
        perm = perm_ref[...]
        half = LANES
        for c in range(2 * D_EXPERT // (2 * half)):
            slab = wgu_st[:, c * 2 * half:(c + 1) * 2 * half].astype(BF16)
            sep = _dot(slab, perm)
            wg_ref[:, c * half:(c + 1) * half] = sep[:, 0:half].astype(BF16)
            wl_ref[:, c * half:(c + 1) * half] = sep[:, half:2 * half].astype(BF16)
        wd_ref[...] = wdn_st[...].astype(BF16)

        def same_expert(j):
            return jnp.logical_and(j < used, be_ref[jnp.minimum(j, used - 1)] == e)

        nxt = lax.while_loop(same_expert, lambda j: j + 1, i + 1)

        @pl.when(nxt < used)
        def _():
            for cp in weight_copies(be_ref[nxt]):
                cp.start()

    @pl.when(i < used_ref[0])
    def _():
        xb = _unpack_bf16_pairs(jnp.concatenate([r[...] for r in xs_refs], axis=1))
        glu = jnp.minimum(_dot(xb, wg_ref[...]) + bg_ref[0], SWIGLU_LIMIT)
        lin = jnp.clip(_dot(xb, wl_ref[...]) + bl_ref[0], -SWIGLU_LIMIT, SWIGLU_LIMIT)
        act = glu * _sigmoid(SWIGLU_ALPHA * glu) * (lin + 1.0)
        y = _pack_bf16_pairs(_dot(act.astype(BF16), wd_ref[...]) + bd_ref[0])
        for q, ys_ref in enumerate(ys_refs):
            ys_ref[...] = y[:, q * QUARTER:(q + 1) * QUARTER]

    @pl.when(i >= used_ref[0])
    def _():
        for ys_ref in ys_refs:
            ys_ref[...] = jnp.zeros_like(ys_ref)


def _experts(block_e, used, xs, w_gu, w_dn, b_g, b_l, b_d, perm):
    n_rows = xs[0].shape[0]
    n_blocks = n_rows // MOE_BLOCK
    wmap = lambda i, be, u: (be[i], 0, 0)
    rows = pl.BlockSpec((MOE_BLOCK, QUARTER), lambda i, be, u: (i, 0))
    return pl.pallas_call(
        _experts_kernel,
        grid_spec=pltpu.PrefetchScalarGridSpec(
            num_scalar_prefetch=2,
            grid=(n_blocks,),
            in_specs=[rows] * N_PACKED + [
                pl.BlockSpec(memory_space=pl.ANY),
                pl.BlockSpec(memory_space=pl.ANY),
                pl.BlockSpec((1, 1, D_EXPERT), wmap),
                pl.BlockSpec((1, 1, D_EXPERT), wmap),
                pl.BlockSpec((1, 1, D_MODEL), wmap),
                pl.BlockSpec((2 * LANES, 2 * LANES), lambda i, be, u: (0, 0)),
            ],
            out_specs=[rows] * N_PACKED,
            scratch_shapes=[
                pltpu.VMEM((D_MODEL, 2 * D_EXPERT), F32),
                pltpu.VMEM((D_EXPERT, D_MODEL), F32),
                pltpu.VMEM((D_MODEL, D_EXPERT), BF16),
                pltpu.VMEM((D_MODEL, D_EXPERT), BF16),
                pltpu.VMEM((D_EXPERT, D_MODEL), BF16),
                pltpu.SemaphoreType.DMA((2,)),
            ],
        ),
        out_shape=[jax.ShapeDtypeStruct((n_rows, QUARTER), U32)] * N_PACKED,
        compiler_params=pltpu.CompilerParams(
            dimension_semantics=("arbitrary",), vmem_limit_bytes=VMEM_LIMIT),
        name="experts",
    )(block_e, used, *xs, w_gu, w_dn, b_g, b_l, b_d, perm)


def _combine_kernel(h_ref, gate_ref, g_ref, b_ref, *refs, tm, alpha):
    yg_refs, o_ref = refs[:N_PACKED], refs[N_PACKED]
    gate = jnp.transpose(gate_ref[...])
    gks = [gate[:, k:k + 1] for k in range(TOP_K)]
    lows, highs = [], []
    for yg_ref in yg_refs:
        lo = jnp.zeros((tm, QUARTER), F32)
        hi = jnp.zeros((tm, QUARTER), F32)
        for k in range(TOP_K):
            w = yg_ref[:, k * QUARTER:(k + 1) * QUARTER]
            lo = lo + gks[k] * lax.bitcast_convert_type(w << 16, F32)
            hi = hi + gks[k] * lax.bitcast_convert_type(w & jnp.uint32(0xFFFF0000), F32)
        lows.append(lo)
        highs.append(hi)
    ffn = jnp.concatenate(lows + highs, axis=1)
    o_ref[...] = _layer_norm(alpha * h_ref[...] + ffn, g_ref[...], b_ref[...])


def _combine(h, gate, g, b, yg, alpha):
    t = h.shape[0]
    tm = 256
    kern = functools.partial(_combine_kernel, tm=tm, alpha=alpha)
    tok = lambda i: (i, 0)
    return pl.pallas_call(
        kern,
        grid=(t // tm,),
        in_specs=[
            pl.BlockSpec((tm, D_MODEL), tok),
            pl.BlockSpec((8, tm), lambda i: (0, i)),
            pl.BlockSpec((1, D_MODEL), lambda i: (0, 0)),
            pl.BlockSpec((1, D_MODEL), lambda i: (0, 0)),
        ] + [pl.BlockSpec((tm, TOP_K * QUARTER), tok)] * N_PACKED,
        out_specs=pl.BlockSpec((tm, D_MODEL), tok),
        out_shape=jax.ShapeDtypeStruct((t, D_MODEL), F32),
        compiler_params=pltpu.CompilerParams(
            dimension_semantics=("arbitrary",), vmem_limit_bytes=VMEM_LIMIT),
        name="combine",
    )(h, gate, g, b, *yg)


def _column_split_permutation():
    n = 2 * LANES
    p = np.zeros((n, n), np.float32)
    j = np.arange(n)
    p[j, j // 2 + LANES * (j % 2)] = 1.0
    return jnp.asarray(p, dtype=BF16)


def kernel(x, mem, w_in, lam_q1, lam_k1, lam_q2, lam_k2, diff_norm_w, hgrn_lb_fwd,
           hgrn_lb_bwd, hgrn_norm_w, w_mem_kv, w_o, ln1_g, ln1_b, router_w, router_b,
           w_gate_up, b_gate_up, w_down, b_down, ln2_g, ln2_b):
    batch, seq, d = x.shape
    mlen = mem.shape[1]
    assert d == D_MODEL and w_in.shape == (N_LAYERS, D_MODEL, D_IN_PROJ)
    assert hgrn_lb_fwd.shape == (N_LAYERS + 1, D_HGRN)
    t = batch * seq
    alpha = (2.0 * N_LAYERS) ** 0.25
    lam_init = 0.8 - 0.6 * math.exp(-0.3 * 0)

    x2d = x.reshape(t, D_MODEL)
    ua, uh, um = _in_proj(x2d, w_in[0].astype(BF16), hgrn_lb_fwd, hgrn_lb_bwd)
    o_diff = _diff_attn(ua, diff_norm_w, lam_q1, lam_k1, lam_q2, lam_k2,
                        batch, seq, lam_init)
    o_mem = _mem_attn(um, mem.reshape(batch * mlen, D_MODEL), w_mem_kv[0].astype(BF16),
                      batch, seq, mlen)
    o_hgrn = _hgrn2(uh, hgrn_norm_w, batch, seq)

    h1, h1b, eidx, gate, rank, cnt = _post_mixer(
        x2d, o_diff, o_hgrn, o_mem, w_o[0].astype(BF16), ln1_g, ln1_b, router_w[0].T,
        router_b.reshape(N_EXPERTS, 1), alpha)

    n_blocks = -(-(t * TOP_K) // MOE_BLOCK) + N_EXPERTS
    dest, meta = _route_meta(eidx, rank, cnt, n_blocks)
    dest_km = dest[0:TOP_K]
    block_e = meta[0, 0:n_blocks]
    used = meta[1, 0:1]

    xs = _dispatch(dest_km, h1b, n_blocks * MOE_BLOCK)
    b_g = b_gate_up[0][:, 0::2].reshape(N_EXPERTS, 1, D_EXPERT)
    b_l = b_gate_up[0][:, 1::2].reshape(N_EXPERTS, 1, D_EXPERT)
    ys = _experts(block_e, used, xs, w_gate_up[0], w_down[0], b_g, b_l,
                  b_down[0].reshape(N_EXPERTS, 1, D_MODEL), _column_split_permutation())
    out = _combine(h1, gate, ln2_g, ln2_b, _gather_back(dest_km, ys), alpha)
    return out.reshape(batch, seq, D_MODEL)
```

```python
import functools
import math

import jax
import jax.numpy as jnp
import numpy as np
from jax import lax
from jax.experimental import pallas as pl
from jax.experimental.pallas import tpu as pltpu
from jax.experimental.pallas import tpu_sc as plsc

F32 = jnp.float32
BF16 = jnp.bfloat16
I32 = jnp.int32
U32 = jnp.uint32

D_MODEL = 1024
N_LAYERS = 1
D_DIFF = 512
D_HGRN = 256
D_XMEM = 256
N_HEADS = 4
DH = 64
DH_SHIFT = 6
D_IN_PROJ = 3 * D_DIFF + 5 * D_HGRN + D_XMEM
CHUNK = 16
N_EXPERTS = 32
TOP_K = 4
D_EXPERT = 1024
MOE_BLOCK = 512
SWIGLU_ALPHA = 1.702
SWIGLU_LIMIT = 7.0
NORM_EPS = 1e-5
LOG2E = math.log2(math.e)
LANES = 128
VMEM_LIMIT = 56 * 1024 * 1024


def _nt_dot(a, b):
    return lax.dot_general(a, b, (((1,), (1,)), ((), ())), preferred_element_type=F32)


def _tn_dot(a, b):
    return lax.dot_general(a, b, (((0,), (0,)), ((), ())), preferred_element_type=F32)


def _dot(a, b):
    return jnp.dot(a, b, preferred_element_type=F32)


def _sigmoid(x):
    return 1.0 / (1.0 + jnp.exp(-x))


def _layer_norm(y, g, b):
    mu = jnp.mean(y, axis=-1, keepdims=True)
    yc = y - mu
    var = jnp.mean(yc * yc, axis=-1, keepdims=True)
    return yc * lax.rsqrt(var + NORM_EPS) * g + b


def _in_proj_kernel(x_ref, w_ref, lbf_ref, lbb_ref, ua_ref, uh_ref, um_ref):
    xb = x_ref[...].astype(BF16)

    def proj(c0, width):
        return _dot(xb, w_ref[:, c0:c0 + width])

    ua_ref[:, 0:D_DIFF] = (proj(0, D_DIFF) * (LOG2E / math.sqrt(DH))).astype(BF16)
    ua_ref[:, D_DIFF:2 * D_DIFF] = proj(D_DIFF, D_DIFF).astype(BF16)
    ua_ref[:, 2 * D_DIFF:3 * D_DIFF] = proj(2 * D_DIFF, D_DIFF).astype(BF16)

    def lower_bound(lb_ref):
        a = lb_ref[0:1, :]
        b = lb_ref[1:2, :]
        m = jnp.maximum(a, b)
        ea = jnp.exp(a - m)
        eb = jnp.exp(b - m)
        return ea / (ea + eb)

    base = 3 * D_DIFF
    hq = proj(base, D_HGRN)
    uh_ref[:, 0:D_HGRN] = hq * _sigmoid(hq)
    for d, lb_ref in enumerate((lbf_ref, lbb_ref)):
        lb = lower_bound(lb_ref)
        f = lb + (1.0 - lb) * _sigmoid(proj(base + (1 + d) * D_HGRN, D_HGRN))
        uh_ref[:, (1 + 2 * d) * D_HGRN:(2 + 2 * d) * D_HGRN] = 1.0 - f
        uh_ref[:, (2 + 2 * d) * D_HGRN:(3 + 2 * d) * D_HGRN] = jnp.log(f)
    uh_ref[:, 5 * D_HGRN:6 * D_HGRN] = proj(base + 3 * D_HGRN, D_HGRN)
    uh_ref[:, 6 * D_HGRN:7 * D_HGRN] = _sigmoid(proj(base + 4 * D_HGRN, D_HGRN))
    um_ref[...] = proj(base + 5 * D_HGRN, D_XMEM).astype(BF16)


def _in_proj(x2d, w_in_bf, lb_f, lb_b):
    t = x2d.shape[0]
    tm = 1024
    return pl.pallas_call(
        _in_proj_kernel,
        grid=(t // tm,),
        in_specs=[
            pl.BlockSpec((tm, D_MODEL), lambda i: (i, 0)),
            pl.BlockSpec((D_MODEL, D_IN_PROJ), lambda i: (0, 0)),
            pl.BlockSpec((N_LAYERS + 1, D_HGRN), lambda i: (0, 0)),
            pl.BlockSpec((N_LAYERS + 1, D_HGRN), lambda i: (0, 0)),
        ],
        out_specs=[
            pl.BlockSpec((tm, 3 * D_DIFF), lambda i: (i, 0)),
            pl.BlockSpec((tm, 7 * D_HGRN), lambda i: (i, 0)),
            pl.BlockSpec((tm, D_XMEM), lambda i: (i, 0)),
        ],
        out_shape=[
            jax.ShapeDtypeStruct((t, 3 * D_DIFF), BF16),
            jax.ShapeDtypeStruct((t, 7 * D_HGRN), F32),
            jax.ShapeDtypeStruct((t, D_XMEM), BF16),
        ],
        compiler_params=pltpu.CompilerParams(
            dimension_semantics=("arbitrary",), vmem_limit_bytes=VMEM_LIMIT),
        name="in_proj",
    )(x2d, w_in_bf, lb_f, lb_b)


N_POS_FEATURES = 12


def _bf16_pieces(x):
    x = np.asarray(x, np.float32)
    out = []
    for _ in range(3):
        p = x.astype(BF16).astype(np.float32)
        out.append(p)
        x = x - p
    return np.stack(out, axis=-1)


def _alibi_tables(seq, t):
    slopes = (2.0 ** (-8.0 * np.arange(1, N_HEADS + 1) / N_HEADS)).astype(np.float32)
    c3 = _bf16_pieces(slopes * np.float32(LOG2E))
    pos = np.arange(seq)
    hi = (pos >> DH_SHIFT).astype(np.float32)
    lo = (pos & (DH - 1)).astype(np.float32)
    qf = np.zeros((N_HEADS, 2, seq, 2 * DH), np.float32)
    kf = np.zeros((N_HEADS, 2, seq, 2 * DH), np.float32)
    for m, base in enumerate((DH, 0)):
        for j in range(3):
            qf[:, m, :, base + j] = hi
            qf[:, m, :, base + 3 + j] = lo
            qf[:, m, :, base + 6 + j] = 64.0 * c3[:, j, None]
            qf[:, m, :, base + 9 + j] = c3[:, j, None]
            kf[:, m, :, base + j] = -64.0 * c3[:, j, None]
            kf[:, m, :, base + 3 + j] = -c3[:, j, None]
            kf[:, m, :, base + 6 + j] = hi
            kf[:, m, :, base + 9 + j] = lo
    c = c3.sum(axis=-1)
    kk = np.arange(t)[:, None]
    qq = np.arange(t)[None, :]
    corr = 2.0 * c[:, None, None] * np.minimum(qq - kk, 0).astype(np.float32)
    return (jnp.asarray(qf, dtype=BF16), jnp.asarray(kf, dtype=BF16),
            jnp.asarray(corr, dtype=F32))


def _diff_attn_kernel(q_ref, k_ref, v_ref, qf_ref, kf_ref, corr_ref, nw_ref, lq1_ref,
                      lk1_ref, lq2_ref, lk2_ref, o_ref, ka1_ref, ka2_ref, vt_ref, acc_ref,
                      m_ref, l_ref, *, t, qs, seq, lam_init):
    i = pl.program_id(2)
    nk = seq // t
    lane = lax.broadcasted_iota(I32, (t, 2 * DH), 1)
    first_half = lane < DH

    @pl.when(i == 0)
    def _():
        def build(r, carry):
            rows = pl.ds(pl.multiple_of(r * t, t), t)
            kblk = k_ref[rows, :]
            ka1_ref[rows, :] = jnp.where(first_half, kblk, kf_ref[0, 0, rows, :])
            ka2_ref[rows, :] = jnp.where(first_half, kf_ref[0, 1, rows, :], kblk)
            vt_ref[r] = v_ref[rows, :].astype(F32).T.astype(BF16)
            return carry

        lax.fori_loop(0, nk, build, 0)

    q = q_ref[...]
    qf1 = qf_ref[0, 0]
    qf2 = qf_ref[0, 1]
    qa1_before = jnp.where(first_half, q, qf1)
    qa1_after = jnp.where(first_half, q, -qf1)
    qa2_before = jnp.where(first_half, qf2, q)
    qa2_after = jnp.where(first_half, -qf2, q)

    m_ref[...] = jnp.full(m_ref.shape, -jnp.inf, F32)
    l_ref[...] = jnp.zeros(l_ref.shape, F32)
    acc_ref[...] = jnp.zeros(acc_ref.shape, F32)

    def chunk(j, qa1, qa2, diagonal):
        rows = pl.ds(pl.multiple_of(j * t, t), t)
        vt = vt_ref[j]
        for mp, (qa, ka_ref) in enumerate(((qa1, ka1_ref), (qa2, ka2_ref))):
            ka = ka_ref[rows, :]
            for u in range(t // qs):
                qrows = slice(u * qs, (u + 1) * qs)
                cols = slice(mp * t + u * qs, mp * t + (u + 1) * qs)
                s = _nt_dot(ka, qa[qrows, :])
                if diagonal:
                    s = s + corr_ref[0, :, qrows]
                m_old = m_ref[:, cols]
                m_new = jnp.maximum(m_old, jnp.max(s, axis=0, keepdims=True))
                p = jnp.exp2(s - m_new)
                r = jnp.exp2(m_old - m_new)
                l_ref[:, cols] = r * l_ref[:, cols] + jnp.sum(p, axis=0, keepdims=True)
                acc_ref[mp, :, qrows] = (r * acc_ref[mp, :, qrows]
                                         + _dot(vt, p.astype(BF16)))
                m_ref[:, cols] = m_new

    chunk(i, qa1_before, qa2_before, True)
    for jj in range(nk - 1):
        j = jj + (jj >= i).astype(I32)
        keys_first = j < i
        chunk(j, jnp.where(keys_first, qa1_before, qa1_after),
              jnp.where(keys_first, qa2_before, qa2_after), False)

    lam = (jnp.exp(jnp.sum(lq1_ref[...] * lk1_ref[...], axis=-1, keepdims=True))
           - jnp.exp(jnp.sum(lq2_ref[...] * lk2_ref[...], axis=-1, keepdims=True))
           + lam_init)
    o = acc_ref[0] / l_ref[:, 0:t] - lam * (acc_ref[1] / l_ref[:, t:2 * t])
    o = o * lax.rsqrt(jnp.mean(o * o, axis=0, keepdims=True) + NORM_EPS)
    o_ref[...] = (o.T * nw_ref[...] * (1.0 - lam_init)).astype(o_ref.dtype)


def _diff_attn(ua, nw, lq1, lk1, lq2, lk2, batch, seq, lam_init):
    t = 512
    nq = seq // t
    qf, kf, corr = _alibi_tables(seq, t)
    kern = functools.partial(_diff_attn_kernel, t=t, qs=LANES, seq=seq, lam_init=lam_init)
    small = lambda b, h, i: (0, 0)
    return pl.pallas_call(
        kern,
        grid_spec=pltpu.PrefetchScalarGridSpec(
            num_scalar_prefetch=0,
            grid=(batch, N_HEADS, nq),
            in_specs=[
                pl.BlockSpec((t, 2 * DH), lambda b, h, i: (b * nq + i, h)),
                pl.BlockSpec((seq, 2 * DH), lambda b, h, i: (b, N_HEADS + h)),
                pl.BlockSpec((seq, 2 * DH), lambda b, h, i: (b, 2 * N_HEADS + h)),
                pl.BlockSpec((1, 2, t, 2 * DH), lambda b, h, i: (h, 0, i, 0)),
                pl.BlockSpec((1, 2, seq, 2 * DH), lambda b, h, i: (h, 0, 0, 0)),
                pl.BlockSpec((1, t, t), lambda b, h, i: (h, 0, 0)),
                pl.BlockSpec((1, 2 * DH), small),
                pl.BlockSpec((1, DH), small),
                pl.BlockSpec((1, DH), small),
                pl.BlockSpec((1, DH), small),
                pl.BlockSpec((1, DH), small),
            ],
            out_specs=pl.BlockSpec((t, 2 * DH), lambda b, h, i: (b * nq + i, h)),
            scratch_shapes=[
                pltpu.VMEM((seq, 2 * DH), BF16),
                pltpu.VMEM((seq, 2 * DH), BF16),
                pltpu.VMEM((nq, 2 * DH, t), BF16),
                pltpu.VMEM((2, 2 * DH, t), F32),
                pltpu.VMEM((1, 2 * t), F32),
                pltpu.VMEM((1, 2 * t), F32),
            ],
        ),
        out_shape=jax.ShapeDtypeStruct((batch * seq, D_DIFF), BF16),
        compiler_params=pltpu.CompilerParams(
            dimension_semantics=("arbitrary", "arbitrary", "arbitrary"),
            vmem_limit_bytes=VMEM_LIMIT),
        name="diff_attn",
    )(ua, ua, ua, qf, kf, corr, nw, lq1, lk1, lq2, lk2)


def _mem_attn_kernel(q_ref, mem_ref, wkv_ref, o_ref, kv_ref, *, tq, mlen):
    @pl.when(pl.program_id(1) == 0)
    def _():
        kv_ref[...] = _dot(mem_ref[...].astype(BF16), wkv_ref[...]).astype(BF16)

    q = q_ref[...]
    mk = kv_ref[:, 0:D_XMEM]
    mv = kv_ref[:, D_XMEM:2 * D_XMEM]
    qhead = lax.broadcasted_iota(I32, (tq, D_XMEM), 1) >> DH_SHIFT
    vhead = lax.broadcasted_iota(I32, (mlen, D_XMEM), 1) >> DH_SHIFT
    acc = jnp.zeros((tq, D_XMEM), F32)
    for h in range(N_HEADS):
        qh = jnp.where(qhead == h, q, jnp.zeros_like(q))
        s = _nt_dot(qh, mk) * (1.0 / math.sqrt(DH))
        e = jnp.exp(s - jnp.max(s, axis=-1, keepdims=True))
        p = e / jnp.sum(e, axis=-1, keepdims=True)
        vh = jnp.where(vhead == h, mv, jnp.zeros_like(mv))
        acc = acc + _dot(p.astype(BF16), vh)
    o_ref[...] = acc.astype(o_ref.dtype)


def _mem_attn(um, mem2d, wkv_bf, batch, seq, mlen):
    tq = 1024
    nq = seq // tq
    kern = functools.partial(_mem_attn_kernel, tq=tq, mlen=mlen)
    return pl.pallas_call(
        kern,
        grid=(batch, nq),
        in_specs=[
            pl.BlockSpec((tq, D_XMEM), lambda b, i: (b * nq + i, 0)),
            pl.BlockSpec((mlen, D_MODEL), lambda b, i: (b, 0)),
            pl.BlockSpec((D_MODEL, 2 * D_XMEM), lambda b, i: (0, 0)),
        ],
        out_specs=pl.BlockSpec((tq, D_XMEM), lambda b, i: (b * nq + i, 0)),
        out_shape=jax.ShapeDtypeStruct((batch * seq, D_XMEM), BF16),
        scratch_shapes=[pltpu.VMEM((mlen, 2 * D_XMEM), BF16)],
        compiler_params=pltpu.CompilerParams(
            dimension_semantics=("arbitrary", "arbitrary"), vmem_limit_bytes=VMEM_LIMIT),
        name="mem_attn",
    )(um, mem2d, wkv_bf)


def _hgrn2_kernel(q_ref, kf_ref, lf_ref, kb_ref, lb_ref, v_ref, g_ref, nw_ref, o_ref,
                  kfp_ref, kbp_ref, vp_ref, bf_ref, bb_ref, acc_ref, cross_ref, *, seq, rb):
    n_chunks = seq // CHUNK
    w = 2 * DH
    row = lax.broadcasted_iota(I32, (rb, w), 0) & (CHUNK - 1)
    li = lax.broadcasted_iota(I32, (w, w), 0) >> DH_SHIFT
    lj = lax.broadcasted_iota(I32, (w, w), 1) >> DH_SHIFT
    same_head = li == lj
    head_ones = jnp.where(same_head, 1.0, 0.0).astype(BF16)

    def chunk_cumsum(x, reverse):
        for sh in (1, 2, 4, 8):
            if reverse:
                moved = pltpu.roll(x, rb - sh, 0)
                keep = row + sh < CHUNK
            else:
                moved = pltpu.roll(x, sh, 0)
                keep = row >= sh
            x = x + jnp.where(keep, moved, 0.0)
        return x

    halo = jnp.zeros((CHUNK, w), F32)
    for ref in (kfp_ref, kbp_ref, vp_ref, bf_ref, bb_ref):
        ref[0:CHUNK, :] = halo
        ref[seq + CHUNK:seq + 2 * CHUNK, :] = halo

    def prepare(blk, carry):
        r0 = pl.multiple_of(blk * rb, rb)
        rows = pl.ds(r0, rb)
        inner = pl.ds(r0 + CHUNK, rb)
        kfp_ref[inner, :] = kf_ref[rows, :]
        kbp_ref[inner, :] = kb_ref[rows, :]
        vp_ref[inner, :] = v_ref[rows, :]
        bf_ref[inner, :] = chunk_cumsum(lf_ref[rows, :] * LOG2E, False)
        bb_ref[inner, :] = chunk_cumsum(lb_ref[rows, :] * LOG2E, True)
        return carry

    lax.fori_loop(0, seq // rb, prepare, 0)

    def intra(blk):
        r0 = pl.multiple_of(blk * rb, rb)
        rows = pl.ds(r0, rb)
        q = q_ref[rows, :]
        acc = jnp.zeros((rb, w), F32)
        for reverse, kp_ref, b_ref in ((False, kfp_ref, bf_ref), (True, kbp_ref, bb_ref)):
            b = b_ref[pl.ds(r0 + CHUNK, rb), :]
            for d in range(CHUNK):
                src = pl.ds(r0 + CHUNK + (d if reverse else -d), rb)
                ks = kp_ref[src, :]
                vs = vp_ref[src, :]
                if d == 0:
                    z = q * ks
                else:
                    keep = (row + d < CHUNK) if reverse else (row >= d)
                    dec = jnp.exp2(jnp.where(keep, b - b_ref[src, :], -jnp.inf))
                    z = q * ks * dec
                acc = acc + _dot(z.astype(BF16), head_ones) * vs
        acc_ref[rows, :] = acc

    def inter(blk, carry):
        per_dir = ([], [])
        for c in range(rb // CHUNK):
            n = blk * (rb // CHUNK) + c
            for d, (r0, kp_ref, b_ref, edge) in enumerate((
                    (n * CHUNK, kfp_ref, bf_ref, CHUNK - 1),
                    ((n_chunks - 1 - n) * CHUNK, kbp_ref, bb_ref, 0))):
                r0 = pl.multiple_of(r0, CHUNK)
                rows = pl.ds(r0, CHUNK)
                prow = pl.ds(r0 + CHUNK, CHUNK)
                b = b_ref[prow, :]
                b_edge = b_ref[pl.ds(r0 + CHUNK + edge, 1), :]
                qd = (q_ref[rows, :] * jnp.exp2(b)).astype(BF16)
                kd = kp_ref[prow, :] * jnp.exp2(b_edge - b)
                upd = _tn_dot(vp_ref[prow, :].astype(BF16), kd.astype(BF16))
                per_dir[d].append((rows, qd, jnp.exp2(b_edge), jnp.where(same_head, upd, 0.0)))
        states = list(carry)
        for d in range(2):
            for rows, qd, decay, upd in per_dir[d]:
                cross_ref[rows, :] = cross_ref[rows, :] + _nt_dot(qd, states[d].astype(BF16))
                states[d] = states[d] * decay + upd
        return tuple(states)

    cross_ref[...] = jnp.zeros(cross_ref.shape, F32)

    def fused(blk, carry):
        intra(blk)
        return inter(blk, carry)

    z = jnp.zeros((w, w), F32)
    lax.fori_loop(0, seq // rb, fused, (z, z))

    def finish(blk, carry):
        rows = pl.ds(pl.multiple_of(blk * rb, rb), rb)
        o = acc_ref[rows, :] + cross_ref[rows, :]
        lane = lax.broadcasted_iota(I32, (rb, w), 1)
        lo = lane < DH
        sq = o * o
        ms_lo = jnp.sum(jnp.where(lo, sq, 0.0), axis=-1, keepdims=True)
        ms_hi = jnp.sum(jnp.where(lo, 0.0, sq), axis=-1, keepdims=True)
        ms = jnp.where(lo, ms_lo, ms_hi) * (1.0 / DH)
        o = o * lax.rsqrt(ms + NORM_EPS) * nw_ref[...] * g_ref[rows, :]
        o_ref[rows, :] = o.astype(o_ref.dtype)
        return carry

    lax.fori_loop(0, seq // rb, finish, 0)


def _hgrn2(uh, nw, batch, seq):
    w = 2 * DH
    npair = D_HGRN // w
    kern = functools.partial(_hgrn2_kernel, seq=seq, rb=256)

    def sec(s):
        return pl.BlockSpec((seq, w), lambda b, p, s=s: (b, s * npair + p))

    return pl.pallas_call(
        kern,
        grid=(batch, npair),
        in_specs=[sec(0), sec(1), sec(2), sec(3), sec(4), sec(5), sec(6),
                  pl.BlockSpec((1, w), lambda b, p: (0, p))],
        out_specs=pl.BlockSpec((seq, w), lambda b, p: (b, p)),
        out_shape=jax.ShapeDtypeStruct((batch * seq, D_HGRN), BF16),
        scratch_shapes=[pltpu.VMEM((seq + 2 * CHUNK, w), F32)] * 5
        + [pltpu.VMEM((seq, w), F32)] * 2,
        compiler_params=pltpu.CompilerParams(
            dimension_semantics=("arbitrary", "arbitrary"), vmem_limit_bytes=VMEM_LIMIT),
        name="hgrn2",
    )(uh, uh, uh, uh, uh, uh, uh, nw)


def _rows_to_block(rows, tm):
    r = lax.broadcasted_iota(I32, (8, tm), 0)
    out = jnp.zeros((8, tm), rows[0].dtype)
    for k, row in enumerate(rows):
        out = jnp.where(r == k, row, out)
    return out


def _pack_bf16_pairs(x):
    m = x.shape[1] // 2
    u = lax.bitcast_convert_type(x, U32)
    r = u + jnp.uint32(0x7FFF) + ((u >> 16) & jnp.uint32(1))
    return (r[:, 0:m] >> 16) | (r[:, m:2 * m] & jnp.uint32(0xFFFF0000))


def _unpack_bf16_pairs(w):
    lo = lax.bitcast_convert_type(w << 16, F32).astype(BF16)
    hi = lax.bitcast_convert_type(w & jnp.uint32(0xFFFF0000), F32).astype(BF16)
    return jnp.concatenate([lo, hi], axis=1)


def _post_mixer_kernel(x_ref, od_ref, oh_ref, om_ref, wo_ref, g_ref, b_ref, rwt_ref,
                       rbt_ref, h_ref, hp_ref, eidx_ref, gate_ref, rank_ref, cnt_ref,
                       carry_ref, *, tm, alpha):
    i = pl.program_id(0)

    @pl.when(i == 0)
    def _():
        carry_ref[...] = jnp.zeros_like(carry_ref)

    mix = (_dot(od_ref[...], wo_ref[0:D_DIFF, :])
           + _dot(oh_ref[...], wo_ref[D_DIFF:D_DIFF + D_HGRN, :])
           + _dot(om_ref[...], wo_ref[D_DIFF + D_HGRN:, :]))
    h = _layer_norm(alpha * x_ref[...] + mix, g_ref[...], b_ref[...])
    h_ref[...] = h

    h_hi = h.astype(BF16)
    hp_ref[...] = _pack_bf16_pairs(h)
    h_lo = (h - h_hi.astype(F32)).astype(BF16)
    rwt = rwt_ref[...]
    rwt_hi = rwt.astype(BF16)
    rwt_lo = (rwt - rwt_hi.astype(F32)).astype(BF16)
    logits = (_nt_dot(rwt_hi, h_hi) + _nt_dot(rwt_hi, h_lo) + _nt_dot(rwt_lo, h_hi)
              + rbt_ref[...])

    erow = lax.broadcasted_iota(I32, (N_EXPERTS, tm), 0).astype(F32)
    work = logits
    sels, vals, idxs = [], [], []
    for _ in range(TOP_K):
        m = jnp.max(work, axis=0, keepdims=True)
        idx = jnp.min(jnp.where(work == m, erow, float(N_EXPERTS)), axis=0, keepdims=True)
        sel = erow == idx
        work = jnp.where(sel, -jnp.inf, work)
        sels.append(sel)
        vals.append(m)
        idxs.append(idx)
    es = [jnp.exp(v - vals[0]) for v in vals]
    den = es[0] + es[1] + es[2] + es[3]

    chosen = jnp.where(sels[0] | sels[1] | sels[2] | sels[3], 1.0, 0.0)
    ti = lax.broadcasted_iota(I32, (tm, tm), 0)
    tj = lax.broadcasted_iota(I32, (tm, tm), 1)
    earlier = jnp.where(ti < tj, 1.0, 0.0).astype(BF16)
    prefix = _dot(chosen.astype(BF16), earlier) + carry_ref[:, 0:1]

    ranks = [jnp.sum(jnp.where(sels[k], prefix, 0.0), axis=0, keepdims=True)
             for k in range(TOP_K)]
    eidx_ref[...] = _rows_to_block(idxs, tm).astype(I32)
    gate_ref[...] = _rows_to_block([e / den for e in es], tm)
    rank_ref[...] = _rows_to_block(ranks, tm).astype(I32)

    total = carry_ref[:, 0:1] + jnp.sum(chosen, axis=1, keepdims=True)
    carry_ref[...] = jnp.broadcast_to(total, carry_ref.shape)
    cnt_ref[...] = carry_ref[...]


def _post_mixer(x2d, od, oh, om, wo_bf, g, b, rwt, rbt, alpha):
    t = x2d.shape[0]
    tm = 512
    kern = functools.partial(_post_mixer_kernel, tm=tm, alpha=alpha)
    full = lambda i: (0, 0)
    tok = lambda i: (i, 0)
    per_tok = lambda i: (0, i)
    return pl.pallas_call(
        kern,
        grid=(t // tm,),
        in_specs=[
            pl.BlockSpec((tm, D_MODEL), tok),
            pl.BlockSpec((tm, D_DIFF), tok),
            pl.BlockSpec((tm, D_HGRN), tok),
            pl.BlockSpec((tm, D_XMEM), tok),
            pl.BlockSpec((D_MODEL, D_MODEL), full),
            pl.BlockSpec((1, D_MODEL), full),
            pl.BlockSpec((1, D_MODEL), full),
            pl.BlockSpec((N_EXPERTS, D_MODEL), full),
            pl.BlockSpec((N_EXPERTS, 1), full),
        ],
        out_specs=[
            pl.BlockSpec((tm, D_MODEL), tok),
            pl.BlockSpec((tm, D_MODEL // 2), tok),
            pl.BlockSpec((8, tm), per_tok),
            pl.BlockSpec((8, tm), per_tok),
            pl.BlockSpec((8, tm), per_tok),
            pl.BlockSpec((N_EXPERTS, LANES), full),
        ],
        out_shape=[
            jax.ShapeDtypeStruct((t, D_MODEL), F32),
            jax.ShapeDtypeStruct((t, D_MODEL // 2), U32),
            jax.ShapeDtypeStruct((8, t), I32),
            jax.ShapeDtypeStruct((8, t), F32),
            jax.ShapeDtypeStruct((8, t), I32),
            jax.ShapeDtypeStruct((N_EXPERTS, LANES), F32),
        ],
        scratch_shapes=[pltpu.VMEM((N_EXPERTS, LANES), F32)],
        compiler_params=pltpu.CompilerParams(
            dimension_semantics=("arbitrary",), vmem_limit_bytes=VMEM_LIMIT),
        name="post_mixer",
    )(x2d, od, oh, om, wo_bf, g, b, rwt, rbt)


def _route_meta_kernel(eidx_ref, rank_ref, cnt_ref, dest_ref, be_ref, *, tm, nb_lanes):
    cnt = cnt_ref[...]
    blocks = jnp.floor((cnt + (MOE_BLOCK - 1)) * (1.0 / MOE_BLOCK))
    ei = lax.broadcasted_iota(I32, (N_EXPERTS, N_EXPERTS), 0)
    ej = lax.broadcasted_iota(I32, (N_EXPERTS, N_EXPERTS), 1)
    below = jnp.where(ej < ei, 1.0, 0.0).astype(BF16)
    pstart = _dot(below, blocks.astype(BF16))[:, 0:1]
    pend = pstart + blocks[:, 0:1]

    erow = lax.broadcasted_iota(I32, (N_EXPERTS, tm), 0).astype(F32)
    eidx = eidx_ref[...].astype(F32)
    starts = [jnp.sum(jnp.where(erow == eidx[k:k + 1, :], pstart, 0.0), axis=0, keepdims=True)
              for k in range(TOP_K)]
    dest_ref[...] = ((_rows_to_block(starts, tm) * float(MOE_BLOCK)).astype(I32)
                     + rank_ref[...])

    bi = lax.broadcasted_iota(I32, (N_EXPERTS, nb_lanes), 1).astype(F32)
    done = jnp.where(pend <= bi, 1.0, 0.0)
    be = jnp.minimum(jnp.sum(done, axis=0, keepdims=True), float(N_EXPERTS - 1))
    used = jnp.broadcast_to(pend[N_EXPERTS - 1:N_EXPERTS, :], (1, nb_lanes))
    be_ref[...] = _rows_to_block([be, used], nb_lanes).astype(I32)


def _route_meta(eidx, rank, cnt, n_blocks):
    t = eidx.shape[1]
    tm = 2048
    nb_lanes = -(-n_blocks // LANES) * LANES
    kern = functools.partial(_route_meta_kernel, tm=tm, nb_lanes=nb_lanes)
    return pl.pallas_call(
        kern,
        grid=(t // tm,),
        in_specs=[
            pl.BlockSpec((8, tm), lambda i: (0, i)),
            pl.BlockSpec((8, tm), lambda i: (0, i)),
            pl.BlockSpec((N_EXPERTS, LANES), lambda i: (0, 0)),
        ],
        out_specs=[
            pl.BlockSpec((8, tm), lambda i: (0, i)),
            pl.BlockSpec((8, nb_lanes), lambda i: (0, 0)),
        ],
        out_shape=[
            jax.ShapeDtypeStruct((8, t), I32),
            jax.ShapeDtypeStruct((8, nb_lanes), I32),
        ],
        compiler_params=pltpu.CompilerParams(
            dimension_semantics=("arbitrary",), vmem_limit_bytes=VMEM_LIMIT),
        name="route_meta",
    )(eidx, rank, cnt)


SC_WINDOW = LANES
N_QUARTERS = 4
QUARTER = D_MODEL // N_QUARTERS
N_PACKED = D_MODEL // 2 // QUARTER
COMBINE_PARTS = 4


def _sc_mesh():
    return plsc.VectorSubcoreMesh(core_axis_name="c", subcore_axis_name="s")


def _dispatch(dest_km, hp, n_rows):
    t = hp.shape[0]

    @functools.partial(
        pl.kernel, out_type=[jax.ShapeDtypeStruct((n_rows, QUARTER), hp.dtype)] * N_PACKED,
        mesh=_sc_mesh(), scratch_types=[])
    def scatter_rows(h_hbm, d_hbm, *xs_hbm):
        for q in range(N_PACKED):
            def body(x_vmem, i_vmem, q=q):
                for k in range(TOP_K):
                    pltpu.sync_copy(x_vmem, xs_hbm[q].at[i_vmem.at[k]])

            pltpu.emit_pipeline(
                body,
                grid=(t // SC_WINDOW,),
                in_specs=[pl.BlockSpec((SC_WINDOW, QUARTER), lambda i, q=q: (i, q)),
                          pl.BlockSpec((TOP_K, SC_WINDOW), lambda i: (0, i))],
                out_specs=[],
                core_axis_name=("c", "s"),
                dimension_semantics=(pltpu.PARALLEL,),
            )(h_hbm, d_hbm)

    return scatter_rows(hp, dest_km)


def _gather_back(dest_km, ys):
    t = dest_km.shape[1]
    n = len(ys)

    @functools.partial(
        pl.kernel,
        out_type=[jax.ShapeDtypeStruct((t, TOP_K * QUARTER), ys[0].dtype)] * n,
        mesh=_sc_mesh(), scratch_types=[])
    def gather_rows(d_hbm, *refs):
        ys_hbm, out_hbm = refs[:n], refs[n:]
        for q in range(n):
            for k in range(TOP_K):
                def body(i_vmem, o_vmem, q=q, k=k):
                    pltpu.sync_copy(ys_hbm[q].at[i_vmem.at[k]], o_vmem)

                pltpu.emit_pipeline(
                    body,
                    grid=(t // SC_WINDOW,),
                    in_specs=[pl.BlockSpec((TOP_K, SC_WINDOW), lambda i: (0, i))],
                    out_specs=[pl.BlockSpec((SC_WINDOW, QUARTER), lambda i, k=k: (i, k))],
                    core_axis_name=("c", "s"),
                    dimension_semantics=(pltpu.PARALLEL,),
                )(d_hbm, out_hbm[q])

    return gather_rows(dest_km, *ys)


def _experts_kernel(be_ref, used_ref, *refs):
    xs_refs = refs[:N_PACKED]
    wgu_hbm, wdn_hbm, bg_ref, bl_ref, bd_ref, perm_ref = refs[N_PACKED:N_PACKED + 6]
    ys_refs = refs[N_PACKED + 6:2 * N_PACKED + 6]
    wgu_st, wdn_st, wg_ref, wl_ref, wd_ref, sem = refs[2 * N_PACKED + 6:]
    i = pl.program_id(0)
    used = used_ref[0]
    e = be_ref[i]
    prev = be_ref[jnp.maximum(i - 1, 0)]
    fresh = jnp.logical_and(i < used, jnp.logical_or(i == 0, e != prev))

    def weight_copies(ex):
        return (pltpu.make_async_copy(wgu_hbm.at[ex], wgu_st, sem.at[0]),
                pltpu.make_async_copy(wdn_hbm.at[ex], wdn_st, sem.at[1]))

    @pl.when(jnp.logical_and(i == 0, used > 0))
    def _():
        for cp in weight_copies(e):
            cp.start()

    @pl.when(fresh)
    def _():
        for cp in weight_copies(e):
            cp.wait()
        perm = perm_ref[...]
        half = LANES
        for c in range(2 * D_EXPERT // (2 * half)):
            slab = wgu_st[:, c * 2 * half:(c + 1) * 2 * half].astype(BF16)
            sep = _dot(slab, perm)
            wg_ref[:, c * half:(c + 1) * half] = sep[:, 0:half].astype(BF16)
            wl_ref[:, c * half:(c + 1) * half] = sep[:, half:2 * half].astype(BF16)
        wd_ref[...] = wdn_st[...].astype(BF16)

        def same_expert(j):
            return jnp.logical_and(j < used, be_ref[jnp.minimum(j, used - 1)] == e)

        nxt = lax.while_loop(same_expert, lambda j: j + 1, i + 1)

        @pl.when(nxt < used)
        def _():
            for cp in weight_copies(be_ref[nxt]):
                cp.start()

    @pl.when(i < used_ref[0])
    def _():
        xb = _unpack_bf16_pairs(jnp.concatenate([r[...] for r in xs_refs], axis=1))
        glu = jnp.minimum(_dot(xb, wg_ref[...]) + bg_ref[0], SWIGLU_LIMIT)
        lin = jnp.clip(_dot(xb, wl_ref[...]) + bl_ref[0], -SWIGLU_LIMIT, SWIGLU_LIMIT)
        act = glu * _sigmoid(SWIGLU_ALPHA * glu) * (lin + 1.0)
        y = _pack_bf16_pairs(_dot(act.astype(BF16), wd_ref[...]) + bd_ref[0])
        for q, ys_ref in enumerate(ys_refs):
            ys_ref[...] = y[:, q * QUARTER:(q + 1) * QUARTER]

    @pl.when(i >= used_ref[0])
    def _():
        for ys_ref in ys_refs:
            ys_ref[...] = jnp.zeros_like(ys_ref)


def _experts(block_e, used, xs, w_gu, w_dn, b_g, b_l, b_d, perm):
    n_rows = xs[0].shape[0]
    n_blocks = n_rows // MOE_BLOCK
    wmap = lambda i, be, u: (be[i], 0, 0)
    rows = pl.BlockSpec((MOE_BLOCK, QUARTER), lambda i, be, u: (i, 0))
    return pl.pallas_call(
        _experts_kernel,
        grid_spec=pltpu.PrefetchScalarGridSpec(
            num_scalar_prefetch=2,
            grid=(n_blocks,),
            in_specs=[rows] * N_PACKED + [
                pl.BlockSpec(memory_space=pl.ANY),
                pl.BlockSpec(memory_space=pl.ANY),
                pl.BlockSpec((1, 1, D_EXPERT), wmap),
                pl.BlockSpec((1, 1, D_EXPERT), wmap),
                pl.BlockSpec((1, 1, D_MODEL), wmap),
                pl.BlockSpec((2 * LANES, 2 * LANES), lambda i, be, u: (0, 0)),
            ],
            out_specs=[rows] * N_PACKED,
            scratch_shapes=[
                pltpu.VMEM((D_MODEL, 2 * D_EXPERT), F32),
                pltpu.VMEM((D_EXPERT, D_MODEL), F32),
                pltpu.VMEM((D_MODEL, D_EXPERT), BF16),
                pltpu.VMEM((D_MODEL, D_EXPERT), BF16),
                pltpu.VMEM((D_EXPERT, D_MODEL), BF16),
                pltpu.SemaphoreType.DMA((2,)),
            ],
        ),
        out_shape=[jax.ShapeDtypeStruct((n_rows, QUARTER), U32)] * N_PACKED,
        compiler_params=pltpu.CompilerParams(
            dimension_semantics=("arbitrary",), vmem_limit_bytes=VMEM_LIMIT),
        name="experts",
    )(block_e, used, *xs, w_gu, w_dn, b_g, b_l, b_d, perm)


def _combine_kernel(h_ref, gate_ref, g_ref, b_ref, *refs, tm, alpha):
    yg_refs, o_ref = refs[:N_PACKED], refs[N_PACKED]
    gate = jnp.transpose(gate_ref[...])
    gks = [gate[:, k:k + 1] for k in range(TOP_K)]
    lows, highs = [], []
    for yg_ref in yg_refs:
        lo = jnp.zeros((tm, QUARTER), F32)
        hi = jnp.zeros((tm, QUARTER), F32)
        for k in range(TOP_K):
            w = yg_ref[:, k * QUARTER:(k + 1) * QUARTER]
            lo = lo + gks[k] * lax.bitcast_convert_type(w << 16, F32)
            hi = hi + gks[k] * lax.bitcast_convert_type(w & jnp.uint32(0xFFFF0000), F32)
        lows.append(lo)
        highs.append(hi)
    ffn = jnp.concatenate(lows + highs, axis=1)
    o_ref[...] = _layer_norm(alpha * h_ref[...] + ffn, g_ref[...], b_ref[...])


def _combine(h, gate, g, b, yg, alpha, part, n_parts, prev_out):
    t = h.shape[0]
    tm = 256
    tiles = t // n_parts // tm
    kern = functools.partial(_combine_kernel, tm=tm, alpha=alpha)
    tok = lambda i: (part * tiles + i, 0)
    in_specs = [
        pl.BlockSpec((tm, D_MODEL), tok),
        pl.BlockSpec((8, tm), lambda i: (0, part * tiles + i)),
        pl.BlockSpec((1, D_MODEL), lambda i: (0, 0)),
        pl.BlockSpec((1, D_MODEL), lambda i: (0, 0)),
    ] + [pl.BlockSpec((tm, TOP_K * QUARTER), lambda i: (i, 0))] * N_PACKED
    args = [h, gate, g, b, *yg]
    aliases = {}
    if prev_out is not None:
        n_in = len(args)

        def kern(*refs, body=kern):
            body(*refs[:n_in], *refs[n_in + 1:])
        in_specs.append(pl.BlockSpec(memory_space=pl.ANY))
        aliases = {n_in: 0}
        args.append(prev_out)
    return pl.pallas_call(
        kern,
        grid=(tiles,),
        in_specs=in_specs,
        out_specs=pl.BlockSpec((tm, D_MODEL), tok),
        out_shape=jax.ShapeDtypeStruct((t, D_MODEL), F32),
        input_output_aliases=aliases,
        compiler_params=pltpu.CompilerParams(
            dimension_semantics=("arbitrary",), vmem_limit_bytes=VMEM_LIMIT),
        name="combine",
    )(*args)


def _column_split_permutation():
    n = 2 * LANES
    p = np.zeros((n, n), np.float32)
    j = np.arange(n)
    p[j, j // 2 + LANES * (j % 2)] = 1.0
    return jnp.asarray(p, dtype=BF16)


def kernel(x, mem, w_in, lam_q1, lam_k1, lam_q2, lam_k2, diff_norm_w, hgrn_lb_fwd,
           hgrn_lb_bwd, hgrn_norm_w, w_mem_kv, w_o, ln1_g, ln1_b, router_w, router_b,
           w_gate_up, b_gate_up, w_down, b_down, ln2_g, ln2_b):
    batch, seq, d = x.shape
    mlen = mem.shape[1]
    assert d == D_MODEL and w_in.shape == (N_LAYERS, D_MODEL, D_IN_PROJ)
    assert hgrn_lb_fwd.shape == (N_LAYERS + 1, D_HGRN)
    t = batch * seq
    alpha = (2.0 * N_LAYERS) ** 0.25
    lam_init = 0.8 - 0.6 * math.exp(-0.3 * 0)

    x2d = x.reshape(t, D_MODEL)
    ua, uh, um = _in_proj(x2d, w_in[0].astype(BF16), hgrn_lb_fwd, hgrn_lb_bwd)
    o_diff = _diff_attn(ua, diff_norm_w, lam_q1, lam_k1, lam_q2, lam_k2,
                        batch, seq, lam_init)
    o_mem = _mem_attn(um, mem.reshape(batch * mlen, D_MODEL), w_mem_kv[0].astype(BF16),
                      batch, seq, mlen)
    o_hgrn = _hgrn2(uh, hgrn_norm_w, batch, seq)

    h1, h1b, eidx, gate, rank, cnt = _post_mixer(
        x2d, o_diff, o_hgrn, o_mem, w_o[0].astype(BF16), ln1_g, ln1_b, router_w[0].T,
        router_b.reshape(N_EXPERTS, 1), alpha)

    n_blocks = -(-(t * TOP_K) // MOE_BLOCK) + N_EXPERTS
    dest, meta = _route_meta(eidx, rank, cnt, n_blocks)
    dest_km = dest[0:TOP_K]
    block_e = meta[0, 0:n_blocks]
    used = meta[1, 0:1]

    xs = _dispatch(dest_km, h1b, n_blocks * MOE_BLOCK)
    b_g = b_gate_up[0][:, 0::2].reshape(N_EXPERTS, 1, D_EXPERT)
    b_l = b_gate_up[0][:, 1::2].reshape(N_EXPERTS, 1, D_EXPERT)
    ys = _experts(block_e, used, xs, w_gate_up[0], w_down[0], b_g, b_l,
                  b_down[0].reshape(N_EXPERTS, 1, D_MODEL), _column_split_permutation())
    out = None
    for part in range(COMBINE_PARTS):
        lo = part * (t // COMBINE_PARTS)
        yg = _gather_back(dest_km[:, lo:lo + t // COMBINE_PARTS], ys)
        out = _combine(h1, gate, ln2_g, ln2_b, yg, alpha, part, COMBINE_PARTS, out)
    return out.reshape(batch, seq, D_MODEL)
```

```python
import functools
import math

import jax
import jax.numpy as jnp
import numpy as np
from jax import lax
from jax.experimental import pallas as pl
from jax.experimental.pallas import tpu as pltpu
from jax.experimental.pallas import tpu_sc as plsc

F32 = jnp.float32
BF16 = jnp.bfloat16
I32 = jnp.int32
U32 = jnp.uint32

D_MODEL = 1024
N_LAYERS = 1
D_DIFF = 512
D_HGRN = 256
D_XMEM = 256
N_HEADS = 4
DH = 64
DH_SHIFT = 6
D_IN_PROJ = 3 * D_DIFF + 5 * D_HGRN + D_XMEM
CHUNK = 16
N_EXPERTS = 32
TOP_K = 4
D_EXPERT = 1024
MOE_BLOCK = 512
SWIGLU_ALPHA = 1.702
SWIGLU_LIMIT = 7.0
NORM_EPS = 1e-5
LOG2E = math.log2(math.e)
LANES = 128
VMEM_LIMIT = 56 * 1024 * 1024


def _nt_dot(a, b):
    return lax.dot_general(a, b, (((1,), (1,)), ((), ())), preferred_element_type=F32)


def _tn_dot(a, b):
    return lax.dot_general(a, b, (((0,), (0,)), ((), ())), preferred_element_type=F32)


def _dot(a, b):
    return jnp.dot(a, b, preferred_element_type=F32)


def _sigmoid(x):
    return 1.0 / (1.0 + jnp.exp(-x))


def _layer_norm(y, g, b):
    mu = jnp.mean(y, axis=-1, keepdims=True)
    yc = y - mu
    var = jnp.mean(yc * yc, axis=-1, keepdims=True)
    return yc * lax.rsqrt(var + NORM_EPS) * g + b


def _in_proj_kernel(x_ref, w_ref, lbf_ref, lbb_ref, ua_ref, uh_ref, um_ref):
    xb = x_ref[...].astype(BF16)

    def proj(c0, width):
        return _dot(xb, w_ref[:, c0:c0 + width])

    ua_ref[:, 0:D_DIFF] = (proj(0, D_DIFF) * (LOG2E / math.sqrt(DH))).astype(BF16)
    ua_ref[:, D_DIFF:2 * D_DIFF] = proj(D_DIFF, D_DIFF).astype(BF16)
    ua_ref[:, 2 * D_DIFF:3 * D_DIFF] = proj(2 * D_DIFF, D_DIFF).astype(BF16)

    def lower_bound(lb_ref):
        a = lb_ref[0:1, :]
        b = lb_ref[1:2, :]
        m = jnp.maximum(a, b)
        ea = jnp.exp(a - m)
        eb = jnp.exp(b - m)
        return ea / (ea + eb)

    base = 3 * D_DIFF
    hq = proj(base, D_HGRN)
    uh_ref[:, 0:D_HGRN] = hq * _sigmoid(hq)
    for d, lb_ref in enumerate((lbf_ref, lbb_ref)):
        lb = lower_bound(lb_ref)
        f = lb + (1.0 - lb) * _sigmoid(proj(base + (1 + d) * D_HGRN, D_HGRN))
        uh_ref[:, (1 + 2 * d) * D_HGRN:(2 + 2 * d) * D_HGRN] = 1.0 - f
        uh_ref[:, (2 + 2 * d) * D_HGRN:(3 + 2 * d) * D_HGRN] = jnp.log(f)
    uh_ref[:, 5 * D_HGRN:6 * D_HGRN] = proj(base + 3 * D_HGRN, D_HGRN)
    uh_ref[:, 6 * D_HGRN:7 * D_HGRN] = _sigmoid(proj(base + 4 * D_HGRN, D_HGRN))
    um_ref[...] = proj(base + 5 * D_HGRN, D_XMEM).astype(BF16)


def _in_proj(x2d, w_in_bf, lb_f, lb_b):
    t = x2d.shape[0]
    tm = 1024
    return pl.pallas_call(
        _in_proj_kernel,
        grid=(t // tm,),
        in_specs=[
            pl.BlockSpec((tm, D_MODEL), lambda i: (i, 0)),
            pl.BlockSpec((D_MODEL, D_IN_PROJ), lambda i: (0, 0)),
            pl.BlockSpec((N_LAYERS + 1, D_HGRN), lambda i: (0, 0)),
            pl.BlockSpec((N_LAYERS + 1, D_HGRN), lambda i: (0, 0)),
        ],
        out_specs=[
            pl.BlockSpec((tm, 3 * D_DIFF), lambda i: (i, 0)),
            pl.BlockSpec((tm, 7 * D_HGRN), lambda i: (i, 0)),
            pl.BlockSpec((tm, D_XMEM), lambda i: (i, 0)),
        ],
        out_shape=[
            jax.ShapeDtypeStruct((t, 3 * D_DIFF), BF16),
            jax.ShapeDtypeStruct((t, 7 * D_HGRN), F32),
            jax.ShapeDtypeStruct((t, D_XMEM), BF16),
        ],
        compiler_params=pltpu.CompilerParams(
            dimension_semantics=("arbitrary",), vmem_limit_bytes=VMEM_LIMIT),
        name="in_proj",
    )(x2d, w_in_bf, lb_f, lb_b)


N_POS_FEATURES = 12


def _bf16_pieces(x):
    x = np.asarray(x, np.float32)
    out = []
    for _ in range(3):
        p = x.astype(BF16).astype(np.float32)
        out.append(p)
        x = x - p
    return np.stack(out, axis=-1)


def _alibi_tables(seq, t):
    slopes = (2.0 ** (-8.0 * np.arange(1, N_HEADS + 1) / N_HEADS)).astype(np.float32)
    c3 = _bf16_pieces(slopes * np.float32(LOG2E))
    pos = np.arange(seq)
    hi = (pos >> DH_SHIFT).astype(np.float32)
    lo = (pos & (DH - 1)).astype(np.float32)
    qf = np.zeros((N_HEADS, 2, seq, 2 * DH), np.float32)
    kf = np.zeros((N_HEADS, 2, seq, 2 * DH), np.float32)
    for m, base in enumerate((DH, 0)):
        for j in range(3):
            qf[:, m, :, base + j] = hi
            qf[:, m, :, base + 3 + j] = lo
            qf[:, m, :, base + 6 + j] = 64.0 * c3[:, j, None]
            qf[:, m, :, base + 9 + j] = c3[:, j, None]
            kf[:, m, :, base + j] = -64.0 * c3[:, j, None]
            kf[:, m, :, base + 3 + j] = -c3[:, j, None]
            kf[:, m, :, base + 6 + j] = hi
            kf[:, m, :, base + 9 + j] = lo
    c = c3.sum(axis=-1)
    kk = np.arange(t)[:, None]
    qq = np.arange(t)[None, :]
    corr = 2.0 * c[:, None, None] * np.minimum(qq - kk, 0).astype(np.float32)
    return (jnp.asarray(qf, dtype=BF16), jnp.asarray(kf, dtype=BF16),
            jnp.asarray(corr, dtype=F32))


def _diff_attn_kernel(q_ref, k_ref, v_ref, qf_ref, kf_ref, corr_ref, nw_ref, lq1_ref,
                      lk1_ref, lq2_ref, lk2_ref, o_ref, ka1_ref, ka2_ref, vt_ref, acc_ref,
                      m_ref, l_ref, *, t, qs, qpt, seq, lam_init):
    i = pl.program_id(2)
    nk = seq // t
    lane = lax.broadcasted_iota(I32, (t, 2 * DH), 1)
    first_half = lane < DH

    @pl.when(i == 0)
    def _():
        def build(r, carry):
            rows = pl.ds(pl.multiple_of(r * t, t), t)
            kblk = k_ref[rows, :]
            ka1_ref[rows, :] = jnp.where(first_half, kblk, kf_ref[0, 0, rows, :])
            ka2_ref[rows, :] = jnp.where(first_half, kf_ref[0, 1, rows, :], kblk)
            vt_ref[r] = v_ref[rows, :].astype(F32).T.astype(BF16)
            return carry

        lax.fori_loop(0, nk, build, 0)

    lam = (jnp.exp(jnp.sum(lq1_ref[...] * lk1_ref[...], axis=-1, keepdims=True))
           - jnp.exp(jnp.sum(lq2_ref[...] * lk2_ref[...], axis=-1, keepdims=True))
           + lam_init)

    for qt in range(qpt):
        qi = i * qpt + qt
        qrange = slice(qt * t, (qt + 1) * t)
        q = q_ref[qrange, :]
        qf1 = qf_ref[0, 0, qrange, :]
        qf2 = qf_ref[0, 1, qrange, :]
        qa1_before = jnp.where(first_half, q, qf1)
        qa1_after = jnp.where(first_half, q, -qf1)
        qa2_before = jnp.where(first_half, qf2, q)
        qa2_after = jnp.where(first_half, -qf2, q)

        m_ref[qt] = jnp.full(m_ref.shape[1:], -jnp.inf, F32)
        l_ref[qt] = jnp.zeros(l_ref.shape[1:], F32)
        acc_ref[qt] = jnp.zeros(acc_ref.shape[1:], F32)

        def chunk(j, qa1, qa2, diagonal, qt=qt):
            rows = pl.ds(pl.multiple_of(j * t, t), t)
            vt = vt_ref[j]
            for mp, (qa, ka_ref) in enumerate(((qa1, ka1_ref), (qa2, ka2_ref))):
                ka = ka_ref[rows, :]
                for u in range(t // qs):
                    qrows = slice(u * qs, (u + 1) * qs)
                    cols = slice(mp * t + u * qs, mp * t + (u + 1) * qs)
                    s = _nt_dot(ka, qa[qrows, :])
                    if diagonal:
                        s = s + corr_ref[0, :, qrows]
                    m_old = m_ref[qt, :, cols]
                    m_new = jnp.maximum(m_old, jnp.max(s, axis=0, keepdims=True))
                    p = jnp.exp2(s - m_new)
                    r = jnp.exp2(m_old - m_new)
                    l_ref[qt, :, cols] = (r * l_ref[qt, :, cols]
                                          + jnp.sum(p, axis=0, keepdims=True))
                    acc_ref[qt, mp, :, qrows] = (r * acc_ref[qt, mp, :, qrows]
                                                 + _dot(vt, p.astype(BF16)))
                    m_ref[qt, :, cols] = m_new

        chunk(qi, qa1_before, qa2_before, True)
        for jj in range(nk - 1):
            j = jj + (jj >= qi).astype(I32)
            keys_first = j < qi
            chunk(j, jnp.where(keys_first, qa1_before, qa1_after),
                  jnp.where(keys_first, qa2_before, qa2_after), False)

        o = (acc_ref[qt, 0] / l_ref[qt, :, 0:t]
             - lam * (acc_ref[qt, 1] / l_ref[qt, :, t:2 * t]))
        o = o * lax.rsqrt(jnp.mean(o * o, axis=0, keepdims=True) + NORM_EPS)
        o_ref[qrange, :] = (o.T * nw_ref[...] * (1.0 - lam_init)).astype(o_ref.dtype)


def _diff_attn(ua, nw, lq1, lk1, lq2, lk2, batch, seq, lam_init):
    t = 512
    qpt = 2
    nq = seq // (t * qpt)
    qf, kf, corr = _alibi_tables(seq, t)
    kern = functools.partial(_diff_attn_kernel, t=t, qs=LANES, qpt=qpt, seq=seq,
                             lam_init=lam_init)
    small = lambda b, h, i: (0, 0)
    return pl.pallas_call(
        kern,
        grid_spec=pltpu.PrefetchScalarGridSpec(
            num_scalar_prefetch=0,
            grid=(batch, N_HEADS, nq),
            in_specs=[
                pl.BlockSpec((qpt * t, 2 * DH), lambda b, h, i: (b * nq + i, h)),
                pl.BlockSpec((seq, 2 * DH), lambda b, h, i: (b, N_HEADS + h)),
                pl.BlockSpec((seq, 2 * DH), lambda b, h, i: (b, 2 * N_HEADS + h)),
                pl.BlockSpec((1, 2, qpt * t, 2 * DH), lambda b, h, i: (h, 0, i, 0)),
                pl.BlockSpec((1, 2, seq, 2 * DH), lambda b, h, i: (h, 0, 0, 0)),
                pl.BlockSpec((1, t, t), lambda b, h, i: (h, 0, 0)),
                pl.BlockSpec((1, 2 * DH), small),
                pl.BlockSpec((1, DH), small),
                pl.BlockSpec((1, DH), small),
                pl.BlockSpec((1, DH), small),
                pl.BlockSpec((1, DH), small),
            ],
            out_specs=pl.BlockSpec((qpt * t, 2 * DH), lambda b, h, i: (b * nq + i, h)),
            scratch_shapes=[
                pltpu.VMEM((seq, 2 * DH), BF16),
                pltpu.VMEM((seq, 2 * DH), BF16),
                pltpu.VMEM((seq // t, 2 * DH, t), BF16),
                pltpu.VMEM((qpt, 2, 2 * DH, t), F32),
                pltpu.VMEM((qpt, 1, 2 * t), F32),
                pltpu.VMEM((qpt, 1, 2 * t), F32),
            ],
        ),
        out_shape=jax.ShapeDtypeStruct((batch * seq, D_DIFF), BF16),
        compiler_params=pltpu.CompilerParams(
            dimension_semantics=("arbitrary", "arbitrary", "arbitrary"),
            vmem_limit_bytes=VMEM_LIMIT),
        name="diff_attn",
    )(ua, ua, ua, qf, kf, corr, nw, lq1, lk1, lq2, lk2)


def _mem_attn_kernel(q_ref, mem_ref, wkv_ref, o_ref, kv_ref, *, tq, mlen):
    @pl.when(pl.program_id(1) == 0)
    def _():
        kv_ref[...] = _dot(mem_ref[...].astype(BF16), wkv_ref[...]).astype(BF16)

    q = q_ref[...]
    mk = kv_ref[:, 0:D_XMEM]
    mv = kv_ref[:, D_XMEM:2 * D_XMEM]
    qhead = lax.broadcasted_iota(I32, (tq, D_XMEM), 1) >> DH_SHIFT
    vhead = lax.broadcasted_iota(I32, (mlen, D_XMEM), 1) >> DH_SHIFT
    acc = jnp.zeros((tq, D_XMEM), F32)
    for h in range(N_HEADS):
        qh = jnp.where(qhead == h, q, jnp.zeros_like(q))
        s = _nt_dot(qh, mk) * (1.0 / math.sqrt(DH))
        e = jnp.exp(s - jnp.max(s, axis=-1, keepdims=True))
        p = e / jnp.sum(e, axis=-1, keepdims=True)
        vh = jnp.where(vhead == h, mv, jnp.zeros_like(mv))
        acc = acc + _dot(p.astype(BF16), vh)
    o_ref[...] = acc.astype(o_ref.dtype)


def _mem_attn(um, mem2d, wkv_bf, batch, seq, mlen):
    tq = 1024
    nq = seq // tq
    kern = functools.partial(_mem_attn_kernel, tq=tq, mlen=mlen)
    return pl.pallas_call(
        kern,
        grid=(batch, nq),
        in_specs=[
            pl.BlockSpec((tq, D_XMEM), lambda b, i: (b * nq + i, 0)),
            pl.BlockSpec((mlen, D_MODEL), lambda b, i: (b, 0)),
            pl.BlockSpec((D_MODEL, 2 * D_XMEM), lambda b, i: (0, 0)),
        ],
        out_specs=pl.BlockSpec((tq, D_XMEM), lambda b, i: (b * nq + i, 0)),
        out_shape=jax.ShapeDtypeStruct((batch * seq, D_XMEM), BF16),
        scratch_shapes=[pltpu.VMEM((mlen, 2 * D_XMEM), BF16)],
        compiler_params=pltpu.CompilerParams(
            dimension_semantics=("arbitrary", "arbitrary"), vmem_limit_bytes=VMEM_LIMIT),
        name="mem_attn",
    )(um, mem2d, wkv_bf)


def _hgrn2_kernel(q_ref, kf_ref, lf_ref, kb_ref, lb_ref, v_ref, g_ref, nw_ref, o_ref,
                  kfp_ref, kbp_ref, vp_ref, bf_ref, bb_ref, acc_ref, cross_ref, *, seq, rb):
    n_chunks = seq // CHUNK
    w = 2 * DH
    row = lax.broadcasted_iota(I32, (rb, w), 0) & (CHUNK - 1)
    li = lax.broadcasted_iota(I32, (w, w), 0) >> DH_SHIFT
    lj = lax.broadcasted_iota(I32, (w, w), 1) >> DH_SHIFT
    same_head = li == lj
    head_ones = jnp.where(same_head, 1.0, 0.0).astype(BF16)

    def chunk_cumsum(x, reverse):
        for sh in (1, 2, 4, 8):
            if reverse:
                moved = pltpu.roll(x, rb - sh, 0)
                keep = row + sh < CHUNK
            else:
                moved = pltpu.roll(x, sh, 0)
                keep = row >= sh
            x = x + jnp.where(keep, moved, 0.0)
        return x

    halo = jnp.zeros((CHUNK, w), F32)
    for ref in (kfp_ref, kbp_ref, vp_ref, bf_ref, bb_ref):
        ref[0:CHUNK, :] = halo
        ref[seq + CHUNK:seq + 2 * CHUNK, :] = halo

    def prepare(blk, carry):
        r0 = pl.multiple_of(blk * rb, rb)
        rows = pl.ds(r0, rb)
        inner = pl.ds(r0 + CHUNK, rb)
        kfp_ref[inner, :] = kf_ref[rows, :]
        kbp_ref[inner, :] = kb_ref[rows, :]
        vp_ref[inner, :] = v_ref[rows, :]
        bf_ref[inner, :] = chunk_cumsum(lf_ref[rows, :] * LOG2E, False)
        bb_ref[inner, :] = chunk_cumsum(lb_ref[rows, :] * LOG2E, True)
        return carry

    lax.fori_loop(0, seq // rb, prepare, 0)

    def intra(blk):
        r0 = pl.multiple_of(blk * rb, rb)
        rows = pl.ds(r0, rb)
        q = q_ref[rows, :]
        acc = jnp.zeros((rb, w), F32)
        for reverse, kp_ref, b_ref in ((False, kfp_ref, bf_ref), (True, kbp_ref, bb_ref)):
            b = b_ref[pl.ds(r0 + CHUNK, rb), :]
            for d in range(CHUNK):
                src = pl.ds(r0 + CHUNK + (d if reverse else -d), rb)
                ks = kp_ref[src, :]
                vs = vp_ref[src, :]
                if d == 0:
                    z = q * ks
                else:
                    keep = (row + d < CHUNK) if reverse else (row >= d)
                    dec = jnp.exp2(jnp.where(keep, b - b_ref[src, :], -jnp.inf))
                    z = q * ks * dec
                acc = acc + _dot(z.astype(BF16), head_ones) * vs
        acc_ref[rows, :] = acc

    def inter(blk, carry):
        per_dir = ([], [])
        for c in range(rb // CHUNK):
            n = blk * (rb // CHUNK) + c
            for d, (r0, kp_ref, b_ref, edge) in enumerate((
                    (n * CHUNK, kfp_ref, bf_ref, CHUNK - 1),
                    ((n_chunks - 1 - n) * CHUNK, kbp_ref, bb_ref, 0))):
                r0 = pl.multiple_of(r0, CHUNK)
                rows = pl.ds(r0, CHUNK)
                prow = pl.ds(r0 + CHUNK, CHUNK)
                b = b_ref[prow, :]
                b_edge = b_ref[pl.ds(r0 + CHUNK + edge, 1), :]
                qd = (q_ref[rows, :] * jnp.exp2(b)).astype(BF16)
                kd = kp_ref[prow, :] * jnp.exp2(b_edge - b)
                upd = _tn_dot(vp_ref[prow, :].astype(BF16), kd.astype(BF16))
                per_dir[d].append((rows, qd, jnp.exp2(b_edge), jnp.where(same_head, upd, 0.0)))
        states = list(carry)
        for d in range(2):
            for rows, qd, decay, upd in per_dir[d]:
                cross_ref[rows, :] = cross_ref[rows, :] + _nt_dot(qd, states[d].astype(BF16))
                states[d] = states[d] * decay + upd
        return tuple(states)

    cross_ref[...] = jnp.zeros(cross_ref.shape, F32)

    def fused(blk, carry):
        intra(blk)
        return inter(blk, carry)

    z = jnp.zeros((w, w), F32)
    lax.fori_loop(0, seq // rb, fused, (z, z))

    def finish(blk, carry):
        rows = pl.ds(pl.multiple_of(blk * rb, rb), rb)
        o = acc_ref[rows, :] + cross_ref[rows, :]
        lane = lax.broadcasted_iota(I32, (rb, w), 1)
        lo = lane < DH
        sq = o * o
        ms_lo = jnp.sum(jnp.where(lo, sq, 0.0), axis=-1, keepdims=True)
        ms_hi = jnp.sum(jnp.where(lo, 0.0, sq), axis=-1, keepdims=True)
        ms = jnp.where(lo, ms_lo, ms_hi) * (1.0 / DH)
        o = o * lax.rsqrt(ms + NORM_EPS) * nw_ref[...] * g_ref[rows, :]
        o_ref[rows, :] = o.astype(o_ref.dtype)
        return carry

    lax.fori_loop(0, seq // rb, finish, 0)


def _hgrn2(uh, nw, batch, seq):
    w = 2 * DH
    npair = D_HGRN // w
    kern = functools.partial(_hgrn2_kernel, seq=seq, rb=256)

    def sec(s):
        return pl.BlockSpec((seq, w), lambda b, p, s=s: (b, s * npair + p))

    return pl.pallas_call(
        kern,
        grid=(batch, npair),
        in_specs=[sec(0), sec(1), sec(2), sec(3), sec(4), sec(5), sec(6),
                  pl.BlockSpec((1, w), lambda b, p: (0, p))],
        out_specs=pl.BlockSpec((seq, w), lambda b, p: (b, p)),
        out_shape=jax.ShapeDtypeStruct((batch * seq, D_HGRN), BF16),
        scratch_shapes=[pltpu.VMEM((seq + 2 * CHUNK, w), F32)] * 5
        + [pltpu.VMEM((seq, w), F32)] * 2,
        compiler_params=pltpu.CompilerParams(
            dimension_semantics=("arbitrary", "arbitrary"), vmem_limit_bytes=VMEM_LIMIT),
        name="hgrn2",
    )(uh, uh, uh, uh, uh, uh, uh, nw)


def _rows_to_block(rows, tm):
    r = lax.broadcasted_iota(I32, (8, tm), 0)
    out = jnp.zeros((8, tm), rows[0].dtype)
    for k, row in enumerate(rows):
        out = jnp.where(r == k, row, out)
    return out


def _pack_bf16_pairs(x):
    m = x.shape[1] // 2
    u = lax.bitcast_convert_type(x, U32)
    r = u + jnp.uint32(0x7FFF) + ((u >> 16) & jnp.uint32(1))
    return (r[:, 0:m] >> 16) | (r[:, m:2 * m] & jnp.uint32(0xFFFF0000))


def _unpack_bf16_pairs(w):
    lo = lax.bitcast_convert_type(w << 16, F32).astype(BF16)
    hi = lax.bitcast_convert_type(w & jnp.uint32(0xFFFF0000), F32).astype(BF16)
    return jnp.concatenate([lo, hi], axis=1)


def _post_mixer_kernel(x_ref, od_ref, oh_ref, om_ref, wo_ref, g_ref, b_ref, rwt_ref,
                       rbt_ref, h_ref, hp_ref, eidx_ref, gate_ref, rank_ref, cnt_ref,
                       carry_ref, *, tm, alpha):
    i = pl.program_id(0)

    @pl.when(i == 0)
    def _():
        carry_ref[...] = jnp.zeros_like(carry_ref)

    mix = (_dot(od_ref[...], wo_ref[0:D_DIFF, :])
           + _dot(oh_ref[...], wo_ref[D_DIFF:D_DIFF + D_HGRN, :])
           + _dot(om_ref[...], wo_ref[D_DIFF + D_HGRN:, :]))
    h = _layer_norm(alpha * x_ref[...] + mix, g_ref[...], b_ref[...])
    h_ref[...] = h

    h_hi = h.astype(BF16)
    hp_ref[...] = _pack_bf16_pairs(h)
    h_lo = (h - h_hi.astype(F32)).astype(BF16)
    rwt = rwt_ref[...]
    rwt_hi = rwt.astype(BF16)
    rwt_lo = (rwt - rwt_hi.astype(F32)).astype(BF16)
    logits = (_nt_dot(rwt_hi, h_hi) + _nt_dot(rwt_hi, h_lo) + _nt_dot(rwt_lo, h_hi)
              + rbt_ref[...])

    erow = lax.broadcasted_iota(I32, (N_EXPERTS, tm), 0).astype(F32)
    work = logits
    sels, vals, idxs = [], [], []
    for _ in range(TOP_K):
        m = jnp.max(work, axis=0, keepdims=True)
        idx = jnp.min(jnp.where(work == m, erow, float(N_EXPERTS)), axis=0, keepdims=True)
        sel = erow == idx
        work = jnp.where(sel, -jnp.inf, work)
        sels.append(sel)
        vals.append(m)
        idxs.append(idx)
    es = [jnp.exp(v - vals[0]) for v in vals]
    den = es[0] + es[1] + es[2] + es[3]

    chosen = jnp.where(sels[0] | sels[1] | sels[2] | sels[3], 1.0, 0.0)
    ti = lax.broadcasted_iota(I32, (tm, tm), 0)
    tj = lax.broadcasted_iota(I32, (tm, tm), 1)
    earlier = jnp.where(ti < tj, 1.0, 0.0).astype(BF16)
    prefix = _dot(chosen.astype(BF16), earlier) + carry_ref[:, 0:1]

    ranks = [jnp.sum(jnp.where(sels[k], prefix, 0.0), axis=0, keepdims=True)
             for k in range(TOP_K)]
    eidx_ref[...] = _rows_to_block(idxs, tm).astype(I32)
    gate_ref[...] = _rows_to_block([e / den for e in es], tm)
    rank_ref[...] = _rows_to_block(ranks, tm).astype(I32)

    total = carry_ref[:, 0:1] + jnp.sum(chosen, axis=1, keepdims=True)
    carry_ref[...] = jnp.broadcast_to(total, carry_ref.shape)
    cnt_ref[...] = carry_ref[...]


def _post_mixer(x2d, od, oh, om, wo_bf, g, b, rwt, rbt, alpha):
    t = x2d.shape[0]
    tm = 512
    kern = functools.partial(_post_mixer_kernel, tm=tm, alpha=alpha)
    full = lambda i: (0, 0)
    tok = lambda i: (i, 0)
    per_tok = lambda i: (0, i)
    return pl.pallas_call(
        kern,
        grid=(t // tm,),
        in_specs=[
            pl.BlockSpec((tm, D_MODEL), tok),
            pl.BlockSpec((tm, D_DIFF), tok),
            pl.BlockSpec((tm, D_HGRN), tok),
            pl.BlockSpec((tm, D_XMEM), tok),
            pl.BlockSpec((D_MODEL, D_MODEL), full),
            pl.BlockSpec((1, D_MODEL), full),
            pl.BlockSpec((1, D_MODEL), full),
            pl.BlockSpec((N_EXPERTS, D_MODEL), full),
            pl.BlockSpec((N_EXPERTS, 1), full),
        ],
        out_specs=[
            pl.BlockSpec((tm, D_MODEL), tok),
            pl.BlockSpec((tm, D_MODEL // 2), tok),
            pl.BlockSpec((8, tm), per_tok),
            pl.BlockSpec((8, tm), per_tok),
            pl.BlockSpec((8, tm), per_tok),
            pl.BlockSpec((N_EXPERTS, LANES), full),
        ],
        out_shape=[
            jax.ShapeDtypeStruct((t, D_MODEL), F32),
            jax.ShapeDtypeStruct((t, D_MODEL // 2), U32),
            jax.ShapeDtypeStruct((8, t), I32),
            jax.ShapeDtypeStruct((8, t), F32),
            jax.ShapeDtypeStruct((8, t), I32),
            jax.ShapeDtypeStruct((N_EXPERTS, LANES), F32),
        ],
        scratch_shapes=[pltpu.VMEM((N_EXPERTS, LANES), F32)],
        compiler_params=pltpu.CompilerParams(
            dimension_semantics=("arbitrary",), vmem_limit_bytes=VMEM_LIMIT),
        name="post_mixer",
    )(x2d, od, oh, om, wo_bf, g, b, rwt, rbt)


def _route_meta_kernel(eidx_ref, rank_ref, cnt_ref, dest_ref, be_ref, *, tm, nb_lanes):
    cnt = cnt_ref[...]
    blocks = jnp.floor((cnt + (MOE_BLOCK - 1)) * (1.0 / MOE_BLOCK))
    ei = lax.broadcasted_iota(I32, (N_EXPERTS, N_EXPERTS), 0)
    ej = lax.broadcasted_iota(I32, (N_EXPERTS, N_EXPERTS), 1)
    below = jnp.where(ej < ei, 1.0, 0.0).astype(BF16)
    pstart = _dot(below, blocks.astype(BF16))[:, 0:1]
    pend = pstart + blocks[:, 0:1]

    erow = lax.broadcasted_iota(I32, (N_EXPERTS, tm), 0).astype(F32)
    eidx = eidx_ref[...].astype(F32)
    starts = [jnp.sum(jnp.where(erow == eidx[k:k + 1, :], pstart, 0.0), axis=0, keepdims=True)
              for k in range(TOP_K)]
    dest_ref[...] = ((_rows_to_block(starts, tm) * float(MOE_BLOCK)).astype(I32)
                     + rank_ref[...])

    bi = lax.broadcasted_iota(I32, (N_EXPERTS, nb_lanes), 1).astype(F32)
    done = jnp.where(pend <= bi, 1.0, 0.0)
    be = jnp.minimum(jnp.sum(done, axis=0, keepdims=True), float(N_EXPERTS - 1))
    used = jnp.broadcast_to(pend[N_EXPERTS - 1:N_EXPERTS, :], (1, nb_lanes))
    be_ref[...] = _rows_to_block([be, used], nb_lanes).astype(I32)


def _route_meta(eidx, rank, cnt, n_blocks):
    t = eidx.shape[1]
    tm = 2048
    nb_lanes = -(-n_blocks // LANES) * LANES
    kern = functools.partial(_route_meta_kernel, tm=tm, nb_lanes=nb_lanes)
    return pl.pallas_call(
        kern,
        grid=(t // tm,),
        in_specs=[
            pl.BlockSpec((8, tm), lambda i: (0, i)),
            pl.BlockSpec((8, tm), lambda i: (0, i)),
            pl.BlockSpec((N_EXPERTS, LANES), lambda i: (0, 0)),
        ],
        out_specs=[
            pl.BlockSpec((8, tm), lambda i: (0, i)),
            pl.BlockSpec((8, nb_lanes), lambda i: (0, 0)),
        ],
        out_shape=[
            jax.ShapeDtypeStruct((8, t), I32),
            jax.ShapeDtypeStruct((8, nb_lanes), I32),
        ],
        compiler_params=pltpu.CompilerParams(
            dimension_semantics=("arbitrary",), vmem_limit_bytes=VMEM_LIMIT),
        name="route_meta",
    )(eidx, rank, cnt)


SC_WINDOW = LANES
N_QUARTERS = 4
QUARTER = D_MODEL // N_QUARTERS
N_PACKED = D_MODEL // 2 // QUARTER
COMBINE_PARTS = 1


def _sc_mesh():
    return plsc.VectorSubcoreMesh(core_axis_name="c", subcore_axis_name="s")


def _dispatch(dest_km, hp, n_rows):
    t = hp.shape[0]

    @functools.partial(
        pl.kernel, out_type=[jax.ShapeDtypeStruct((n_rows, QUARTER), hp.dtype)] * N_PACKED,
        mesh=_sc_mesh(), scratch_types=[])
    def scatter_rows(h_hbm, d_hbm, *xs_hbm):
        for q in range(N_PACKED):
            def body(x_vmem, i_vmem, q=q):
                for k in range(TOP_K):
                    pltpu.sync_copy(x_vmem, xs_hbm[q].at[i_vmem.at[k]])

            pltpu.emit_pipeline(
                body,
                grid=(t // SC_WINDOW,),
                in_specs=[pl.BlockSpec((SC_WINDOW, QUARTER), lambda i, q=q: (i, q)),
                          pl.BlockSpec((TOP_K, SC_WINDOW), lambda i: (0, i))],
                out_specs=[],
                core_axis_name=("c", "s"),
                dimension_semantics=(pltpu.PARALLEL,),
            )(h_hbm, d_hbm)

    return scatter_rows(hp, dest_km)


def _gather_back(dest_km, ys):
    t = dest_km.shape[1]
    n = len(ys)

    @functools.partial(
        pl.kernel,
        out_type=[jax.ShapeDtypeStruct((t, TOP_K * QUARTER), ys[0].dtype)] * n,
        mesh=_sc_mesh(), scratch_types=[])
    def gather_rows(d_hbm, *refs):
        ys_hbm, out_hbm = refs[:n], refs[n:]
        for q in range(n):
            for k in range(TOP_K):
                def body(i_vmem, o_vmem, q=q, k=k):
                    pltpu.sync_copy(ys_hbm[q].at[i_vmem.at[k]], o_vmem)

                pltpu.emit_pipeline(
                    body,
                    grid=(t // SC_WINDOW,),
                    in_specs=[pl.BlockSpec((TOP_K, SC_WINDOW), lambda i: (0, i))],
                    out_specs=[pl.BlockSpec((SC_WINDOW, QUARTER), lambda i, k=k: (i, k))],
                    core_axis_name=("c", "s"),
                    dimension_semantics=(pltpu.PARALLEL,),
                )(d_hbm, out_hbm[q])

    return gather_rows(dest_km, *ys)


def _experts_kernel(be_ref, used_ref, *refs):
    xs_refs = refs[:N_PACKED]
    wgu_hbm, wdn_hbm, bg_ref, bl_ref, bd_ref, perm_ref = refs[N_PACKED:N_PACKED + 6]
    ys_refs = refs[N_PACKED + 6:2 * N_PACKED + 6]
    wgu_st, wdn_st, wg_ref, wl_ref, wd_ref, sem = refs[2 * N_PACKED + 6:]
    i = pl.program_id(0)
    used = used_ref[0]
    e = be_ref[i]
    prev = be_ref[jnp.maximum(i - 1, 0)]
    fresh = jnp.logical_and(i < used, jnp.logical_or(i == 0, e != prev))

    def weight_copies(ex):
        return (pltpu.make_async_copy(wgu_hbm.at[ex], wgu_st, sem.at[0]),
                pltpu.make_async_copy(wdn_hbm.at[ex], wdn_st, sem.at[1]))

    @pl.when(jnp.logical_and(i == 0, used > 0))
    def _():
        for cp in weight_copies(e):
            cp.start()

    @pl.when(fresh)
    def _():
        for cp in weight_copies(e):
            cp.wait()
        perm = perm_ref[...]
        half = LANES
        for c in range(2 * D_EXPERT // (2 * half)):
            slab = wgu_st[:, c * 2 * half:(c + 1) * 2 * half].astype(BF16)
            sep = _dot(slab, perm)
            wg_ref[:, c * half:(c + 1) * half] = sep[:, 0:half].astype(BF16)
            wl_ref[:, c * half:(c + 1) * half] = sep[:, half:2 * half].astype(BF16)
        wd_ref[...] = wdn_st[...].astype(BF16)

        def same_expert(j):
            return jnp.logical_and(j < used, be_ref[jnp.minimum(j, used - 1)] == e)

        nxt = lax.while_loop(same_expert, lambda j: j + 1, i + 1)

        @pl.when(nxt < used)
        def _():
            for cp in weight_copies(be_ref[nxt]):
                cp.start()

    @pl.when(i < used_ref[0])
    def _():
        xb = _unpack_bf16_pairs(jnp.concatenate([r[...] for r in xs_refs], axis=1))
        glu = jnp.minimum(_dot(xb, wg_ref[...]) + bg_ref[0], SWIGLU_LIMIT)
        lin = jnp.clip(_dot(xb, wl_ref[...]) + bl_ref[0], -SWIGLU_LIMIT, SWIGLU_LIMIT)
        act = glu * _sigmoid(SWIGLU_ALPHA * glu) * (lin + 1.0)
        y = _pack_bf16_pairs(_dot(act.astype(BF16), wd_ref[...]) + bd_ref[0])
        for q, ys_ref in enumerate(ys_refs):
            ys_ref[...] = y[:, q * QUARTER:(q + 1) * QUARTER]

    @pl.when(i >= used_ref[0])
    def _():
        for ys_ref in ys_refs:
            ys_ref[...] = jnp.zeros_like(ys_ref)


def _experts(block_e, used, xs, w_gu, w_dn, b_g, b_l, b_d, perm):
    n_rows = xs[0].shape[0]
    n_blocks = n_rows // MOE_BLOCK
    wmap = lambda i, be, u: (be[i], 0, 0)
    rows = pl.BlockSpec((MOE_BLOCK, QUARTER), lambda i, be, u: (i, 0))
    return pl.pallas_call(
        _experts_kernel,
        grid_spec=pltpu.PrefetchScalarGridSpec(
            num_scalar_prefetch=2,
            grid=(n_blocks,),
            in_specs=[rows] * N_PACKED + [
                pl.BlockSpec(memory_space=pl.ANY),
                pl.BlockSpec(memory_space=pl.ANY),
                pl.BlockSpec((1, 1, D_EXPERT), wmap),
                pl.BlockSpec((1, 1, D_EXPERT), wmap),
                pl.BlockSpec((1, 1, D_MODEL), wmap),
                pl.BlockSpec((2 * LANES, 2 * LANES), lambda i, be, u: (0, 0)),
            ],
            out_specs=[rows] * N_PACKED,
            scratch_shapes=[
                pltpu.VMEM((D_MODEL, 2 * D_EXPERT), F32),
                pltpu.VMEM((D_EXPERT, D_MODEL), F32),
                pltpu.VMEM((D_MODEL, D_EXPERT), BF16),
                pltpu.VMEM((D_MODEL, D_EXPERT), BF16),
                pltpu.VMEM((D_EXPERT, D_MODEL), BF16),
                pltpu.SemaphoreType.DMA((2,)),
            ],
        ),
        out_shape=[jax.ShapeDtypeStruct((n_rows, QUARTER), U32)] * N_PACKED,
        compiler_params=pltpu.CompilerParams(
            dimension_semantics=("arbitrary",), vmem_limit_bytes=VMEM_LIMIT),
        name="experts",
    )(block_e, used, *xs, w_gu, w_dn, b_g, b_l, b_d, perm)


def _combine_kernel(h_ref, gate_ref, g_ref, b_ref, *refs, tm, alpha):
    yg_refs, o_ref = refs[:N_PACKED], refs[N_PACKED]
    gate = jnp.transpose(gate_ref[...])
    gks = [gate[:, k:k + 1] for k in range(TOP_K)]
    lows, highs = [], []
    for yg_ref in yg_refs:
        lo = jnp.zeros((tm, QUARTER), F32)
        hi = jnp.zeros((tm, QUARTER), F32)
        for k in range(TOP_K):
            w = yg_ref[:, k * QUARTER:(k + 1) * QUARTER]
            lo = lo + gks[k] * lax.bitcast_convert_type(w << 16, F32)
            hi = hi + gks[k] * lax.bitcast_convert_type(w & jnp.uint32(0xFFFF0000), F32)
        lows.append(lo)
        highs.append(hi)
    ffn = jnp.concatenate(lows + highs, axis=1)
    o_ref[...] = _layer_norm(alpha * h_ref[...] + ffn, g_ref[...], b_ref[...])


def _combine(h, gate, g, b, yg, alpha, part, n_parts, prev_out):
    t = h.shape[0]
    tm = 256
    tiles = t // n_parts // tm
    kern = functools.partial(_combine_kernel, tm=tm, alpha=alpha)
    tok = lambda i: (part * tiles + i, 0)
    in_specs = [
        pl.BlockSpec((tm, D_MODEL), tok),
        pl.BlockSpec((8, tm), lambda i: (0, part * tiles + i)),
        pl.BlockSpec((1, D_MODEL), lambda i: (0, 0)),
        pl.BlockSpec((1, D_MODEL), lambda i: (0, 0)),
    ] + [pl.BlockSpec((tm, TOP_K * QUARTER), lambda i: (i, 0))] * N_PACKED
    args = [h, gate, g, b, *yg]
    aliases = {}
    if prev_out is not None:
        n_in = len(args)

        def kern(*refs, body=kern):
            body(*refs[:n_in], *refs[n_in + 1:])
        in_specs.append(pl.BlockSpec(memory_space=pl.ANY))
        aliases = {n_in: 0}
        args.append(prev_out)
    return pl.pallas_call(
        kern,
        grid=(tiles,),
        in_specs=in_specs,
        out_specs=pl.BlockSpec((tm, D_MODEL), tok),
        out_shape=jax.ShapeDtypeStruct((t, D_MODEL), F32),
        input_output_aliases=aliases,
        compiler_params=pltpu.CompilerParams(
            dimension_semantics=("arbitrary",), vmem_limit_bytes=VMEM_LIMIT),
        name="combine",
    )(*args)


def _column_split_permutation():
    n = 2 * LANES
    p = np.zeros((n, n), np.float32)
    j = np.arange(n)
    p[j, j // 2 + LANES * (j % 2)] = 1.0
    return jnp.asarray(p, dtype=BF16)


def kernel(x, mem, w_in, lam_q1, lam_k1, lam_q2, lam_k2, diff_norm_w, hgrn_lb_fwd,
           hgrn_lb_bwd, hgrn_norm_w, w_mem_kv, w_o, ln1_g, ln1_b, router_w, router_b,
           w_gate_up, b_gate_up, w_down, b_down, ln2_g, ln2_b):
    batch, seq, d = x.shape
    mlen = mem.shape[1]
    assert d == D_MODEL and w_in.shape == (N_LAYERS, D_MODEL, D_IN_PROJ)
    assert hgrn_lb_fwd.shape == (N_LAYERS + 1, D_HGRN)
    t = batch * seq
    alpha = (2.0 * N_LAYERS) ** 0.25
    lam_init = 0.8 - 0.6 * math.exp(-0.3 * 0)

    x2d = x.reshape(t, D_MODEL)
    ua, uh, um = _in_proj(x2d, w_in[0].astype(BF16), hgrn_lb_fwd, hgrn_lb_bwd)
    o_diff = _diff_attn(ua, diff_norm_w, lam_q1, lam_k1, lam_q2, lam_k2,
                        batch, seq, lam_init)
    o_mem = _mem_attn(um, mem.reshape(batch * mlen, D_MODEL), w_mem_kv[0].astype(BF16),
                      batch, seq, mlen)
    o_hgrn = _hgrn2(uh, hgrn_norm_w, batch, seq)

    h1, h1b, eidx, gate, rank, cnt = _post_mixer(
        x2d, o_diff, o_hgrn, o_mem, w_o[0].astype(BF16), ln1_g, ln1_b, router_w[0].T,
        router_b.reshape(N_EXPERTS, 1), alpha)

    n_blocks = -(-(t * TOP_K) // MOE_BLOCK) + N_EXPERTS
    dest, meta = _route_meta(eidx, rank, cnt, n_blocks)
    dest_km = dest[0:TOP_K]
    block_e = meta[0, 0:n_blocks]
    used = meta[1, 0:1]

    xs = _dispatch(dest_km, h1b, n_blocks * MOE_BLOCK)
    b_g = b_gate_up[0][:, 0::2].reshape(N_EXPERTS, 1, D_EXPERT)
    b_l = b_gate_up[0][:, 1::2].reshape(N_EXPERTS, 1, D_EXPERT)
    ys = _experts(block_e, used, xs, w_gate_up[0], w_down[0], b_g, b_l,
                  b_down[0].reshape(N_EXPERTS, 1, D_MODEL), _column_split_permutation())
    out = None
    for part in range(COMBINE_PARTS):
        lo = part * (t // COMBINE_PARTS)
        yg = _gather_back(dest_km[:, lo:lo + t // COMBINE_PARTS], ys)
        out = _combine(h1, gate, ln2_g, ln2_b, yg, alpha, part, COMBINE_PARTS, out)
    return out.reshape(batch, seq, D_MODEL)
```

```python
import functools
import math

import jax
import jax.numpy as jnp
import numpy as np
from jax import lax
from jax.experimental import pallas as pl
from jax.experimental.pallas import tpu as pltpu
from jax.experimental.pallas import tpu_sc as plsc

F32 = jnp.float32
BF16 = jnp.bfloat16
I32 = jnp.int32
U32 = jnp.uint32

D_MODEL = 1024
N_LAYERS = 1
D_DIFF = 512
D_HGRN = 256
D_XMEM = 256
N_HEADS = 4
DH = 64
DH_SHIFT = 6
D_IN_PROJ = 3 * D_DIFF + 5 * D_HGRN + D_XMEM
CHUNK = 16
N_EXPERTS = 32
TOP_K = 4
D_EXPERT = 1024
MOE_BLOCK = 256
SWIGLU_ALPHA = 1.702
SWIGLU_LIMIT = 7.0
NORM_EPS = 1e-5
LOG2E = math.log2(math.e)
LANES = 128
VMEM_LIMIT = 56 * 1024 * 1024


def _nt_dot(a, b):
    return lax.dot_general(a, b, (((1,), (1,)), ((), ())), preferred_element_type=F32)


def _tn_dot(a, b):
    return lax.dot_general(a, b, (((0,), (0,)), ((), ())), preferred_element_type=F32)


def _dot(a, b):
    return jnp.dot(a, b, preferred_element_type=F32)


def _sigmoid(x):
    return 1.0 / (1.0 + jnp.exp(-x))


def _layer_norm(y, g, b):
    mu = jnp.mean(y, axis=-1, keepdims=True)
    yc = y - mu
    var = jnp.mean(yc * yc, axis=-1, keepdims=True)
    return yc * lax.rsqrt(var + NORM_EPS) * g + b


def _in_proj_kernel(x_ref, w_ref, lbf_ref, lbb_ref, ua_ref, uh_ref, um_ref):
    xb = x_ref[...].astype(BF16)

    def proj(c0, width):
        return _dot(xb, w_ref[:, c0:c0 + width])

    ua_ref[:, 0:D_DIFF] = (proj(0, D_DIFF) * (LOG2E / math.sqrt(DH))).astype(BF16)
    ua_ref[:, D_DIFF:2 * D_DIFF] = proj(D_DIFF, D_DIFF).astype(BF16)
    ua_ref[:, 2 * D_DIFF:3 * D_DIFF] = proj(2 * D_DIFF, D_DIFF).astype(BF16)

    def lower_bound(lb_ref):
        a = lb_ref[0:1, :]
        b = lb_ref[1:2, :]
        m = jnp.maximum(a, b)
        ea = jnp.exp(a - m)
        eb = jnp.exp(b - m)
        return ea / (ea + eb)

    base = 3 * D_DIFF
    hq = proj(base, D_HGRN)
    uh_ref[:, 0:D_HGRN] = hq * _sigmoid(hq)
    for d, lb_ref in enumerate((lbf_ref, lbb_ref)):
        lb = lower_bound(lb_ref)
        f = lb + (1.0 - lb) * _sigmoid(proj(base + (1 + d) * D_HGRN, D_HGRN))
        uh_ref[:, (1 + 2 * d) * D_HGRN:(2 + 2 * d) * D_HGRN] = 1.0 - f
        uh_ref[:, (2 + 2 * d) * D_HGRN:(3 + 2 * d) * D_HGRN] = jnp.log(f)
    uh_ref[:, 5 * D_HGRN:6 * D_HGRN] = proj(base + 3 * D_HGRN, D_HGRN)
    uh_ref[:, 6 * D_HGRN:7 * D_HGRN] = _sigmoid(proj(base + 4 * D_HGRN, D_HGRN))
    um_ref[...] = proj(base + 5 * D_HGRN, D_XMEM).astype(BF16)


def _in_proj(x2d, w_in_bf, lb_f, lb_b):
    t = x2d.shape[0]
    tm = 1024
    return pl.pallas_call(
        _in_proj_kernel,
        grid=(t // tm,),
        in_specs=[
            pl.BlockSpec((tm, D_MODEL), lambda i: (i, 0)),
            pl.BlockSpec((D_MODEL, D_IN_PROJ), lambda i: (0, 0)),
            pl.BlockSpec((N_LAYERS + 1, D_HGRN), lambda i: (0, 0)),
            pl.BlockSpec((N_LAYERS + 1, D_HGRN), lambda i: (0, 0)),
        ],
        out_specs=[
            pl.BlockSpec((tm, 3 * D_DIFF), lambda i: (i, 0)),
            pl.BlockSpec((tm, 7 * D_HGRN), lambda i: (i, 0)),
            pl.BlockSpec((tm, D_XMEM), lambda i: (i, 0)),
        ],
        out_shape=[
            jax.ShapeDtypeStruct((t, 3 * D_DIFF), BF16),
            jax.ShapeDtypeStruct((t, 7 * D_HGRN), F32),
            jax.ShapeDtypeStruct((t, D_XMEM), BF16),
        ],
        compiler_params=pltpu.CompilerParams(
            dimension_semantics=("arbitrary",), vmem_limit_bytes=VMEM_LIMIT),
        name="in_proj",
    )(x2d, w_in_bf, lb_f, lb_b)


N_POS_FEATURES = 12


def _bf16_pieces(x):
    x = np.asarray(x, np.float32)
    out = []
    for _ in range(3):
        p = x.astype(BF16).astype(np.float32)
        out.append(p)
        x = x - p
    return np.stack(out, axis=-1)


def _alibi_tables(seq, t):
    slopes = (2.0 ** (-8.0 * np.arange(1, N_HEADS + 1) / N_HEADS)).astype(np.float32)
    c3 = _bf16_pieces(slopes * np.float32(LOG2E))
    pos = np.arange(seq)
    hi = (pos >> DH_SHIFT).astype(np.float32)
    lo = (pos & (DH - 1)).astype(np.float32)
    qf = np.zeros((N_HEADS, 2, seq, 2 * DH), np.float32)
    kf = np.zeros((N_HEADS, 2, seq, 2 * DH), np.float32)
    for m, base in enumerate((DH, 0)):
        for j in range(3):
            qf[:, m, :, base + j] = hi
            qf[:, m, :, base + 3 + j] = lo
            qf[:, m, :, base + 6 + j] = 64.0 * c3[:, j, None]
            qf[:, m, :, base + 9 + j] = c3[:, j, None]
            kf[:, m, :, base + j] = -64.0 * c3[:, j, None]
            kf[:, m, :, base + 3 + j] = -c3[:, j, None]
            kf[:, m, :, base + 6 + j] = hi
            kf[:, m, :, base + 9 + j] = lo
    c = c3.sum(axis=-1)
    kk = np.arange(t)[:, None]
    qq = np.arange(t)[None, :]
    corr = 2.0 * c[:, None, None] * np.minimum(qq - kk, 0).astype(np.float32)
    return (jnp.asarray(qf, dtype=BF16), jnp.asarray(kf, dtype=BF16),
            jnp.asarray(corr, dtype=F32))


def _diff_attn_kernel(q_ref, k_ref, v_ref, qf_ref, kf_ref, corr_ref, nw_ref, lq1_ref,
                      lk1_ref, lq2_ref, lk2_ref, o_ref, ka1_ref, ka2_ref, vt_ref, acc_ref,
                      m_ref, l_ref, *, t, qs, ks, qpt, seq, lam_init):
    i = pl.program_id(2)
    nk = seq // t
    lane = lax.broadcasted_iota(I32, (t, 2 * DH), 1)
    first_half = lane < DH

    @pl.when(i == 0)
    def _():
        def build(r, carry):
            rows = pl.ds(pl.multiple_of(r * t, t), t)
            kblk = k_ref[rows, :]
            ka1_ref[rows, :] = jnp.where(first_half, kblk, kf_ref[0, 0, rows, :])
            ka2_ref[rows, :] = jnp.where(first_half, kf_ref[0, 1, rows, :], kblk)
            vt_ref[r] = v_ref[rows, :].astype(F32).T.astype(BF16)
            return carry

        lax.fori_loop(0, nk, build, 0)

    lam = (jnp.exp(jnp.sum(lq1_ref[...] * lk1_ref[...], axis=-1, keepdims=True))
           - jnp.exp(jnp.sum(lq2_ref[...] * lk2_ref[...], axis=-1, keepdims=True))
           + lam_init)

    for qt in range(qpt):
        qi = i * qpt + qt
        qrange = slice(qt * t, (qt + 1) * t)
        q = q_ref[qrange, :]
        qf1 = qf_ref[0, 0, qrange, :]
        qf2 = qf_ref[0, 1, qrange, :]
        qa1_before = jnp.where(first_half, q, qf1)
        qa1_after = jnp.where(first_half, q, -qf1)
        qa2_before = jnp.where(first_half, qf2, q)
        qa2_after = jnp.where(first_half, -qf2, q)

        m_ref[qt] = jnp.full(m_ref.shape[1:], -jnp.inf, F32)
        l_ref[qt] = jnp.zeros(l_ref.shape[1:], F32)
        acc_ref[qt] = jnp.zeros(acc_ref.shape[1:], F32)

        def chunk(j, qa1, qa2, diagonal, qt=qt):
            k0 = pl.multiple_of(j * t, t)
            for mp, (qa, ka_ref) in enumerate(((qa1, ka1_ref), (qa2, ka2_ref))):
                for u in range(t // qs):
                    qrows = slice(u * qs, (u + 1) * qs)
                    cols = slice(mp * t + u * qs, mp * t + (u + 1) * qs)
                    for kb in range(t // ks):
                        krows = slice(kb * ks, (kb + 1) * ks)
                        s = _nt_dot(ka_ref[pl.ds(k0 + kb * ks, ks), :], qa[qrows, :])
                        if diagonal:
                            s = s + corr_ref[0, krows, qrows]
                        m_old = m_ref[qt, :, cols]
                        m_new = jnp.maximum(m_old, jnp.max(s, axis=0, keepdims=True))
                        p = jnp.exp2(s - m_new)
                        r = jnp.exp2(m_old - m_new)
                        l_ref[qt, :, cols] = (r * l_ref[qt, :, cols]
                                              + jnp.sum(p, axis=0, keepdims=True))
                        acc_ref[qt, mp, :, qrows] = (
                            r * acc_ref[qt, mp, :, qrows]
                            + _dot(vt_ref[j, :, krows], p.astype(BF16)))
                        m_ref[qt, :, cols] = m_new

        chunk(qi, qa1_before, qa2_before, True)
        for jj in range(nk - 1):
            j = jj + (jj >= qi).astype(I32)
            keys_first = j < qi
            chunk(j, jnp.where(keys_first, qa1_before, qa1_after),
                  jnp.where(keys_first, qa2_before, qa2_after), False)

        o = (acc_ref[qt, 0] / l_ref[qt, :, 0:t]
             - lam * (acc_ref[qt, 1] / l_ref[qt, :, t:2 * t]))
        o = o * lax.rsqrt(jnp.mean(o * o, axis=0, keepdims=True) + NORM_EPS)
        o_ref[qrange, :] = (o.T * nw_ref[...] * (1.0 - lam_init)).astype(o_ref.dtype)


def _diff_attn(ua, nw, lq1, lk1, lq2, lk2, batch, seq, lam_init):
    t = 512
    qpt = 2
    nq = seq // (t * qpt)
    qf, kf, corr = _alibi_tables(seq, t)
    kern = functools.partial(_diff_attn_kernel, t=t, qs=LANES, ks=256, qpt=qpt, seq=seq,
                             lam_init=lam_init)
    small = lambda b, h, i: (0, 0)
    return pl.pallas_call(
        kern,
        grid_spec=pltpu.PrefetchScalarGridSpec(
            num_scalar_prefetch=0,
            grid=(batch, N_HEADS, nq),
            in_specs=[
                pl.BlockSpec((qpt * t, 2 * DH), lambda b, h, i: (b * nq + i, h)),
                pl.BlockSpec((seq, 2 * DH), lambda b, h, i: (b, N_HEADS + h)),
                pl.BlockSpec((seq, 2 * DH), lambda b, h, i: (b, 2 * N_HEADS + h)),
                pl.BlockSpec((1, 2, qpt * t, 2 * DH), lambda b, h, i: (h, 0, i, 0)),
                pl.BlockSpec((1, 2, seq, 2 * DH), lambda b, h, i: (h, 0, 0, 0)),
                pl.BlockSpec((1, t, t), lambda b, h, i: (h, 0, 0)),
                pl.BlockSpec((1, 2 * DH), small),
                pl.BlockSpec((1, DH), small),
                pl.BlockSpec((1, DH), small),
                pl.BlockSpec((1, DH), small),
                pl.BlockSpec((1, DH), small),
            ],
            out_specs=pl.BlockSpec((qpt * t, 2 * DH), lambda b, h, i: (b * nq + i, h)),
            scratch_shapes=[
                pltpu.VMEM((seq, 2 * DH), BF16),
                pltpu.VMEM((seq, 2 * DH), BF16),
                pltpu.VMEM((seq // t, 2 * DH, t), BF16),
                pltpu.VMEM((qpt, 2, 2 * DH, t), F32),
                pltpu.VMEM((qpt, 1, 2 * t), F32),
                pltpu.VMEM((qpt, 1, 2 * t), F32),
            ],
        ),
        out_shape=jax.ShapeDtypeStruct((batch * seq, D_DIFF), BF16),
        compiler_params=pltpu.CompilerParams(
            dimension_semantics=("arbitrary", "arbitrary", "arbitrary"),
            vmem_limit_bytes=VMEM_LIMIT),
        name="diff_attn",
    )(ua, ua, ua, qf, kf, corr, nw, lq1, lk1, lq2, lk2)


def _mem_attn_kernel(q_ref, mem_ref, wkv_ref, o_ref, kv_ref, *, tq, mlen):
    @pl.when(pl.program_id(1) == 0)
    def _():
        kv_ref[...] = _dot(mem_ref[...].astype(BF16), wkv_ref[...]).astype(BF16)

    q = q_ref[...]
    mk = kv_ref[:, 0:D_XMEM]
    mv = kv_ref[:, D_XMEM:2 * D_XMEM]
    qhead = lax.broadcasted_iota(I32, (tq, D_XMEM), 1) >> DH_SHIFT
    vhead = lax.broadcasted_iota(I32, (mlen, D_XMEM), 1) >> DH_SHIFT
    acc = jnp.zeros((tq, D_XMEM), F32)
    for h in range(N_HEADS):
        qh = jnp.where(qhead == h, q, jnp.zeros_like(q))
        s = _nt_dot(qh, mk) * (1.0 / math.sqrt(DH))
        e = jnp.exp(s - jnp.max(s, axis=-1, keepdims=True))
        p = e / jnp.sum(e, axis=-1, keepdims=True)
        vh = jnp.where(vhead == h, mv, jnp.zeros_like(mv))
        acc = acc + _dot(p.astype(BF16), vh)
    o_ref[...] = acc.astype(o_ref.dtype)


def _mem_attn(um, mem2d, wkv_bf, batch, seq, mlen):
    tq = 1024
    nq = seq // tq
    kern = functools.partial(_mem_attn_kernel, tq=tq, mlen=mlen)
    return pl.pallas_call(
        kern,
        grid=(batch, nq),
        in_specs=[
            pl.BlockSpec((tq, D_XMEM), lambda b, i: (b * nq + i, 0)),
            pl.BlockSpec((mlen, D_MODEL), lambda b, i: (b, 0)),
            pl.BlockSpec((D_MODEL, 2 * D_XMEM), lambda b, i: (0, 0)),
        ],
        out_specs=pl.BlockSpec((tq, D_XMEM), lambda b, i: (b * nq + i, 0)),
        out_shape=jax.ShapeDtypeStruct((batch * seq, D_XMEM), BF16),
        scratch_shapes=[pltpu.VMEM((mlen, 2 * D_XMEM), BF16)],
        compiler_params=pltpu.CompilerParams(
            dimension_semantics=("arbitrary", "arbitrary"), vmem_limit_bytes=VMEM_LIMIT),
        name="mem_attn",
    )(um, mem2d, wkv_bf)


def _hgrn2_kernel(q_ref, kf_ref, lf_ref, kb_ref, lb_ref, v_ref, g_ref, nw_ref, o_ref,
                  kfp_ref, kbp_ref, vp_ref, bf_ref, bb_ref, acc_ref, cross_ref, *, seq, rb):
    n_chunks = seq // CHUNK
    w = 2 * DH
    row = lax.broadcasted_iota(I32, (rb, w), 0) & (CHUNK - 1)
    li = lax.broadcasted_iota(I32, (w, w), 0) >> DH_SHIFT
    lj = lax.broadcasted_iota(I32, (w, w), 1) >> DH_SHIFT
    same_head = li == lj
    head_ones = jnp.where(same_head, 1.0, 0.0).astype(BF16)

    def chunk_cumsum(x, reverse):
        for sh in (1, 2, 4, 8):
            if reverse:
                moved = pltpu.roll(x, rb - sh, 0)
                keep = row + sh < CHUNK
            else:
                moved = pltpu.roll(x, sh, 0)
                keep = row >= sh
            x = x + jnp.where(keep, moved, 0.0)
        return x

    halo = jnp.zeros((CHUNK, w), F32)
    for ref in (kfp_ref, kbp_ref, vp_ref, bf_ref, bb_ref):
        ref[0:CHUNK, :] = halo
        ref[seq + CHUNK:seq + 2 * CHUNK, :] = halo

    def prepare(blk, carry):
        r0 = pl.multiple_of(blk * rb, rb)
        rows = pl.ds(r0, rb)
        inner = pl.ds(r0 + CHUNK, rb)
        kfp_ref[inner, :] = kf_ref[rows, :]
        kbp_ref[inner, :] = kb_ref[rows, :]
        vp_ref[inner, :] = v_ref[rows, :]
        bf_ref[inner, :] = chunk_cumsum(lf_ref[rows, :] * LOG2E, False)
        bb_ref[inner, :] = chunk_cumsum(lb_ref[rows, :] * LOG2E, True)
        return carry

    lax.fori_loop(0, seq // rb, prepare, 0)

    def intra(blk):
        r0 = pl.multiple_of(blk * rb, rb)
        rows = pl.ds(r0, rb)
        q = q_ref[rows, :]
        acc = jnp.zeros((rb, w), F32)
        for reverse, kp_ref, b_ref in ((False, kfp_ref, bf_ref), (True, kbp_ref, bb_ref)):
            b = b_ref[pl.ds(r0 + CHUNK, rb), :]
            for d in range(CHUNK):
                src = pl.ds(r0 + CHUNK + (d if reverse else -d), rb)
                ks = kp_ref[src, :]
                vs = vp_ref[src, :]
                if d == 0:
                    z = q * ks
                else:
                    keep = (row + d < CHUNK) if reverse else (row >= d)
                    dec = jnp.exp2(jnp.where(keep, b - b_ref[src, :], -jnp.inf))
                    z = q * ks * dec
                acc = acc + _dot(z.astype(BF16), head_ones) * vs
        acc_ref[rows, :] = acc

    def inter(blk, carry):
        per_dir = ([], [])
        for c in range(rb // CHUNK):
            n = blk * (rb // CHUNK) + c
            for d, (r0, kp_ref, b_ref, edge) in enumerate((
                    (n * CHUNK, kfp_ref, bf_ref, CHUNK - 1),
                    ((n_chunks - 1 - n) * CHUNK, kbp_ref, bb_ref, 0))):
                r0 = pl.multiple_of(r0, CHUNK)
                rows = pl.ds(r0, CHUNK)
                prow = pl.ds(r0 + CHUNK, CHUNK)
                b = b_ref[prow, :]
                b_edge = b_ref[pl.ds(r0 + CHUNK + edge, 1), :]
                qd = (q_ref[rows, :] * jnp.exp2(b)).astype(BF16)
                kd = kp_ref[prow, :] * jnp.exp2(b_edge - b)
                upd = _tn_dot(vp_ref[prow, :].astype(BF16), kd.astype(BF16))
                per_dir[d].append((rows, qd, jnp.exp2(b_edge), jnp.where(same_head, upd, 0.0)))
        states = list(carry)
        for d in range(2):
            for rows, qd, decay, upd in per_dir[d]:
                cross_ref[rows, :] = cross_ref[rows, :] + _nt_dot(qd, states[d].astype(BF16))
                states[d] = states[d] * decay + upd
        return tuple(states)

    cross_ref[...] = jnp.zeros(cross_ref.shape, F32)

    def fused(blk, carry):
        intra(blk)
        return inter(blk, carry)

    z = jnp.zeros((w, w), F32)
    lax.fori_loop(0, seq // rb, fused, (z, z))

    def finish(blk, carry):
        rows = pl.ds(pl.multiple_of(blk * rb, rb), rb)
        o = acc_ref[rows, :] + cross_ref[rows, :]
        lane = lax.broadcasted_iota(I32, (rb, w), 1)
        lo = lane < DH
        sq = o * o
        ms_lo = jnp.sum(jnp.where(lo, sq, 0.0), axis=-1, keepdims=True)
        ms_hi = jnp.sum(jnp.where(lo, 0.0, sq), axis=-1, keepdims=True)
        ms = jnp.where(lo, ms_lo, ms_hi) * (1.0 / DH)
        o = o * lax.rsqrt(ms + NORM_EPS) * nw_ref[...] * g_ref[rows, :]
        o_ref[rows, :] = o.astype(o_ref.dtype)
        return carry

    lax.fori_loop(0, seq // rb, finish, 0)


def _hgrn2(uh, nw, batch, seq):
    w = 2 * DH
    npair = D_HGRN // w
    kern = functools.partial(_hgrn2_kernel, seq=seq, rb=256)

    def sec(s):
        return pl.BlockSpec((seq, w), lambda b, p, s=s: (b, s * npair + p))

    return pl.pallas_call(
        kern,
        grid=(batch, npair),
        in_specs=[sec(0), sec(1), sec(2), sec(3), sec(4), sec(5), sec(6),
                  pl.BlockSpec((1, w), lambda b, p: (0, p))],
        out_specs=pl.BlockSpec((seq, w), lambda b, p: (b, p)),
        out_shape=jax.ShapeDtypeStruct((batch * seq, D_HGRN), BF16),
        scratch_shapes=[pltpu.VMEM((seq + 2 * CHUNK, w), F32)] * 5
        + [pltpu.VMEM((seq, w), F32)] * 2,
        compiler_params=pltpu.CompilerParams(
            dimension_semantics=("arbitrary", "arbitrary"), vmem_limit_bytes=VMEM_LIMIT),
        name="hgrn2",
    )(uh, uh, uh, uh, uh, uh, uh, nw)


def _rows_to_block(rows, tm):
    r = lax.broadcasted_iota(I32, (8, tm), 0)
    out = jnp.zeros((8, tm), rows[0].dtype)
    for k, row in enumerate(rows):
        out = jnp.where(r == k, row, out)
    return out


def _pack_bf16_pairs(x):
    m = x.shape[1] // 2
    u = lax.bitcast_convert_type(x, U32)
    r = u + jnp.uint32(0x7FFF) + ((u >> 16) & jnp.uint32(1))
    return (r[:, 0:m] >> 16) | (r[:, m:2 * m] & jnp.uint32(0xFFFF0000))


def _unpack_bf16_pairs(w):
    lo = lax.bitcast_convert_type(w << 16, F32).astype(BF16)
    hi = lax.bitcast_convert_type(w & jnp.uint32(0xFFFF0000), F32).astype(BF16)
    return jnp.concatenate([lo, hi], axis=1)


def _post_mixer_kernel(x_ref, od_ref, oh_ref, om_ref, wo_ref, g_ref, b_ref, rwt_ref,
                       rbt_ref, h_ref, hp_ref, eidx_ref, gate_ref, rank_ref, cnt_ref,
                       carry_ref, *, tm, alpha):
    i = pl.program_id(0)

    @pl.when(i == 0)
    def _():
        carry_ref[...] = jnp.zeros_like(carry_ref)

    mix = (_dot(od_ref[...], wo_ref[0:D_DIFF, :])
           + _dot(oh_ref[...], wo_ref[D_DIFF:D_DIFF + D_HGRN, :])
           + _dot(om_ref[...], wo_ref[D_DIFF + D_HGRN:, :]))
    h = _layer_norm(alpha * x_ref[...] + mix, g_ref[...], b_ref[...])
    h_ref[...] = h

    h_hi = h.astype(BF16)
    hp_ref[...] = _pack_bf16_pairs(h)
    h_lo = (h - h_hi.astype(F32)).astype(BF16)
    rwt = rwt_ref[...]
    rwt_hi = rwt.astype(BF16)
    rwt_lo = (rwt - rwt_hi.astype(F32)).astype(BF16)
    logits = (_nt_dot(rwt_hi, h_hi) + _nt_dot(rwt_hi, h_lo) + _nt_dot(rwt_lo, h_hi)
              + rbt_ref[...])

    erow = lax.broadcasted_iota(I32, (N_EXPERTS, tm), 0).astype(F32)
    work = logits
    sels, vals, idxs = [], [], []
    for _ in range(TOP_K):
        m = jnp.max(work, axis=0, keepdims=True)
        idx = jnp.min(jnp.where(work == m, erow, float(N_EXPERTS)), axis=0, keepdims=True)
        sel = erow == idx
        work = jnp.where(sel, -jnp.inf, work)
        sels.append(sel)
        vals.append(m)
        idxs.append(idx)
    es = [jnp.exp(v - vals[0]) for v in vals]
    den = es[0] + es[1] + es[2] + es[3]

    chosen = jnp.where(sels[0] | sels[1] | sels[2] | sels[3], 1.0, 0.0)
    ti = lax.broadcasted_iota(I32, (tm, tm), 0)
    tj = lax.broadcasted_iota(I32, (tm, tm), 1)
    earlier = jnp.where(ti < tj, 1.0, 0.0).astype(BF16)
    prefix = _dot(chosen.astype(BF16), earlier) + carry_ref[:, 0:1]

    ranks = [jnp.sum(jnp.where(sels[k], prefix, 0.0), axis=0, keepdims=True)
             for k in range(TOP_K)]
    eidx_ref[...] = _rows_to_block(idxs, tm).astype(I32)
    gate_ref[...] = _rows_to_block([e / den for e in es], tm)
    rank_ref[...] = _rows_to_block(ranks, tm).astype(I32)

    total = carry_ref[:, 0:1] + jnp.sum(chosen, axis=1, keepdims=True)
    carry_ref[...] = jnp.broadcast_to(total, carry_ref.shape)
    cnt_ref[...] = carry_ref[...]


def _post_mixer(x2d, od, oh, om, wo_bf, g, b, rwt, rbt, alpha):
    t = x2d.shape[0]
    tm = 512
    kern = functools.partial(_post_mixer_kernel, tm=tm, alpha=alpha)
    full = lambda i: (0, 0)
    tok = lambda i: (i, 0)
    per_tok = lambda i: (0, i)
    return pl.pallas_call(
        kern,
        grid=(t // tm,),
        in_specs=[
            pl.BlockSpec((tm, D_MODEL), tok),
            pl.BlockSpec((tm, D_DIFF), tok),
            pl.BlockSpec((tm, D_HGRN), tok),
            pl.BlockSpec((tm, D_XMEM), tok),
            pl.BlockSpec((D_MODEL, D_MODEL), full),
            pl.BlockSpec((1, D_MODEL), full),
            pl.BlockSpec((1, D_MODEL), full),
            pl.BlockSpec((N_EXPERTS, D_MODEL), full),
            pl.BlockSpec((N_EXPERTS, 1), full),
        ],
        out_specs=[
            pl.BlockSpec((tm, D_MODEL), tok),
            pl.BlockSpec((tm, D_MODEL // 2), tok),
            pl.BlockSpec((8, tm), per_tok),
            pl.BlockSpec((8, tm), per_tok),
            pl.BlockSpec((8, tm), per_tok),
            pl.BlockSpec((N_EXPERTS, LANES), full),
        ],
        out_shape=[
            jax.ShapeDtypeStruct((t, D_MODEL), F32),
            jax.ShapeDtypeStruct((t, D_MODEL // 2), U32),
            jax.ShapeDtypeStruct((8, t), I32),
            jax.ShapeDtypeStruct((8, t), F32),
            jax.ShapeDtypeStruct((8, t), I32),
            jax.ShapeDtypeStruct((N_EXPERTS, LANES), F32),
        ],
        scratch_shapes=[pltpu.VMEM((N_EXPERTS, LANES), F32)],
        compiler_params=pltpu.CompilerParams(
            dimension_semantics=("arbitrary",), vmem_limit_bytes=VMEM_LIMIT),
        name="post_mixer",
    )(x2d, od, oh, om, wo_bf, g, b, rwt, rbt)


def _route_meta_kernel(eidx_ref, rank_ref, cnt_ref, dest_ref, be_ref, *, tm, nb_lanes):
    cnt = cnt_ref[...]
    blocks = jnp.floor((cnt + (MOE_BLOCK - 1)) * (1.0 / MOE_BLOCK))
    ei = lax.broadcasted_iota(I32, (N_EXPERTS, N_EXPERTS), 0)
    ej = lax.broadcasted_iota(I32, (N_EXPERTS, N_EXPERTS), 1)
    below = jnp.where(ej < ei, 1.0, 0.0).astype(BF16)
    pstart = _dot(below, blocks.astype(BF16))[:, 0:1]
    pend = pstart + blocks[:, 0:1]

    erow = lax.broadcasted_iota(I32, (N_EXPERTS, tm), 0).astype(F32)
    eidx = eidx_ref[...].astype(F32)
    starts = [jnp.sum(jnp.where(erow == eidx[k:k + 1, :], pstart, 0.0), axis=0, keepdims=True)
              for k in range(TOP_K)]
    dest_ref[...] = ((_rows_to_block(starts, tm) * float(MOE_BLOCK)).astype(I32)
                     + rank_ref[...])

    bi = lax.broadcasted_iota(I32, (N_EXPERTS, nb_lanes), 1).astype(F32)
    done = jnp.where(pend <= bi, 1.0, 0.0)
    be = jnp.minimum(jnp.sum(done, axis=0, keepdims=True), float(N_EXPERTS - 1))
    used = jnp.broadcast_to(pend[N_EXPERTS - 1:N_EXPERTS, :], (1, nb_lanes))
    be_ref[...] = _rows_to_block([be, used], nb_lanes).astype(I32)


def _route_meta(eidx, rank, cnt, n_blocks):
    t = eidx.shape[1]
    tm = 2048
    nb_lanes = -(-n_blocks // LANES) * LANES
    kern = functools.partial(_route_meta_kernel, tm=tm, nb_lanes=nb_lanes)
    return pl.pallas_call(
        kern,
        grid=(t // tm,),
        in_specs=[
            pl.BlockSpec((8, tm), lambda i: (0, i)),
            pl.BlockSpec((8, tm), lambda i: (0, i)),
            pl.BlockSpec((N_EXPERTS, LANES), lambda i: (0, 0)),
        ],
        out_specs=[
            pl.BlockSpec((8, tm), lambda i: (0, i)),
            pl.BlockSpec((8, nb_lanes), lambda i: (0, 0)),
        ],
        out_shape=[
            jax.ShapeDtypeStruct((8, t), I32),
            jax.ShapeDtypeStruct((8, nb_lanes), I32),
        ],
        compiler_params=pltpu.CompilerParams(
            dimension_semantics=("arbitrary",), vmem_limit_bytes=VMEM_LIMIT),
        name="route_meta",
    )(eidx, rank, cnt)


SC_WINDOW = LANES
N_QUARTERS = 4
QUARTER = D_MODEL // N_QUARTERS
N_PACKED = D_MODEL // 2 // QUARTER
COMBINE_PARTS = 1


def _sc_mesh():
    return plsc.VectorSubcoreMesh(core_axis_name="c", subcore_axis_name="s")


def _dispatch(dest_km, hp, n_rows):
    t = hp.shape[0]

    @functools.partial(
        pl.kernel, out_type=[jax.ShapeDtypeStruct((n_rows, QUARTER), hp.dtype)] * N_PACKED,
        mesh=_sc_mesh(), scratch_types=[])
    def scatter_rows(h_hbm, d_hbm, *xs_hbm):
        for q in range(N_PACKED):
            def body(x_vmem, i_vmem, q=q):
                for k in range(TOP_K):
                    pltpu.sync_copy(x_vmem, xs_hbm[q].at[i_vmem.at[k]])

            pltpu.emit_pipeline(
                body,
                grid=(t // SC_WINDOW,),
                in_specs=[pl.BlockSpec((SC_WINDOW, QUARTER), lambda i, q=q: (i, q)),
                          pl.BlockSpec((TOP_K, SC_WINDOW), lambda i: (0, i))],
                out_specs=[],
                core_axis_name=("c", "s"),
                dimension_semantics=(pltpu.PARALLEL,),
            )(h_hbm, d_hbm)

    return scatter_rows(hp, dest_km)


def _gather_back(dest_km, ys):
    t = dest_km.shape[1]
    n = len(ys)

    @functools.partial(
        pl.kernel,
        out_type=[jax.ShapeDtypeStruct((t, TOP_K * QUARTER), ys[0].dtype)] * n,
        mesh=_sc_mesh(), scratch_types=[])
    def gather_rows(d_hbm, *refs):
        ys_hbm, out_hbm = refs[:n], refs[n:]
        for q in range(n):
            for k in range(TOP_K):
                def body(i_vmem, o_vmem, q=q, k=k):
                    pltpu.sync_copy(ys_hbm[q].at[i_vmem.at[k]], o_vmem)

                pltpu.emit_pipeline(
                    body,
                    grid=(t // SC_WINDOW,),
                    in_specs=[pl.BlockSpec((TOP_K, SC_WINDOW), lambda i: (0, i))],
                    out_specs=[pl.BlockSpec((SC_WINDOW, QUARTER), lambda i, k=k: (i, k))],
                    core_axis_name=("c", "s"),
                    dimension_semantics=(pltpu.PARALLEL,),
                )(d_hbm, out_hbm[q])

    return gather_rows(dest_km, *ys)


def _experts_kernel(be_ref, used_ref, *refs):
    xs_refs = refs[:N_PACKED]
    wgu_hbm, wdn_hbm, bg_ref, bl_ref, bd_ref, perm_ref = refs[N_PACKED:N_PACKED + 6]
    ys_refs = refs[N_PACKED + 6:2 * N_PACKED + 6]
    wgu_st, wdn_st, wg_ref, wl_ref, wd_ref, sem = refs[2 * N_PACKED + 6:]
    p = pl.program_id(0)
    used = used_ref[0]
    b0 = 2 * p
    b1 = b0 + 1
    e0 = be_ref[b0]
    e1 = be_ref[b1]
    act0 = b0 < used
    act1 = b1 < used
    fresh0 = jnp.logical_and(act0, jnp.logical_or(p == 0, e0 != be_ref[jnp.maximum(b0 - 1, 0)]))
    same = jnp.logical_and(act1, e1 == e0)
    fresh1 = jnp.logical_and(act1, e1 != e0)

    def weight_copies(ex):
        return (pltpu.make_async_copy(wgu_hbm.at[ex], wgu_st, sem.at[0]),
                pltpu.make_async_copy(wdn_hbm.at[ex], wdn_st, sem.at[1]))

    def load_expert(ex, blk):
        for cp in weight_copies(ex):
            cp.wait()
        perm = perm_ref[...]
        half = LANES
        for c in range(2 * D_EXPERT // (2 * half)):
            slab = wgu_st[:, c * 2 * half:(c + 1) * 2 * half].astype(BF16)
            sep = _dot(slab, perm)
            wg_ref[:, c * half:(c + 1) * half] = sep[:, 0:half].astype(BF16)
            wl_ref[:, c * half:(c + 1) * half] = sep[:, half:2 * half].astype(BF16)
        wd_ref[...] = wdn_st[...].astype(BF16)

        def same_expert(j):
            return jnp.logical_and(j < used, be_ref[jnp.minimum(j, used - 1)] == ex)

        nxt = lax.while_loop(same_expert, lambda j: j + 1, blk + 1)

        @pl.when(nxt < used)
        def _():
            for cp in weight_copies(be_ref[nxt]):
                cp.start()

    def compute(ex, r0, nrows):
        rows = slice(r0, r0 + nrows)
        xb = _unpack_bf16_pairs(jnp.concatenate([r[rows, :] for r in xs_refs], axis=1))
        glu = jnp.minimum(_dot(xb, wg_ref[...]) + bg_ref[ex], SWIGLU_LIMIT)
        lin = jnp.clip(_dot(xb, wl_ref[...]) + bl_ref[ex], -SWIGLU_LIMIT, SWIGLU_LIMIT)
        act = glu * _sigmoid(SWIGLU_ALPHA * glu) * (lin + 1.0)
        y = _pack_bf16_pairs(_dot(act.astype(BF16), wd_ref[...]) + bd_ref[ex])
        for q, ys_ref in enumerate(ys_refs):
            ys_ref[rows, :] = y[:, q * QUARTER:(q + 1) * QUARTER]

    @pl.when(jnp.logical_and(p == 0, used > 0))
    def _():
        for cp in weight_copies(e0):
            cp.start()

    @pl.when(fresh0)
    def _():
        load_expert(e0, b0)

    @pl.when(jnp.logical_and(act0, same))
    def _():
        compute(e0, 0, 2 * MOE_BLOCK)

    @pl.when(jnp.logical_and(act0, jnp.logical_not(same)))
    def _():
        compute(e0, 0, MOE_BLOCK)

    @pl.when(fresh1)
    def _():
        load_expert(e1, b1)
        compute(e1, MOE_BLOCK, MOE_BLOCK)

    for blk_active, r0 in ((act0, 0), (act1, MOE_BLOCK)):
        @pl.when(jnp.logical_not(blk_active))
        def _(r0=r0):
            for ys_ref in ys_refs:
                ys_ref[r0:r0 + MOE_BLOCK, :] = jnp.zeros((MOE_BLOCK, QUARTER), U32)


def _experts(block_e, used, xs, w_gu, w_dn, b_g, b_l, b_d, perm):
    n_rows = xs[0].shape[0]
    n_blocks = n_rows // MOE_BLOCK
    assert n_blocks % 2 == 0
    wmap = lambda i, be, u: (0, 0, 0)
    rows = pl.BlockSpec((2 * MOE_BLOCK, QUARTER), lambda i, be, u: (i, 0))
    return pl.pallas_call(
        _experts_kernel,
        grid_spec=pltpu.PrefetchScalarGridSpec(
            num_scalar_prefetch=2,
            grid=(n_blocks // 2,),
            in_specs=[rows] * N_PACKED + [
                pl.BlockSpec(memory_space=pl.ANY),
                pl.BlockSpec(memory_space=pl.ANY),
                pl.BlockSpec((N_EXPERTS, 1, D_EXPERT), wmap),
                pl.BlockSpec((N_EXPERTS, 1, D_EXPERT), wmap),
                pl.BlockSpec((N_EXPERTS, 1, D_MODEL), wmap),
                pl.BlockSpec((2 * LANES, 2 * LANES), lambda i, be, u: (0, 0)),
            ],
            out_specs=[rows] * N_PACKED,
            scratch_shapes=[
                pltpu.VMEM((D_MODEL, 2 * D_EXPERT), F32),
                pltpu.VMEM((D_EXPERT, D_MODEL), F32),
                pltpu.VMEM((D_MODEL, D_EXPERT), BF16),
                pltpu.VMEM((D_MODEL, D_EXPERT), BF16),
                pltpu.VMEM((D_EXPERT, D_MODEL), BF16),
                pltpu.SemaphoreType.DMA((2,)),
            ],
        ),
        out_shape=[jax.ShapeDtypeStruct((n_rows, QUARTER), U32)] * N_PACKED,
        compiler_params=pltpu.CompilerParams(
            dimension_semantics=("arbitrary",), vmem_limit_bytes=VMEM_LIMIT),
        name="experts",
    )(block_e, used, *xs, w_gu, w_dn, b_g, b_l, b_d, perm)


def _combine_kernel(h_ref, gate_ref, g_ref, b_ref, *refs, tm, alpha):
    yg_refs, o_ref = refs[:N_PACKED], refs[N_PACKED]
    gate = jnp.transpose(gate_ref[...])
    gks = [gate[:, k:k + 1] for k in range(TOP_K)]
    lows, highs = [], []
    for yg_ref in yg_refs:
        lo = jnp.zeros((tm, QUARTER), F32)
        hi = jnp.zeros((tm, QUARTER), F32)
        for k in range(TOP_K):
            w = yg_ref[:, k * QUARTER:(k + 1) * QUARTER]
            lo = lo + gks[k] * lax.bitcast_convert_type(w << 16, F32)
            hi = hi + gks[k] * lax.bitcast_convert_type(w & jnp.uint32(0xFFFF0000), F32)
        lows.append(lo)
        highs.append(hi)
    ffn = jnp.concatenate(lows + highs, axis=1)
    o_ref[...] = _layer_norm(alpha * h_ref[...] + ffn, g_ref[...], b_ref[...])


def _combine(h, gate, g, b, yg, alpha, part, n_parts, prev_out):
    t = h.shape[0]
    tm = 256
    tiles = t // n_parts // tm
    kern = functools.partial(_combine_kernel, tm=tm, alpha=alpha)
    tok = lambda i: (part * tiles + i, 0)
    in_specs = [
        pl.BlockSpec((tm, D_MODEL), tok),
        pl.BlockSpec((8, tm), lambda i: (0, part * tiles + i)),
        pl.BlockSpec((1, D_MODEL), lambda i: (0, 0)),
        pl.BlockSpec((1, D_MODEL), lambda i: (0, 0)),
    ] + [pl.BlockSpec((tm, TOP_K * QUARTER), lambda i: (i, 0))] * N_PACKED
    args = [h, gate, g, b, *yg]
    aliases = {}
    if prev_out is not None:
        n_in = len(args)

        def kern(*refs, body=kern):
            body(*refs[:n_in], *refs[n_in + 1:])
        in_specs.append(pl.BlockSpec(memory_space=pl.ANY))
        aliases = {n_in: 0}
        args.append(prev_out)
    return pl.pallas_call(
        kern,
        grid=(tiles,),
        in_specs=in_specs,
        out_specs=pl.BlockSpec((tm, D_MODEL), tok),
        out_shape=jax.ShapeDtypeStruct((t, D_MODEL), F32),
        input_output_aliases=aliases,
        compiler_params=pltpu.CompilerParams(
            dimension_semantics=("arbitrary",), vmem_limit_bytes=VMEM_LIMIT),
        name="combine",
    )(*args)


def _column_split_permutation():
    n = 2 * LANES
    p = np.zeros((n, n), np.float32)
    j = np.arange(n)
    p[j, j // 2 + LANES * (j % 2)] = 1.0
    return jnp.asarray(p, dtype=BF16)


def kernel(x, mem, w_in, lam_q1, lam_k1, lam_q2, lam_k2, diff_norm_w, hgrn_lb_fwd,
           hgrn_lb_bwd, hgrn_norm_w, w_mem_kv, w_o, ln1_g, ln1_b, router_w, router_b,
           w_gate_up, b_gate_up, w_down, b_down, ln2_g, ln2_b):
    batch, seq, d = x.shape
    mlen = mem.shape[1]
    assert d == D_MODEL and w_in.shape == (N_LAYERS, D_MODEL, D_IN_PROJ)
    assert hgrn_lb_fwd.shape == (N_LAYERS + 1, D_HGRN)
    t = batch * seq
    alpha = (2.0 * N_LAYERS) ** 0.25
    lam_init = 0.8 - 0.6 * math.exp(-0.3 * 0)

    x2d = x.reshape(t, D_MODEL)
    ua, uh, um = _in_proj(x2d, w_in[0].astype(BF16), hgrn_lb_fwd, hgrn_lb_bwd)
    o_diff = _diff_attn(ua, diff_norm_w, lam_q1, lam_k1, lam_q2, lam_k2,
                        batch, seq, lam_init)
    o_mem = _mem_attn(um, mem.reshape(batch * mlen, D_MODEL), w_mem_kv[0].astype(BF16),
                      batch, seq, mlen)
    o_hgrn = _hgrn2(uh, hgrn_norm_w, batch, seq)

    h1, h1b, eidx, gate, rank, cnt = _post_mixer(
        x2d, o_diff, o_hgrn, o_mem, w_o[0].astype(BF16), ln1_g, ln1_b, router_w[0].T,
        router_b.reshape(N_EXPERTS, 1), alpha)

    n_blocks = -(-(t * TOP_K) // MOE_BLOCK) + N_EXPERTS
    dest, meta = _route_meta(eidx, rank, cnt, n_blocks)
    dest_km = dest[0:TOP_K]
    block_e = meta[0, 0:n_blocks]
    used = meta[1, 0:1]

    xs = _dispatch(dest_km, h1b, n_blocks * MOE_BLOCK)
    b_g = b_gate_up[0][:, 0::2].reshape(N_EXPERTS, 1, D_EXPERT)
    b_l = b_gate_up[0][:, 1::2].reshape(N_EXPERTS, 1, D_EXPERT)
    ys = _experts(block_e, used, xs, w_gate_up[0], w_down[0], b_g, b_l,
                  b_down[0].reshape(N_EXPERTS, 1, D_MODEL), _column_split_permutation())
    out = None
    for part in range(COMBINE_PARTS):
        lo = part * (t // COMBINE_PARTS)
        yg = _gather_back(dest_km[:, lo:lo + t // COMBINE_PARTS], ys)
        out = _combine(h1, gate, ln2_g, ln2_b, yg, alpha, part, COMBINE_PARTS, out)
    return out.reshape(batch, seq, D_MODEL)
```

```python
import functools
import math

import jax
import jax.numpy as jnp
import numpy as np
from jax import lax
from jax.experimental import pallas as pl
from jax.experimental.pallas import tpu as pltpu
from jax.experimental.pallas import tpu_sc as plsc

F32 = jnp.float32
BF16 = jnp.bfloat16
I32 = jnp.int32
U32 = jnp.uint32

D_MODEL = 1024
N_LAYERS = 1
D_DIFF = 512
D_HGRN = 256
D_XMEM = 256
N_HEADS = 4
DH = 64
DH_SHIFT = 6
D_IN_PROJ = 3 * D_DIFF + 5 * D_HGRN + D_XMEM
CHUNK = 16
N_EXPERTS = 32
TOP_K = 4
D_EXPERT = 1024
MOE_BLOCK = 256
SWIGLU_ALPHA = 1.702
SWIGLU_LIMIT = 7.0
NORM_EPS = 1e-5
LOG2E = math.log2(math.e)
LANES = 128
VMEM_LIMIT = 56 * 1024 * 1024


def _nt_dot(a, b):
    return lax.dot_general(a, b, (((1,), (1,)), ((), ())), preferred_element_type=F32)


def _tn_dot(a, b):
    return lax.dot_general(a, b, (((0,), (0,)), ((), ())), preferred_element_type=F32)


def _dot(a, b):
    return jnp.dot(a, b, preferred_element_type=F32)


def _sigmoid(x):
    return 1.0 / (1.0 + jnp.exp(-x))


def _layer_norm(y, g, b):
    mu = jnp.mean(y, axis=-1, keepdims=True)
    yc = y - mu
    var = jnp.mean(yc * yc, axis=-1, keepdims=True)
    return yc * lax.rsqrt(var + NORM_EPS) * g + b


def _in_proj_kernel(x_ref, w_ref, lbf_ref, lbb_ref, ua_ref, uh_ref, um_ref):
    xb = x_ref[...].astype(BF16)

    def proj(c0, width):
        return _dot(xb, w_ref[:, c0:c0 + width])

    ua_ref[:, 0:D_DIFF] = (proj(0, D_DIFF) * (LOG2E / math.sqrt(DH))).astype(BF16)
    ua_ref[:, D_DIFF:2 * D_DIFF] = proj(D_DIFF, D_DIFF).astype(BF16)
    ua_ref[:, 2 * D_DIFF:3 * D_DIFF] = proj(2 * D_DIFF, D_DIFF).astype(BF16)

    def lower_bound(lb_ref):
        a = lb_ref[0:1, :]
        b = lb_ref[1:2, :]
        m = jnp.maximum(a, b)
        ea = jnp.exp(a - m)
        eb = jnp.exp(b - m)
        return ea / (ea + eb)

    base = 3 * D_DIFF
    hq = proj(base, D_HGRN)
    uh_ref[:, 0:D_HGRN] = hq * _sigmoid(hq)
    for d, lb_ref in enumerate((lbf_ref, lbb_ref)):
        lb = lower_bound(lb_ref)
        f = lb + (1.0 - lb) * _sigmoid(proj(base + (1 + d) * D_HGRN, D_HGRN))
        uh_ref[:, (1 + 2 * d) * D_HGRN:(2 + 2 * d) * D_HGRN] = 1.0 - f
        uh_ref[:, (2 + 2 * d) * D_HGRN:(3 + 2 * d) * D_HGRN] = jnp.log(f)
    uh_ref[:, 5 * D_HGRN:6 * D_HGRN] = proj(base + 3 * D_HGRN, D_HGRN)
    uh_ref[:, 6 * D_HGRN:7 * D_HGRN] = _sigmoid(proj(base + 4 * D_HGRN, D_HGRN))
    um_ref[...] = proj(base + 5 * D_HGRN, D_XMEM).astype(BF16)


def _in_proj(x2d, w_in_bf, lb_f, lb_b):
    t = x2d.shape[0]
    tm = 1024
    return pl.pallas_call(
        _in_proj_kernel,
        grid=(t // tm,),
        in_specs=[
            pl.BlockSpec((tm, D_MODEL), lambda i: (i, 0)),
            pl.BlockSpec((D_MODEL, D_IN_PROJ), lambda i: (0, 0)),
            pl.BlockSpec((N_LAYERS + 1, D_HGRN), lambda i: (0, 0)),
            pl.BlockSpec((N_LAYERS + 1, D_HGRN), lambda i: (0, 0)),
        ],
        out_specs=[
            pl.BlockSpec((tm, 3 * D_DIFF), lambda i: (i, 0)),
            pl.BlockSpec((tm, 7 * D_HGRN), lambda i: (i, 0)),
            pl.BlockSpec((tm, D_XMEM), lambda i: (i, 0)),
        ],
        out_shape=[
            jax.ShapeDtypeStruct((t, 3 * D_DIFF), BF16),
            jax.ShapeDtypeStruct((t, 7 * D_HGRN), F32),
            jax.ShapeDtypeStruct((t, D_XMEM), BF16),
        ],
        compiler_params=pltpu.CompilerParams(
            dimension_semantics=("arbitrary",), vmem_limit_bytes=VMEM_LIMIT),
        name="in_proj",
    )(x2d, w_in_bf, lb_f, lb_b)


def _bf16_pieces(x):
    x = np.asarray(x, np.float32)
    out = []
    for _ in range(3):
        p = x.astype(BF16).astype(np.float32)
        out.append(p)
        x = x - p
    return np.stack(out, axis=-1)


def _alibi_tables(seq, t):
    slopes = (2.0 ** (-8.0 * np.arange(1, N_HEADS + 1) / N_HEADS)).astype(np.float32)
    c3 = _bf16_pieces(slopes * np.float32(LOG2E))
    pos = np.arange(seq)
    hi = (pos >> DH_SHIFT).astype(np.float32)
    lo = (pos & (DH - 1)).astype(np.float32)
    qf = np.zeros((N_HEADS, 2, seq, 2 * DH), np.float32)
    kf = np.zeros((N_HEADS, 2, seq, 2 * DH), np.float32)
    for m, base in enumerate((DH, 0)):
        for j in range(3):
            qf[:, m, :, base + j] = hi
            qf[:, m, :, base + 3 + j] = lo
            qf[:, m, :, base + 6 + j] = 64.0 * c3[:, j, None]
            qf[:, m, :, base + 9 + j] = c3[:, j, None]
            kf[:, m, :, base + j] = -64.0 * c3[:, j, None]
            kf[:, m, :, base + 3 + j] = -c3[:, j, None]
            kf[:, m, :, base + 6 + j] = hi
            kf[:, m, :, base + 9 + j] = lo
    c = c3.sum(axis=-1)
    kk = np.arange(t)[:, None]
    qq = np.arange(t)[None, :]
    corr = 2.0 * c[:, None, None] * np.minimum(qq - kk, 0).astype(np.float32)
    return (jnp.asarray(qf, dtype=BF16), jnp.asarray(kf, dtype=BF16),
            jnp.asarray(corr, dtype=F32))


def _diff_attn_kernel(q_ref, k_ref, v_ref, qf_ref, kf_ref, corr_ref, nw_ref, lq1_ref,
                      lk1_ref, lq2_ref, lk2_ref, o_ref, ka1_ref, ka2_ref, vt_ref, acc_ref,
                      m_ref, l_ref, *, t, qs, ks, qpt, seq, lam_init):
    i = pl.program_id(2)
    nk = seq // t
    lane = lax.broadcasted_iota(I32, (t, 2 * DH), 1)
    first_half = lane < DH

    @pl.when(i == 0)
    def _():
        def build(r, carry):
            rows = pl.ds(pl.multiple_of(r * t, t), t)
            kblk = k_ref[rows, :]
            ka1_ref[rows, :] = jnp.where(first_half, kblk, kf_ref[0, 0, rows, :])
            ka2_ref[rows, :] = jnp.where(first_half, kf_ref[0, 1, rows, :], kblk)
            vt_ref[r] = v_ref[rows, :].astype(F32).T.astype(BF16)
            return carry

        lax.fori_loop(0, nk, build, 0)

    lam = (jnp.exp(jnp.sum(lq1_ref[...] * lk1_ref[...], axis=-1, keepdims=True))
           - jnp.exp(jnp.sum(lq2_ref[...] * lk2_ref[...], axis=-1, keepdims=True))
           + lam_init)

    for qt in range(qpt):
        qi = i * qpt + qt
        qrange = slice(qt * t, (qt + 1) * t)
        q = q_ref[qrange, :]
        qf1 = qf_ref[0, 0, qrange, :]
        qf2 = qf_ref[0, 1, qrange, :]
        qa1_before = jnp.where(first_half, q, qf1)
        qa1_after = jnp.where(first_half, q, -qf1)
        qa2_before = jnp.where(first_half, qf2, q)
        qa2_after = jnp.where(first_half, -qf2, q)

        m_ref[qt] = jnp.full(m_ref.shape[1:], -jnp.inf, F32)
        l_ref[qt] = jnp.zeros(l_ref.shape[1:], F32)
        acc_ref[qt] = jnp.zeros(acc_ref.shape[1:], F32)

        def chunk(j, qa1, qa2, diagonal, qt=qt):
            k0 = pl.multiple_of(j * t, t)
            for mp, (qa, ka_ref) in enumerate(((qa1, ka1_ref), (qa2, ka2_ref))):
                for u in range(t // qs):
                    qrows = slice(u * qs, (u + 1) * qs)
                    cols = slice(mp * t + u * qs, mp * t + (u + 1) * qs)
                    for kb in range(t // ks):
                        krows = slice(kb * ks, (kb + 1) * ks)
                        s = _nt_dot(ka_ref[pl.ds(k0 + kb * ks, ks), :], qa[qrows, :])
                        if diagonal:
                            s = s + corr_ref[0, krows, qrows]
                        m_old = m_ref[qt, :, cols]
                        m_new = jnp.maximum(m_old, jnp.max(s, axis=0, keepdims=True))
                        p = jnp.exp2(s - m_new)
                        r = jnp.exp2(m_old - m_new)
                        l_ref[qt, :, cols] = (r * l_ref[qt, :, cols]
                                              + jnp.sum(p, axis=0, keepdims=True))
                        acc_ref[qt, mp, :, qrows] = (
                            r * acc_ref[qt, mp, :, qrows]
                            + _dot(vt_ref[j, :, krows], p.astype(BF16)))
                        m_ref[qt, :, cols] = m_new

        chunk(qi, qa1_before, qa2_before, True)
        for jj in range(nk - 1):
            j = jj + (jj >= qi).astype(I32)
            keys_first = j < qi
            chunk(j, jnp.where(keys_first, qa1_before, qa1_after),
                  jnp.where(keys_first, qa2_before, qa2_after), False)

        o = (acc_ref[qt, 0] / l_ref[qt, :, 0:t]
             - lam * (acc_ref[qt, 1] / l_ref[qt, :, t:2 * t]))
        o = o * lax.rsqrt(jnp.mean(o * o, axis=0, keepdims=True) + NORM_EPS)
        o_ref[qrange, :] = (o.T * nw_ref[...] * (1.0 - lam_init)).astype(o_ref.dtype)


def _diff_attn(ua, nw, lq1, lk1, lq2, lk2, batch, seq, lam_init):
    t = 512
    qpt = 2
    nq = seq // (t * qpt)
    qf, kf, corr = _alibi_tables(seq, t)
    kern = functools.partial(_diff_attn_kernel, t=t, qs=LANES, ks=256, qpt=qpt, seq=seq,
                             lam_init=lam_init)
    small = lambda b, h, i: (0, 0)
    return pl.pallas_call(
        kern,
        grid_spec=pltpu.PrefetchScalarGridSpec(
            num_scalar_prefetch=0,
            grid=(batch, N_HEADS, nq),
            in_specs=[
                pl.BlockSpec((qpt * t, 2 * DH), lambda b, h, i: (b * nq + i, h)),
                pl.BlockSpec((seq, 2 * DH), lambda b, h, i: (b, N_HEADS + h)),
                pl.BlockSpec((seq, 2 * DH), lambda b, h, i: (b, 2 * N_HEADS + h)),
                pl.BlockSpec((1, 2, qpt * t, 2 * DH), lambda b, h, i: (h, 0, i, 0)),
                pl.BlockSpec((1, 2, seq, 2 * DH), lambda b, h, i: (h, 0, 0, 0)),
                pl.BlockSpec((1, t, t), lambda b, h, i: (h, 0, 0)),
                pl.BlockSpec((1, 2 * DH), small),
                pl.BlockSpec((1, DH), small),
                pl.BlockSpec((1, DH), small),
                pl.BlockSpec((1, DH), small),
                pl.BlockSpec((1, DH), small),
            ],
            out_specs=pl.BlockSpec((qpt * t, 2 * DH), lambda b, h, i: (b * nq + i, h)),
            scratch_shapes=[
                pltpu.VMEM((seq, 2 * DH), BF16),
                pltpu.VMEM((seq, 2 * DH), BF16),
                pltpu.VMEM((seq // t, 2 * DH, t), BF16),
                pltpu.VMEM((qpt, 2, 2 * DH, t), F32),
                pltpu.VMEM((qpt, 1, 2 * t), F32),
                pltpu.VMEM((qpt, 1, 2 * t), F32),
            ],
        ),
        out_shape=jax.ShapeDtypeStruct((batch * seq, D_DIFF), BF16),
        compiler_params=pltpu.CompilerParams(
            dimension_semantics=("arbitrary", "arbitrary", "arbitrary"),
            vmem_limit_bytes=VMEM_LIMIT),
        name="diff_attn",
    )(ua, ua, ua, qf, kf, corr, nw, lq1, lk1, lq2, lk2)


def _mem_attn_kernel(q_ref, mem_ref, wkv_ref, o_ref, kv_ref, *, tq, mlen):
    @pl.when(pl.program_id(1) == 0)
    def _():
        kv_ref[...] = _dot(mem_ref[...].astype(BF16), wkv_ref[...]).astype(BF16)

    q = q_ref[...]
    mk = kv_ref[:, 0:D_XMEM]
    mv = kv_ref[:, D_XMEM:2 * D_XMEM]
    qhead = lax.broadcasted_iota(I32, (tq, D_XMEM), 1) >> DH_SHIFT
    vhead = lax.broadcasted_iota(I32, (mlen, D_XMEM), 1) >> DH_SHIFT
    acc = jnp.zeros((tq, D_XMEM), F32)
    for h in range(N_HEADS):
        qh = jnp.where(qhead == h, q, jnp.zeros_like(q))
        s = _nt_dot(qh, mk) * (1.0 / math.sqrt(DH))
        e = jnp.exp(s - jnp.max(s, axis=-1, keepdims=True))
        p = e / jnp.sum(e, axis=-1, keepdims=True)
        vh = jnp.where(vhead == h, mv, jnp.zeros_like(mv))
        acc = acc + _dot(p.astype(BF16), vh)
    o_ref[...] = acc.astype(o_ref.dtype)


def _mem_attn(um, mem2d, wkv_bf, batch, seq, mlen):
    tq = 1024
    nq = seq // tq
    kern = functools.partial(_mem_attn_kernel, tq=tq, mlen=mlen)
    return pl.pallas_call(
        kern,
        grid=(batch, nq),
        in_specs=[
            pl.BlockSpec((tq, D_XMEM), lambda b, i: (b * nq + i, 0)),
            pl.BlockSpec((mlen, D_MODEL), lambda b, i: (b, 0)),
            pl.BlockSpec((D_MODEL, 2 * D_XMEM), lambda b, i: (0, 0)),
        ],
        out_specs=pl.BlockSpec((tq, D_XMEM), lambda b, i: (b * nq + i, 0)),
        out_shape=jax.ShapeDtypeStruct((batch * seq, D_XMEM), BF16),
        scratch_shapes=[pltpu.VMEM((mlen, 2 * D_XMEM), BF16)],
        compiler_params=pltpu.CompilerParams(
            dimension_semantics=("arbitrary", "arbitrary"), vmem_limit_bytes=VMEM_LIMIT),
        name="mem_attn",
    )(um, mem2d, wkv_bf)


def _hgrn2_kernel(q_ref, kf_ref, lf_ref, kb_ref, lb_ref, v_ref, g_ref, nw_ref, o_ref,
                  kfp_ref, kbp_ref, vp_ref, gf_ref, gb_ref, bf_ref, bb_ref, acc_ref, cross_ref,
                  *, seq, rb):
    n_chunks = seq // CHUNK
    w = 2 * DH
    row = lax.broadcasted_iota(I32, (rb, w), 0) & (CHUNK - 1)
    li = lax.broadcasted_iota(I32, (w, w), 0) >> DH_SHIFT
    lj = lax.broadcasted_iota(I32, (w, w), 1) >> DH_SHIFT
    same_head = li == lj
    head_ones = jnp.where(same_head, 1.0, 0.0).astype(BF16)

    def chunk_cumsum(x, reverse):
        for sh in (1, 2, 4, 8):
            if reverse:
                moved = pltpu.roll(x, rb - sh, 0)
                keep = row + sh < CHUNK
            else:
                moved = pltpu.roll(x, sh, 0)
                keep = row >= sh
            x = x + jnp.where(keep, moved, 0.0)
        return x

    halo = jnp.zeros((CHUNK, w), F32)
    for ref in (kfp_ref, kbp_ref, vp_ref, gf_ref, gb_ref, bf_ref, bb_ref):
        ref[0:CHUNK, :] = halo
        ref[seq + CHUNK:seq + 2 * CHUNK, :] = halo

    def prepare(blk, carry):
        r0 = pl.multiple_of(blk * rb, rb)
        rows = pl.ds(r0, rb)
        inner = pl.ds(r0 + CHUNK, rb)
        kfp_ref[inner, :] = kf_ref[rows, :]
        kbp_ref[inner, :] = kb_ref[rows, :]
        vp_ref[inner, :] = v_ref[rows, :]
        lf2 = lf_ref[rows, :] * LOG2E
        lb2 = lb_ref[rows, :] * LOG2E
        gf_ref[inner, :] = jnp.where(row == 0, 0.0, jnp.exp2(lf2))
        gb_ref[inner, :] = jnp.where(row == CHUNK - 1, 0.0, jnp.exp2(lb2))
        bf_ref[inner, :] = chunk_cumsum(lf2, False)
        bb_ref[inner, :] = chunk_cumsum(lb2, True)
        return carry

    lax.fori_loop(0, seq // rb, prepare, 0)

    def intra(blk):
        r0 = pl.multiple_of(blk * rb, rb)
        rows = pl.ds(r0, rb)
        q = q_ref[rows, :]
        acc = jnp.zeros((rb, w), F32)
        for reverse, kp_ref, g_ref2 in ((False, kfp_ref, gf_ref), (True, kbp_ref, gb_ref)):
            dec = None
            for d in range(CHUNK):
                src = pl.ds(r0 + CHUNK + (d if reverse else -d), rb)
                ks = kp_ref[src, :]
                vs = vp_ref[src, :]
                if d == 0:
                    z = q * ks
                else:
                    gsrc = pl.ds(r0 + CHUNK + ((d - 1) if reverse else -(d - 1)), rb)
                    gs = g_ref2[gsrc, :]
                    dec = gs if dec is None else dec * gs
                    z = q * ks * dec
                acc = acc + _dot(z.astype(BF16), head_ones) * vs
        acc_ref[rows, :] = acc

    def chunk_updates(blk):
        per_dir = ([], [])
        for c in range(rb // CHUNK):
            n = blk * (rb // CHUNK) + c
            for d, (r0, kp_ref, b_ref, edge) in enumerate((
                    (n * CHUNK, kfp_ref, bf_ref, CHUNK - 1),
                    ((n_chunks - 1 - n) * CHUNK, kbp_ref, bb_ref, 0))):
                r0 = pl.multiple_of(r0, CHUNK)
                rows = pl.ds(r0, CHUNK)
                prow = pl.ds(r0 + CHUNK, CHUNK)
                b = b_ref[prow, :]
                b_edge = b_ref[pl.ds(r0 + CHUNK + edge, 1), :]
                qd = (q_ref[rows, :] * jnp.exp2(b)).astype(BF16)
                kd = kp_ref[prow, :] * jnp.exp2(b_edge - b)
                upd = _tn_dot(vp_ref[prow, :].astype(BF16), kd.astype(BF16))
                per_dir[d].append((rows, qd, jnp.exp2(b_edge), jnp.where(same_head, upd, 0.0)))
        return per_dir

    def state_chain(per_dir, states):
        for d in range(2):
            for rows, qd, decay, upd in per_dir[d]:
                cross_ref[rows, :] = cross_ref[rows, :] + _nt_dot(qd, states[d].astype(BF16))
                states[d] = states[d] * decay + upd

    cross_ref[...] = jnp.zeros(cross_ref.shape, F32)

    def fused(blk, carry):
        states = list(carry)
        intra(blk)
        state_chain(chunk_updates(blk), states)
        return tuple(states)

    z = jnp.zeros((w, w), F32)
    lax.fori_loop(0, seq // rb, fused, (z, z))

    def finish(blk, carry):
        rows = pl.ds(pl.multiple_of(blk * rb, rb), rb)
        o = acc_ref[rows, :] + cross_ref[rows, :]
        lane = lax.broadcasted_iota(I32, (rb, w), 1)
        lo = lane < DH
        sq = o * o
        ms_lo = jnp.sum(jnp.where(lo, sq, 0.0), axis=-1, keepdims=True)
        ms_hi = jnp.sum(jnp.where(lo, 0.0, sq), axis=-1, keepdims=True)
        ms = jnp.where(lo, ms_lo, ms_hi) * (1.0 / DH)
        o = o * lax.rsqrt(ms + NORM_EPS) * nw_ref[...] * g_ref[rows, :]
        o_ref[rows, :] = o.astype(o_ref.dtype)
        return carry

    lax.fori_loop(0, seq // rb, finish, 0)


def _hgrn2(uh, nw, batch, seq):
    w = 2 * DH
    npair = D_HGRN // w
    kern = functools.partial(_hgrn2_kernel, seq=seq, rb=256)

    def sec(s):
        return pl.BlockSpec((seq, w), lambda b, p, s=s: (b, s * npair + p))

    return pl.pallas_call(
        kern,
        grid=(batch, npair),
        in_specs=[sec(0), sec(1), sec(2), sec(3), sec(4), sec(5), sec(6),
                  pl.BlockSpec((1, w), lambda b, p: (0, p))],
        out_specs=pl.BlockSpec((seq, w), lambda b, p: (b, p)),
        out_shape=jax.ShapeDtypeStruct((batch * seq, D_HGRN), BF16),
        scratch_shapes=[pltpu.VMEM((seq + 2 * CHUNK, w), F32)] * 7
        + [pltpu.VMEM((seq, w), F32)] * 2,
        compiler_params=pltpu.CompilerParams(
            dimension_semantics=("arbitrary", "arbitrary"), vmem_limit_bytes=VMEM_LIMIT),
        name="hgrn2",
    )(uh, uh, uh, uh, uh, uh, uh, nw)


def _rows_to_block(rows, tm):
    r = lax.broadcasted_iota(I32, (8, tm), 0)
    out = jnp.zeros((8, tm), rows[0].dtype)
    for k, row in enumerate(rows):
        out = jnp.where(r == k, row, out)
    return out


def _pack_bf16_pairs(x):
    m = x.shape[1] // 2
    u = lax.bitcast_convert_type(x, U32)
    r = u + jnp.uint32(0x7FFF) + ((u >> 16) & jnp.uint32(1))
    return (r[:, 0:m] >> 16) | (r[:, m:2 * m] & jnp.uint32(0xFFFF0000))


def _unpack_bf16_pairs(w):
    lo = lax.bitcast_convert_type(w << 16, F32).astype(BF16)
    hi = lax.bitcast_convert_type(w & jnp.uint32(0xFFFF0000), F32).astype(BF16)
    return jnp.concatenate([lo, hi], axis=1)


def _post_mixer_kernel(x_ref, od_ref, oh_ref, om_ref, wo_ref, g_ref, b_ref, rwt_ref,
                       rbt_ref, h_ref, hp_ref, eidx_ref, gate_ref, rank_ref, cnt_ref,
                       carry_ref, *, tm, alpha):
    i = pl.program_id(0)

    @pl.when(i == 0)
    def _():
        carry_ref[...] = jnp.zeros_like(carry_ref)

    mix = (_dot(od_ref[...], wo_ref[0:D_DIFF, :])
           + _dot(oh_ref[...], wo_ref[D_DIFF:D_DIFF + D_HGRN, :])
           + _dot(om_ref[...], wo_ref[D_DIFF + D_HGRN:, :]))
    h = _layer_norm(alpha * x_ref[...] + mix, g_ref[...], b_ref[...])
    h_ref[...] = h

    h_hi = h.astype(BF16)
    hp_ref[...] = _pack_bf16_pairs(h)
    h_lo = (h - h_hi.astype(F32)).astype(BF16)
    rwt = rwt_ref[...]
    rwt_hi = rwt.astype(BF16)
    rwt_lo = (rwt - rwt_hi.astype(F32)).astype(BF16)
    logits = (_nt_dot(rwt_hi, h_hi) + _nt_dot(rwt_hi, h_lo) + _nt_dot(rwt_lo, h_hi)
              + rbt_ref[...])

    erow = lax.broadcasted_iota(I32, (N_EXPERTS, tm), 0).astype(F32)
    work = logits
    sels, vals, idxs = [], [], []
    for _ in range(TOP_K):
        m = jnp.max(work, axis=0, keepdims=True)
        idx = jnp.min(jnp.where(work == m, erow, float(N_EXPERTS)), axis=0, keepdims=True)
        sel = erow == idx
        work = jnp.where(sel, -jnp.inf, work)
        sels.append(sel)
        vals.append(m)
        idxs.append(idx)
    es = [jnp.exp(v - vals[0]) for v in vals]
    den = es[0] + es[1] + es[2] + es[3]

    chosen = jnp.where(sels[0] | sels[1] | sels[2] | sels[3], 1.0, 0.0)
    ti = lax.broadcasted_iota(I32, (tm, tm), 0)
    tj = lax.broadcasted_iota(I32, (tm, tm), 1)
    earlier = jnp.where(ti < tj, 1.0, 0.0).astype(BF16)
    prefix = _dot(chosen.astype(BF16), earlier) + carry_ref[:, 0:1]

    ranks = [jnp.sum(jnp.where(sels[k], prefix, 0.0), axis=0, keepdims=True)
             for k in range(TOP_K)]
    eidx_ref[...] = _rows_to_block(idxs, tm).astype(I32)
    gate_ref[...] = _rows_to_block([e / den for e in es], tm)
    rank_ref[...] = _rows_to_block(ranks, tm).astype(I32)

    total = carry_ref[:, 0:1] + jnp.sum(chosen, axis=1, keepdims=True)
    carry_ref[...] = jnp.broadcast_to(total, carry_ref.shape)
    cnt_ref[...] = carry_ref[...]


def _post_mixer(x2d, od, oh, om, wo_bf, g, b, rwt, rbt, alpha):
    t = x2d.shape[0]
    tm = 512
    kern = functools.partial(_post_mixer_kernel, tm=tm, alpha=alpha)
    full = lambda i: (0, 0)
    tok = lambda i: (i, 0)
    per_tok = lambda i: (0, i)
    return pl.pallas_call(
        kern,
        grid=(t // tm,),
        in_specs=[
            pl.BlockSpec((tm, D_MODEL), tok),
            pl.BlockSpec((tm, D_DIFF), tok),
            pl.BlockSpec((tm, D_HGRN), tok),
            pl.BlockSpec((tm, D_XMEM), tok),
            pl.BlockSpec((D_MODEL, D_MODEL), full),
            pl.BlockSpec((1, D_MODEL), full),
            pl.BlockSpec((1, D_MODEL), full),
            pl.BlockSpec((N_EXPERTS, D_MODEL), full),
            pl.BlockSpec((N_EXPERTS, 1), full),
        ],
        out_specs=[
            pl.BlockSpec((tm, D_MODEL), tok),
            pl.BlockSpec((tm, D_MODEL // 2), tok),
            pl.BlockSpec((8, tm), per_tok),
            pl.BlockSpec((8, tm), per_tok),
            pl.BlockSpec((8, tm), per_tok),
            pl.BlockSpec((N_EXPERTS, LANES), full),
        ],
        out_shape=[
            jax.ShapeDtypeStruct((t, D_MODEL), F32),
            jax.ShapeDtypeStruct((t, D_MODEL // 2), U32),
            jax.ShapeDtypeStruct((8, t), I32),
            jax.ShapeDtypeStruct((8, t), F32),
            jax.ShapeDtypeStruct((8, t), I32),
            jax.ShapeDtypeStruct((N_EXPERTS, LANES), F32),
        ],
        scratch_shapes=[pltpu.VMEM((N_EXPERTS, LANES), F32)],
        compiler_params=pltpu.CompilerParams(
            dimension_semantics=("arbitrary",), vmem_limit_bytes=VMEM_LIMIT),
        name="post_mixer",
    )(x2d, od, oh, om, wo_bf, g, b, rwt, rbt)


def _route_meta_kernel(eidx_ref, rank_ref, cnt_ref, dest_ref, be_ref, *, tm, nb_lanes):
    cnt = cnt_ref[...]
    blocks = jnp.floor((cnt + (MOE_BLOCK - 1)) * (1.0 / MOE_BLOCK))
    ei = lax.broadcasted_iota(I32, (N_EXPERTS, N_EXPERTS), 0)
    ej = lax.broadcasted_iota(I32, (N_EXPERTS, N_EXPERTS), 1)
    below = jnp.where(ej < ei, 1.0, 0.0).astype(BF16)
    pstart = _dot(below, blocks.astype(BF16))[:, 0:1]
    pend = pstart + blocks[:, 0:1]

    erow = lax.broadcasted_iota(I32, (N_EXPERTS, tm), 0).astype(F32)
    eidx = eidx_ref[...].astype(F32)
    starts = [jnp.sum(jnp.where(erow == eidx[k:k + 1, :], pstart, 0.0), axis=0, keepdims=True)
              for k in range(TOP_K)]
    dest_ref[...] = ((_rows_to_block(starts, tm) * float(MOE_BLOCK)).astype(I32)
                     + rank_ref[...])

    bi = lax.broadcasted_iota(I32, (N_EXPERTS, nb_lanes), 1).astype(F32)
    done = jnp.where(pend <= bi, 1.0, 0.0)
    be = jnp.minimum(jnp.sum(done, axis=0, keepdims=True), float(N_EXPERTS - 1))
    used = jnp.broadcast_to(pend[N_EXPERTS - 1:N_EXPERTS, :], (1, nb_lanes))
    be_ref[...] = _rows_to_block([be, used], nb_lanes).astype(I32)


def _route_meta(eidx, rank, cnt, n_blocks):
    t = eidx.shape[1]
    tm = 2048
    nb_lanes = -(-n_blocks // LANES) * LANES
    kern = functools.partial(_route_meta_kernel, tm=tm, nb_lanes=nb_lanes)
    return pl.pallas_call(
        kern,
        grid=(t // tm,),
        in_specs=[
            pl.BlockSpec((8, tm), lambda i: (0, i)),
            pl.BlockSpec((8, tm), lambda i: (0, i)),
            pl.BlockSpec((N_EXPERTS, LANES), lambda i: (0, 0)),
        ],
        out_specs=[
            pl.BlockSpec((8, tm), lambda i: (0, i)),
            pl.BlockSpec((8, nb_lanes), lambda i: (0, 0)),
        ],
        out_shape=[
            jax.ShapeDtypeStruct((8, t), I32),
            jax.ShapeDtypeStruct((8, nb_lanes), I32),
        ],
        compiler_params=pltpu.CompilerParams(
            dimension_semantics=("arbitrary",), vmem_limit_bytes=VMEM_LIMIT),
        name="route_meta",
    )(eidx, rank, cnt)


SC_WINDOW = LANES
N_QUARTERS = 4
QUARTER = D_MODEL // N_QUARTERS
N_PACKED = D_MODEL // 2 // QUARTER


def _sc_mesh():
    return plsc.VectorSubcoreMesh(core_axis_name="c", subcore_axis_name="s")


def _dispatch(dest_km, hp, n_rows):
    t = hp.shape[0]

    @functools.partial(
        pl.kernel, out_type=[jax.ShapeDtypeStruct((n_rows, QUARTER), hp.dtype)] * N_PACKED,
        mesh=_sc_mesh(), scratch_types=[])
    def scatter_rows(h_hbm, d_hbm, *xs_hbm):
        for q in range(N_PACKED):
            def body(x_vmem, i_vmem, q=q):
                for k in range(TOP_K):
                    pltpu.sync_copy(x_vmem, xs_hbm[q].at[i_vmem.at[k]])

            pltpu.emit_pipeline(
                body,
                grid=(t // SC_WINDOW,),
                in_specs=[pl.BlockSpec((SC_WINDOW, QUARTER), lambda i, q=q: (i, q)),
                          pl.BlockSpec((TOP_K, SC_WINDOW), lambda i: (0, i))],
                out_specs=[],
                core_axis_name=("c", "s"),
                dimension_semantics=(pltpu.PARALLEL,),
            )(h_hbm, d_hbm)

    return scatter_rows(hp, dest_km)


def _gather_back(dest_km, ys):
    t = dest_km.shape[1]
    n = len(ys)

    @functools.partial(
        pl.kernel,
        out_type=[jax.ShapeDtypeStruct((t, TOP_K * QUARTER), ys[0].dtype)] * n,
        mesh=_sc_mesh(), scratch_types=[])
    def gather_rows(d_hbm, *refs):
        ys_hbm, out_hbm = refs[:n], refs[n:]
        for q in range(n):
            for k in range(TOP_K):
                def body(i_vmem, o_vmem, q=q, k=k):
                    pltpu.sync_copy(ys_hbm[q].at[i_vmem.at[k]], o_vmem)

                pltpu.emit_pipeline(
                    body,
                    grid=(t // SC_WINDOW,),
                    in_specs=[pl.BlockSpec((TOP_K, SC_WINDOW), lambda i: (0, i))],
                    out_specs=[pl.BlockSpec((SC_WINDOW, QUARTER), lambda i, k=k: (i, k))],
                    core_axis_name=("c", "s"),
                    dimension_semantics=(pltpu.PARALLEL,),
                )(d_hbm, out_hbm[q])

    return gather_rows(dest_km, *ys)


def _experts_kernel(be_ref, used_ref, *refs):
    xs_refs = refs[:N_PACKED]
    wgu_hbm, wdn_hbm, bg_ref, bl_ref, bd_ref, perm_ref = refs[N_PACKED:N_PACKED + 6]
    ys_refs = refs[N_PACKED + 6:2 * N_PACKED + 6]
    wgu_st, wdn_st, wg_ref, wl_ref, wd_ref, sem = refs[2 * N_PACKED + 6:]
    p = pl.program_id(0)
    used = used_ref[0]
    b0 = 2 * p
    b1 = b0 + 1
    e0 = be_ref[b0]
    e1 = be_ref[b1]
    act0 = b0 < used
    act1 = b1 < used
    fresh0 = jnp.logical_and(act0, jnp.logical_or(p == 0, e0 != be_ref[jnp.maximum(b0 - 1, 0)]))
    same = jnp.logical_and(act1, e1 == e0)
    fresh1 = jnp.logical_and(act1, e1 != e0)

    def weight_copies(ex):
        return (pltpu.make_async_copy(wgu_hbm.at[ex], wgu_st, sem.at[0]),
                pltpu.make_async_copy(wdn_hbm.at[ex], wdn_st, sem.at[1]))

    def load_expert(ex, blk):
        for cp in weight_copies(ex):
            cp.wait()
        perm = perm_ref[...]
        half = LANES
        for c in range(2 * D_EXPERT // (2 * half)):
            slab = wgu_st[:, c * 2 * half:(c + 1) * 2 * half].astype(BF16)
            sep = _dot(slab, perm)
            wg_ref[:, c * half:(c + 1) * half] = sep[:, 0:half].astype(BF16)
            wl_ref[:, c * half:(c + 1) * half] = sep[:, half:2 * half].astype(BF16)
        wd_ref[...] = wdn_st[...].astype(BF16)

        def same_expert(j):
            return jnp.logical_and(j < used, be_ref[jnp.minimum(j, used - 1)] == ex)

        nxt = lax.while_loop(same_expert, lambda j: j + 1, blk + 1)

        @pl.when(nxt < used)
        def _():
            for cp in weight_copies(be_ref[nxt]):
                cp.start()

    def compute(ex, r0, nrows):
        rows = slice(r0, r0 + nrows)
        xb = _unpack_bf16_pairs(jnp.concatenate([r[rows, :] for r in xs_refs], axis=1))
        glu = jnp.minimum(_dot(xb, wg_ref[...]) + bg_ref[ex], SWIGLU_LIMIT)
        lin = jnp.clip(_dot(xb, wl_ref[...]) + bl_ref[ex], -SWIGLU_LIMIT, SWIGLU_LIMIT)
        act = glu * _sigmoid(SWIGLU_ALPHA * glu) * (lin + 1.0)
        y = _pack_bf16_pairs(_dot(act.astype(BF16), wd_ref[...]) + bd_ref[ex])
        for q, ys_ref in enumerate(ys_refs):
            ys_ref[rows, :] = y[:, q * QUARTER:(q + 1) * QUARTER]

    @pl.when(jnp.logical_and(p == 0, used > 0))
    def _():
        for cp in weight_copies(e0):
            cp.start()

    @pl.when(fresh0)
    def _():
        load_expert(e0, b0)

    @pl.when(jnp.logical_and(act0, same))
    def _():
        compute(e0, 0, 2 * MOE_BLOCK)

    @pl.when(jnp.logical_and(act0, jnp.logical_not(same)))
    def _():
        compute(e0, 0, MOE_BLOCK)

    @pl.when(fresh1)
    def _():
        load_expert(e1, b1)
        compute(e1, MOE_BLOCK, MOE_BLOCK)

    for blk_active, r0 in ((act0, 0), (act1, MOE_BLOCK)):
        @pl.when(jnp.logical_not(blk_active))
        def _(r0=r0):
            for ys_ref in ys_refs:
                ys_ref[r0:r0 + MOE_BLOCK, :] = jnp.zeros((MOE_BLOCK, QUARTER), U32)


def _experts(block_e, used, xs, w_gu, w_dn, b_g, b_l, b_d, perm):
    n_rows = xs[0].shape[0]
    n_blocks = n_rows // MOE_BLOCK
    assert n_blocks % 2 == 0
    wmap = lambda i, be, u: (0, 0, 0)
    rows = pl.BlockSpec((2 * MOE_BLOCK, QUARTER), lambda i, be, u: (i, 0))
    return pl.pallas_call(
        _experts_kernel,
        grid_spec=pltpu.PrefetchScalarGridSpec(
            num_scalar_prefetch=2,
            grid=(n_blocks // 2,),
            in_specs=[rows] * N_PACKED + [
                pl.BlockSpec(memory_space=pl.ANY),
                pl.BlockSpec(memory_space=pl.ANY),
                pl.BlockSpec((N_EXPERTS, 1, D_EXPERT), wmap),
                pl.BlockSpec((N_EXPERTS, 1, D_EXPERT), wmap),
                pl.BlockSpec((N_EXPERTS, 1, D_MODEL), wmap),
                pl.BlockSpec((2 * LANES, 2 * LANES), lambda i, be, u: (0, 0)),
            ],
            out_specs=[rows] * N_PACKED,
            scratch_shapes=[
                pltpu.VMEM((D_MODEL, 2 * D_EXPERT), F32),
                pltpu.VMEM((D_EXPERT, D_MODEL), F32),
                pltpu.VMEM((D_MODEL, D_EXPERT), BF16),
                pltpu.VMEM((D_MODEL, D_EXPERT), BF16),
                pltpu.VMEM((D_EXPERT, D_MODEL), BF16),
                pltpu.SemaphoreType.DMA((2,)),
            ],
        ),
        out_shape=[jax.ShapeDtypeStruct((n_rows, QUARTER), U32)] * N_PACKED,
        compiler_params=pltpu.CompilerParams(
            dimension_semantics=("arbitrary",), vmem_limit_bytes=VMEM_LIMIT),
        name="experts",
    )(block_e, used, *xs, w_gu, w_dn, b_g, b_l, b_d, perm)


def _combine_kernel(h_ref, gate_ref, g_ref, b_ref, *refs, tm, alpha):
    yg_refs, o_ref = refs[:N_PACKED], refs[N_PACKED]
    gate = jnp.transpose(gate_ref[...])
    gks = [gate[:, k:k + 1] for k in range(TOP_K)]
    lows, highs = [], []
    for yg_ref in yg_refs:
        lo = jnp.zeros((tm, QUARTER), F32)
        hi = jnp.zeros((tm, QUARTER), F32)
        for k in range(TOP_K):
            w = yg_ref[:, k * QUARTER:(k + 1) * QUARTER]
            lo = lo + gks[k] * lax.bitcast_convert_type(w << 16, F32)
            hi = hi + gks[k] * lax.bitcast_convert_type(w & jnp.uint32(0xFFFF0000), F32)
        lows.append(lo)
        highs.append(hi)
    ffn = jnp.concatenate(lows + highs, axis=1)
    o_ref[...] = _layer_norm(alpha * h_ref[...] + ffn, g_ref[...], b_ref[...])


def _combine(h, gate, g, b, yg, alpha):
    t = h.shape[0]
    tm = 256
    kern = functools.partial(_combine_kernel, tm=tm, alpha=alpha)
    tok = lambda i: (i, 0)
    return pl.pallas_call(
        kern,
        grid=(t // tm,),
        in_specs=[
            pl.BlockSpec((tm, D_MODEL), tok),
            pl.BlockSpec((8, tm), lambda i: (0, i)),
            pl.BlockSpec((1, D_MODEL), lambda i: (0, 0)),
            pl.BlockSpec((1, D_MODEL), lambda i: (0, 0)),
        ] + [pl.BlockSpec((tm, TOP_K * QUARTER), tok)] * N_PACKED,
        out_specs=pl.BlockSpec((tm, D_MODEL), tok),
        out_shape=jax.ShapeDtypeStruct((t, D_MODEL), F32),
        compiler_params=pltpu.CompilerParams(
            dimension_semantics=("arbitrary",), vmem_limit_bytes=VMEM_LIMIT),
        name="combine",
    )(h, gate, g, b, *yg)


def _column_split_permutation():
    n = 2 * LANES
    p = np.zeros((n, n), np.float32)
    j = np.arange(n)
    p[j, j // 2 + LANES * (j % 2)] = 1.0
    return jnp.asarray(p, dtype=BF16)


def kernel(x, mem, w_in, lam_q1, lam_k1, lam_q2, lam_k2, diff_norm_w, hgrn_lb_fwd,
           hgrn_lb_bwd, hgrn_norm_w, w_mem_kv, w_o, ln1_g, ln1_b, router_w, router_b,
           w_gate_up, b_gate_up, w_down, b_down, ln2_g, ln2_b):
    batch, seq, d = x.shape
    mlen = mem.shape[1]
    assert d == D_MODEL and w_in.shape == (N_LAYERS, D_MODEL, D_IN_PROJ)
    assert hgrn_lb_fwd.shape == (N_LAYERS + 1, D_HGRN)
    t = batch * seq
    alpha = (2.0 * N_LAYERS) ** 0.25
    lam_init = 0.8 - 0.6 * math.exp(-0.3 * 0)

    x2d = x.reshape(t, D_MODEL)
    ua, uh, um = _in_proj(x2d, w_in[0].astype(BF16), hgrn_lb_fwd, hgrn_lb_bwd)
    o_diff = _diff_attn(ua, diff_norm_w, lam_q1, lam_k1, lam_q2, lam_k2,
                        batch, seq, lam_init)
    o_mem = _mem_attn(um, mem.reshape(batch * mlen, D_MODEL), w_mem_kv[0].astype(BF16),
                      batch, seq, mlen)
    o_hgrn = _hgrn2(uh, hgrn_norm_w, batch, seq)

    h1, h1_packed, eidx, gate, rank, cnt = _post_mixer(
        x2d, o_diff, o_hgrn, o_mem, w_o[0].astype(BF16), ln1_g, ln1_b, router_w[0].T,
        router_b.reshape(N_EXPERTS, 1), alpha)

    n_blocks = -(-(t * TOP_K) // MOE_BLOCK) + N_EXPERTS
    dest, meta = _route_meta(eidx, rank, cnt, n_blocks)
    dest_km = dest[0:TOP_K]
    block_e = meta[0, 0:n_blocks]
    used = meta[1, 0:1]

    xs = _dispatch(dest_km, h1_packed, n_blocks * MOE_BLOCK)
    b_g = b_gate_up[0][:, 0::2].reshape(N_EXPERTS, 1, D_EXPERT)
    b_l = b_gate_up[0][:, 1::2].reshape(N_EXPERTS, 1, D_EXPERT)
    ys = _experts(block_e, used, xs, w_gate_up[0], w_down[0], b_g, b_l,
                  b_down[0].reshape(N_EXPERTS, 1, D_MODEL), _column_split_permutation())
    out = _combine(h1, gate, ln2_g, ln2_b, _gather_back(dest_km, ys), alpha)
    return out.reshape(batch, seq, D_MODEL)
```

```python
import functools
import math

import jax
import jax.numpy as jnp
import numpy as np
from jax import lax
from jax.experimental import pallas as pl
from jax.experimental.pallas import tpu as pltpu
from jax.experimental.pallas import tpu_sc as plsc

F32 = jnp.float32
BF16 = jnp.bfloat16
I32 = jnp.int32
U32 = jnp.uint32

D_MODEL = 1024
N_LAYERS = 1
D_DIFF = 512
D_HGRN = 256
D_XMEM = 256
N_HEADS = 4
DH = 64
DH_SHIFT = 6
D_IN_PROJ = 3 * D_DIFF + 5 * D_HGRN + D_XMEM
CHUNK = 16
N_EXPERTS = 32
TOP_K = 4
D_EXPERT = 1024
MOE_BLOCK = 256
SWIGLU_ALPHA = 1.702
SWIGLU_LIMIT = 7.0
NORM_EPS = 1e-5
LOG2E = math.log2(math.e)
LANES = 128
VMEM_LIMIT = 56 * 1024 * 1024


def _nt_dot(a, b):
    return lax.dot_general(a, b, (((1,), (1,)), ((), ())), preferred_element_type=F32)


def _tn_dot(a, b):
    return lax.dot_general(a, b, (((0,), (0,)), ((), ())), preferred_element_type=F32)


def _dot(a, b):
    return jnp.dot(a, b, preferred_element_type=F32)


def _sigmoid(x):
    return 1.0 / (1.0 + jnp.exp(-x))


def _layer_norm(y, g, b):
    mu = jnp.mean(y, axis=-1, keepdims=True)
    yc = y - mu
    var = jnp.mean(yc * yc, axis=-1, keepdims=True)
    return yc * lax.rsqrt(var + NORM_EPS) * g + b


def _in_proj_kernel(x_ref, w_ref, lbf_ref, lbb_ref, ua_ref, uh_ref, um_ref):
    xb = x_ref[...].astype(BF16)

    def proj(c0, width):
        return _dot(xb, w_ref[:, c0:c0 + width])

    ua_ref[:, 0:D_DIFF] = (proj(0, D_DIFF) * (LOG2E / math.sqrt(DH))).astype(BF16)
    ua_ref[:, D_DIFF:2 * D_DIFF] = proj(D_DIFF, D_DIFF).astype(BF16)
    ua_ref[:, 2 * D_DIFF:3 * D_DIFF] = proj(2 * D_DIFF, D_DIFF).astype(BF16)

    def lower_bound(lb_ref):
        a = lb_ref[0:1, :]
        b = lb_ref[1:2, :]
        m = jnp.maximum(a, b)
        ea = jnp.exp(a - m)
        eb = jnp.exp(b - m)
        return ea / (ea + eb)

    base = 3 * D_DIFF
    hq = proj(base, D_HGRN)
    uh_ref[:, 0:D_HGRN] = hq * _sigmoid(hq)
    for d, lb_ref in enumerate((lbf_ref, lbb_ref)):
        lb = lower_bound(lb_ref)
        f = lb + (1.0 - lb) * _sigmoid(proj(base + (1 + d) * D_HGRN, D_HGRN))
        uh_ref[:, (1 + 2 * d) * D_HGRN:(2 + 2 * d) * D_HGRN] = 1.0 - f
        uh_ref[:, (2 + 2 * d) * D_HGRN:(3 + 2 * d) * D_HGRN] = jnp.log(f)
    uh_ref[:, 5 * D_HGRN:6 * D_HGRN] = proj(base + 3 * D_HGRN, D_HGRN)
    uh_ref[:, 6 * D_HGRN:7 * D_HGRN] = _sigmoid(proj(base + 4 * D_HGRN, D_HGRN))
    um_ref[...] = proj(base + 5 * D_HGRN, D_XMEM).astype(BF16)


def _in_proj(x2d, w_in_bf, lb_f, lb_b):
    t = x2d.shape[0]
    tm = 1024
    return pl.pallas_call(
        _in_proj_kernel,
        grid=(t // tm,),
        in_specs=[
            pl.BlockSpec((tm, D_MODEL), lambda i: (i, 0)),
            pl.BlockSpec((D_MODEL, D_IN_PROJ), lambda i: (0, 0)),
            pl.BlockSpec((N_LAYERS + 1, D_HGRN), lambda i: (0, 0)),
            pl.BlockSpec((N_LAYERS + 1, D_HGRN), lambda i: (0, 0)),
        ],
        out_specs=[
            pl.BlockSpec((tm, 3 * D_DIFF), lambda i: (i, 0)),
            pl.BlockSpec((tm, 7 * D_HGRN), lambda i: (i, 0)),
            pl.BlockSpec((tm, D_XMEM), lambda i: (i, 0)),
        ],
        out_shape=[
            jax.ShapeDtypeStruct((t, 3 * D_DIFF), BF16),
            jax.ShapeDtypeStruct((t, 7 * D_HGRN), F32),
            jax.ShapeDtypeStruct((t, D_XMEM), BF16),
        ],
        compiler_params=pltpu.CompilerParams(
            dimension_semantics=("arbitrary",), vmem_limit_bytes=VMEM_LIMIT),
        name="in_proj",
    )(x2d, w_in_bf, lb_f, lb_b)


def _bf16_pieces(x):
    x = np.asarray(x, np.float32)
    out = []
    for _ in range(3):
        p = x.astype(BF16).astype(np.float32)
        out.append(p)
        x = x - p
    return np.stack(out, axis=-1)


def _alibi_tables(seq, t):
    slopes = (2.0 ** (-8.0 * np.arange(1, N_HEADS + 1) / N_HEADS)).astype(np.float32)
    c3 = _bf16_pieces(slopes * np.float32(LOG2E))
    pos = np.arange(seq)
    hi = (pos >> DH_SHIFT).astype(np.float32)
    lo = (pos & (DH - 1)).astype(np.float32)
    qf = np.zeros((N_HEADS, 2, seq, 2 * DH), np.float32)
    kf = np.zeros((N_HEADS, 2, seq, 2 * DH), np.float32)
    for m, base in enumerate((DH, 0)):
        for j in range(3):
            qf[:, m, :, base + j] = hi
            qf[:, m, :, base + 3 + j] = lo
            qf[:, m, :, base + 6 + j] = 64.0 * c3[:, j, None]
            qf[:, m, :, base + 9 + j] = c3[:, j, None]
            kf[:, m, :, base + j] = -64.0 * c3[:, j, None]
            kf[:, m, :, base + 3 + j] = -c3[:, j, None]
            kf[:, m, :, base + 6 + j] = hi
            kf[:, m, :, base + 9 + j] = lo
    c = c3.sum(axis=-1)
    kk = np.arange(t)[:, None]
    qq = np.arange(t)[None, :]
    corr = 2.0 * c[:, None, None] * np.minimum(qq - kk, 0).astype(np.float32)
    return (jnp.asarray(qf, dtype=BF16), jnp.asarray(kf, dtype=BF16),
            jnp.asarray(corr, dtype=F32))


def _diff_attn_kernel(q_ref, k_ref, v_ref, qf_ref, kf_ref, corr_ref, nw_ref, lq1_ref,
                      lk1_ref, lq2_ref, lk2_ref, o_ref, ka1_ref, ka2_ref, vt_ref, acc_ref,
                      m_ref, l_ref, *, t, qs, ks, qpt, seq, lam_init):
    i = pl.program_id(2)
    nk = seq // t
    lane = lax.broadcasted_iota(I32, (t, 2 * DH), 1)
    first_half = lane < DH

    @pl.when(i == 0)
    def _():
        def build(r, carry):
            rows = pl.ds(pl.multiple_of(r * t, t), t)
            kblk = k_ref[rows, :]
            ka1_ref[rows, :] = jnp.where(first_half, kblk, kf_ref[0, 0, rows, :])
            ka2_ref[rows, :] = jnp.where(first_half, kf_ref[0, 1, rows, :], kblk)
            vt_ref[r] = v_ref[rows, :].astype(F32).T.astype(BF16)
            return carry

        lax.fori_loop(0, nk, build, 0)

    lam = (jnp.exp(jnp.sum(lq1_ref[...] * lk1_ref[...], axis=-1, keepdims=True))
           - jnp.exp(jnp.sum(lq2_ref[...] * lk2_ref[...], axis=-1, keepdims=True))
           + lam_init)

    for qt in range(qpt):
        qi = i * qpt + qt
        qrange = slice(qt * t, (qt + 1) * t)
        q = q_ref[qrange, :]
        qf1 = qf_ref[0, 0, qrange, :]
        qf2 = qf_ref[0, 1, qrange, :]
        qa1_before = jnp.where(first_half, q, qf1)
        qa1_after = jnp.where(first_half, q, -qf1)
        qa2_before = jnp.where(first_half, qf2, q)
        qa2_after = jnp.where(first_half, -qf2, q)

        m_ref[qt] = jnp.full(m_ref.shape[1:], -jnp.inf, F32)
        l_ref[qt] = jnp.zeros(l_ref.shape[1:], F32)
        acc_ref[qt] = jnp.zeros(acc_ref.shape[1:], F32)

        def chunk(j, qa1, qa2, diagonal, qt=qt):
            k0 = pl.multiple_of(j * t, t)
            for mp, (qa, ka_ref) in enumerate(((qa1, ka1_ref), (qa2, ka2_ref))):
                for u in range(t // qs):
                    qrows = slice(u * qs, (u + 1) * qs)
                    cols = slice(mp * t + u * qs, mp * t + (u + 1) * qs)
                    for kb in range(t // ks):
                        krows = slice(kb * ks, (kb + 1) * ks)
                        s = _nt_dot(ka_ref[pl.ds(k0 + kb * ks, ks), :], qa[qrows, :])
                        if diagonal:
                            s = s + corr_ref[0, krows, qrows]
                        m_old = m_ref[qt, :, cols]
                        m_new = jnp.maximum(m_old, jnp.max(s, axis=0, keepdims=True))
                        p = jnp.exp2(s - m_new)
                        r = jnp.exp2(m_old - m_new)
                        l_ref[qt, :, cols] = (r * l_ref[qt, :, cols]
                                              + jnp.sum(p, axis=0, keepdims=True))
                        acc_ref[qt, mp, :, qrows] = (
                            r * acc_ref[qt, mp, :, qrows]
                            + _dot(vt_ref[j, :, krows], p.astype(BF16)))
                        m_ref[qt, :, cols] = m_new

        chunk(qi, qa1_before, qa2_before, True)
        for jj in range(nk - 1):
            j = jj + (jj >= qi).astype(I32)
            keys_first = j < qi
            chunk(j, jnp.where(keys_first, qa1_before, qa1_after),
                  jnp.where(keys_first, qa2_before, qa2_after), False)

        o = (acc_ref[qt, 0] / l_ref[qt, :, 0:t]
             - lam * (acc_ref[qt, 1] / l_ref[qt, :, t:2 * t]))
        o = o * lax.rsqrt(jnp.mean(o * o, axis=0, keepdims=True) + NORM_EPS)
        o_ref[qrange, :] = (o.T * nw_ref[...] * (1.0 - lam_init)).astype(o_ref.dtype)


def _diff_attn(ua, nw, lq1, lk1, lq2, lk2, batch, seq, lam_init):
    t = 512
    qpt = 2
    nq = seq // (t * qpt)
    qf, kf, corr = _alibi_tables(seq, t)
    kern = functools.partial(_diff_attn_kernel, t=t, qs=LANES, ks=256, qpt=qpt, seq=seq,
                             lam_init=lam_init)
    small = lambda b, h, i: (0, 0)
    return pl.pallas_call(
        kern,
        grid_spec=pltpu.PrefetchScalarGridSpec(
            num_scalar_prefetch=0,
            grid=(batch, N_HEADS, nq),
            in_specs=[
                pl.BlockSpec((qpt * t, 2 * DH), lambda b, h, i: (b * nq + i, h)),
                pl.BlockSpec((seq, 2 * DH), lambda b, h, i: (b, N_HEADS + h)),
                pl.BlockSpec((seq, 2 * DH), lambda b, h, i: (b, 2 * N_HEADS + h)),
                pl.BlockSpec((1, 2, qpt * t, 2 * DH), lambda b, h, i: (h, 0, i, 0)),
                pl.BlockSpec((1, 2, seq, 2 * DH), lambda b, h, i: (h, 0, 0, 0)),
                pl.BlockSpec((1, t, t), lambda b, h, i: (h, 0, 0)),
                pl.BlockSpec((1, 2 * DH), small),
                pl.BlockSpec((1, DH), small),
                pl.BlockSpec((1, DH), small),
                pl.BlockSpec((1, DH), small),
                pl.BlockSpec((1, DH), small),
            ],
            out_specs=pl.BlockSpec((qpt * t, 2 * DH), lambda b, h, i: (b * nq + i, h)),
            scratch_shapes=[
                pltpu.VMEM((seq, 2 * DH), BF16),
                pltpu.VMEM((seq, 2 * DH), BF16),
                pltpu.VMEM((seq // t, 2 * DH, t), BF16),
                pltpu.VMEM((qpt, 2, 2 * DH, t), F32),
                pltpu.VMEM((qpt, 1, 2 * t), F32),
                pltpu.VMEM((qpt, 1, 2 * t), F32),
            ],
        ),
        out_shape=jax.ShapeDtypeStruct((batch * seq, D_DIFF), BF16),
        compiler_params=pltpu.CompilerParams(
            dimension_semantics=("arbitrary", "arbitrary", "arbitrary"),
            vmem_limit_bytes=VMEM_LIMIT),
        name="diff_attn",
    )(ua, ua, ua, qf, kf, corr, nw, lq1, lk1, lq2, lk2)


def _mem_attn_kernel(q_ref, mem_ref, wkv_ref, o_ref, kv_ref, *, tq, mlen):
    @pl.when(pl.program_id(1) == 0)
    def _():
        kv_ref[...] = _dot(mem_ref[...].astype(BF16), wkv_ref[...]).astype(BF16)

    q = q_ref[...]
    mk = kv_ref[:, 0:D_XMEM]
    mv = kv_ref[:, D_XMEM:2 * D_XMEM]
    qhead = lax.broadcasted_iota(I32, (tq, D_XMEM), 1) >> DH_SHIFT
    vhead = lax.broadcasted_iota(I32, (mlen, D_XMEM), 1) >> DH_SHIFT
    acc = jnp.zeros((tq, D_XMEM), F32)
    for h in range(N_HEADS):
        qh = jnp.where(qhead == h, q, jnp.zeros_like(q))
        s = _nt_dot(qh, mk) * (1.0 / math.sqrt(DH))
        e = jnp.exp(s - jnp.max(s, axis=-1, keepdims=True))
        p = e / jnp.sum(e, axis=-1, keepdims=True)
        vh = jnp.where(vhead == h, mv, jnp.zeros_like(mv))
        acc = acc + _dot(p.astype(BF16), vh)
    o_ref[...] = acc.astype(o_ref.dtype)


def _mem_attn(um, mem2d, wkv_bf, batch, seq, mlen):
    tq = 1024
    nq = seq // tq
    kern = functools.partial(_mem_attn_kernel, tq=tq, mlen=mlen)
    return pl.pallas_call(
        kern,
        grid=(batch, nq),
        in_specs=[
            pl.BlockSpec((tq, D_XMEM), lambda b, i: (b * nq + i, 0)),
            pl.BlockSpec((mlen, D_MODEL), lambda b, i: (b, 0)),
            pl.BlockSpec((D_MODEL, 2 * D_XMEM), lambda b, i: (0, 0)),
        ],
        out_specs=pl.BlockSpec((tq, D_XMEM), lambda b, i: (b * nq + i, 0)),
        out_shape=jax.ShapeDtypeStruct((batch * seq, D_XMEM), BF16),
        scratch_shapes=[pltpu.VMEM((mlen, 2 * D_XMEM), BF16)],
        compiler_params=pltpu.CompilerParams(
            dimension_semantics=("arbitrary", "arbitrary"), vmem_limit_bytes=VMEM_LIMIT),
        name="mem_attn",
    )(um, mem2d, wkv_bf)


def _hgrn2_kernel(q_ref, kf_ref, lf_ref, kb_ref, lb_ref, v_ref, g_ref, nw_ref, o_ref,
                  kfp_ref, kbp_ref, vp_ref, gf_ref, gb_ref, bf_ref, bb_ref, acc_ref, cross_ref,
                  *, seq, rb):
    n_chunks = seq // CHUNK
    w = 2 * DH
    sb = 128
    row = lax.broadcasted_iota(I32, (rb, w), 0) & (CHUNK - 1)
    li = lax.broadcasted_iota(I32, (w, w), 0) >> DH_SHIFT
    lj = lax.broadcasted_iota(I32, (w, w), 1) >> DH_SHIFT
    same_head = li == lj
    head_ones = jnp.where(same_head, 1.0, 0.0).astype(BF16)

    def chunk_cumsum(x, reverse):
        for sh in (1, 2, 4, 8):
            if reverse:
                moved = pltpu.roll(x, rb - sh, 0)
                keep = row + sh < CHUNK
            else:
                moved = pltpu.roll(x, sh, 0)
                keep = row >= sh
            x = x + jnp.where(keep, moved, 0.0)
        return x

    halo = jnp.zeros((CHUNK, w), F32)
    for ref in (kfp_ref, kbp_ref, vp_ref, gf_ref, gb_ref, bf_ref, bb_ref):
        ref[0:CHUNK, :] = halo
        ref[seq + CHUNK:seq + 2 * CHUNK, :] = halo

    def prepare(blk, carry):
        r0 = pl.multiple_of(blk * rb, rb)
        rows = pl.ds(r0, rb)
        inner = pl.ds(r0 + CHUNK, rb)
        kfp_ref[inner, :] = kf_ref[rows, :]
        kbp_ref[inner, :] = kb_ref[rows, :]
        vp_ref[inner, :] = v_ref[rows, :]
        lf2 = lf_ref[rows, :] * LOG2E
        lb2 = lb_ref[rows, :] * LOG2E
        gf_ref[inner, :] = jnp.where(row == 0, 0.0, jnp.exp2(lf2))
        gb_ref[inner, :] = jnp.where(row == CHUNK - 1, 0.0, jnp.exp2(lb2))
        bf_ref[inner, :] = chunk_cumsum(lf2, False)
        bb_ref[inner, :] = chunk_cumsum(lb2, True)
        return carry

    lax.fori_loop(0, seq // rb, prepare, 0)

    def intra(blk):
        for part in range(rb // sb):
            r0 = pl.multiple_of(blk * rb + part * sb, sb)
            rows = pl.ds(r0, sb)
            q = q_ref[rows, :]
            acc = jnp.zeros((sb, w), F32)
            for reverse, kp_ref, g_ref2 in ((False, kfp_ref, gf_ref), (True, kbp_ref, gb_ref)):
                dec = None
                for d in range(CHUNK):
                    src = pl.ds(r0 + CHUNK + (d if reverse else -d), sb)
                    ks = kp_ref[src, :]
                    vs = vp_ref[src, :]
                    if d == 0:
                        z = q * ks
                    else:
                        gsrc = pl.ds(r0 + CHUNK + ((d - 1) if reverse else -(d - 1)), sb)
                        gs = g_ref2[gsrc, :]
                        dec = gs if dec is None else dec * gs
                        z = q * ks * dec
                    acc = acc + _dot(z.astype(BF16), head_ones) * vs
            acc_ref[rows, :] = acc

    def chunk_updates(blk):
        per_dir = ([], [])
        for c in range(rb // CHUNK):
            n = blk * (rb // CHUNK) + c
            for d, (r0, kp_ref, b_ref, edge) in enumerate((
                    (n * CHUNK, kfp_ref, bf_ref, CHUNK - 1),
                    ((n_chunks - 1 - n) * CHUNK, kbp_ref, bb_ref, 0))):
                r0 = pl.multiple_of(r0, CHUNK)
                rows = pl.ds(r0, CHUNK)
                prow = pl.ds(r0 + CHUNK, CHUNK)
                b = b_ref[prow, :]
                b_edge = b_ref[pl.ds(r0 + CHUNK + edge, 1), :]
                qd = (q_ref[rows, :] * jnp.exp2(b)).astype(BF16)
                kd = kp_ref[prow, :] * jnp.exp2(b_edge - b)
                upd = _tn_dot(vp_ref[prow, :].astype(BF16), kd.astype(BF16))
                per_dir[d].append((rows, qd, jnp.exp2(b_edge), jnp.where(same_head, upd, 0.0)))
        return per_dir

    def state_chain(per_dir, states):
        for d in range(2):
            for rows, qd, decay, upd in per_dir[d]:
                cross_ref[rows, :] = cross_ref[rows, :] + _nt_dot(qd, states[d].astype(BF16))
                states[d] = states[d] * decay + upd

    cross_ref[...] = jnp.zeros(cross_ref.shape, F32)

    def fused(blk, carry):
        states = list(carry)
        intra(blk)
        state_chain(chunk_updates(blk), states)
        return tuple(states)

    z = jnp.zeros((w, w), F32)
    lax.fori_loop(0, seq // rb, fused, (z, z))

    def finish(blk, carry):
        rows = pl.ds(pl.multiple_of(blk * rb, rb), rb)
        o = acc_ref[rows, :] + cross_ref[rows, :]
        lane = lax.broadcasted_iota(I32, (rb, w), 1)
        lo = lane < DH
        sq = o * o
        ms_lo = jnp.sum(jnp.where(lo, sq, 0.0), axis=-1, keepdims=True)
        ms_hi = jnp.sum(jnp.where(lo, 0.0, sq), axis=-1, keepdims=True)
        ms = jnp.where(lo, ms_lo, ms_hi) * (1.0 / DH)
        o = o * lax.rsqrt(ms + NORM_EPS) * nw_ref[...] * g_ref[rows, :]
        o_ref[rows, :] = o.astype(o_ref.dtype)
        return carry

    lax.fori_loop(0, seq // rb, finish, 0)


def _hgrn2(uh, nw, batch, seq):
    w = 2 * DH
    npair = D_HGRN // w
    kern = functools.partial(_hgrn2_kernel, seq=seq, rb=256)

    def sec(s):
        return pl.BlockSpec((seq, w), lambda b, p, s=s: (b, s * npair + p))

    return pl.pallas_call(
        kern,
        grid=(batch, npair),
        in_specs=[sec(0), sec(1), sec(2), sec(3), sec(4), sec(5), sec(6),
                  pl.BlockSpec((1, w), lambda b, p: (0, p))],
        out_specs=pl.BlockSpec((seq, w), lambda b, p: (b, p)),
        out_shape=jax.ShapeDtypeStruct((batch * seq, D_HGRN), BF16),
        scratch_shapes=[pltpu.VMEM((seq + 2 * CHUNK, w), F32)] * 7
        + [pltpu.VMEM((seq, w), F32)] * 2,
        compiler_params=pltpu.CompilerParams(
            dimension_semantics=("arbitrary", "arbitrary"), vmem_limit_bytes=VMEM_LIMIT),
        name="hgrn2",
    )(uh, uh, uh, uh, uh, uh, uh, nw)


def _rows_to_block(rows, tm):
    r = lax.broadcasted_iota(I32, (8, tm), 0)
    out = jnp.zeros((8, tm), rows[0].dtype)
    for k, row in enumerate(rows):
        out = jnp.where(r == k, row, out)
    return out


def _pack_bf16_pairs(x):
    m = x.shape[1] // 2
    u = lax.bitcast_convert_type(x, U32)
    r = u + jnp.uint32(0x7FFF) + ((u >> 16) & jnp.uint32(1))
    return (r[:, 0:m] >> 16) | (r[:, m:2 * m] & jnp.uint32(0xFFFF0000))


def _unpack_bf16_pairs(w):
    lo = lax.bitcast_convert_type(w << 16, F32).astype(BF16)
    hi = lax.bitcast_convert_type(w & jnp.uint32(0xFFFF0000), F32).astype(BF16)
    return jnp.concatenate([lo, hi], axis=1)


def _post_mixer_kernel(x_ref, od_ref, oh_ref, om_ref, wo_ref, g_ref, b_ref, rwt_ref,
                       rbt_ref, h_ref, hp_ref, eidx_ref, gate_ref, rank_ref, cnt_ref,
                       carry_ref, *, tm, alpha):
    i = pl.program_id(0)

    @pl.when(i == 0)
    def _():
        carry_ref[...] = jnp.zeros_like(carry_ref)

    mix = (_dot(od_ref[...], wo_ref[0:D_DIFF, :])
           + _dot(oh_ref[...], wo_ref[D_DIFF:D_DIFF + D_HGRN, :])
           + _dot(om_ref[...], wo_ref[D_DIFF + D_HGRN:, :]))
    h = _layer_norm(alpha * x_ref[...] + mix, g_ref[...], b_ref[...])
    h_ref[...] = h

    h_hi = h.astype(BF16)
    hp_ref[...] = _pack_bf16_pairs(h)
    h_lo = (h - h_hi.astype(F32)).astype(BF16)
    rwt = rwt_ref[...]
    rwt_hi = rwt.astype(BF16)
    rwt_lo = (rwt - rwt_hi.astype(F32)).astype(BF16)
    logits = (_nt_dot(rwt_hi, h_hi) + _nt_dot(rwt_hi, h_lo) + _nt_dot(rwt_lo, h_hi)
              + rbt_ref[...])

    erow = lax.broadcasted_iota(I32, (N_EXPERTS, tm), 0).astype(F32)
    work = logits
    sels, vals, idxs = [], [], []
    for _ in range(TOP_K):
        m = jnp.max(work, axis=0, keepdims=True)
        idx = jnp.min(jnp.where(work == m, erow, float(N_EXPERTS)), axis=0, keepdims=True)
        sel = erow == idx
        work = jnp.where(sel, -jnp.inf, work)
        sels.append(sel)
        vals.append(m)
        idxs.append(idx)
    es = [jnp.exp(v - vals[0]) for v in vals]
    den = es[0] + es[1] + es[2] + es[3]

    chosen = jnp.where(sels[0] | sels[1] | sels[2] | sels[3], 1.0, 0.0)
    ti = lax.broadcasted_iota(I32, (tm, tm), 0)
    tj = lax.broadcasted_iota(I32, (tm, tm), 1)
    earlier = jnp.where(ti < tj, 1.0, 0.0).astype(BF16)
    prefix = _dot(chosen.astype(BF16), earlier) + carry_ref[:, 0:1]

    ranks = [jnp.sum(jnp.where(sels[k], prefix, 0.0), axis=0, keepdims=True)
             for k in range(TOP_K)]
    eidx_ref[...] = _rows_to_block(idxs, tm).astype(I32)
    gate_ref[...] = _rows_to_block([e / den for e in es], tm)
    rank_ref[...] = _rows_to_block(ranks, tm).astype(I32)

    total = carry_ref[:, 0:1] + jnp.sum(chosen, axis=1, keepdims=True)
    carry_ref[...] = jnp.broadcast_to(total, carry_ref.shape)
    cnt_ref[...] = carry_ref[...]


def _post_mixer(x2d, od, oh, om, wo_bf, g, b, rwt, rbt, alpha):
    t = x2d.shape[0]
    tm = 512
    kern = functools.partial(_post_mixer_kernel, tm=tm, alpha=alpha)
    full = lambda i: (0, 0)
    tok = lambda i: (i, 0)
    per_tok = lambda i: (0, i)
    return pl.pallas_call(
        kern,
        grid=(t // tm,),
        in_specs=[
            pl.BlockSpec((tm, D_MODEL), tok),
            pl.BlockSpec((tm, D_DIFF), tok),
            pl.BlockSpec((tm, D_HGRN), tok),
            pl.BlockSpec((tm, D_XMEM), tok),
            pl.BlockSpec((D_MODEL, D_MODEL), full),
            pl.BlockSpec((1, D_MODEL), full),
            pl.BlockSpec((1, D_MODEL), full),
            pl.BlockSpec((N_EXPERTS, D_MODEL), full),
            pl.BlockSpec((N_EXPERTS, 1), full),
        ],
        out_specs=[
            pl.BlockSpec((tm, D_MODEL), tok),
            pl.BlockSpec((tm, D_MODEL // 2), tok),
            pl.BlockSpec((8, tm), per_tok),
            pl.BlockSpec((8, tm), per_tok),
            pl.BlockSpec((8, tm), per_tok),
            pl.BlockSpec((N_EXPERTS, LANES), full),
        ],
        out_shape=[
            jax.ShapeDtypeStruct((t, D_MODEL), F32),
            jax.ShapeDtypeStruct((t, D_MODEL // 2), U32),
            jax.ShapeDtypeStruct((8, t), I32),
            jax.ShapeDtypeStruct((8, t), F32),
            jax.ShapeDtypeStruct((8, t), I32),
            jax.ShapeDtypeStruct((N_EXPERTS, LANES), F32),
        ],
        scratch_shapes=[pltpu.VMEM((N_EXPERTS, LANES), F32)],
        compiler_params=pltpu.CompilerParams(
            dimension_semantics=("arbitrary",), vmem_limit_bytes=VMEM_LIMIT),
        name="post_mixer",
    )(x2d, od, oh, om, wo_bf, g, b, rwt, rbt)


def _route_meta_kernel(eidx_ref, rank_ref, cnt_ref, dest_ref, be_ref, *, tm, nb_lanes):
    cnt = cnt_ref[...]
    blocks = jnp.floor((cnt + (MOE_BLOCK - 1)) * (1.0 / MOE_BLOCK))
    ei = lax.broadcasted_iota(I32, (N_EXPERTS, N_EXPERTS), 0)
    ej = lax.broadcasted_iota(I32, (N_EXPERTS, N_EXPERTS), 1)
    below = jnp.where(ej < ei, 1.0, 0.0).astype(BF16)
    pstart = _dot(below, blocks.astype(BF16))[:, 0:1]
    pend = pstart + blocks[:, 0:1]

    erow = lax.broadcasted_iota(I32, (N_EXPERTS, tm), 0).astype(F32)
    eidx = eidx_ref[...].astype(F32)
    starts = [jnp.sum(jnp.where(erow == eidx[k:k + 1, :], pstart, 0.0), axis=0, keepdims=True)
              for k in range(TOP_K)]
    dest_ref[...] = ((_rows_to_block(starts, tm) * float(MOE_BLOCK)).astype(I32)
                     + rank_ref[...])

    bi = lax.broadcasted_iota(I32, (N_EXPERTS, nb_lanes), 1).astype(F32)
    done = jnp.where(pend <= bi, 1.0, 0.0)
    be = jnp.minimum(jnp.sum(done, axis=0, keepdims=True), float(N_EXPERTS - 1))
    used = jnp.broadcast_to(pend[N_EXPERTS - 1:N_EXPERTS, :], (1, nb_lanes))
    be_ref[...] = _rows_to_block([be, used], nb_lanes).astype(I32)


def _route_meta(eidx, rank, cnt, n_blocks):
    t = eidx.shape[1]
    tm = 2048
    nb_lanes = -(-n_blocks // LANES) * LANES
    kern = functools.partial(_route_meta_kernel, tm=tm, nb_lanes=nb_lanes)
    return pl.pallas_call(
        kern,
        grid=(t // tm,),
        in_specs=[
            pl.BlockSpec((8, tm), lambda i: (0, i)),
            pl.BlockSpec((8, tm), lambda i: (0, i)),
            pl.BlockSpec((N_EXPERTS, LANES), lambda i: (0, 0)),
        ],
        out_specs=[
            pl.BlockSpec((8, tm), lambda i: (0, i)),
            pl.BlockSpec((8, nb_lanes), lambda i: (0, 0)),
        ],
        out_shape=[
            jax.ShapeDtypeStruct((8, t), I32),
            jax.ShapeDtypeStruct((8, nb_lanes), I32),
        ],
        compiler_params=pltpu.CompilerParams(
            dimension_semantics=("arbitrary",), vmem_limit_bytes=VMEM_LIMIT),
        name="route_meta",
    )(eidx, rank, cnt)


SC_WINDOW = LANES
N_QUARTERS = 4
QUARTER = D_MODEL // N_QUARTERS
N_PACKED = D_MODEL // 2 // QUARTER


def _sc_mesh():
    return plsc.VectorSubcoreMesh(core_axis_name="c", subcore_axis_name="s")


def _dispatch(dest_km, hp, n_rows):
    t = hp.shape[0]

    @functools.partial(
        pl.kernel, out_type=[jax.ShapeDtypeStruct((n_rows, QUARTER), hp.dtype)] * N_PACKED,
        mesh=_sc_mesh(), scratch_types=[])
    def scatter_rows(h_hbm, d_hbm, *xs_hbm):
        for q in range(N_PACKED):
            def body(x_vmem, i_vmem, q=q):
                for k in range(TOP_K):
                    pltpu.sync_copy(x_vmem, xs_hbm[q].at[i_vmem.at[k]])

            pltpu.emit_pipeline(
                body,
                grid=(t // SC_WINDOW,),
                in_specs=[pl.BlockSpec((SC_WINDOW, QUARTER), lambda i, q=q: (i, q)),
                          pl.BlockSpec((TOP_K, SC_WINDOW), lambda i: (0, i))],
                out_specs=[],
                core_axis_name=("c", "s"),
                dimension_semantics=(pltpu.PARALLEL,),
            )(h_hbm, d_hbm)

    return scatter_rows(hp, dest_km)


def _gather_back(dest_km, ys):
    t = dest_km.shape[1]
    n = len(ys)

    @functools.partial(
        pl.kernel,
        out_type=[jax.ShapeDtypeStruct((t, TOP_K * QUARTER), ys[0].dtype)] * n,
        mesh=_sc_mesh(), scratch_types=[])
    def gather_rows(d_hbm, *refs):
        ys_hbm, out_hbm = refs[:n], refs[n:]
        for q in range(n):
            for k in range(TOP_K):
                def body(i_vmem, o_vmem, q=q, k=k):
                    pltpu.sync_copy(ys_hbm[q].at[i_vmem.at[k]], o_vmem)

                pltpu.emit_pipeline(
                    body,
                    grid=(t // SC_WINDOW,),
                    in_specs=[pl.BlockSpec((TOP_K, SC_WINDOW), lambda i: (0, i))],
                    out_specs=[pl.BlockSpec((SC_WINDOW, QUARTER), lambda i, k=k: (i, k))],
                    core_axis_name=("c", "s"),
                    dimension_semantics=(pltpu.PARALLEL,),
                )(d_hbm, out_hbm[q])

    return gather_rows(dest_km, *ys)


def _experts_kernel(be_ref, used_ref, *refs):
    xs_refs = refs[:N_PACKED]
    wgu_hbm, wdn_hbm, bg_ref, bl_ref, bd_ref, perm_ref = refs[N_PACKED:N_PACKED + 6]
    ys_refs = refs[N_PACKED + 6:2 * N_PACKED + 6]
    wgu_st, wdn_st, wg_ref, wl_ref, wd_ref, sem = refs[2 * N_PACKED + 6:]
    p = pl.program_id(0)
    used = used_ref[0]
    b0 = 2 * p
    b1 = b0 + 1
    e0 = be_ref[b0]
    e1 = be_ref[b1]
    act0 = b0 < used
    act1 = b1 < used
    fresh0 = jnp.logical_and(act0, jnp.logical_or(p == 0, e0 != be_ref[jnp.maximum(b0 - 1, 0)]))
    same = jnp.logical_and(act1, e1 == e0)
    fresh1 = jnp.logical_and(act1, e1 != e0)

    def weight_copies(ex):
        return (pltpu.make_async_copy(wgu_hbm.at[ex], wgu_st, sem.at[0]),
                pltpu.make_async_copy(wdn_hbm.at[ex], wdn_st, sem.at[1]))

    def load_expert(ex, blk):
        for cp in weight_copies(ex):
            cp.wait()
        perm = perm_ref[...]
        half = LANES
        for c in range(2 * D_EXPERT // (2 * half)):
            slab = wgu_st[:, c * 2 * half:(c + 1) * 2 * half].astype(BF16)
            sep = _dot(slab, perm)
            wg_ref[:, c * half:(c + 1) * half] = sep[:, 0:half].astype(BF16)
            wl_ref[:, c * half:(c + 1) * half] = sep[:, half:2 * half].astype(BF16)
        wd_ref[...] = wdn_st[...].astype(BF16)

        def same_expert(j):
            return jnp.logical_and(j < used, be_ref[jnp.minimum(j, used - 1)] == ex)

        nxt = lax.while_loop(same_expert, lambda j: j + 1, blk + 1)

        @pl.when(nxt < used)
        def _():
            for cp in weight_copies(be_ref[nxt]):
                cp.start()

    def compute(ex, r0, nrows):
        rows = slice(r0, r0 + nrows)
        xb = _unpack_bf16_pairs(jnp.concatenate([r[rows, :] for r in xs_refs], axis=1))
        glu = jnp.minimum(_dot(xb, wg_ref[...]) + bg_ref[ex], SWIGLU_LIMIT)
        lin = jnp.clip(_dot(xb, wl_ref[...]) + bl_ref[ex], -SWIGLU_LIMIT, SWIGLU_LIMIT)
        act = glu * _sigmoid(SWIGLU_ALPHA * glu) * (lin + 1.0)
        y = _pack_bf16_pairs(_dot(act.astype(BF16), wd_ref[...]) + bd_ref[ex])
        for q, ys_ref in enumerate(ys_refs):
            ys_ref[rows, :] = y[:, q * QUARTER:(q + 1) * QUARTER]

    @pl.when(jnp.logical_and(p == 0, used > 0))
    def _():
        for cp in weight_copies(e0):
            cp.start()

    @pl.when(fresh0)
    def _():
        load_expert(e0, b0)

    @pl.when(jnp.logical_and(act0, same))
    def _():
        compute(e0, 0, 2 * MOE_BLOCK)

    @pl.when(jnp.logical_and(act0, jnp.logical_not(same)))
    def _():
        compute(e0, 0, MOE_BLOCK)

    @pl.when(fresh1)
    def _():
        load_expert(e1, b1)
        compute(e1, MOE_BLOCK, MOE_BLOCK)

    for blk_active, r0 in ((act0, 0), (act1, MOE_BLOCK)):
        @pl.when(jnp.logical_not(blk_active))
        def _(r0=r0):
            for ys_ref in ys_refs:
                ys_ref[r0:r0 + MOE_BLOCK, :] = jnp.zeros((MOE_BLOCK, QUARTER), U32)


def _experts(block_e, used, xs, w_gu, w_dn, b_g, b_l, b_d, perm):
    n_rows = xs[0].shape[0]
    n_blocks = n_rows // MOE_BLOCK
    assert n_blocks % 2 == 0
    wmap = lambda i, be, u: (0, 0, 0)
    rows = pl.BlockSpec((2 * MOE_BLOCK, QUARTER), lambda i, be, u: (i, 0))
    return pl.pallas_call(
        _experts_kernel,
        grid_spec=pltpu.PrefetchScalarGridSpec(
            num_scalar_prefetch=2,
            grid=(n_blocks // 2,),
            in_specs=[rows] * N_PACKED + [
                pl.BlockSpec(memory_space=pl.ANY),
                pl.BlockSpec(memory_space=pl.ANY),
                pl.BlockSpec((N_EXPERTS, 1, D_EXPERT), wmap),
                pl.BlockSpec((N_EXPERTS, 1, D_EXPERT), wmap),
                pl.BlockSpec((N_EXPERTS, 1, D_MODEL), wmap),
                pl.BlockSpec((2 * LANES, 2 * LANES), lambda i, be, u: (0, 0)),
            ],
            out_specs=[rows] * N_PACKED,
            scratch_shapes=[
                pltpu.VMEM((D_MODEL, 2 * D_EXPERT), F32),
                pltpu.VMEM((D_EXPERT, D_MODEL), F32),
                pltpu.VMEM((D_MODEL, D_EXPERT), BF16),
                pltpu.VMEM((D_MODEL, D_EXPERT), BF16),
                pltpu.VMEM((D_EXPERT, D_MODEL), BF16),
                pltpu.SemaphoreType.DMA((2,)),
            ],
        ),
        out_shape=[jax.ShapeDtypeStruct((n_rows, QUARTER), U32)] * N_PACKED,
        compiler_params=pltpu.CompilerParams(
            dimension_semantics=("arbitrary",), vmem_limit_bytes=VMEM_LIMIT),
        name="experts",
    )(block_e, used, *xs, w_gu, w_dn, b_g, b_l, b_d, perm)


def _combine_kernel(h_ref, gate_ref, g_ref, b_ref, *refs, tm, alpha):
    yg_refs, o_ref = refs[:N_PACKED], refs[N_PACKED]
    gate = jnp.transpose(gate_ref[...])
    gks = [gate[:, k:k + 1] for k in range(TOP_K)]
    lows, highs = [], []
    for yg_ref in yg_refs:
        lo = jnp.zeros((tm, QUARTER), F32)
        hi = jnp.zeros((tm, QUARTER), F32)
        for k in range(TOP_K):
            w = yg_ref[:, k * QUARTER:(k + 1) * QUARTER]
            lo = lo + gks[k] * lax.bitcast_convert_type(w << 16, F32)
            hi = hi + gks[k] * lax.bitcast_convert_type(w & jnp.uint32(0xFFFF0000), F32)
        lows.append(lo)
        highs.append(hi)
    ffn = jnp.concatenate(lows + highs, axis=1)
    o_ref[...] = _layer_norm(alpha * h_ref[...] + ffn, g_ref[...], b_ref[...])


def _combine(h, gate, g, b, yg, alpha):
    t = h.shape[0]
    tm = 512
    kern = functools.partial(_combine_kernel, tm=tm, alpha=alpha)
    tok = lambda i: (i, 0)
    return pl.pallas_call(
        kern,
        grid=(t // tm,),
        in_specs=[
            pl.BlockSpec((tm, D_MODEL), tok),
            pl.BlockSpec((8, tm), lambda i: (0, i)),
            pl.BlockSpec((1, D_MODEL), lambda i: (0, 0)),
            pl.BlockSpec((1, D_MODEL), lambda i: (0, 0)),
        ] + [pl.BlockSpec((tm, TOP_K * QUARTER), tok)] * N_PACKED,
        out_specs=pl.BlockSpec((tm, D_MODEL), tok),
        out_shape=jax.ShapeDtypeStruct((t, D_MODEL), F32),
        compiler_params=pltpu.CompilerParams(
            dimension_semantics=("arbitrary",), vmem_limit_bytes=VMEM_LIMIT),
        name="combine",
    )(h, gate, g, b, *yg)


def _column_split_permutation():
    n = 2 * LANES
    p = np.zeros((n, n), np.float32)
    j = np.arange(n)
    p[j, j // 2 + LANES * (j % 2)] = 1.0
    return jnp.asarray(p, dtype=BF16)


def kernel(x, mem, w_in, lam_q1, lam_k1, lam_q2, lam_k2, diff_norm_w, hgrn_lb_fwd,
           hgrn_lb_bwd, hgrn_norm_w, w_mem_kv, w_o, ln1_g, ln1_b, router_w, router_b,
           w_gate_up, b_gate_up, w_down, b_down, ln2_g, ln2_b):
    batch, seq, d = x.shape
    mlen = mem.shape[1]
    assert d == D_MODEL and w_in.shape == (N_LAYERS, D_MODEL, D_IN_PROJ)
    assert hgrn_lb_fwd.shape == (N_LAYERS + 1, D_HGRN)
    t = batch * seq
    alpha = (2.0 * N_LAYERS) ** 0.25
    lam_init = 0.8 - 0.6 * math.exp(-0.3 * 0)

    x2d = x.reshape(t, D_MODEL)
    ua, uh, um = _in_proj(x2d, w_in[0].astype(BF16), hgrn_lb_fwd, hgrn_lb_bwd)
    o_diff = _diff_attn(ua, diff_norm_w, lam_q1, lam_k1, lam_q2, lam_k2,
                        batch, seq, lam_init)
    o_mem = _mem_attn(um, mem.reshape(batch * mlen, D_MODEL), w_mem_kv[0].astype(BF16),
                      batch, seq, mlen)
    o_hgrn = _hgrn2(uh, hgrn_norm_w, batch, seq)

    h1, h1_packed, eidx, gate, rank, cnt = _post_mixer(
        x2d, o_diff, o_hgrn, o_mem, w_o[0].astype(BF16), ln1_g, ln1_b, router_w[0].T,
        router_b.reshape(N_EXPERTS, 1), alpha)

    n_blocks = -(-(t * TOP_K) // MOE_BLOCK) + N_EXPERTS
    dest, meta = _route_meta(eidx, rank, cnt, n_blocks)
    dest_km = dest[0:TOP_K]
    block_e = meta[0, 0:n_blocks]
    used = meta[1, 0:1]

    xs = _dispatch(dest_km, h1_packed, n_blocks * MOE_BLOCK)
    b_g = b_gate_up[0][:, 0::2].reshape(N_EXPERTS, 1, D_EXPERT)
    b_l = b_gate_up[0][:, 1::2].reshape(N_EXPERTS, 1, D_EXPERT)
    ys = _experts(block_e, used, xs, w_gate_up[0], w_down[0], b_g, b_l,
                  b_down[0].reshape(N_EXPERTS, 1, D_MODEL), _column_split_permutation())
    out = _combine(h1, gate, ln2_g, ln2_b, _gather_back(dest_km, ys), alpha)
    return out.reshape(batch, seq, D_MODEL)
```

```python
import functools
import math

import jax
import jax.numpy as jnp
import numpy as np
from jax import lax
from jax.experimental import pallas as pl
from jax.experimental.pallas import tpu as pltpu
from jax.experimental.pallas import tpu_sc as plsc

F32 = jnp.float32
BF16 = jnp.bfloat16
I32 = jnp.int32
U32 = jnp.uint32

D_MODEL = 1024
N_LAYERS = 1
D_DIFF = 512
D_HGRN = 256
D_XMEM = 256
N_HEADS = 4
DH = 64
DH_SHIFT = 6
D_IN_PROJ = 3 * D_DIFF + 5 * D_HGRN + D_XMEM
CHUNK = 16
N_EXPERTS = 32
TOP_K = 4
D_EXPERT = 1024
MOE_BLOCK = 256
SWIGLU_ALPHA = 1.702
SWIGLU_LIMIT = 7.0
NORM_EPS = 1e-5
LOG2E = math.log2(math.e)
LANES = 128
VMEM_LIMIT = 56 * 1024 * 1024


def _nt_dot(a, b):
    return lax.dot_general(a, b, (((1,), (1,)), ((), ())), preferred_element_type=F32)


def _tn_dot(a, b):
    return lax.dot_general(a, b, (((0,), (0,)), ((), ())), preferred_element_type=F32)


def _dot(a, b):
    return jnp.dot(a, b, preferred_element_type=F32)


def _sigmoid(x):
    return 1.0 / (1.0 + jnp.exp(-x))


def _layer_norm(y, g, b):
    mu = jnp.mean(y, axis=-1, keepdims=True)
    yc = y - mu
    var = jnp.mean(yc * yc, axis=-1, keepdims=True)
    return yc * lax.rsqrt(var + NORM_EPS) * g + b


def _in_proj_kernel(x_ref, w_ref, lbf_ref, lbb_ref, ua_ref, uh_ref, um_ref):
    xb = x_ref[...].astype(BF16)

    def proj(c0, width):
        return _dot(xb, w_ref[:, c0:c0 + width])

    ua_ref[:, 0:D_DIFF] = (proj(0, D_DIFF) * (LOG2E / math.sqrt(DH))).astype(BF16)
    ua_ref[:, D_DIFF:2 * D_DIFF] = proj(D_DIFF, D_DIFF).astype(BF16)
    ua_ref[:, 2 * D_DIFF:3 * D_DIFF] = proj(2 * D_DIFF, D_DIFF).astype(BF16)

    def lower_bound(lb_ref):
        a = lb_ref[0:1, :]
        b = lb_ref[1:2, :]
        m = jnp.maximum(a, b)
        ea = jnp.exp(a - m)
        eb = jnp.exp(b - m)
        return ea / (ea + eb)

    base = 3 * D_DIFF
    hq = proj(base, D_HGRN)
    uh_ref[:, 0:D_HGRN] = hq * _sigmoid(hq)
    for d, lb_ref in enumerate((lbf_ref, lbb_ref)):
        lb = lower_bound(lb_ref)
        f = lb + (1.0 - lb) * _sigmoid(proj(base + (1 + d) * D_HGRN, D_HGRN))
        uh_ref[:, (1 + 2 * d) * D_HGRN:(2 + 2 * d) * D_HGRN] = 1.0 - f
        uh_ref[:, (2 + 2 * d) * D_HGRN:(3 + 2 * d) * D_HGRN] = jnp.log(f)
    uh_ref[:, 5 * D_HGRN:6 * D_HGRN] = proj(base + 3 * D_HGRN, D_HGRN)
    uh_ref[:, 6 * D_HGRN:7 * D_HGRN] = _sigmoid(proj(base + 4 * D_HGRN, D_HGRN))
    um_ref[...] = proj(base + 5 * D_HGRN, D_XMEM).astype(BF16)


def _in_proj(x2d, w_in_bf, lb_f, lb_b):
    t = x2d.shape[0]
    tm = 1024
    return pl.pallas_call(
        _in_proj_kernel,
        grid=(t // tm,),
        in_specs=[
            pl.BlockSpec((tm, D_MODEL), lambda i: (i, 0)),
            pl.BlockSpec((D_MODEL, D_IN_PROJ), lambda i: (0, 0)),
            pl.BlockSpec((N_LAYERS + 1, D_HGRN), lambda i: (0, 0)),
            pl.BlockSpec((N_LAYERS + 1, D_HGRN), lambda i: (0, 0)),
        ],
        out_specs=[
            pl.BlockSpec((tm, 3 * D_DIFF), lambda i: (i, 0)),
            pl.BlockSpec((tm, 7 * D_HGRN), lambda i: (i, 0)),
            pl.BlockSpec((tm, D_XMEM), lambda i: (i, 0)),
        ],
        out_shape=[
            jax.ShapeDtypeStruct((t, 3 * D_DIFF), BF16),
            jax.ShapeDtypeStruct((t, 7 * D_HGRN), F32),
            jax.ShapeDtypeStruct((t, D_XMEM), BF16),
        ],
        compiler_params=pltpu.CompilerParams(
            dimension_semantics=("arbitrary",), vmem_limit_bytes=VMEM_LIMIT),
        name="in_proj",
    )(x2d, w_in_bf, lb_f, lb_b)


def _bf16_pieces(x):
    x = np.asarray(x, np.float32)
    out = []
    for _ in range(3):
        p = x.astype(BF16).astype(np.float32)
        out.append(p)
        x = x - p
    return np.stack(out, axis=-1)


def _alibi_tables(seq, t):
    slopes = (2.0 ** (-8.0 * np.arange(1, N_HEADS + 1) / N_HEADS)).astype(np.float32)
    c3 = _bf16_pieces(slopes * np.float32(LOG2E))
    pos = np.arange(seq)
    hi = (pos >> DH_SHIFT).astype(np.float32)
    lo = (pos & (DH - 1)).astype(np.float32)
    qf = np.zeros((N_HEADS, 2, seq, 2 * DH), np.float32)
    kf = np.zeros((N_HEADS, 2, seq, 2 * DH), np.float32)
    for m, base in enumerate((DH, 0)):
        for j in range(3):
            qf[:, m, :, base + j] = hi
            qf[:, m, :, base + 3 + j] = lo
            qf[:, m, :, base + 6 + j] = 64.0 * c3[:, j, None]
            qf[:, m, :, base + 9 + j] = c3[:, j, None]
            kf[:, m, :, base + j] = -64.0 * c3[:, j, None]
            kf[:, m, :, base + 3 + j] = -c3[:, j, None]
            kf[:, m, :, base + 6 + j] = hi
            kf[:, m, :, base + 9 + j] = lo
    c = c3.sum(axis=-1)
    kk = np.arange(t)[:, None]
    qq = np.arange(t)[None, :]
    corr = 2.0 * c[:, None, None] * np.minimum(qq - kk, 0).astype(np.float32)
    return (jnp.asarray(qf, dtype=BF16), jnp.asarray(kf, dtype=BF16),
            jnp.asarray(corr, dtype=F32))


def _diff_attn_kernel(q_ref, k_ref, v_ref, qf_ref, kf_ref, corr_ref, nw_ref, lq1_ref,
                      lk1_ref, lq2_ref, lk2_ref, o_ref, ka1_ref, ka2_ref, vt_ref, acc_ref,
                      m_ref, l_ref, *, t, qs, ks, qpt, seq, lam_init):
    i = pl.program_id(2)
    nk = seq // t
    lane = lax.broadcasted_iota(I32, (t, 2 * DH), 1)
    first_half = lane < DH

    @pl.when(i == 0)
    def _():
        def build(r, carry):
            rows = pl.ds(pl.multiple_of(r * t, t), t)
            kblk = k_ref[rows, :]
            ka1_ref[rows, :] = jnp.where(first_half, kblk, kf_ref[0, 0, rows, :])
            ka2_ref[rows, :] = jnp.where(first_half, kf_ref[0, 1, rows, :], kblk)
            vt_ref[r] = v_ref[rows, :].astype(F32).T.astype(BF16)
            return carry

        lax.fori_loop(0, nk, build, 0)

    lam = (jnp.exp(jnp.sum(lq1_ref[...] * lk1_ref[...], axis=-1, keepdims=True))
           - jnp.exp(jnp.sum(lq2_ref[...] * lk2_ref[...], axis=-1, keepdims=True))
           + lam_init)

    for qt in range(qpt):
        qi = i * qpt + qt
        qrange = slice(qt * t, (qt + 1) * t)
        q = q_ref[qrange, :]
        qf1 = qf_ref[0, 0, qrange, :]
        qf2 = qf_ref[0, 1, qrange, :]
        qa1_before = jnp.where(first_half, q, qf1)
        qa1_after = jnp.where(first_half, q, -qf1)
        qa2_before = jnp.where(first_half, qf2, q)
        qa2_after = jnp.where(first_half, -qf2, q)

        m_ref[qt] = jnp.full(m_ref.shape[1:], -jnp.inf, F32)
        l_ref[qt] = jnp.zeros(l_ref.shape[1:], F32)
        acc_ref[qt] = jnp.zeros(acc_ref.shape[1:], F32)

        def chunk(j, qa1, qa2, diagonal, qt=qt):
            k0 = pl.multiple_of(j * t, t)
            for mp, (qa, ka_ref) in enumerate(((qa1, ka1_ref), (qa2, ka2_ref))):
                for u in range(t // qs):
                    qrows = slice(u * qs, (u + 1) * qs)
                    cols = slice(mp * t + u * qs, mp * t + (u + 1) * qs)
                    for kb in range(t // ks):
                        krows = slice(kb * ks, (kb + 1) * ks)
                        s = _nt_dot(ka_ref[pl.ds(k0 + kb * ks, ks), :], qa[qrows, :])
                        if diagonal:
                            s = s + corr_ref[0, krows, qrows]
                        m_old = m_ref[qt, :, cols]
                        m_new = jnp.maximum(m_old, jnp.max(s, axis=0, keepdims=True))
                        p = jnp.exp2(s - m_new)
                        r = jnp.exp2(m_old - m_new)
                        l_ref[qt, :, cols] = (r * l_ref[qt, :, cols]
                                              + jnp.sum(p, axis=0, keepdims=True))
                        acc_ref[qt, mp, :, qrows] = (
                            r * acc_ref[qt, mp, :, qrows]
                            + _dot(vt_ref[j, :, krows], p.astype(BF16)))
                        m_ref[qt, :, cols] = m_new

        chunk(qi, qa1_before, qa2_before, True)
        for jj in range(nk - 1):
            j = jj + (jj >= qi).astype(I32)
            keys_first = j < qi
            chunk(j, jnp.where(keys_first, qa1_before, qa1_after),
                  jnp.where(keys_first, qa2_before, qa2_after), False)

        o = (acc_ref[qt, 0] / l_ref[qt, :, 0:t]
             - lam * (acc_ref[qt, 1] / l_ref[qt, :, t:2 * t]))
        o = o * lax.rsqrt(jnp.mean(o * o, axis=0, keepdims=True) + NORM_EPS)
        o_ref[qrange, :] = (o.T * nw_ref[...] * (1.0 - lam_init)).astype(o_ref.dtype)


def _diff_attn(ua, nw, lq1, lk1, lq2, lk2, batch, seq, lam_init):
    t = 512
    qpt = 2
    nq = seq // (t * qpt)
    qf, kf, corr = _alibi_tables(seq, t)
    kern = functools.partial(_diff_attn_kernel, t=t, qs=LANES, ks=256, qpt=qpt, seq=seq,
                             lam_init=lam_init)
    small = lambda b, h, i: (0, 0)
    return pl.pallas_call(
        kern,
        grid_spec=pltpu.PrefetchScalarGridSpec(
            num_scalar_prefetch=0,
            grid=(batch, N_HEADS, nq),
            in_specs=[
                pl.BlockSpec((qpt * t, 2 * DH), lambda b, h, i: (b * nq + i, h)),
                pl.BlockSpec((seq, 2 * DH), lambda b, h, i: (b, N_HEADS + h)),
                pl.BlockSpec((seq, 2 * DH), lambda b, h, i: (b, 2 * N_HEADS + h)),
                pl.BlockSpec((1, 2, qpt * t, 2 * DH), lambda b, h, i: (h, 0, i, 0)),
                pl.BlockSpec((1, 2, seq, 2 * DH), lambda b, h, i: (h, 0, 0, 0)),
                pl.BlockSpec((1, t, t), lambda b, h, i: (h, 0, 0)),
                pl.BlockSpec((1, 2 * DH), small),
                pl.BlockSpec((1, DH), small),
                pl.BlockSpec((1, DH), small),
                pl.BlockSpec((1, DH), small),
                pl.BlockSpec((1, DH), small),
            ],
            out_specs=pl.BlockSpec((qpt * t, 2 * DH), lambda b, h, i: (b * nq + i, h)),
            scratch_shapes=[
                pltpu.VMEM((seq, 2 * DH), BF16),
                pltpu.VMEM((seq, 2 * DH), BF16),
                pltpu.VMEM((seq // t, 2 * DH, t), BF16),
                pltpu.VMEM((qpt, 2, 2 * DH, t), F32),
                pltpu.VMEM((qpt, 1, 2 * t), F32),
                pltpu.VMEM((qpt, 1, 2 * t), F32),
            ],
        ),
        out_shape=jax.ShapeDtypeStruct((batch * seq, D_DIFF), BF16),
        compiler_params=pltpu.CompilerParams(
            dimension_semantics=("arbitrary", "arbitrary", "arbitrary"),
            vmem_limit_bytes=VMEM_LIMIT),
        name="diff_attn",
    )(ua, ua, ua, qf, kf, corr, nw, lq1, lk1, lq2, lk2)


def _mem_attn_kernel(q_ref, mem_ref, wkv_ref, o_ref, kv_ref, *, tq, mlen):
    @pl.when(pl.program_id(1) == 0)
    def _():
        kv_ref[...] = _dot(mem_ref[...].astype(BF16), wkv_ref[...]).astype(BF16)

    q = q_ref[...]
    mk = kv_ref[:, 0:D_XMEM]
    mv = kv_ref[:, D_XMEM:2 * D_XMEM]
    qhead = lax.broadcasted_iota(I32, (tq, D_XMEM), 1) >> DH_SHIFT
    vhead = lax.broadcasted_iota(I32, (mlen, D_XMEM), 1) >> DH_SHIFT
    acc = jnp.zeros((tq, D_XMEM), F32)
    for h in range(N_HEADS):
        qh = jnp.where(qhead == h, q, jnp.zeros_like(q))
        s = _nt_dot(qh, mk) * (1.0 / math.sqrt(DH))
        e = jnp.exp(s - jnp.max(s, axis=-1, keepdims=True))
        p = e / jnp.sum(e, axis=-1, keepdims=True)
        vh = jnp.where(vhead == h, mv, jnp.zeros_like(mv))
        acc = acc + _dot(p.astype(BF16), vh)
    o_ref[...] = acc.astype(o_ref.dtype)


def _mem_attn(um, mem2d, wkv_bf, batch, seq, mlen):
    tq = 1024
    nq = seq // tq
    kern = functools.partial(_mem_attn_kernel, tq=tq, mlen=mlen)
    return pl.pallas_call(
        kern,
        grid=(batch, nq),
        in_specs=[
            pl.BlockSpec((tq, D_XMEM), lambda b, i: (b * nq + i, 0)),
            pl.BlockSpec((mlen, D_MODEL), lambda b, i: (b, 0)),
            pl.BlockSpec((D_MODEL, 2 * D_XMEM), lambda b, i: (0, 0)),
        ],
        out_specs=pl.BlockSpec((tq, D_XMEM), lambda b, i: (b * nq + i, 0)),
        out_shape=jax.ShapeDtypeStruct((batch * seq, D_XMEM), BF16),
        scratch_shapes=[pltpu.VMEM((mlen, 2 * D_XMEM), BF16)],
        compiler_params=pltpu.CompilerParams(
            dimension_semantics=("arbitrary", "arbitrary"), vmem_limit_bytes=VMEM_LIMIT),
        name="mem_attn",
    )(um, mem2d, wkv_bf)


def _hgrn2_kernel(q_ref, kf_ref, lf_ref, kb_ref, lb_ref, v_ref, g_ref, nw_ref, o_ref,
                  kfp_ref, kbp_ref, vp_ref, gf_ref, gb_ref, bf_ref, bb_ref, acc_ref, cross_ref,
                  *, seq, rb):
    n_chunks = seq // CHUNK
    w = 2 * DH
    sb = 128
    row = lax.broadcasted_iota(I32, (rb, w), 0) & (CHUNK - 1)
    li = lax.broadcasted_iota(I32, (w, w), 0) >> DH_SHIFT
    lj = lax.broadcasted_iota(I32, (w, w), 1) >> DH_SHIFT
    same_head = li == lj
    head_ones = jnp.where(same_head, 1.0, 0.0).astype(BF16)

    def chunk_cumsum(x, reverse):
        for sh in (1, 2, 4, 8):
            if reverse:
                moved = pltpu.roll(x, rb - sh, 0)
                keep = row + sh < CHUNK
            else:
                moved = pltpu.roll(x, sh, 0)
                keep = row >= sh
            x = x + jnp.where(keep, moved, 0.0)
        return x

    halo = jnp.zeros((CHUNK, w), F32)
    for ref in (kfp_ref, kbp_ref, vp_ref, gf_ref, gb_ref, bf_ref, bb_ref):
        ref[0:CHUNK, :] = halo
        ref[seq + CHUNK:seq + 2 * CHUNK, :] = halo

    def prepare(blk, carry):
        r0 = pl.multiple_of(blk * rb, rb)
        rows = pl.ds(r0, rb)
        inner = pl.ds(r0 + CHUNK, rb)
        kfp_ref[inner, :] = kf_ref[rows, :]
        kbp_ref[inner, :] = kb_ref[rows, :]
        vp_ref[inner, :] = v_ref[rows, :]
        lf2 = lf_ref[rows, :] * LOG2E
        lb2 = lb_ref[rows, :] * LOG2E
        gf_ref[inner, :] = jnp.where(row == 0, 0.0, jnp.exp2(lf2))
        gb_ref[inner, :] = jnp.where(row == CHUNK - 1, 0.0, jnp.exp2(lb2))
        bf_ref[inner, :] = chunk_cumsum(lf2, False)
        bb_ref[inner, :] = chunk_cumsum(lb2, True)
        return carry

    lax.fori_loop(0, seq // rb, prepare, 0)

    def intra(blk):
        for part in range(rb // sb):
            r0 = pl.multiple_of(blk * rb + part * sb, sb)
            rows = pl.ds(r0, sb)
            q = q_ref[rows, :]
            acc = jnp.zeros((sb, w), F32)
            for reverse, kp_ref, g_ref2 in ((False, kfp_ref, gf_ref), (True, kbp_ref, gb_ref)):
                dec = None
                for d in range(CHUNK):
                    src = pl.ds(r0 + CHUNK + (d if reverse else -d), sb)
                    ks = kp_ref[src, :]
                    vs = vp_ref[src, :]
                    if d == 0:
                        z = q * ks
                    else:
                        gsrc = pl.ds(r0 + CHUNK + ((d - 1) if reverse else -(d - 1)), sb)
                        gs = g_ref2[gsrc, :]
                        dec = gs if dec is None else dec * gs
                        z = q * ks * dec
                    acc = acc + _dot(z.astype(BF16), head_ones) * vs
            acc_ref[rows, :] = acc

    def chunk_updates(blk):
        per_dir = ([], [])
        for c in range(rb // CHUNK):
            n = blk * (rb // CHUNK) + c
            for d, (r0, kp_ref, b_ref, edge) in enumerate((
                    (n * CHUNK, kfp_ref, bf_ref, CHUNK - 1),
                    ((n_chunks - 1 - n) * CHUNK, kbp_ref, bb_ref, 0))):
                r0 = pl.multiple_of(r0, CHUNK)
                rows = pl.ds(r0, CHUNK)
                prow = pl.ds(r0 + CHUNK, CHUNK)
                b = b_ref[prow, :]
                b_edge = b_ref[pl.ds(r0 + CHUNK + edge, 1), :]
                qd = (q_ref[rows, :] * jnp.exp2(b)).astype(BF16)
                kd = kp_ref[prow, :] * jnp.exp2(b_edge - b)
                upd = _tn_dot(vp_ref[prow, :].astype(BF16), kd.astype(BF16))
                per_dir[d].append((rows, qd, jnp.exp2(b_edge), jnp.where(same_head, upd, 0.0)))
        return per_dir

    def state_chain(per_dir, states):
        for d in range(2):
            for rows, qd, decay, upd in per_dir[d]:
                cross_ref[rows, :] = cross_ref[rows, :] + _nt_dot(qd, states[d].astype(BF16))
                states[d] = states[d] * decay + upd

    cross_ref[...] = jnp.zeros(cross_ref.shape, F32)

    def fused(blk, carry):
        states = list(carry)
        intra(blk)
        state_chain(chunk_updates(blk), states)
        return tuple(states)

    z = jnp.zeros((w, w), F32)
    lax.fori_loop(0, seq // rb, fused, (z, z))

    def finish(blk, carry):
        rows = pl.ds(pl.multiple_of(blk * rb, rb), rb)
        o = acc_ref[rows, :] + cross_ref[rows, :]
        lane = lax.broadcasted_iota(I32, (rb, w), 1)
        lo = lane < DH
        sq = o * o
        ms_lo = jnp.sum(jnp.where(lo, sq, 0.0), axis=-1, keepdims=True)
        ms_hi = jnp.sum(jnp.where(lo, 0.0, sq), axis=-1, keepdims=True)
        ms = jnp.where(lo, ms_lo, ms_hi) * (1.0 / DH)
        o = o * lax.rsqrt(ms + NORM_EPS) * nw_ref[...] * g_ref[rows, :]
        o_ref[rows, :] = o.astype(o_ref.dtype)
        return carry

    lax.fori_loop(0, seq // rb, finish, 0)


def _hgrn2(uh, nw, batch, seq):
    w = 2 * DH
    npair = D_HGRN // w
    kern = functools.partial(_hgrn2_kernel, seq=seq, rb=256)

    def sec(s):
        return pl.BlockSpec((seq, w), lambda b, p, s=s: (b, s * npair + p))

    return pl.pallas_call(
        kern,
        grid=(batch, npair),
        in_specs=[sec(0), sec(1), sec(2), sec(3), sec(4), sec(5), sec(6),
                  pl.BlockSpec((1, w), lambda b, p: (0, p))],
        out_specs=pl.BlockSpec((seq, w), lambda b, p: (b, p)),
        out_shape=jax.ShapeDtypeStruct((batch * seq, D_HGRN), BF16),
        scratch_shapes=[pltpu.VMEM((seq + 2 * CHUNK, w), F32)] * 7
        + [pltpu.VMEM((seq, w), F32)] * 2,
        compiler_params=pltpu.CompilerParams(
            dimension_semantics=("arbitrary", "arbitrary"), vmem_limit_bytes=VMEM_LIMIT),
        name="hgrn2",
    )(uh, uh, uh, uh, uh, uh, uh, nw)


def _rows_to_block(rows, tm):
    r = lax.broadcasted_iota(I32, (8, tm), 0)
    out = jnp.zeros((8, tm), rows[0].dtype)
    for k, row in enumerate(rows):
        out = jnp.where(r == k, row, out)
    return out


def _pack_bf16_pairs(x):
    m = x.shape[1] // 2
    u = lax.bitcast_convert_type(x, U32)
    r = u + jnp.uint32(0x7FFF) + ((u >> 16) & jnp.uint32(1))
    return (r[:, 0:m] >> 16) | (r[:, m:2 * m] & jnp.uint32(0xFFFF0000))


def _unpack_bf16_pairs(w):
    lo = lax.bitcast_convert_type(w << 16, F32).astype(BF16)
    hi = lax.bitcast_convert_type(w & jnp.uint32(0xFFFF0000), F32).astype(BF16)
    return jnp.concatenate([lo, hi], axis=1)


def _post_mixer_kernel(x_ref, od_ref, oh_ref, om_ref, wo_ref, g_ref, b_ref, rwt_ref,
                       rbt_ref, h_ref, hp_ref, eidx_ref, gate_ref, rank_ref, cnt_ref,
                       carry_ref, *, tm, alpha):
    i = pl.program_id(0)

    @pl.when(i == 0)
    def _():
        carry_ref[...] = jnp.zeros_like(carry_ref)

    mix = (_dot(od_ref[...], wo_ref[0:D_DIFF, :])
           + _dot(oh_ref[...], wo_ref[D_DIFF:D_DIFF + D_HGRN, :])
           + _dot(om_ref[...], wo_ref[D_DIFF + D_HGRN:, :]))
    h = _layer_norm(alpha * x_ref[...] + mix, g_ref[...], b_ref[...])
    h_ref[...] = h

    h_hi = h.astype(BF16)
    hp_ref[...] = _pack_bf16_pairs(h)
    h_lo = (h - h_hi.astype(F32)).astype(BF16)
    rwt = rwt_ref[...]
    rwt_hi = rwt.astype(BF16)
    rwt_lo = (rwt - rwt_hi.astype(F32)).astype(BF16)
    logits = (_nt_dot(rwt_hi, h_hi) + _nt_dot(rwt_hi, h_lo) + _nt_dot(rwt_lo, h_hi)
              + rbt_ref[...])

    erow = lax.broadcasted_iota(I32, (N_EXPERTS, tm), 0).astype(F32)
    work = logits
    sels, vals, idxs = [], [], []
    for _ in range(TOP_K):
        m = jnp.max(work, axis=0, keepdims=True)
        idx = jnp.min(jnp.where(work == m, erow, float(N_EXPERTS)), axis=0, keepdims=True)
        sel = erow == idx
        work = jnp.where(sel, -jnp.inf, work)
        sels.append(sel)
        vals.append(m)
        idxs.append(idx)
    es = [jnp.exp(v - vals[0]) for v in vals]
    den = es[0] + es[1] + es[2] + es[3]

    chosen = jnp.where(sels[0] | sels[1] | sels[2] | sels[3], 1.0, 0.0)
    ti = lax.broadcasted_iota(I32, (tm, tm), 0)
    tj = lax.broadcasted_iota(I32, (tm, tm), 1)
    earlier = jnp.where(ti < tj, 1.0, 0.0).astype(BF16)
    prefix = _dot(chosen.astype(BF16), earlier) + carry_ref[:, 0:1]

    ranks = [jnp.sum(jnp.where(sels[k], prefix, 0.0), axis=0, keepdims=True)
             for k in range(TOP_K)]
    eidx_ref[...] = _rows_to_block(idxs, tm).astype(I32)
    gate_ref[...] = _rows_to_block([e / den for e in es], tm)
    rank_ref[...] = _rows_to_block(ranks, tm).astype(I32)

    total = carry_ref[:, 0:1] + jnp.sum(chosen, axis=1, keepdims=True)
    carry_ref[...] = jnp.broadcast_to(total, carry_ref.shape)
    cnt_ref[...] = carry_ref[...]


def _post_mixer(x2d, od, oh, om, wo_bf, g, b, rwt, rbt, alpha):
    t = x2d.shape[0]
    tm = 512
    kern = functools.partial(_post_mixer_kernel, tm=tm, alpha=alpha)
    full = lambda i: (0, 0)
    tok = lambda i: (i, 0)
    per_tok = lambda i: (0, i)
    return pl.pallas_call(
        kern,
        grid=(t // tm,),
        in_specs=[
            pl.BlockSpec((tm, D_MODEL), tok),
            pl.BlockSpec((tm, D_DIFF), tok),
            pl.BlockSpec((tm, D_HGRN), tok),
            pl.BlockSpec((tm, D_XMEM), tok),
            pl.BlockSpec((D_MODEL, D_MODEL), full),
            pl.BlockSpec((1, D_MODEL), full),
            pl.BlockSpec((1, D_MODEL), full),
            pl.BlockSpec((N_EXPERTS, D_MODEL), full),
            pl.BlockSpec((N_EXPERTS, 1), full),
        ],
        out_specs=[
            pl.BlockSpec((tm, D_MODEL), tok),
            pl.BlockSpec((tm, D_MODEL // 2), tok),
            pl.BlockSpec((8, tm), per_tok),
            pl.BlockSpec((8, tm), per_tok),
            pl.BlockSpec((8, tm), per_tok),
            pl.BlockSpec((N_EXPERTS, LANES), full),
        ],
        out_shape=[
            jax.ShapeDtypeStruct((t, D_MODEL), F32),
            jax.ShapeDtypeStruct((t, D_MODEL // 2), U32),
            jax.ShapeDtypeStruct((8, t), I32),
            jax.ShapeDtypeStruct((8, t), F32),
            jax.ShapeDtypeStruct((8, t), I32),
            jax.ShapeDtypeStruct((N_EXPERTS, LANES), F32),
        ],
        scratch_shapes=[pltpu.VMEM((N_EXPERTS, LANES), F32)],
        compiler_params=pltpu.CompilerParams(
            dimension_semantics=("arbitrary",), vmem_limit_bytes=VMEM_LIMIT),
        name="post_mixer",
    )(x2d, od, oh, om, wo_bf, g, b, rwt, rbt)


def _route_meta_kernel(eidx_ref, rank_ref, cnt_ref, dest_ref, be_ref, *, tm, nb_lanes):
    cnt = cnt_ref[...]
    blocks = jnp.floor((cnt + (MOE_BLOCK - 1)) * (1.0 / MOE_BLOCK))
    ei = lax.broadcasted_iota(I32, (N_EXPERTS, N_EXPERTS), 0)
    ej = lax.broadcasted_iota(I32, (N_EXPERTS, N_EXPERTS), 1)
    below = jnp.where(ej < ei, 1.0, 0.0).astype(BF16)
    pstart = _dot(below, blocks.astype(BF16))[:, 0:1]
    pend = pstart + blocks[:, 0:1]

    erow = lax.broadcasted_iota(I32, (N_EXPERTS, tm), 0).astype(F32)
    eidx = eidx_ref[...].astype(F32)
    starts = [jnp.sum(jnp.where(erow == eidx[k:k + 1, :], pstart, 0.0), axis=0, keepdims=True)
              for k in range(TOP_K)]
    dest_ref[...] = ((_rows_to_block(starts, tm) * float(MOE_BLOCK)).astype(I32)
                     + rank_ref[...])

    bi = lax.broadcasted_iota(I32, (N_EXPERTS, nb_lanes), 1).astype(F32)
    done = jnp.where(pend <= bi, 1.0, 0.0)
    be = jnp.minimum(jnp.sum(done, axis=0, keepdims=True), float(N_EXPERTS - 1))
    used = jnp.broadcast_to(pend[N_EXPERTS - 1:N_EXPERTS, :], (1, nb_lanes))
    be_ref[...] = _rows_to_block([be, used], nb_lanes).astype(I32)


def _route_meta(eidx, rank, cnt, n_blocks):
    t = eidx.shape[1]
    tm = 2048
    nb_lanes = -(-n_blocks // LANES) * LANES
    kern = functools.partial(_route_meta_kernel, tm=tm, nb_lanes=nb_lanes)
    return pl.pallas_call(
        kern,
        grid=(t // tm,),
        in_specs=[
            pl.BlockSpec((8, tm), lambda i: (0, i)),
            pl.BlockSpec((8, tm), lambda i: (0, i)),
            pl.BlockSpec((N_EXPERTS, LANES), lambda i: (0, 0)),
        ],
        out_specs=[
            pl.BlockSpec((8, tm), lambda i: (0, i)),
            pl.BlockSpec((8, nb_lanes), lambda i: (0, 0)),
        ],
        out_shape=[
            jax.ShapeDtypeStruct((8, t), I32),
            jax.ShapeDtypeStruct((8, nb_lanes), I32),
        ],
        compiler_params=pltpu.CompilerParams(
            dimension_semantics=("arbitrary",), vmem_limit_bytes=VMEM_LIMIT),
        name="route_meta",
    )(eidx, rank, cnt)


SC_WINDOW = LANES
N_QUARTERS = 4
QUARTER = D_MODEL // N_QUARTERS
N_PACKED = D_MODEL // 2 // QUARTER


def _sc_mesh():
    return plsc.VectorSubcoreMesh(core_axis_name="c", subcore_axis_name="s")


def _dispatch(dest_km, hp, n_rows):
    t = hp.shape[0]

    @functools.partial(
        pl.kernel, out_type=[jax.ShapeDtypeStruct((n_rows, QUARTER), hp.dtype)] * N_PACKED,
        mesh=_sc_mesh(), scratch_types=[])
    def scatter_rows(h_hbm, d_hbm, *xs_hbm):
        for q in range(N_PACKED):
            def body(x_vmem, i_vmem, q=q):
                for k in range(TOP_K):
                    pltpu.sync_copy(x_vmem, xs_hbm[q].at[i_vmem.at[k]])

            pltpu.emit_pipeline(
                body,
                grid=(t // SC_WINDOW,),
                in_specs=[pl.BlockSpec((SC_WINDOW, QUARTER), lambda i, q=q: (i, q)),
                          pl.BlockSpec((TOP_K, SC_WINDOW), lambda i: (0, i))],
                out_specs=[],
                core_axis_name=("c", "s"),
                dimension_semantics=(pltpu.PARALLEL,),
            )(h_hbm, d_hbm)

    return scatter_rows(hp, dest_km)


def _gather_back(dest_km, ys):
    t = dest_km.shape[1]
    n = len(ys)

    @functools.partial(
        pl.kernel,
        out_type=[jax.ShapeDtypeStruct((t, TOP_K * QUARTER), ys[0].dtype)] * n,
        mesh=_sc_mesh(), scratch_types=[])
    def gather_rows(d_hbm, *refs):
        ys_hbm, out_hbm = refs[:n], refs[n:]
        for q in range(n):
            for k in range(TOP_K):
                def body(i_vmem, o_vmem, q=q, k=k):
                    pltpu.sync_copy(ys_hbm[q].at[i_vmem.at[k]], o_vmem)

                pltpu.emit_pipeline(
                    body,
                    grid=(t // SC_WINDOW,),
                    in_specs=[pl.BlockSpec((TOP_K, SC_WINDOW), lambda i: (0, i))],
                    out_specs=[pl.BlockSpec((SC_WINDOW, QUARTER), lambda i, k=k: (i, k))],
                    core_axis_name=("c", "s"),
                    dimension_semantics=(pltpu.PARALLEL,),
                )(d_hbm, out_hbm[q])

    return gather_rows(dest_km, *ys)


def _experts_kernel(be_ref, used_ref, *refs):
    xs_refs = refs[:N_PACKED]
    wgu_hbm, wdn_hbm, bg_ref, bl_ref, bd_ref, perm_ref = refs[N_PACKED:N_PACKED + 6]
    ys_refs = refs[N_PACKED + 6:2 * N_PACKED + 6]
    wgu_st, wdn_st, wg_ref, wl_ref, wd_ref, sem = refs[2 * N_PACKED + 6:]
    p = pl.program_id(0)
    used = used_ref[0]
    b0 = 2 * p
    b1 = b0 + 1
    e0 = be_ref[b0]
    e1 = be_ref[b1]
    act0 = b0 < used
    act1 = b1 < used
    fresh0 = jnp.logical_and(act0, jnp.logical_or(p == 0, e0 != be_ref[jnp.maximum(b0 - 1, 0)]))
    same = jnp.logical_and(act1, e1 == e0)
    fresh1 = jnp.logical_and(act1, e1 != e0)

    def weight_copies(ex):
        return (pltpu.make_async_copy(wgu_hbm.at[ex], wgu_st, sem.at[0]),
                pltpu.make_async_copy(wdn_hbm.at[ex], wdn_st, sem.at[1]))

    def load_expert(ex, blk):
        for cp in weight_copies(ex):
            cp.wait()
        perm = perm_ref[...]
        half = LANES
        for c in range(2 * D_EXPERT // (2 * half)):
            slab = wgu_st[:, c * 2 * half:(c + 1) * 2 * half].astype(BF16)
            sep = _dot(slab, perm)
            wg_ref[:, c * half:(c + 1) * half] = sep[:, 0:half].astype(BF16)
            wl_ref[:, c * half:(c + 1) * half] = sep[:, half:2 * half].astype(BF16)
        wd_ref[...] = wdn_st[...].astype(BF16)

        def same_expert(j):
            return jnp.logical_and(j < used, be_ref[jnp.minimum(j, used - 1)] == ex)

        nxt = lax.while_loop(same_expert, lambda j: j + 1, blk + 1)

        @pl.when(nxt < used)
        def _():
            for cp in weight_copies(be_ref[nxt]):
                cp.start()

    def compute(ex, r0, nrows):
        rows = slice(r0, r0 + nrows)
        xb = _unpack_bf16_pairs(jnp.concatenate([r[rows, :] for r in xs_refs], axis=1))
        glu = jnp.minimum(_dot(xb, wg_ref[...]) + bg_ref[ex], SWIGLU_LIMIT)
        lin = jnp.clip(_dot(xb, wl_ref[...]) + bl_ref[ex], -SWIGLU_LIMIT, SWIGLU_LIMIT)
        act = glu * _sigmoid(SWIGLU_ALPHA * glu) * (lin + 1.0)
        y = _pack_bf16_pairs(_dot(act.astype(BF16), wd_ref[...]) + bd_ref[ex])
        for q, ys_ref in enumerate(ys_refs):
            ys_ref[rows, :] = y[:, q * QUARTER:(q + 1) * QUARTER]

    @pl.when(jnp.logical_and(p == 0, used > 0))
    def _():
        for cp in weight_copies(e0):
            cp.start()

    @pl.when(fresh0)
    def _():
        load_expert(e0, b0)

    @pl.when(jnp.logical_and(act0, same))
    def _():
        compute(e0, 0, 2 * MOE_BLOCK)

    @pl.when(jnp.logical_and(act0, jnp.logical_not(same)))
    def _():
        compute(e0, 0, MOE_BLOCK)

    @pl.when(fresh1)
    def _():
        load_expert(e1, b1)
        compute(e1, MOE_BLOCK, MOE_BLOCK)

    for blk_active, r0 in ((act0, 0), (act1, MOE_BLOCK)):
        @pl.when(jnp.logical_not(blk_active))
        def _(r0=r0):
            for ys_ref in ys_refs:
                ys_ref[r0:r0 + MOE_BLOCK, :] = jnp.zeros((MOE_BLOCK, QUARTER), U32)


def _experts(block_e, used, xs, w_gu, w_dn, b_g, b_l, b_d, perm):
    n_rows = xs[0].shape[0]
    n_blocks = n_rows // MOE_BLOCK
    assert n_blocks % 2 == 0
    wmap = lambda i, be, u: (0, 0, 0)
    rows = pl.BlockSpec((2 * MOE_BLOCK, QUARTER), lambda i, be, u: (i, 0))
    return pl.pallas_call(
        _experts_kernel,
        grid_spec=pltpu.PrefetchScalarGridSpec(
            num_scalar_prefetch=2,
            grid=(n_blocks // 2,),
            in_specs=[rows] * N_PACKED + [
                pl.BlockSpec(memory_space=pl.ANY),
                pl.BlockSpec(memory_space=pl.ANY),
                pl.BlockSpec((N_EXPERTS, 1, D_EXPERT), wmap),
                pl.BlockSpec((N_EXPERTS, 1, D_EXPERT), wmap),
                pl.BlockSpec((N_EXPERTS, 1, D_MODEL), wmap),
                pl.BlockSpec((2 * LANES, 2 * LANES), lambda i, be, u: (0, 0)),
            ],
            out_specs=[rows] * N_PACKED,
            scratch_shapes=[
                pltpu.VMEM((D_MODEL, 2 * D_EXPERT), F32),
                pltpu.VMEM((D_EXPERT, D_MODEL), F32),
                pltpu.VMEM((D_MODEL, D_EXPERT), BF16),
                pltpu.VMEM((D_MODEL, D_EXPERT), BF16),
                pltpu.VMEM((D_EXPERT, D_MODEL), BF16),
                pltpu.SemaphoreType.DMA((2,)),
            ],
        ),
        out_shape=[jax.ShapeDtypeStruct((n_rows, QUARTER), U32)] * N_PACKED,
        compiler_params=pltpu.CompilerParams(
            dimension_semantics=("arbitrary",), vmem_limit_bytes=VMEM_LIMIT),
        name="experts",
    )(block_e, used, *xs, w_gu, w_dn, b_g, b_l, b_d, perm)


def _combine_kernel(h_ref, gate_ref, g_ref, b_ref, *refs, tm, alpha):
    yg_refs, o_ref = refs[:N_PACKED], refs[N_PACKED]
    gate = jnp.transpose(gate_ref[...])
    gks = [gate[:, k:k + 1] for k in range(TOP_K)]
    lows, highs = [], []
    for yg_ref in yg_refs:
        lo = jnp.zeros((tm, QUARTER), F32)
        hi = jnp.zeros((tm, QUARTER), F32)
        for k in range(TOP_K):
            w = yg_ref[:, k * QUARTER:(k + 1) * QUARTER]
            lo = lo + gks[k] * lax.bitcast_convert_type(w << 16, F32)
            hi = hi + gks[k] * lax.bitcast_convert_type(w & jnp.uint32(0xFFFF0000), F32)
        lows.append(lo)
        highs.append(hi)
    ffn = jnp.concatenate(lows + highs, axis=1)
    o_ref[...] = _layer_norm(alpha * h_ref[...] + ffn, g_ref[...], b_ref[...])


def _combine(h, gate, g, b, yg, alpha):
    t = h.shape[0]
    tm = 1024
    kern = functools.partial(_combine_kernel, tm=tm, alpha=alpha)
    tok = lambda i: (i, 0)
    return pl.pallas_call(
        kern,
        grid=(t // tm,),
        in_specs=[
            pl.BlockSpec((tm, D_MODEL), tok),
            pl.BlockSpec((8, tm), lambda i: (0, i)),
            pl.BlockSpec((1, D_MODEL), lambda i: (0, 0)),
            pl.BlockSpec((1, D_MODEL), lambda i: (0, 0)),
        ] + [pl.BlockSpec((tm, TOP_K * QUARTER), tok)] * N_PACKED,
        out_specs=pl.BlockSpec((tm, D_MODEL), tok),
        out_shape=jax.ShapeDtypeStruct((t, D_MODEL), F32),
        compiler_params=pltpu.CompilerParams(
            dimension_semantics=("arbitrary",), vmem_limit_bytes=VMEM_LIMIT),
        name="combine",
    )(h, gate, g, b, *yg)


def _column_split_permutation():
    n = 2 * LANES
    p = np.zeros((n, n), np.float32)
    j = np.arange(n)
    p[j, j // 2 + LANES * (j % 2)] = 1.0
    return jnp.asarray(p, dtype=BF16)


def kernel(x, mem, w_in, lam_q1, lam_k1, lam_q2, lam_k2, diff_norm_w, hgrn_lb_fwd,
           hgrn_lb_bwd, hgrn_norm_w, w_mem_kv, w_o, ln1_g, ln1_b, router_w, router_b,
           w_gate_up, b_gate_up, w_down, b_down, ln2_g, ln2_b):
    batch, seq, d = x.shape
    mlen = mem.shape[1]
    assert d == D_MODEL and w_in.shape == (N_LAYERS, D_MODEL, D_IN_PROJ)
    assert hgrn_lb_fwd.shape == (N_LAYERS + 1, D_HGRN)
    t = batch * seq
    alpha = (2.0 * N_LAYERS) ** 0.25
    lam_init = 0.8 - 0.6 * math.exp(-0.3 * 0)

    x2d = x.reshape(t, D_MODEL)
    ua, uh, um = _in_proj(x2d, w_in[0].astype(BF16), hgrn_lb_fwd, hgrn_lb_bwd)
    o_diff = _diff_attn(ua, diff_norm_w, lam_q1, lam_k1, lam_q2, lam_k2,
                        batch, seq, lam_init)
    o_mem = _mem_attn(um, mem.reshape(batch * mlen, D_MODEL), w_mem_kv[0].astype(BF16),
                      batch, seq, mlen)
    o_hgrn = _hgrn2(uh, hgrn_norm_w, batch, seq)

    h1, h1_packed, eidx, gate, rank, cnt = _post_mixer(
        x2d, o_diff, o_hgrn, o_mem, w_o[0].astype(BF16), ln1_g, ln1_b, router_w[0].T,
        router_b.reshape(N_EXPERTS, 1), alpha)

    n_blocks = -(-(t * TOP_K) // MOE_BLOCK) + N_EXPERTS
    dest, meta = _route_meta(eidx, rank, cnt, n_blocks)
    dest_km = dest[0:TOP_K]
    block_e = meta[0, 0:n_blocks]
    used = meta[1, 0:1]

    xs = _dispatch(dest_km, h1_packed, n_blocks * MOE_BLOCK)
    b_g = b_gate_up[0][:, 0::2].reshape(N_EXPERTS, 1, D_EXPERT)
    b_l = b_gate_up[0][:, 1::2].reshape(N_EXPERTS, 1, D_EXPERT)
    ys = _experts(block_e, used, xs, w_gate_up[0], w_down[0], b_g, b_l,
                  b_down[0].reshape(N_EXPERTS, 1, D_MODEL), _column_split_permutation())
    out = _combine(h1, gate, ln2_g, ln2_b, _gather_back(dest_km, ys), alpha)
    return out.reshape(batch, seq, D_MODEL)
```
